```python
import math
import jax, jax.numpy as jnp
from jax import lax
import numpy as np

D_MODEL = 1024
BATCH = 8
SEQ = 2048
DEPTH = 4
DEC_BATCH = 128
DEC_SEQ = 1
PAST_LEN = 16384
PAGE_SIZE = 128

H_A = 6
DK_A = 128
DV_A = 128
QK_A = H_A * DK_A
V_A = H_A * DV_A
CONV_W = 4
CONV_DIM = 2 * QK_A + V_A
GDN_CHUNK = 64
G_B = 6
CH_B = 128
D_B = G_B * CH_B
CHUNK_B = 128
N_MEM = 256
H_X = 4
DH_X = 64
D_X = H_X * DH_X
D_MIX = V_A + D_X
IN_A = CONV_DIM + V_A + 2 * H_A + D_X
IN_B = 2 * D_B + D_X
D_FF = 3584
N_EXP = 8
TOP_K = 2
N_A = (DEPTH + 1) // 2
N_B = DEPTH // 2
ALPHA = (2 * DEPTH) ** 0.25
BETA_INIT = (8 * DEPTH) ** -0.25
LN_EPS = 1e-5
RMS_EPS = 1e-6

kernel_name = "hybrid_gdn_gmlp_memxattn_deepnorm_step"


def layer_norm(x, g, b):
    xf = x.astype(jnp.float32)
    mu = jnp.mean(xf, axis=-1, keepdims=True)
    var = jnp.mean(jnp.square(xf - mu), axis=-1, keepdims=True)
    y = (xf - mu) * lax.rsqrt(var + LN_EPS) * g.astype(jnp.float32) + b.astype(jnp.float32)
    return y.astype(x.dtype)


def l2norm(x):
    return x * lax.rsqrt(jnp.sum(x * x, axis=-1, keepdims=True) + RMS_EPS)


def causal_conv_silu(xh, w):
    T = xh.shape[1] - (CONV_W - 1)
    y = sum(xh[:, j:j + T] * w[j] for j in range(CONV_W))
    return jax.nn.silu(y)


def gdn_chunked(q, k, v, g, beta, S0):
    B, T, H, _ = q.shape
    C = GDN_CHUNK
    N = T // C

    def to_chunks(t):
        return jnp.moveaxis(t.reshape(B, N, C, *t.shape[2:]), 3, 2)

    qc, kc, vc, gc, bc = (to_chunks(t) for t in (q, k, v, g, beta))
    gcum = jnp.cumsum(gc, axis=-1)
    idx = jnp.arange(C)
    causal = idx[:, None] >= idx[None, :]
    strict = idx[:, None] > idx[None, :]
    diff = gcum[..., :, None] - gcum[..., None, :]
    decay = jnp.exp(jnp.where(causal, diff, -jnp.inf))
    kb = kc * bc[..., None]
    A = jnp.where(strict, jnp.einsum('bnhik,bnhjk->bnhij', kb, kc) * decay, 0.0)
    eye = jnp.eye(C, dtype=jnp.float32)
    rhs = jnp.concatenate([vc * bc[..., None], kb * jnp.exp(gcum)[..., None]], axis=-1)
    sol = lax.linalg.triangular_solve(eye + A, rhs, left_side=True, lower=True)
    u_val, w_cum = sol[..., :DV_A], sol[..., DV_A:]
    attn_intra = jnp.einsum('bnhik,bnhjk->bnhij', qc, kc) * decay
    q_dec = qc * jnp.exp(gcum)[..., None]
    k_dec = kc * jnp.exp(gcum[..., -1:] - gcum)[..., None]
    g_last = jnp.exp(gcum[..., -1])
    xs = tuple(jnp.moveaxis(t, 1, 0) for t in (attn_intra, q_dec, k_dec, u_val, w_cum, g_last))

    def step(S, inp):
        attn, qd, kd, uv, wc, gl = inp
        vnew = uv - jnp.einsum('bhck,bhkv->bhcv', wc, S)
        o = jnp.einsum('bhck,bhkv->bhcv', qd, S) + jnp.einsum('bhij,bhjv->bhiv', attn, vnew)
        S = S * gl[..., None, None] + jnp.einsum('bhck,bhcv->bhkv', kd, vnew)
        return S, o

    S, o = lax.scan(step, S0, xs)
    o = jnp.transpose(o, (1, 0, 3, 2, 4)).reshape(B, T, H, DV_A)
    return o, S


def gdn_recurrent(q, k, v, g, beta, S0):
    def step(S, inp):
        qt, kt, vt, gt, bt = inp
        S = S * jnp.exp(gt)[..., None, None]
        u = bt[..., None] * (vt - jnp.einsum('bhk,bhkv->bhv', kt, S))
        S = S + jnp.einsum('bhk,bhv->bhkv', kt, u)
        return S, jnp.einsum('bhk,bhkv->bhv', qt, S)

    xs = tuple(jnp.moveaxis(t, 1, 0) for t in (q, k, v, g, beta))
    S, o = lax.scan(step, S0, xs)
    return jnp.moveaxis(o, 0, 1), S


def gdn_mixer(proj, conv_hist, conv_w, A_log, dt_bias, norm_w, S0, chunked):
    B, T, _ = proj.shape
    f32 = jnp.float32
    qkv_pre = proj[..., :CONV_DIM]
    z = proj[..., CONV_DIM:CONV_DIM + V_A]
    a = proj[..., CONV_DIM + V_A:CONV_DIM + V_A + H_A]
    bb = proj[..., CONV_DIM + V_A + H_A:]
    xh = jnp.concatenate([conv_hist.astype(qkv_pre.dtype), qkv_pre], axis=1)
    new_hist = xh[:, -(CONV_W - 1):]
    qkv = causal_conv_silu(xh, conv_w).astype(f32)
    q = l2norm(qkv[..., :QK_A].reshape(B, T, H_A, DK_A)) * (DK_A ** -0.5)
    k = l2norm(qkv[..., QK_A:2 * QK_A].reshape(B, T, H_A, DK_A))
    v = qkv[..., 2 * QK_A:].reshape(B, T, H_A, DV_A)
    g = -jnp.exp(A_log.astype(f32)) * jax.nn.softplus(a.astype(f32) + dt_bias.astype(f32))
    beta = jax.nn.sigmoid(bb.astype(f32))
    fn = gdn_chunked if chunked else gdn_recurrent
    o, S = fn(q, k, v, g, beta, S0.astype(f32))
    o = o * lax.rsqrt(jnp.mean(o * o, axis=-1, keepdims=True) + RMS_EPS) * norm_w.astype(f32)
    o = o * jax.nn.silu(z.astype(f32).reshape(B, T, H_A, DV_A))
    return o.reshape(B, T, V_A).astype(proj.dtype), new_hist, S.astype(proj.dtype)


def sgu_mixer(proj, ln_g, ln_b, w_s, b_s):
    B, T, _ = proj.shape
    u = jax.nn.gelu(proj[..., :D_B])
    v = layer_norm(jax.nn.gelu(proj[..., D_B:]), ln_g, ln_b)
    n = -(-T // CHUNK_B)
    pad = n * CHUNK_B - T
    vp = jnp.pad(v, ((0, 0), (0, pad), (0, 0))).reshape(B, n, CHUNK_B, G_B, CH_B)
    mask = jnp.tril(jnp.ones((CHUNK_B, CHUNK_B), dtype=bool))
    ws = jnp.where(mask, w_s, 0.0).astype(v.dtype)
    mixed = jnp.einsum('gij,bnjgc->bnigc', ws, vp) + b_s.T.astype(v.dtype)[None, None, :, :, None]
    mixed = mixed.reshape(B, n * CHUNK_B, D_B)[:, :T]
    return u * mixed, v


def mem_kv(mem, w_kv):
    B = mem.shape[0]
    kv = jnp.einsum('bmd,de->bme', mem, w_kv)
    return (kv[..., :D_X].reshape(B, N_MEM, H_X, DH_X),
            kv[..., D_X:].reshape(B, N_MEM, H_X, DH_X))


def cross_attend(xq, mk, mv):
    B, T, _ = xq.shape
    q = xq.reshape(B, T, H_X, DH_X)
    s = jnp.einsum('bthd,bmhd->bhtm', q, mk).astype(jnp.float32) * (DH_X ** -0.5)
    p = jax.nn.softmax(s, axis=-1).astype(mv.dtype)
    return jnp.einsum('bhtm,bmhd->bthd', p, mv).reshape(B, T, D_X)


def swiglu(x, w_gu, w_down):
    F = w_down.shape[-2]
    gu = x @ w_gu
    return (jax.nn.silu(gu[..., :F]) * gu[..., F:]) @ w_down


def moe_swiglu(x, w_router, w_gu, w_down):
    logits = jnp.einsum('btd,de->bte', x, w_router).astype(jnp.float32)
    top_val, top_idx = lax.top_k(logits, TOP_K)
    gates = jax.nn.softmax(top_val, axis=-1)
    combine = jnp.einsum('btk,btke->bte', gates,
                         jax.nn.one_hot(top_idx, N_EXP, dtype=jnp.float32)).astype(x.dtype)
    y = jnp.zeros_like(x)
    for e in range(N_EXP):
        y = y + combine[..., e:e + 1] * swiglu(x, w_gu[e], w_down[e])
    return y


def trunk(x, mem_k, mem_v, conv_hist, gdn_s0, chunked,
          a_w_in, a_conv_w, a_A_log, a_dt_bias, a_norm_w,
          b_w_in, b_ln_g, b_ln_b, b_w_s, b_b_s,
          w_out, ln1_g, ln1_b, ln2_g, ln2_b,
          ffn_w_gu, ffn_w_down, moe_router, moe_w_gu, moe_w_down):
    new_conv, new_s, sgu_v = [], [], []
    for i in range(DEPTH):
        j = i // 2
        if i % 2 == 0:
            proj = x @ a_w_in[j]
            mix, hist, S = gdn_mixer(proj[..., :IN_A - D_X], conv_hist[j], a_conv_w[j],
                                     a_A_log[j], a_dt_bias[j], a_norm_w[j], gdn_s0[j], chunked)
            new_conv.append(hist)
            new_s.append(S)
        else:
            proj = x @ b_w_in[j]
            mix, v = sgu_mixer(proj[..., :2 * D_B], b_ln_g[j], b_ln_b[j], b_w_s[j], b_b_s[j])
            sgu_v.append(v)
        xo = cross_attend(proj[..., -D_X:], mem_k[i], mem_v[i])
        h = jnp.concatenate([mix, xo], axis=-1) @ w_out[i]
        x = layer_norm(ALPHA * x + h, ln1_g[i], ln1_b[i])
        if i % 2 == 0:
            f = swiglu(x, ffn_w_gu[j], ffn_w_down[j])
        else:
            f = moe_swiglu(x, moe_router[j], moe_w_gu[j], moe_w_down[j])
        x = layer_norm(ALPHA * x + f, ln2_g[i], ln2_b[i])
    return x, jnp.stack(new_conv), jnp.stack(new_s), jnp.stack(sgu_v)


def setup_inputs(seed: int = 0) -> dict:
    key = jax.random.key(seed)
    ks = jax.random.split(key, 32)
    nrm = jax.random.normal
    f32 = jnp.float32
    dt = jnp.exp(jax.random.uniform(ks[9], (N_A, H_A)) * (math.log(0.1) - math.log(0.001)) + math.log(0.001))
    return {
        "x_prompt": nrm(ks[0], (BATCH, SEQ, D_MODEL), f32),
        "x_sample": nrm(ks[1], (DEC_BATCH, DEC_SEQ, D_MODEL), f32),
        "state_gdn": nrm(ks[2], (N_A, DEC_BATCH, H_A, DK_A, DV_A), f32) * DK_A ** -0.5,
        "state_conv": nrm(ks[3], (N_A, DEC_BATCH, CONV_W - 1, CONV_DIM), f32),
        "cache_mem_k": nrm(ks[4], (DEPTH, DEC_BATCH, N_MEM, H_X, DH_X), f32),
        "cache_mem_v": nrm(ks[5], (DEPTH, DEC_BATCH, N_MEM, H_X, DH_X), f32),
        "mem_prompt": nrm(ks[6], (BATCH, N_MEM, D_MODEL), f32),
        "a_w_in": nrm(ks[7], (N_A, D_MODEL, IN_A), f32) * D_MODEL ** -0.5,
        "a_conv_w": nrm(ks[8], (N_A, CONV_W, CONV_DIM), f32) * CONV_W ** -0.5,
        "a_A_log": jnp.log(jax.random.uniform(ks[10], (N_A, H_A), f32, 1.0, 16.0)),
        "a_dt_bias": dt + jnp.log(-jnp.expm1(-dt)),
        "a_norm_w": 1.0 + 0.02 * nrm(ks[11], (N_A, DV_A), f32),
        "b_w_in": nrm(ks[12], (N_B, D_MODEL, IN_B), f32) * D_MODEL ** -0.5,
        "b_ln_g": 1.0 + 0.02 * nrm(ks[13], (N_B, D_B), f32),
        "b_ln_b": 0.02 * nrm(ks[14], (N_B, D_B), f32),
        "b_w_s": nrm(ks[15], (N_B, G_B, CHUNK_B, CHUNK_B), f32) * CHUNK_B ** -0.5,
        "b_b_s": 1.0 + 0.02 * nrm(ks[16], (N_B, G_B, CHUNK_B), f32),
        "w_mem_kv": nrm(ks[17], (DEPTH, D_MODEL, 2 * D_X), f32) * D_MODEL ** -0.5,
        "w_out": nrm(ks[18], (DEPTH, D_MIX, D_MODEL), f32) * (D_MIX ** -0.5 * BETA_INIT),
        "ln1_g": 1.0 + 0.02 * nrm(ks[19], (DEPTH, D_MODEL), f32),
        "ln1_b": 0.02 * nrm(ks[20], (DEPTH, D_MODEL), f32),
        "ln2_g": 1.0 + 0.02 * nrm(ks[21], (DEPTH, D_MODEL), f32),
        "ln2_b": 0.02 * nrm(ks[22], (DEPTH, D_MODEL), f32),
        "ffn_w_gu": nrm(ks[23], (N_A, D_MODEL, 2 * D_FF), f32) * D_MODEL ** -0.5,
        "ffn_w_down": nrm(ks[24], (N_A, D_FF, D_MODEL), f32) * (D_FF ** -0.5 * BETA_INIT),
        "moe_router": nrm(ks[25], (N_B, D_MODEL, N_EXP), f32) * D_MODEL ** -0.5,
        "moe_w_gu": nrm(ks[26], (N_B, N_EXP, D_MODEL, 2 * D_FF), f32) * D_MODEL ** -0.5,
        "moe_w_down": nrm(ks[27], (N_B, N_EXP, D_FF, D_MODEL), f32) * (D_FF ** -0.5 * BETA_INIT),
    }


def reference(x_prompt, x_sample, state_gdn, state_conv, cache_mem_k, cache_mem_v, mem_prompt,
              a_w_in, a_conv_w, a_A_log, a_dt_bias, a_norm_w,
              b_w_in, b_ln_g, b_ln_b, b_w_s, b_b_s,
              w_mem_kv, w_out, ln1_g, ln1_b, ln2_g, ln2_b,
              ffn_w_gu, ffn_w_down, moe_router, moe_w_gu, moe_w_down):
    pk, pv = [], []
    for i in range(DEPTH):
        k_i, v_i = mem_kv(mem_prompt, w_mem_kv[i])
        pk.append(k_i)
        pv.append(v_i)
    p_mem_k = jnp.stack(pk)
    p_mem_v = jnp.stack(pv)
    p_hist0 = jnp.zeros((N_A, BATCH, CONV_W - 1, CONV_DIM), x_prompt.dtype)
    p_s0 = jnp.zeros((N_A, BATCH, H_A, DK_A, DV_A), jnp.float32)
    y_prompt, p_conv, p_gdn, _ = trunk(
        x_prompt, p_mem_k, p_mem_v, p_hist0, p_s0, True,
        a_w_in, a_conv_w, a_A_log, a_dt_bias, a_norm_w,
        b_w_in, b_ln_g, b_ln_b, b_w_s, b_b_s,
        w_out, ln1_g, ln1_b, ln2_g, ln2_b,
        ffn_w_gu, ffn_w_down, moe_router, moe_w_gu, moe_w_down)
    y_sample, s_conv, s_gdn, s_sgu_v = trunk(
        x_sample, cache_mem_k, cache_mem_v, state_conv, state_gdn, False,
        a_w_in, a_conv_w, a_A_log, a_dt_bias, a_norm_w,
        b_w_in, b_ln_g, b_ln_b, b_w_s, b_b_s,
        w_out, ln1_g, ln1_b, ln2_g, ln2_b,
        ffn_w_gu, ffn_w_down, moe_router, moe_w_gu, moe_w_down)
    return (y_prompt, y_sample, p_gdn, p_conv, p_mem_k, p_mem_v, s_gdn, s_conv, s_sgu_v)
```

```python
import functools

import jax
import jax.numpy as jnp
from jax import lax
from jax.experimental import pallas as pl
from jax.experimental.pallas import tpu as pltpu

F32 = jnp.float32
BF16 = jnp.bfloat16

D_MODEL = 1024
DEPTH = 4
H_A = 6
DK_A = 128
DV_A = 128
QK_A = H_A * DK_A
V_A = H_A * DV_A
CONV_W = 4
CONV_DIM = 2 * QK_A + V_A
GDN_CHUNK = 64
G_B = 6
CH_B = 128
D_B = G_B * CH_B
CHUNK_B = 128
N_MEM = 256
H_X = 4
DH_X = 64
D_X = H_X * DH_X
D_FF = 3584
N_EXP = 8
ALPHA = (2 * DEPTH) ** 0.25
LN_EPS = 1e-5
RMS_EPS = 1e-6

LANES = 128
SUBLANES = 8
VMEM_LIMIT_BYTES = 48 * 1024 * 1024

A_Z_OFF = CONV_DIM
A_XQ_OFF = CONV_DIM + V_A
A_AB_OFF = A_XQ_OFF + D_X
A_COLS = A_AB_OFF + LANES
B_LANE = 8
B_XQ_OFF = 2 * D_B
B_COLS = 2 * D_B + D_X


def _params(*sem):
    return pltpu.CompilerParams(dimension_semantics=sem, vmem_limit_bytes=VMEM_LIMIT_BYTES)


def _split2(x):
    hi = x.astype(BF16)
    lo = (x - hi.astype(F32)).astype(BF16)
    return hi, lo


_NN = (((1,), (0,)), ((), ()))
_NT = (((1,), (1,)), ((), ()))
_TN = (((0,), (0,)), ((), ()))


def _dot(a, b, dims=_NN, passes=1):
    if passes == 1:
        return lax.dot_general(a.astype(BF16), b.astype(BF16), dims, preferred_element_type=F32)
    ah, al = _split2(a)
    bh, bl = _split2(b)
    dg = functools.partial(lax.dot_general, dimension_numbers=dims, preferred_element_type=F32)
    return dg(ah, bh) + (dg(ah, bl) + dg(al, bh))


def _dot_exact_lhs(lhs_bf16, x):
    x1 = x.astype(BF16)
    r1 = x - x1.astype(F32)
    x2 = r1.astype(BF16)
    x3 = (r1 - x2.astype(F32)).astype(BF16)
    dg = functools.partial(lax.dot_general, dimension_numbers=_NN, preferred_element_type=F32)
    return dg(lhs_bf16, x1) + (dg(lhs_bf16, x2) + dg(lhs_bf16, x3))


def _sigmoid(x):
    return 1.0 / (1.0 + jnp.exp(-x))


def _silu(x):
    return x * _sigmoid(x)


def _softplus(x):
    return jnp.maximum(x, 0.0) + jnp.log1p(jnp.exp(-jnp.abs(x)))


def _gelu_tanh(x):
    c = 0.7978845608028654
    return 0.5 * x * (1.0 + jnp.tanh(c * (x + 0.044715 * (x * x * x))))


def _layer_norm(y, g, b):
    mu = jnp.mean(y, axis=-1, keepdims=True)
    d = y - mu
    var = jnp.mean(d * d, axis=-1, keepdims=True)
    return d * lax.rsqrt(var + LN_EPS) * g + b


def _lane_pick(x, lane_iota, idx):
    return jnp.sum(jnp.where(lane_iota == idx, x, 0.0), axis=-1, keepdims=True)


def _mm_kernel(x_ref, w_ref, o_ref):
    o_ref[...] = jnp.dot(x_ref[...].astype(BF16), w_ref[...],
                         preferred_element_type=F32).astype(o_ref.dtype)


def _matmul(x, w, tm, tn, name):
    m, k = x.shape
    n = w.shape[1]
    return pl.pallas_call(
        _mm_kernel,
        grid=(m // tm, n // tn),
        in_specs=[pl.BlockSpec((tm, k), lambda i, j: (i, 0)),
                  pl.BlockSpec((k, tn), lambda i, j: (0, j))],
        out_specs=pl.BlockSpec((tm, tn), lambda i, j: (i, j)),
        out_shape=jax.ShapeDtypeStruct((m, n), F32),
        compiler_params=_params("parallel", "parallel"),
        name=name,
    )(x, w)


def _outproj_ln_kernel(mix_ref, xo_ref, x_ref, wm_ref, wx_ref, g_ref, b_ref, o_ref):
    h = jnp.dot(mix_ref[...].astype(BF16), wm_ref[...], preferred_element_type=F32)
    h = h + jnp.dot(xo_ref[...].astype(BF16), wx_ref[...], preferred_element_type=F32)
    o_ref[...] = _layer_norm(ALPHA * x_ref[...] + h, g_ref[...], b_ref[...])


def _outproj_ln(mix, xo, x, w_out, g, b, tm):
    m = x.shape[0]
    d_mix = mix.shape[1]
    n_blk = d_mix // D_X
    return pl.pallas_call(
        _outproj_ln_kernel,
        grid=(m // tm,),
        in_specs=[pl.BlockSpec((tm, d_mix), lambda i: (i, 0)),
                  pl.BlockSpec((tm, D_X), lambda i: (i, 0)),
                  pl.BlockSpec((tm, D_MODEL), lambda i: (i, 0)),
                  pl.BlockSpec((d_mix, D_MODEL), lambda i: (0, 0)),
                  pl.BlockSpec((D_X, D_MODEL), lambda i: (n_blk, 0)),
                  pl.BlockSpec((1, D_MODEL), lambda i: (0, 0)),
                  pl.BlockSpec((1, D_MODEL), lambda i: (0, 0))],
        out_specs=pl.BlockSpec((tm, D_MODEL), lambda i: (i, 0)),
        out_shape=jax.ShapeDtypeStruct((m, D_MODEL), F32),
        compiler_params=_params("parallel"),
        name="outproj_ln",
    )(mix, xo, x, w_out, w_out, g, b)


def _ffn_kernel(*refs, moe):
    if moe:
        x_ref, comb_ref, wg_ref, wu_ref, wd_ref, g_ref, b_ref, o_ref, xb_ref, acc_ref = refs
    else:
        x_ref, wg_ref, wu_ref, wd_ref, g_ref, b_ref, o_ref, xb_ref, acc_ref = refs
    e = pl.program_id(1)
    f = pl.program_id(2)
    first = jnp.logical_and(e == 0, f == 0)
    last = jnp.logical_and(e == pl.num_programs(1) - 1, f == pl.num_programs(2) - 1)

    @pl.when(first)
    def _():
        xb_ref[...] = x_ref[...].astype(BF16)
        acc_ref[...] = jnp.zeros_like(acc_ref)

    xb = xb_ref[...]
    hg = jnp.dot(xb, wg_ref[...], preferred_element_type=F32)
    hu = jnp.dot(xb, wu_ref[...], preferred_element_type=F32)
    h = _silu(hg) * hu
    if moe:
        comb = comb_ref[...]
        lane = lax.broadcasted_iota(jnp.int32, comb.shape, 1)
        h = h * _lane_pick(comb, lane, e)
    acc_ref[...] += jnp.dot(h.astype(BF16), wd_ref[...], preferred_element_type=F32)

    @pl.when(last)
    def _():
        o_ref[...] = _layer_norm(ALPHA * x_ref[...] + acc_ref[...], g_ref[...], b_ref[...])


def _ffn_ln(x, comb, w_gu, w_down, g, b, tm, tf):
    m = x.shape[0]
    moe = comb is not None
    n_e = w_gu.shape[0]
    n_f = D_FF // tf
    in_specs = [pl.BlockSpec((tm, D_MODEL), lambda i, e, f: (i, 0))]
    args = [x]
    if moe:
        in_specs.append(pl.BlockSpec((tm, LANES), lambda i, e, f: (i, 0)))
        args.append(comb)
    in_specs += [pl.BlockSpec((None, D_MODEL, tf), lambda i, e, f: (e, 0, f)),
                 pl.BlockSpec((None, D_MODEL, tf), lambda i, e, f: (e, 0, n_f + f)),
                 pl.BlockSpec((None, tf, D_MODEL), lambda i, e, f: (e, f, 0)),
                 pl.BlockSpec((1, D_MODEL), lambda i, e, f: (0, 0)),
                 pl.BlockSpec((1, D_MODEL), lambda i, e, f: (0, 0))]
    args += [w_gu, w_gu, w_down, g, b]
    return pl.pallas_call(
        functools.partial(_ffn_kernel, moe=moe),
        grid=(m // tm, n_e, n_f),
        in_specs=in_specs,
        out_specs=pl.BlockSpec((tm, D_MODEL), lambda i, e, f: (i, 0)),
        out_shape=jax.ShapeDtypeStruct((m, D_MODEL), F32),
        scratch_shapes=[pltpu.VMEM((tm, D_MODEL), BF16), pltpu.VMEM((tm, D_MODEL), F32)],
        compiler_params=_params("parallel", "arbitrary", "arbitrary"),
        name="moe_ln" if moe else "ffn_ln",
    )(*args)


def _router_kernel(x_ref, wh_ref, wl_ref, comb_ref):
    x = x_ref[...]
    xh, xl = _split2(x)
    wh = wh_ref[...]
    lg = jnp.dot(xh, wh, preferred_element_type=F32)
    lg = lg + (jnp.dot(xh, wl_ref[...], preferred_element_type=F32)
               + jnp.dot(xl, wh, preferred_element_type=F32))
    lane = lax.broadcasted_iota(jnp.int32, lg.shape, 1)
    neg = -jnp.inf
    lg = jnp.where(lane < N_EXP, lg, neg)
    m1 = jnp.max(lg, axis=-1, keepdims=True)
    i1 = jnp.min(jnp.where(lg == m1, lane, LANES), axis=-1, keepdims=True)
    lg2 = jnp.where(lane == i1, neg, lg)
    m2 = jnp.max(lg2, axis=-1, keepdims=True)
    i2 = jnp.min(jnp.where(lg2 == m2, lane, LANES), axis=-1, keepdims=True)
    e2 = jnp.exp(m2 - m1)
    den = 1.0 + e2
    comb_ref[...] = jnp.where(lane == i1, 1.0 / den, 0.0) + jnp.where(lane == i2, e2 / den, 0.0)


def _router(x, w_hi, w_lo, tm):
    m = x.shape[0]
    return pl.pallas_call(
        _router_kernel,
        grid=(m // tm,),
        in_specs=[pl.BlockSpec((tm, D_MODEL), lambda i: (i, 0)),
                  pl.BlockSpec((D_MODEL, LANES), lambda i: (0, 0)),
                  pl.BlockSpec((D_MODEL, LANES), lambda i: (0, 0))],
        out_specs=pl.BlockSpec((tm, LANES), lambda i: (i, 0)),
        out_shape=jax.ShapeDtypeStruct((m, LANES), F32),
        compiler_params=_params("parallel"),
        name="router",
    )(x, w_hi, w_lo)


GDN_PASSES_QK = 1
GDN_PASSES_SOLVE = 3
GDN_PASSES_STATE = 1
INV_BLOCK = 16


def _gdn_chunk_kernel(qkv_ref, z_ref, ab_ref, cw_ref, alog_ref, dtb_ref, nw_ref, hist0_ref, s0_ref,
                      mix_ref, sout_ref, hout_ref, xh_ref, s_ref):
    n = pl.program_id(1)
    c = GDN_CHUNK
    hrows = SUBLANES

    @pl.when(n == 0)
    def _():
        xh_ref[0:hrows, :] = hist0_ref[...]
        s_ref[...] = s0_ref[...]

    xh_ref[hrows:hrows + c, :] = qkv_ref[...]

    row = lax.broadcasted_iota(jnp.int32, (c, c), 0)
    col = lax.broadcasted_iota(jnp.int32, (c, c), 1)
    causal = row >= col
    strict = row > col
    eye = jnp.where(row == col, 1.0, 0.0)
    blockdiag = (row // INV_BLOCK) == (col // INV_BLOCK)
    tril_ones = jnp.where(causal, 1.0, 0.0).astype(BF16)
    lane = lax.broadcasted_iota(jnp.int32, (c, LANES), 1)

    ab = ab_ref[...]
    g_all = -jnp.exp(alog_ref[...]) * _softplus(ab + dtb_ref[...])
    beta_all = _sigmoid(ab)
    gcum_all = _dot_exact_lhs(tril_ones, g_all)

    def conv_silu(c0):
        acc = xh_ref[hrows - 3:hrows - 3 + c, c0:c0 + LANES] * cw_ref[0:1, c0:c0 + LANES]
        for j in range(1, CONV_W):
            acc = acc + (xh_ref[hrows - 3 + j:hrows - 3 + j + c, c0:c0 + LANES]
                         * cw_ref[j:j + 1, c0:c0 + LANES])
        return _silu(acc)

    mm_s = functools.partial(_dot, passes=GDN_PASSES_SOLVE)

    for h in range(H_A):
        q = conv_silu(h * DK_A)
        k = conv_silu(QK_A + h * DK_A)
        v = conv_silu(2 * QK_A + h * DV_A)
        q = q * lax.rsqrt(jnp.sum(q * q, axis=-1, keepdims=True) + RMS_EPS) * (DK_A ** -0.5)
        k = k * lax.rsqrt(jnp.sum(k * k, axis=-1, keepdims=True) + RMS_EPS)
        g_col = _lane_pick(g_all, lane, h)
        gc_col = _lane_pick(gcum_all, lane, h)
        beta_col = _lane_pick(beta_all, lane, B_LANE + h)
        gc_last = gc_col[c - 1:c, :]

        dlog = _dot_exact_lhs(tril_ones, jnp.where(strict, g_col, 0.0))
        decay = jnp.where(causal, jnp.exp(dlog), 0.0)
        egc = jnp.exp(gc_col)
        kb = k * beta_col
        kk = _dot(jnp.concatenate([kb, q], axis=0), k, _NT, GDN_PASSES_QK)
        a_mat = jnp.where(strict, kk[:c] * decay, 0.0)
        attn = kk[c:] * decay

        a_d = jnp.where(blockdiag, a_mat, 0.0)
        a_n = a_mat - a_d
        p2 = mm_s(a_d, a_d)
        p4 = mm_s(p2, p2)
        p8 = mm_s(p4, p4)
        d_inv = eye - a_d
        d_inv = d_inv + mm_s(d_inv, p2)
        d_inv = d_inv + mm_s(d_inv, p4)
        d_inv = d_inv + mm_s(d_inv, p8)
        m_blk = mm_s(d_inv, a_n)
        m2 = mm_s(m_blk, m_blk)
        y = eye - m_blk
        y = y + mm_s(y, m2)
        t_inv = mm_s(y, d_inv)

        rhs = jnp.concatenate([v * beta_col, kb * egc], axis=1)
        sol = mm_s(t_inv, rhs)
        u_val = sol[:, :DV_A]
        w_cum = sol[:, DV_A:]

        s_h = s_ref[h]
        wq = _dot(jnp.concatenate([w_cum, q * egc], axis=0), s_h, _NN, GDN_PASSES_STATE)
        v_new = u_val - wq[:c]
        o = wq[c:] + _dot(attn, v_new, _NN, GDN_PASSES_STATE)
        k_dec = k * jnp.exp(gc_last - gc_col)
        s_ref[h] = s_h * jnp.exp(gc_last) + _dot(k_dec, v_new, _TN, GDN_PASSES_STATE)

        o = o * lax.rsqrt(jnp.mean(o * o, axis=-1, keepdims=True) + RMS_EPS) * nw_ref[...]
        mix_ref[:, h * DV_A:(h + 1) * DV_A] = o * _silu(z_ref[:, h * DV_A:(h + 1) * DV_A])

    xh_ref[0:hrows, :] = xh_ref[c:c + hrows, :]

    @pl.when(n == pl.num_programs(1) - 1)
    def _():
        sout_ref[...] = s_ref[...]
        hout_ref[...] = xh_ref[c:c + hrows, :]


def _gdn_chunked(proj, conv_w, alog_row, dtb_row, nw_row, hist0, s0):
    bsz, t, _ = proj.shape
    c = GDN_CHUNK
    n_chunks = t // c
    row_spec = lambda width, blk: pl.BlockSpec((None, c, width), lambda b, n: (b, n, blk))
    full2 = lambda shape: pl.BlockSpec(shape, lambda b, n: (0, 0))
    return pl.pallas_call(
        _gdn_chunk_kernel,
        grid=(bsz, n_chunks),
        in_specs=[row_spec(CONV_DIM, 0),
                  row_spec(V_A, A_Z_OFF // V_A),
                  row_spec(LANES, A_AB_OFF // LANES),
                  full2((CONV_W, CONV_DIM)),
                  full2((1, LANES)), full2((1, LANES)), full2((1, DV_A)),
                  pl.BlockSpec((None, SUBLANES, CONV_DIM), lambda b, n: (b, 0, 0)),
                  pl.BlockSpec((None, H_A, DK_A, DV_A), lambda b, n: (b, 0, 0, 0))],
        out_specs=[pl.BlockSpec((None, c, V_A), lambda b, n: (b, n, 0)),
                   pl.BlockSpec((None, H_A, DK_A, DV_A), lambda b, n: (b, 0, 0, 0)),
                   pl.BlockSpec((None, SUBLANES, CONV_DIM), lambda b, n: (b, 0, 0))],
        out_shape=[jax.ShapeDtypeStruct((bsz, t, V_A), F32),
                   jax.ShapeDtypeStruct((bsz, H_A, DK_A, DV_A), F32),
                   jax.ShapeDtypeStruct((bsz, SUBLANES, CONV_DIM), F32)],
        scratch_shapes=[pltpu.VMEM((SUBLANES + c, CONV_DIM), F32),
                        pltpu.VMEM((H_A, DK_A, DV_A), F32)],
        compiler_params=_params("parallel", "arbitrary"),
        name="gdn_chunked",
    )(proj, proj, proj, conv_w, alog_row, dtb_row, nw_row, hist0, s0)


def _gdn_step_kernel(qkv_ref, z_ref, ab_ref, cw_ref, alog_ref, dtb_ref, nw_ref, hist_ref, s_ref,
                     mix_ref, sout_ref, hout_ref, o_scr):
    rows = qkv_ref.shape[0]
    new = qkv_ref[...]
    y = hist_ref[0] * cw_ref[0:1, :]
    for j in range(1, CONV_W - 1):
        y = y + hist_ref[j] * cw_ref[j:j + 1, :]
    y = y + new * cw_ref[CONV_W - 1:CONV_W, :]
    qkv = _silu(y)
    for j in range(CONV_W - 2):
        hout_ref[j] = hist_ref[j + 1]
    hout_ref[CONV_W - 2] = new

    ab = ab_ref[...]
    lane = lax.broadcasted_iota(jnp.int32, ab.shape, 1)
    g_all = -jnp.exp(alog_ref[...]) * _softplus(ab + dtb_ref[...])
    beta_all = _sigmoid(ab)
    ri = lax.broadcasted_iota(jnp.int32, (DK_A, DK_A), 0)
    ci = lax.broadcasted_iota(jnp.int32, (DK_A, DK_A), 1)
    diag = ri == ci

    def as_column(r):
        return jnp.sum(jnp.where(diag, r, 0.0), axis=-1, keepdims=True)

    for h in range(H_A):
        q = qkv[:, h * DK_A:(h + 1) * DK_A]
        k = qkv[:, QK_A + h * DK_A:QK_A + (h + 1) * DK_A]
        v = qkv[:, 2 * QK_A + h * DV_A:2 * QK_A + (h + 1) * DV_A]
        q = q * lax.rsqrt(jnp.sum(q * q, axis=-1, keepdims=True) + RMS_EPS) * (DK_A ** -0.5)
        k = k * lax.rsqrt(jnp.sum(k * k, axis=-1, keepdims=True) + RMS_EPS)
        eg = jnp.exp(_lane_pick(g_all, lane, h))
        beta = _lane_pick(beta_all, lane, B_LANE + h)
        for s in range(rows):
            k_col = as_column(k[s:s + 1, :])
            q_col = as_column(q[s:s + 1, :])
            st = s_ref[s, h] * eg[s:s + 1, :]
            ks = jnp.sum(k_col * st, axis=0, keepdims=True)
            u = beta[s:s + 1, :] * (v[s:s + 1, :] - ks)
            st = st + k_col * u
            sout_ref[s, h] = st
            o_scr[s:s + 1, h * DV_A:(h + 1) * DV_A] = jnp.sum(q_col * st, axis=0, keepdims=True)

    for h in range(H_A):
        o = o_scr[:, h * DV_A:(h + 1) * DV_A]
        o = o * lax.rsqrt(jnp.mean(o * o, axis=-1, keepdims=True) + RMS_EPS) * nw_ref[...]
        mix_ref[:, h * DV_A:(h + 1) * DV_A] = o * _silu(z_ref[:, h * DV_A:(h + 1) * DV_A])


def _gdn_step(proj, conv_w, alog_row, dtb_row, nw_row, hist_t, s0):
    bsz = proj.shape[0]
    rows = SUBLANES
    full1 = lambda shape: pl.BlockSpec(shape, lambda i: (0, 0))
    return pl.pallas_call(
        _gdn_step_kernel,
        grid=(bsz // rows,),
        in_specs=[pl.BlockSpec((rows, CONV_DIM), lambda i: (i, 0)),
                  pl.BlockSpec((rows, V_A), lambda i: (i, A_Z_OFF // V_A)),
                  pl.BlockSpec((rows, LANES), lambda i: (i, A_AB_OFF // LANES)),
                  full1((CONV_W, CONV_DIM)),
                  full1((1, LANES)), full1((1, LANES)), full1((1, DV_A)),
                  pl.BlockSpec((CONV_W - 1, rows, CONV_DIM), lambda i: (0, i, 0)),
                  pl.BlockSpec((rows, H_A, DK_A, DV_A), lambda i: (i, 0, 0, 0))],
        out_specs=[pl.BlockSpec((rows, V_A), lambda i: (i, 0)),
                   pl.BlockSpec((rows, H_A, DK_A, DV_A), lambda i: (i, 0, 0, 0)),
                   pl.BlockSpec((CONV_W - 1, rows, CONV_DIM), lambda i: (0, i, 0))],
        out_shape=[jax.ShapeDtypeStruct((bsz, V_A), F32),
                   jax.ShapeDtypeStruct((bsz, H_A, DK_A, DV_A), F32),
                   jax.ShapeDtypeStruct((CONV_W - 1, bsz, CONV_DIM), F32)],
        scratch_shapes=[pltpu.VMEM((rows, V_A), F32)],
        compiler_params=_params("parallel"),
        name="gdn_step",
    )(proj, proj, proj, conv_w, alog_row, dtb_row, nw_row, hist_t, s0)


def _sgu_chunk_kernel(u_ref, v_ref, g_ref, b_ref, ws_ref, bs_ref, mix_ref):
    c = CHUNK_B
    row = lax.broadcasted_iota(jnp.int32, (c, c), 0)
    col = lax.broadcasted_iota(jnp.int32, (c, c), 1)
    u = _gelu_tanh(u_ref[...])
    v = _layer_norm(_gelu_tanh(v_ref[...]), g_ref[...], b_ref[...])
    for gi in range(G_B):
        ws = jnp.where(row >= col, ws_ref[gi], 0.0)
        mixed = _dot(ws, v[:, gi * CH_B:(gi + 1) * CH_B]) + bs_ref[gi]
        mix_ref[:, gi * CH_B:(gi + 1) * CH_B] = u[:, gi * CH_B:(gi + 1) * CH_B] * mixed


def _sgu_chunked(proj, ln_g, ln_b, w_s, bs_b):
    bsz, t, _ = proj.shape
    c = CHUNK_B
    return pl.pallas_call(
        _sgu_chunk_kernel,
        grid=(bsz, t // c),
        in_specs=[pl.BlockSpec((None, c, D_B), lambda b, n: (b, n, 0)),
                  pl.BlockSpec((None, c, D_B), lambda b, n: (b, n, 1)),
                  pl.BlockSpec((1, D_B), lambda b, n: (0, 0)),
                  pl.BlockSpec((1, D_B), lambda b, n: (0, 0)),
                  pl.BlockSpec((G_B, c, c), lambda b, n: (0, 0, 0)),
                  pl.BlockSpec((G_B, c, CH_B), lambda b, n: (0, 0, 0))],
        out_specs=pl.BlockSpec((None, c, D_B), lambda b, n: (b, n, 0)),
        out_shape=jax.ShapeDtypeStruct((bsz, t, D_B), F32),
        compiler_params=_params("parallel", "parallel"),
        name="sgu_chunked",
    )(proj, proj, ln_g, ln_b, w_s, bs_b)


def _sgu_first_kernel(u_ref, v_ref, g_ref, b_ref, w00_ref, b0_ref, mix_ref, vout_ref):
    u = _gelu_tanh(u_ref[...])
    v = _layer_norm(_gelu_tanh(v_ref[...]), g_ref[...], b_ref[...])
    vout_ref[...] = v
    mix_ref[...] = u * (w00_ref[...] * v + b0_ref[...])


def _sgu_first(proj, ln_g, ln_b, w00_row, b0_row):
    bsz = proj.shape[0]
    row = lambda: pl.BlockSpec((1, D_B), lambda i: (0, 0))
    return pl.pallas_call(
        _sgu_first_kernel,
        grid=(1,),
        in_specs=[pl.BlockSpec((bsz, D_B), lambda i: (0, 0)),
                  pl.BlockSpec((bsz, D_B), lambda i: (0, 1)),
                  row(), row(), row(), row()],
        out_specs=[pl.BlockSpec((bsz, D_B), lambda i: (0, 0)),
                   pl.BlockSpec((bsz, D_B), lambda i: (0, 0))],
        out_shape=[jax.ShapeDtypeStruct((bsz, D_B), F32),
                   jax.ShapeDtypeStruct((bsz, D_B), F32)],
        compiler_params=_params("arbitrary"),
        name="sgu_first",
    )(proj, proj, ln_g, ln_b, w00_row, b0_row)


def _head_masks(shape):
    lane = lax.broadcasted_iota(jnp.int32, shape, len(shape) - 1)
    return [(lane // DH_X) == h for h in range(H_X)]


def _xattn_kernel(q_ref, k_ref, v_ref, o_ref):
    q = q_ref[...]
    kb = k_ref[...].astype(BF16)
    vb = v_ref[...].astype(BF16)
    masks = _head_masks(q.shape)
    out = jnp.zeros(q.shape, F32)
    for h in range(H_X):
        qh = jnp.where(masks[h], q, 0.0)
        s = _dot(qh, kb, _NT) * (DH_X ** -0.5)
        s = s - jnp.max(s, axis=-1, keepdims=True)
        p = jnp.exp(s)
        p = p / jnp.sum(p, axis=-1, keepdims=True)
        out = out + jnp.where(masks[h], _dot(p, vb), 0.0)
    o_ref[...] = out


def _xattn(proj, xq_blk, mem_k, mem_v, tq):
    bsz, t, _ = proj.shape
    return pl.pallas_call(
        _xattn_kernel,
        grid=(bsz, t // tq),
        in_specs=[pl.BlockSpec((None, tq, D_X), lambda b, i: (b, i, xq_blk)),
                  pl.BlockSpec((None, N_MEM, D_X), lambda b, i: (b, 0, 0)),
                  pl.BlockSpec((None, N_MEM, D_X), lambda b, i: (b, 0, 0))],
        out_specs=pl.BlockSpec((None, tq, D_X), lambda b, i: (b, i, 0)),
        out_shape=jax.ShapeDtypeStruct((bsz, t, D_X), F32),
        compiler_params=_params("parallel", "parallel"),
        name="xattn",
    )(proj, mem_k, mem_v)


def _xattn_step_kernel(q_ref, k_ref, v_ref, o_ref):
    rows = q_ref.shape[0]
    q = q_ref[...]
    hrow = lax.broadcasted_iota(jnp.int32, (SUBLANES, D_X), 0)
    hlane = lax.broadcasted_iota(jnp.int32, (SUBLANES, D_X), 1) // DH_X
    sel = hrow == hlane
    for s in range(rows):
        qh = jnp.where(sel, q[s:s + 1, :], 0.0)
        sc = _dot(qh, k_ref[s], _NT) * (DH_X ** -0.5)
        sc = sc - jnp.max(sc, axis=-1, keepdims=True)
        p = jnp.exp(sc)
        p = p / jnp.sum(p, axis=-1, keepdims=True)
        pv = _dot(p, v_ref[s])
        o_ref[s:s + 1, :] = jnp.sum(jnp.where(sel, pv, 0.0), axis=0, keepdims=True)


def _xattn_step(proj, xq_blk, mem_k, mem_v):
    bsz = proj.shape[0]
    rows = SUBLANES
    return pl.pallas_call(
        _xattn_step_kernel,
        grid=(bsz // rows,),
        in_specs=[pl.BlockSpec((rows, D_X), lambda i: (i, xq_blk)),
                  pl.BlockSpec((rows, N_MEM, D_X), lambda i: (i, 0, 0)),
                  pl.BlockSpec((rows, N_MEM, D_X), lambda i: (i, 0, 0))],
        out_specs=pl.BlockSpec((rows, D_X), lambda i: (i, 0)),
        out_shape=jax.ShapeDtypeStruct((bsz, D_X), F32),
        compiler_params=_params("parallel"),
        name="xattn_step",
    )(proj, mem_k, mem_v)


def _pad_lanes(v):
    return jnp.zeros((1, LANES), F32).at[0, :v.shape[0]].set(v.astype(F32))


def _prep_weights(a_w_in, a_A_log, a_dt_bias, b_w_s, b_b_s, moe_router):
    qkvz = a_w_in[:, :, :A_XQ_OFF]
    a_cols = a_w_in[:, :, A_XQ_OFF:A_XQ_OFF + H_A]
    b_cols = a_w_in[:, :, A_XQ_OFF + H_A:A_XQ_OFF + 2 * H_A]
    xq = a_w_in[:, :, A_XQ_OFF + 2 * H_A:]
    ab = jnp.zeros(a_w_in.shape[:2] + (LANES,), F32)
    ab = ab.at[:, :, :H_A].set(a_cols).at[:, :, B_LANE:B_LANE + H_A].set(b_cols)
    a_w = jnp.concatenate([qkvz, xq, ab], axis=-1).astype(BF16)
    alog_rows = [_pad_lanes(a_A_log[j]) for j in range(a_A_log.shape[0])]
    dtb_rows = [_pad_lanes(a_dt_bias[j]) for j in range(a_dt_bias.shape[0])]
    bs_b = jnp.broadcast_to(b_b_s[..., None], b_b_s.shape + (CH_B,)).astype(F32)
    w00_rows = jnp.repeat(b_w_s[:, :, 0, 0], CH_B, axis=-1)[:, None, :]
    b0_rows = jnp.repeat(b_b_s[:, :, 0], CH_B, axis=-1)[:, None, :]
    r = jnp.zeros(moe_router.shape[:2] + (LANES,), F32).at[:, :, :N_EXP].set(moe_router)
    r_hi = r.astype(BF16)
    r_lo = (r - r_hi.astype(F32)).astype(BF16)
    return a_w, alog_rows, dtb_rows, bs_b, w00_rows, b0_rows, r_hi, r_lo


def kernel(x_prompt, x_sample, state_gdn, state_conv, cache_mem_k, cache_mem_v, mem_prompt,
           a_w_in, a_conv_w, a_A_log, a_dt_bias, a_norm_w,
           b_w_in, b_ln_g, b_ln_b, b_w_s, b_b_s,
           w_mem_kv, w_out, ln1_g, ln1_b, ln2_g, ln2_b,
           ffn_w_gu, ffn_w_down, moe_router, moe_w_gu, moe_w_down):
    n_p, t_p, _ = x_prompt.shape
    n_s = x_sample.shape[0]
    n_a = a_w_in.shape[0]

    a_w, alog_rows, dtb_rows, bs_b, w00_rows, b0_rows, r_hi, r_lo = _prep_weights(
        a_w_in, a_A_log, a_dt_bias, b_w_s, b_b_s, moe_router)
    b_w = b_w_in.astype(BF16)
    w_out_b = w_out.astype(BF16)
    ffn_gu = ffn_w_gu.astype(BF16)
    ffn_dn = ffn_w_down.astype(BF16)
    moe_gu = moe_w_gu.astype(BF16)
    moe_dn = moe_w_down.astype(BF16)
    row = lambda v: v.reshape(1, -1).astype(F32)

    w_kv = jnp.transpose(w_mem_kv, (1, 0, 2)).reshape(D_MODEL, DEPTH * 2 * D_X).astype(BF16)
    kv = _matmul(mem_prompt.reshape(n_p * N_MEM, D_MODEL), w_kv, 1024, 1024, "mem_kv")
    kv = kv.reshape(n_p, N_MEM, DEPTH, 2, D_X)
    p_mem_k = jnp.transpose(kv[:, :, :, 0, :], (2, 0, 1, 3))
    p_mem_v = jnp.transpose(kv[:, :, :, 1, :], (2, 0, 1, 3))
    s_mem_k = cache_mem_k.reshape(DEPTH, n_s, N_MEM, D_X)
    s_mem_v = cache_mem_v.reshape(DEPTH, n_s, N_MEM, D_X)

    xp = x_prompt.reshape(n_p * t_p, D_MODEL)
    xs = x_sample.reshape(n_s, D_MODEL)
    p_hist0 = jnp.zeros((n_p, SUBLANES, CONV_DIM), F32)
    p_s0 = jnp.zeros((n_p, H_A, DK_A, DV_A), F32)
    s_hist_t = jnp.transpose(state_conv, (0, 2, 1, 3))

    p_gdn, p_conv, s_gdn, s_conv, s_sgu_v = [], [], [], [], []
    tm_p, tm_s = 1024, n_s
    for i in range(DEPTH):
        j = i // 2
        if i % 2 == 0:
            proj_p = _matmul(xp, a_w[j], tm_p, 1152, "in_proj_a")
            proj_s = _matmul(xs, a_w[j], tm_s, 1152, "in_proj_a_s")
            nw = row(a_norm_w[j])
            mix_p, sp, hp = _gdn_chunked(proj_p.reshape(n_p, t_p, A_COLS), a_conv_w[j],
                                         alog_rows[j], dtb_rows[j], nw, p_hist0, p_s0)
            mix_p = mix_p.reshape(n_p * t_p, V_A)
            p_gdn.append(sp)
            p_conv.append(hp[:, SUBLANES - (CONV_W - 1):, :])
            mix_s, ss, hs = _gdn_step(proj_s, a_conv_w[j], alog_rows[j], dtb_rows[j], nw,
                                      s_hist_t[j], state_gdn[j])
            s_gdn.append(ss)
            s_conv.append(jnp.transpose(hs, (1, 0, 2)))
            cols, xq_blk = A_COLS, A_XQ_OFF // D_X
        else:
            proj_p = _matmul(xp, b_w[j], tm_p, 896, "in_proj_b")
            proj_s = _matmul(xs, b_w[j], tm_s, 896, "in_proj_b_s")
            mix_p = _sgu_chunked(proj_p.reshape(n_p, t_p, B_COLS), row(b_ln_g[j]), row(b_ln_b[j]),
                                 b_w_s[j], bs_b[j]).reshape(n_p * t_p, D_B)
            mix_s, v_s = _sgu_first(proj_s, row(b_ln_g[j]), row(b_ln_b[j]), w00_rows[j], b0_rows[j])
            s_sgu_v.append(v_s.reshape(n_s, 1, D_B))
            cols, xq_blk = B_COLS, B_XQ_OFF // D_X
        xo_p = _xattn(proj_p.reshape(n_p, t_p, cols), xq_blk, p_mem_k[i], p_mem_v[i], 512)
        xo_s = _xattn_step(proj_s, xq_blk, s_mem_k[i], s_mem_v[i])
        g1, b1, g2, b2 = row(ln1_g[i]), row(ln1_b[i]), row(ln2_g[i]), row(ln2_b[i])
        xp = _outproj_ln(mix_p, xo_p.reshape(n_p * t_p, D_X), xp, w_out_b[i], g1, b1, 512)
        xs = _outproj_ln(mix_s, xo_s, xs, w_out_b[i], g1, b1, tm_s)
        if i % 2 == 0:
            xp = _ffn_ln(xp, None, ffn_gu[j][None], ffn_dn[j][None], g2, b2, tm_p, 512)
            xs = _ffn_ln(xs, None, ffn_gu[j][None], ffn_dn[j][None], g2, b2, tm_s, 512)
        else:
            comb_p = _router(xp, r_hi[j], r_lo[j], tm_p)
            comb_s = _router(xs, r_hi[j], r_lo[j], tm_s)
            xp = _ffn_ln(xp, comb_p, moe_gu[j], moe_dn[j], g2, b2, tm_p, 512)
            xs = _ffn_ln(xs, comb_s, moe_gu[j], moe_dn[j], g2, b2, tm_s, 512)

    mem_shape = (DEPTH, n_p, N_MEM, H_X, DH_X)
    return (xp.reshape(n_p, t_p, D_MODEL),
            xs.reshape(n_s, 1, D_MODEL),
            jnp.stack(p_gdn),
            jnp.stack(p_conv),
            p_mem_k.reshape(mem_shape),
            p_mem_v.reshape(mem_shape),
            jnp.stack(s_gdn),
            jnp.stack(s_conv),
            jnp.stack(s_sgu_v))
```

```python
import functools

import jax
import jax.numpy as jnp
from jax import lax
from jax.experimental import pallas as pl
from jax.experimental.pallas import tpu as pltpu

F32 = jnp.float32
BF16 = jnp.bfloat16

D_MODEL = 1024
DEPTH = 4
H_A = 6
DK_A = 128
DV_A = 128
QK_A = H_A * DK_A
V_A = H_A * DV_A
CONV_W = 4
CONV_DIM = 2 * QK_A + V_A
GDN_CHUNK = 64
G_B = 6
CH_B = 128
D_B = G_B * CH_B
CHUNK_B = 128
N_MEM = 256
H_X = 4
DH_X = 64
D_X = H_X * DH_X
D_FF = 3584
N_EXP = 8
ALPHA = (2 * DEPTH) ** 0.25
LN_EPS = 1e-5
RMS_EPS = 1e-6

LANES = 128
SUBLANES = 8
VMEM_LIMIT_BYTES = 48 * 1024 * 1024

A_Z_OFF = CONV_DIM
A_XQ_OFF = CONV_DIM + V_A
A_AB_OFF = A_XQ_OFF + D_X
A_COLS = A_AB_OFF + LANES
B_LANE = 8
B_XQ_OFF = 2 * D_B
B_COLS = 2 * D_B + D_X


def _params(*sem):
    return pltpu.CompilerParams(dimension_semantics=sem, vmem_limit_bytes=VMEM_LIMIT_BYTES)


def _split2(x):
    hi = x.astype(BF16)
    lo = (x - hi.astype(F32)).astype(BF16)
    return hi, lo


_NN = (((1,), (0,)), ((), ()))
_NT = (((1,), (1,)), ((), ()))
_TN = (((0,), (0,)), ((), ()))


def _dot(a, b, dims=_NN, passes=1):
    if passes == 1:
        return lax.dot_general(a.astype(BF16), b.astype(BF16), dims, preferred_element_type=F32)
    ah, al = _split2(a)
    bh, bl = _split2(b)
    dg = functools.partial(lax.dot_general, dimension_numbers=dims, preferred_element_type=F32)
    return dg(ah, bh) + (dg(ah, bl) + dg(al, bh))


def _dot_exact_lhs(lhs_bf16, x):
    x1 = x.astype(BF16)
    r1 = x - x1.astype(F32)
    x2 = r1.astype(BF16)
    x3 = (r1 - x2.astype(F32)).astype(BF16)
    dg = functools.partial(lax.dot_general, dimension_numbers=_NN, preferred_element_type=F32)
    return dg(lhs_bf16, x1) + (dg(lhs_bf16, x2) + dg(lhs_bf16, x3))


def _sigmoid(x):
    return 1.0 / (1.0 + jnp.exp(-x))


def _silu(x):
    return x * _sigmoid(x)


def _softplus(x):
    return jnp.maximum(x, 0.0) + jnp.log1p(jnp.exp(-jnp.abs(x)))


def _gelu_tanh(x):
    c = 0.7978845608028654
    return 0.5 * x * (1.0 + jnp.tanh(c * (x + 0.044715 * (x * x * x))))


def _layer_norm(y, g, b):
    mu = jnp.mean(y, axis=-1, keepdims=True)
    d = y - mu
    var = jnp.mean(d * d, axis=-1, keepdims=True)
    return d * lax.rsqrt(var + LN_EPS) * g + b


def _lane_pick(x, lane_iota, idx):
    return jnp.sum(jnp.where(lane_iota == idx, x, 0.0), axis=-1, keepdims=True)


def _mm_kernel(x_ref, w_ref, o_ref):
    o_ref[...] = jnp.dot(x_ref[...].astype(BF16), w_ref[...],
                         preferred_element_type=F32).astype(o_ref.dtype)


def _matmul(x, w, tm, tn, name):
    m, k = x.shape
    n = w.shape[1]
    return pl.pallas_call(
        _mm_kernel,
        grid=(m // tm, n // tn),
        in_specs=[pl.BlockSpec((tm, k), lambda i, j: (i, 0)),
                  pl.BlockSpec((k, tn), lambda i, j: (0, j))],
        out_specs=pl.BlockSpec((tm, tn), lambda i, j: (i, j)),
        out_shape=jax.ShapeDtypeStruct((m, n), F32),
        compiler_params=_params("parallel", "parallel"),
        name=name,
    )(x, w)


def _outproj_ln_kernel(mix_ref, xo_ref, x_ref, wm_ref, wx_ref, g_ref, b_ref, o_ref):
    h = jnp.dot(mix_ref[...].astype(BF16), wm_ref[...], preferred_element_type=F32)
    h = h + jnp.dot(xo_ref[...].astype(BF16), wx_ref[...], preferred_element_type=F32)
    o_ref[...] = _layer_norm(ALPHA * x_ref[...] + h, g_ref[...], b_ref[...])


def _outproj_ln(mix, xo, x, w_out, g, b, tm):
    m = x.shape[0]
    d_mix = mix.shape[1]
    n_blk = d_mix // D_X
    return pl.pallas_call(
        _outproj_ln_kernel,
        grid=(m // tm,),
        in_specs=[pl.BlockSpec((tm, d_mix), lambda i: (i, 0)),
                  pl.BlockSpec((tm, D_X), lambda i: (i, 0)),
                  pl.BlockSpec((tm, D_MODEL), lambda i: (i, 0)),
                  pl.BlockSpec((d_mix, D_MODEL), lambda i: (0, 0)),
                  pl.BlockSpec((D_X, D_MODEL), lambda i: (n_blk, 0)),
                  pl.BlockSpec((1, D_MODEL), lambda i: (0, 0)),
                  pl.BlockSpec((1, D_MODEL), lambda i: (0, 0))],
        out_specs=pl.BlockSpec((tm, D_MODEL), lambda i: (i, 0)),
        out_shape=jax.ShapeDtypeStruct((m, D_MODEL), F32),
        compiler_params=_params("parallel"),
        name="outproj_ln",
    )(mix, xo, x, w_out, w_out, g, b)


def _ffn_kernel(*refs, moe):
    if moe:
        x_ref, comb_ref, wg_ref, wu_ref, wd_ref, g_ref, b_ref, o_ref, xb_ref, acc_ref = refs
    else:
        x_ref, wg_ref, wu_ref, wd_ref, g_ref, b_ref, o_ref, xb_ref, acc_ref = refs
    e = pl.program_id(1)
    f = pl.program_id(2)
    first = jnp.logical_and(e == 0, f == 0)
    last = jnp.logical_and(e == pl.num_programs(1) - 1, f == pl.num_programs(2) - 1)

    @pl.when(first)
    def _():
        xb_ref[...] = x_ref[...].astype(BF16)
        acc_ref[...] = jnp.zeros_like(acc_ref)

    xb = xb_ref[...]
    hg = jnp.dot(xb, wg_ref[...], preferred_element_type=F32)
    hu = jnp.dot(xb, wu_ref[...], preferred_element_type=F32)
    h = _silu(hg) * hu
    if moe:
        comb = comb_ref[...]
        lane = lax.broadcasted_iota(jnp.int32, comb.shape, 1)
        h = h * _lane_pick(comb, lane, e)
    acc_ref[...] += jnp.dot(h.astype(BF16), wd_ref[...], preferred_element_type=F32)

    @pl.when(last)
    def _():
        o_ref[...] = _layer_norm(ALPHA * x_ref[...] + acc_ref[...], g_ref[...], b_ref[...])


def _ffn_ln(x, comb, w_gu, w_down, g, b, tm, tf):
    m = x.shape[0]
    moe = comb is not None
    n_e = w_gu.shape[0]
    n_f = D_FF // tf
    in_specs = [pl.BlockSpec((tm, D_MODEL), lambda i, e, f: (i, 0))]
    args = [x]
    if moe:
        in_specs.append(pl.BlockSpec((tm, LANES), lambda i, e, f: (i, 0)))
        args.append(comb)
    in_specs += [pl.BlockSpec((None, D_MODEL, tf), lambda i, e, f: (e, 0, f)),
                 pl.BlockSpec((None, D_MODEL, tf), lambda i, e, f: (e, 0, n_f + f)),
                 pl.BlockSpec((None, tf, D_MODEL), lambda i, e, f: (e, f, 0)),
                 pl.BlockSpec((1, D_MODEL), lambda i, e, f: (0, 0)),
                 pl.BlockSpec((1, D_MODEL), lambda i, e, f: (0, 0))]
    args += [w_gu, w_gu, w_down, g, b]
    return pl.pallas_call(
        functools.partial(_ffn_kernel, moe=moe),
        grid=(m // tm, n_e, n_f),
        in_specs=in_specs,
        out_specs=pl.BlockSpec((tm, D_MODEL), lambda i, e, f: (i, 0)),
        out_shape=jax.ShapeDtypeStruct((m, D_MODEL), F32),
        scratch_shapes=[pltpu.VMEM((tm, D_MODEL), BF16), pltpu.VMEM((tm, D_MODEL), F32)],
        compiler_params=_params("parallel", "arbitrary", "arbitrary"),
        name="moe_ln" if moe else "ffn_ln",
    )(*args)


def _router_kernel(x_ref, wh_ref, wl_ref, comb_ref, gate_ref, idx_ref):
    x = x_ref[...]
    xh, xl = _split2(x)
    wh = wh_ref[...]
    lg = jnp.dot(xh, wh, preferred_element_type=F32)
    lg = lg + (jnp.dot(xh, wl_ref[...], preferred_element_type=F32)
               + jnp.dot(xl, wh, preferred_element_type=F32))
    lane = lax.broadcasted_iota(jnp.int32, lg.shape, 1)
    neg = -jnp.inf
    lg = jnp.where(lane < N_EXP, lg, neg)
    m1 = jnp.max(lg, axis=-1, keepdims=True)
    i1 = jnp.min(jnp.where(lg == m1, lane, LANES), axis=-1, keepdims=True)
    lg2 = jnp.where(lane == i1, neg, lg)
    m2 = jnp.max(lg2, axis=-1, keepdims=True)
    i2 = jnp.min(jnp.where(lg2 == m2, lane, LANES), axis=-1, keepdims=True)
    e2 = jnp.exp(m2 - m1)
    den = 1.0 + e2
    g1 = 1.0 / den
    g2 = e2 / den
    comb_ref[...] = jnp.where(lane == i1, g1, 0.0) + jnp.where(lane == i2, g2, 0.0)
    gate_ref[...] = jnp.where(lane == 0, g1, jnp.where(lane == 1, g2, 0.0))
    idx_ref[...] = jnp.where(lane == 0, i1, jnp.where(lane == 1, i2, 0))


def _router(x, w_hi, w_lo, tm):
    m = x.shape[0]
    out = pl.BlockSpec((tm, LANES), lambda i: (i, 0))
    return pl.pallas_call(
        _router_kernel,
        grid=(m // tm,),
        in_specs=[pl.BlockSpec((tm, D_MODEL), lambda i: (i, 0)),
                  pl.BlockSpec((D_MODEL, LANES), lambda i: (0, 0)),
                  pl.BlockSpec((D_MODEL, LANES), lambda i: (0, 0))],
        out_specs=[out, out, out],
        out_shape=[jax.ShapeDtypeStruct((m, LANES), F32),
                   jax.ShapeDtypeStruct((m, LANES), F32),
                   jax.ShapeDtypeStruct((m, LANES), jnp.int32)],
        compiler_params=_params("parallel"),
        name="router",
    )(x, w_hi, w_lo)


MOE_TM = 512
MOE_TF = 512
DISPATCH_TOKENS = 1024
COMBINE_TOKENS = 256


def _moe_plan(idx2, tm):
    t = idx2.shape[0]
    e_flat = idx2.reshape(-1)
    experts = jnp.arange(N_EXP, dtype=jnp.int32)
    onehot = (e_flat[:, None] == experts[None, :]).astype(jnp.int32)
    csum = jnp.cumsum(onehot, axis=0)
    rank = jnp.sum(csum * onehot, axis=1) - 1
    counts = csum[-1]
    ptiles = (counts + tm - 1) // tm
    tile_end = jnp.cumsum(ptiles)
    pstart = (tile_end - ptiles) * tm
    pos = jnp.sum(onehot * pstart[None, :], axis=1) + rank
    n_used = tile_end[-1]
    n_tiles = (2 * t) // tm + N_EXP
    jj = jnp.minimum(jnp.arange(n_tiles, dtype=jnp.int32), n_used - 1)
    tile_expert = jnp.sum((jj[:, None] >= tile_end[None, :]).astype(jnp.int32), axis=1)
    seg_base = jnp.concatenate([pstart + counts, (n_used * tm).reshape(1)])
    seg_len = jnp.concatenate([ptiles * tm - counts, ((n_tiles - n_used) * tm).reshape(1)])
    seg_end = jnp.cumsum(seg_len)
    j = jnp.arange(N_EXP * tm, dtype=jnp.int32)
    seg = jnp.sum((j[:, None] >= seg_end[None, :]).astype(jnp.int32), axis=1)
    seg_1h = (seg[:, None] == jnp.arange(N_EXP + 1, dtype=jnp.int32)[None, :]).astype(jnp.int32)
    pad_dst = j + jnp.sum(seg_1h * (seg_base - (seg_end - seg_len))[None, :], axis=1)
    return (pos.astype(jnp.int32), tile_expert.astype(jnp.int32), n_used.reshape(1).astype(jnp.int32),
            pad_dst.astype(jnp.int32), n_tiles)


def _row_copy(src_ref, src_row, dst_ref, dst_row, sem):
    return pltpu.make_async_copy(src_ref.at[pl.ds(src_row, 1), :], dst_ref.at[pl.ds(dst_row, 1), :], sem)


def _wait_rows(hbm_ref, n_rows, sem):
    rows = hbm_ref.at[pl.ds(0, n_rows), :]
    pltpu.make_async_copy(rows, rows, sem).wait()


def _moe_dispatch_kernel(pos_ref, pad_dst_ref, x_ref, xs_ref, sem):
    tq = x_ref.shape[0]
    n_pad = pad_dst_ref.shape[1]

    def send(grp, carry):
        r0 = pl.multiple_of(grp * SUBLANES, SUBLANES)
        for jr in range(SUBLANES):
            for k in range(2):
                _row_copy(x_ref, r0 + jr, xs_ref, pos_ref[0, 2 * (r0 + jr) + k], sem).start()
        return carry

    lax.fori_loop(0, tq // SUBLANES, send, 0)

    def pad(grp, carry):
        r0 = pl.multiple_of(grp * SUBLANES, SUBLANES)
        for jr in range(SUBLANES):
            _row_copy(x_ref, jr, xs_ref, pad_dst_ref[0, r0 + jr], sem).start()
        return carry

    lax.fori_loop(0, n_pad // SUBLANES, pad, 0)
    _wait_rows(xs_ref, 2 * tq + n_pad, sem)


def _moe_dispatch(x, pos, pad_dst, n_rows_out):
    m = x.shape[0]
    tq = DISPATCH_TOKENS
    n_steps = m // tq
    n_pad = pad_dst.shape[0] // n_steps
    smem = functools.partial(pl.BlockSpec, memory_space=pltpu.SMEM)
    return pl.pallas_call(
        _moe_dispatch_kernel,
        grid=(n_steps,),
        in_specs=[smem((None, 1, 2 * tq), lambda i: (i, 0, 0)),
                  smem((None, 1, n_pad), lambda i: (i, 0, 0)),
                  pl.BlockSpec((tq, D_MODEL), lambda i: (i, 0))],
        out_specs=pl.BlockSpec(memory_space=pl.ANY),
        out_shape=jax.ShapeDtypeStruct((n_rows_out, D_MODEL), F32),
        scratch_shapes=[pltpu.SemaphoreType.DMA(())],
        compiler_params=_params("arbitrary"),
        name="moe_dispatch",
    )(pos.reshape(n_steps, 1, 2 * tq), pad_dst.reshape(n_steps, 1, n_pad), x)


def _moe_ffn_kernel(te_ref, nu_ref, x_ref, wg_ref, wu_ref, wd_ref, o_ref, xb_ref):
    p = pl.program_id(0)
    f = pl.program_id(1)

    @pl.when(p < nu_ref[0])
    def _():
        @pl.when(f == 0)
        def _():
            xb_ref[...] = x_ref[...].astype(BF16)

        xb = xb_ref[...]
        hg = jnp.dot(xb, wg_ref[...], preferred_element_type=F32)
        hu = jnp.dot(xb, wu_ref[...], preferred_element_type=F32)
        y = jnp.dot((_silu(hg) * hu).astype(BF16), wd_ref[...], preferred_element_type=F32)

        @pl.when(f == 0)
        def _():
            o_ref[...] = y

        @pl.when(f != 0)
        def _():
            o_ref[...] += y

    @pl.when(jnp.logical_and(p >= nu_ref[0], f == 0))
    def _():
        o_ref[...] = jnp.zeros_like(o_ref)


def _moe_ffn(xs, tile_expert, n_used, w_gu, w_down, tm, tf):
    n_tiles = xs.shape[0] // tm
    n_f = D_FF // tf
    tile = lambda p, nu: jnp.minimum(p, nu[0] - 1)
    col = lambda p, f, nu: jnp.where(p < nu[0], f, n_f - 1)
    grid_spec = pltpu.PrefetchScalarGridSpec(
        num_scalar_prefetch=2,
        grid=(n_tiles, n_f),
        in_specs=[pl.BlockSpec((tm, D_MODEL), lambda p, f, te, nu: (tile(p, nu), 0)),
                  pl.BlockSpec((None, D_MODEL, tf), lambda p, f, te, nu: (te[p], 0, col(p, f, nu))),
                  pl.BlockSpec((None, D_MODEL, tf), lambda p, f, te, nu: (te[p], 0, n_f + col(p, f, nu))),
                  pl.BlockSpec((None, tf, D_MODEL), lambda p, f, te, nu: (te[p], col(p, f, nu), 0))],
        out_specs=pl.BlockSpec((tm, D_MODEL), lambda p, f, te, nu: (p, 0)),
        scratch_shapes=[pltpu.VMEM((tm, D_MODEL), BF16)],
    )
    return pl.pallas_call(
        _moe_ffn_kernel,
        grid_spec=grid_spec,
        out_shape=jax.ShapeDtypeStruct(xs.shape, F32),
        compiler_params=_params("arbitrary", "arbitrary"),
        name="moe_ffn",
    )(tile_expert, n_used, xs, w_gu, w_gu, w_down)


def _moe_combine_ln_kernel(pos_ref, gate_ref, x_ref, ye_ref, g_ref, b_ref, o_ref, buf_ref, sem):
    tq = x_ref.shape[0]

    def fetch(grp, carry):
        r0 = pl.multiple_of(grp * SUBLANES, SUBLANES)
        for jr in range(SUBLANES):
            for k in range(2):
                _row_copy(ye_ref, pos_ref[0, 2 * (r0 + jr) + k], buf_ref, k * tq + r0 + jr, sem).start()
        return carry

    lax.fori_loop(0, tq // SUBLANES, fetch, 0)
    _wait_rows(ye_ref, 2 * tq, sem)
    gates = gate_ref[...]
    lane = lax.broadcasted_iota(jnp.int32, gates.shape, 1)
    y = _lane_pick(gates, lane, 0) * buf_ref[0:tq, :] + _lane_pick(gates, lane, 1) * buf_ref[tq:2 * tq, :]
    o_ref[...] = _layer_norm(ALPHA * x_ref[...] + y, g_ref[...], b_ref[...])


def _moe_combine_ln(x, gates, pos, ye, g, b):
    m = x.shape[0]
    tq = COMBINE_TOKENS
    n_steps = m // tq
    return pl.pallas_call(
        _moe_combine_ln_kernel,
        grid=(n_steps,),
        in_specs=[pl.BlockSpec((None, 1, 2 * tq), lambda i: (i, 0, 0), memory_space=pltpu.SMEM),
                  pl.BlockSpec((tq, LANES), lambda i: (i, 0)),
                  pl.BlockSpec((tq, D_MODEL), lambda i: (i, 0)),
                  pl.BlockSpec(memory_space=pl.ANY),
                  pl.BlockSpec((1, D_MODEL), lambda i: (0, 0)),
                  pl.BlockSpec((1, D_MODEL), lambda i: (0, 0))],
        out_specs=pl.BlockSpec((tq, D_MODEL), lambda i: (i, 0)),
        out_shape=jax.ShapeDtypeStruct((m, D_MODEL), F32),
        scratch_shapes=[pltpu.VMEM((2 * tq, D_MODEL), F32), pltpu.SemaphoreType.DMA(())],
        compiler_params=_params("arbitrary"),
        name="moe_combine_ln",
    )(pos.reshape(n_steps, 1, 2 * tq), gates, x, ye, g, b)


def _moe_routed_ln(x, gates, idx, w_gu, w_down, g, b):
    pos, tile_expert, n_used, pad_dst, n_tiles = _moe_plan(idx[:, :2], MOE_TM)
    xs = _moe_dispatch(x, pos, pad_dst, n_tiles * MOE_TM)
    ye = _moe_ffn(xs, tile_expert, n_used, w_gu, w_down, MOE_TM, MOE_TF)
    return _moe_combine_ln(x, gates, pos, ye, g, b)


GDN_PASSES_QK = 1
GDN_PASSES_SOLVE = 3
GDN_PASSES_STATE = 1
INV_BLOCK = 16


def _gdn_chunk_kernel(qkv_ref, z_ref, ab_ref, cw_ref, alog_ref, dtb_ref, nw_ref, hist0_ref, s0_ref,
                      mix_ref, sout_ref, hout_ref, xh_ref, s_ref):
    n = pl.program_id(1)
    c = GDN_CHUNK
    hrows = SUBLANES

    @pl.when(n == 0)
    def _():
        xh_ref[0:hrows, :] = hist0_ref[...]
        s_ref[...] = s0_ref[...]

    xh_ref[hrows:hrows + c, :] = qkv_ref[...]

    row = lax.broadcasted_iota(jnp.int32, (c, c), 0)
    col = lax.broadcasted_iota(jnp.int32, (c, c), 1)
    causal = row >= col
    strict = row > col
    eye = jnp.where(row == col, 1.0, 0.0)
    blockdiag = (row // INV_BLOCK) == (col // INV_BLOCK)
    tril_ones = jnp.where(causal, 1.0, 0.0).astype(BF16)
    lane = lax.broadcasted_iota(jnp.int32, (c, LANES), 1)

    ab = ab_ref[...]
    g_all = -jnp.exp(alog_ref[...]) * _softplus(ab + dtb_ref[...])
    beta_all = _sigmoid(ab)
    gcum_all = _dot_exact_lhs(tril_ones, g_all)

    def conv_silu(c0):
        acc = xh_ref[hrows - 3:hrows - 3 + c, c0:c0 + LANES] * cw_ref[0:1, c0:c0 + LANES]
        for j in range(1, CONV_W):
            acc = acc + (xh_ref[hrows - 3 + j:hrows - 3 + j + c, c0:c0 + LANES]
                         * cw_ref[j:j + 1, c0:c0 + LANES])
        return _silu(acc)

    mm_s = functools.partial(_dot, passes=GDN_PASSES_SOLVE)

    for h in range(H_A):
        q = conv_silu(h * DK_A)
        k = conv_silu(QK_A + h * DK_A)
        v = conv_silu(2 * QK_A + h * DV_A)
        q = q * lax.rsqrt(jnp.sum(q * q, axis=-1, keepdims=True) + RMS_EPS) * (DK_A ** -0.5)
        k = k * lax.rsqrt(jnp.sum(k * k, axis=-1, keepdims=True) + RMS_EPS)
        g_col = _lane_pick(g_all, lane, h)
        gc_col = _lane_pick(gcum_all, lane, h)
        beta_col = _lane_pick(beta_all, lane, B_LANE + h)
        gc_last = gc_col[c - 1:c, :]

        dlog = _dot_exact_lhs(tril_ones, jnp.where(strict, g_col, 0.0))
        decay = jnp.where(causal, jnp.exp(dlog), 0.0)
        egc = jnp.exp(gc_col)
        kb = k * beta_col
        kk = _dot(jnp.concatenate([kb, q], axis=0), k, _NT, GDN_PASSES_QK)
        a_mat = jnp.where(strict, kk[:c] * decay, 0.0)
        attn = kk[c:] * decay

        a_d = jnp.where(blockdiag, a_mat, 0.0)
        a_n = a_mat - a_d
        p2 = mm_s(a_d, a_d)
        p4 = mm_s(p2, p2)
        p8 = mm_s(p4, p4)
        d_inv = eye - a_d
        d_inv = d_inv + mm_s(d_inv, p2)
        d_inv = d_inv + mm_s(d_inv, p4)
        d_inv = d_inv + mm_s(d_inv, p8)
        m_blk = mm_s(d_inv, a_n)
        m2 = mm_s(m_blk, m_blk)
        y = eye - m_blk
        y = y + mm_s(y, m2)
        t_inv = mm_s(y, d_inv)

        rhs = jnp.concatenate([v * beta_col, kb * egc], axis=1)
        sol = mm_s(t_inv, rhs)
        u_val = sol[:, :DV_A]
        w_cum = sol[:, DV_A:]

        s_h = s_ref[h]
        wq = _dot(jnp.concatenate([w_cum, q * egc], axis=0), s_h, _NN, GDN_PASSES_STATE)
        v_new = u_val - wq[:c]
        o = wq[c:] + _dot(attn, v_new, _NN, GDN_PASSES_STATE)
        k_dec = k * jnp.exp(gc_last - gc_col)
        s_ref[h] = s_h * jnp.exp(gc_last) + _dot(k_dec, v_new, _TN, GDN_PASSES_STATE)

        o = o * lax.rsqrt(jnp.mean(o * o, axis=-1, keepdims=True) + RMS_EPS) * nw_ref[...]
        mix_ref[:, h * DV_A:(h + 1) * DV_A] = o * _silu(z_ref[:, h * DV_A:(h + 1) * DV_A])

    xh_ref[0:hrows, :] = xh_ref[c:c + hrows, :]

    @pl.when(n == pl.num_programs(1) - 1)
    def _():
        sout_ref[...] = s_ref[...]
        hout_ref[...] = xh_ref[c:c + hrows, :]


def _gdn_chunked(proj, conv_w, alog_row, dtb_row, nw_row, hist0, s0):
    bsz, t, _ = proj.shape
    c = GDN_CHUNK
    n_chunks = t // c
    row_spec = lambda width, blk: pl.BlockSpec((None, c, width), lambda b, n: (b, n, blk))
    full2 = lambda shape: pl.BlockSpec(shape, lambda b, n: (0, 0))
    return pl.pallas_call(
        _gdn_chunk_kernel,
        grid=(bsz, n_chunks),
        in_specs=[row_spec(CONV_DIM, 0),
                  row_spec(V_A, A_Z_OFF // V_A),
                  row_spec(LANES, A_AB_OFF // LANES),
                  full2((CONV_W, CONV_DIM)),
                  full2((1, LANES)), full2((1, LANES)), full2((1, DV_A)),
                  pl.BlockSpec((None, SUBLANES, CONV_DIM), lambda b, n: (b, 0, 0)),
                  pl.BlockSpec((None, H_A, DK_A, DV_A), lambda b, n: (b, 0, 0, 0))],
        out_specs=[pl.BlockSpec((None, c, V_A), lambda b, n: (b, n, 0)),
                   pl.BlockSpec((None, H_A, DK_A, DV_A), lambda b, n: (b, 0, 0, 0)),
                   pl.BlockSpec((None, SUBLANES, CONV_DIM), lambda b, n: (b, 0, 0))],
        out_shape=[jax.ShapeDtypeStruct((bsz, t, V_A), F32),
                   jax.ShapeDtypeStruct((bsz, H_A, DK_A, DV_A), F32),
                   jax.ShapeDtypeStruct((bsz, SUBLANES, CONV_DIM), F32)],
        scratch_shapes=[pltpu.VMEM((SUBLANES + c, CONV_DIM), F32),
                        pltpu.VMEM((H_A, DK_A, DV_A), F32)],
        compiler_params=_params("parallel", "arbitrary"),
        name="gdn_chunked",
    )(proj, proj, proj, conv_w, alog_row, dtb_row, nw_row, hist0, s0)


def _gdn_step_kernel(qkv_ref, z_ref, ab_ref, cw_ref, alog_ref, dtb_ref, nw_ref, hist_ref, s_ref,
                     mix_ref, sout_ref, hout_ref, o_scr):
    rows = qkv_ref.shape[0]
    new = qkv_ref[...]
    y = hist_ref[0] * cw_ref[0:1, :]
    for j in range(1, CONV_W - 1):
        y = y + hist_ref[j] * cw_ref[j:j + 1, :]
    y = y + new * cw_ref[CONV_W - 1:CONV_W, :]
    qkv = _silu(y)
    for j in range(CONV_W - 2):
        hout_ref[j] = hist_ref[j + 1]
    hout_ref[CONV_W - 2] = new

    ab = ab_ref[...]
    lane = lax.broadcasted_iota(jnp.int32, ab.shape, 1)
    g_all = -jnp.exp(alog_ref[...]) * _softplus(ab + dtb_ref[...])
    beta_all = _sigmoid(ab)
    ri = lax.broadcasted_iota(jnp.int32, (DK_A, DK_A), 0)
    ci = lax.broadcasted_iota(jnp.int32, (DK_A, DK_A), 1)
    diag = ri == ci

    def as_column(r):
        return jnp.sum(jnp.where(diag, r, 0.0), axis=-1, keepdims=True)

    for h in range(H_A):
        q = qkv[:, h * DK_A:(h + 1) * DK_A]
        k = qkv[:, QK_A + h * DK_A:QK_A + (h + 1) * DK_A]
        v = qkv[:, 2 * QK_A + h * DV_A:2 * QK_A + (h + 1) * DV_A]
        q = q * lax.rsqrt(jnp.sum(q * q, axis=-1, keepdims=True) + RMS_EPS) * (DK_A ** -0.5)
        k = k * lax.rsqrt(jnp.sum(k * k, axis=-1, keepdims=True) + RMS_EPS)
        eg = jnp.exp(_lane_pick(g_all, lane, h))
        beta = _lane_pick(beta_all, lane, B_LANE + h)
        for s in range(rows):
            k_col = as_column(k[s:s + 1, :])
            q_col = as_column(q[s:s + 1, :])
            st = s_ref[s, h] * eg[s:s + 1, :]
            ks = jnp.sum(k_col * st, axis=0, keepdims=True)
            u = beta[s:s + 1, :] * (v[s:s + 1, :] - ks)
            st = st + k_col * u
            sout_ref[s, h] = st
            o_scr[s:s + 1, h * DV_A:(h + 1) * DV_A] = jnp.sum(q_col * st, axis=0, keepdims=True)

    for h in range(H_A):
        o = o_scr[:, h * DV_A:(h + 1) * DV_A]
        o = o * lax.rsqrt(jnp.mean(o * o, axis=-1, keepdims=True) + RMS_EPS) * nw_ref[...]
        mix_ref[:, h * DV_A:(h + 1) * DV_A] = o * _silu(z_ref[:, h * DV_A:(h + 1) * DV_A])


def _gdn_step(proj, conv_w, alog_row, dtb_row, nw_row, hist_t, s0):
    bsz = proj.shape[0]
    rows = SUBLANES
    full1 = lambda shape: pl.BlockSpec(shape, lambda i: (0, 0))
    return pl.pallas_call(
        _gdn_step_kernel,
        grid=(bsz // rows,),
        in_specs=[pl.BlockSpec((rows, CONV_DIM), lambda i: (i, 0)),
                  pl.BlockSpec((rows, V_A), lambda i: (i, A_Z_OFF // V_A)),
                  pl.BlockSpec((rows, LANES), lambda i: (i, A_AB_OFF // LANES)),
                  full1((CONV_W, CONV_DIM)),
                  full1((1, LANES)), full1((1, LANES)), full1((1, DV_A)),
                  pl.BlockSpec((CONV_W - 1, rows, CONV_DIM), lambda i: (0, i, 0)),
                  pl.BlockSpec((rows, H_A, DK_A, DV_A), lambda i: (i, 0, 0, 0))],
        out_specs=[pl.BlockSpec((rows, V_A), lambda i: (i, 0)),
                   pl.BlockSpec((rows, H_A, DK_A, DV_A), lambda i: (i, 0, 0, 0)),
                   pl.BlockSpec((CONV_W - 1, rows, CONV_DIM), lambda i: (0, i, 0))],
        out_shape=[jax.ShapeDtypeStruct((bsz, V_A), F32),
                   jax.ShapeDtypeStruct((bsz, H_A, DK_A, DV_A), F32),
                   jax.ShapeDtypeStruct((CONV_W - 1, bsz, CONV_DIM), F32)],
        scratch_shapes=[pltpu.VMEM((rows, V_A), F32)],
        compiler_params=_params("parallel"),
        name="gdn_step",
    )(proj, proj, proj, conv_w, alog_row, dtb_row, nw_row, hist_t, s0)


def _sgu_chunk_kernel(u_ref, v_ref, g_ref, b_ref, ws_ref, bs_ref, mix_ref):
    c = CHUNK_B
    row = lax.broadcasted_iota(jnp.int32, (c, c), 0)
    col = lax.broadcasted_iota(jnp.int32, (c, c), 1)
    u = _gelu_tanh(u_ref[...])
    v = _layer_norm(_gelu_tanh(v_ref[...]), g_ref[...], b_ref[...])
    for gi in range(G_B):
        ws = jnp.where(row >= col, ws_ref[gi], 0.0)
        mixed = _dot(ws, v[:, gi * CH_B:(gi + 1) * CH_B]) + bs_ref[gi]
        mix_ref[:, gi * CH_B:(gi + 1) * CH_B] = u[:, gi * CH_B:(gi + 1) * CH_B] * mixed


def _sgu_chunked(proj, ln_g, ln_b, w_s, bs_b):
    bsz, t, _ = proj.shape
    c = CHUNK_B
    return pl.pallas_call(
        _sgu_chunk_kernel,
        grid=(bsz, t // c),
        in_specs=[pl.BlockSpec((None, c, D_B), lambda b, n: (b, n, 0)),
                  pl.BlockSpec((None, c, D_B), lambda b, n: (b, n, 1)),
                  pl.BlockSpec((1, D_B), lambda b, n: (0, 0)),
                  pl.BlockSpec((1, D_B), lambda b, n: (0, 0)),
                  pl.BlockSpec((G_B, c, c), lambda b, n: (0, 0, 0)),
                  pl.BlockSpec((G_B, c, CH_B), lambda b, n: (0, 0, 0))],
        out_specs=pl.BlockSpec((None, c, D_B), lambda b, n: (b, n, 0)),
        out_shape=jax.ShapeDtypeStruct((bsz, t, D_B), F32),
        compiler_params=_params("parallel", "parallel"),
        name="sgu_chunked",
    )(proj, proj, ln_g, ln_b, w_s, bs_b)


def _sgu_first_kernel(u_ref, v_ref, g_ref, b_ref, w00_ref, b0_ref, mix_ref, vout_ref):
    u = _gelu_tanh(u_ref[...])
    v = _layer_norm(_gelu_tanh(v_ref[...]), g_ref[...], b_ref[...])
    vout_ref[...] = v
    mix_ref[...] = u * (w00_ref[...] * v + b0_ref[...])


def _sgu_first(proj, ln_g, ln_b, w00_row, b0_row):
    bsz = proj.shape[0]
    row = lambda: pl.BlockSpec((1, D_B), lambda i: (0, 0))
    return pl.pallas_call(
        _sgu_first_kernel,
        grid=(1,),
        in_specs=[pl.BlockSpec((bsz, D_B), lambda i: (0, 0)),
                  pl.BlockSpec((bsz, D_B), lambda i: (0, 1)),
                  row(), row(), row(), row()],
        out_specs=[pl.BlockSpec((bsz, D_B), lambda i: (0, 0)),
                   pl.BlockSpec((bsz, D_B), lambda i: (0, 0))],
        out_shape=[jax.ShapeDtypeStruct((bsz, D_B), F32),
                   jax.ShapeDtypeStruct((bsz, D_B), F32)],
        compiler_params=_params("arbitrary"),
        name="sgu_first",
    )(proj, proj, ln_g, ln_b, w00_row, b0_row)


def _head_masks(shape):
    lane = lax.broadcasted_iota(jnp.int32, shape, len(shape) - 1)
    return [(lane // DH_X) == h for h in range(H_X)]


def _xattn_kernel(q_ref, k_ref, v_ref, o_ref):
    q = q_ref[...]
    kb = k_ref[...].astype(BF16)
    vb = v_ref[...].astype(BF16)
    masks = _head_masks(q.shape)
    out = jnp.zeros(q.shape, F32)
    for h in range(H_X):
        qh = jnp.where(masks[h], q, 0.0)
        s = _dot(qh, kb, _NT) * (DH_X ** -0.5)
        s = s - jnp.max(s, axis=-1, keepdims=True)
        p = jnp.exp(s)
        p = p / jnp.sum(p, axis=-1, keepdims=True)
        out = out + jnp.where(masks[h], _dot(p, vb), 0.0)
    o_ref[...] = out


def _xattn(proj, xq_blk, mem_k, mem_v, tq):
    bsz, t, _ = proj.shape
    return pl.pallas_call(
        _xattn_kernel,
        grid=(bsz, t // tq),
        in_specs=[pl.BlockSpec((None, tq, D_X), lambda b, i: (b, i, xq_blk)),
                  pl.BlockSpec((None, N_MEM, D_X), lambda b, i: (b, 0, 0)),
                  pl.BlockSpec((None, N_MEM, D_X), lambda b, i: (b, 0, 0))],
        out_specs=pl.BlockSpec((None, tq, D_X), lambda b, i: (b, i, 0)),
        out_shape=jax.ShapeDtypeStruct((bsz, t, D_X), F32),
        compiler_params=_params("parallel", "parallel"),
        name="xattn",
    )(proj, mem_k, mem_v)


def _xattn_step_kernel(q_ref, k_ref, v_ref, o_ref):
    rows = q_ref.shape[0]
    q = q_ref[...]
    hrow = lax.broadcasted_iota(jnp.int32, (SUBLANES, D_X), 0)
    hlane = lax.broadcasted_iota(jnp.int32, (SUBLANES, D_X), 1) // DH_X
    sel = hrow == hlane
    for s in range(rows):
        qh = jnp.where(sel, q[s:s + 1, :], 0.0)
        sc = _dot(qh, k_ref[s], _NT) * (DH_X ** -0.5)
        sc = sc - jnp.max(sc, axis=-1, keepdims=True)
        p = jnp.exp(sc)
        p = p / jnp.sum(p, axis=-1, keepdims=True)
        pv = _dot(p, v_ref[s])
        o_ref[s:s + 1, :] = jnp.sum(jnp.where(sel, pv, 0.0), axis=0, keepdims=True)


def _xattn_step(proj, xq_blk, mem_k, mem_v):
    bsz = proj.shape[0]
    rows = SUBLANES
    return pl.pallas_call(
        _xattn_step_kernel,
        grid=(bsz // rows,),
        in_specs=[pl.BlockSpec((rows, D_X), lambda i: (i, xq_blk)),
                  pl.BlockSpec((rows, N_MEM, D_X), lambda i: (i, 0, 0)),
                  pl.BlockSpec((rows, N_MEM, D_X), lambda i: (i, 0, 0))],
        out_specs=pl.BlockSpec((rows, D_X), lambda i: (i, 0)),
        out_shape=jax.ShapeDtypeStruct((bsz, D_X), F32),
        compiler_params=_params("parallel"),
        name="xattn_step",
    )(proj, mem_k, mem_v)


def _pad_lanes(v):
    return jnp.zeros((1, LANES), F32).at[0, :v.shape[0]].set(v.astype(F32))


def _prep_weights(a_w_in, a_A_log, a_dt_bias, b_w_s, b_b_s, moe_router):
    qkvz = a_w_in[:, :, :A_XQ_OFF]
    a_cols = a_w_in[:, :, A_XQ_OFF:A_XQ_OFF + H_A]
    b_cols = a_w_in[:, :, A_XQ_OFF + H_A:A_XQ_OFF + 2 * H_A]
    xq = a_w_in[:, :, A_XQ_OFF + 2 * H_A:]
    ab = jnp.zeros(a_w_in.shape[:2] + (LANES,), F32)
    ab = ab.at[:, :, :H_A].set(a_cols).at[:, :, B_LANE:B_LANE + H_A].set(b_cols)
    a_w = jnp.concatenate([qkvz, xq, ab], axis=-1).astype(BF16)
    alog_rows = [_pad_lanes(a_A_log[j]) for j in range(a_A_log.shape[0])]
    dtb_rows = [_pad_lanes(a_dt_bias[j]) for j in range(a_dt_bias.shape[0])]
    bs_b = jnp.broadcast_to(b_b_s[..., None], b_b_s.shape + (CH_B,)).astype(F32)
    w00_rows = jnp.repeat(b_w_s[:, :, 0, 0], CH_B, axis=-1)[:, None, :]
    b0_rows = jnp.repeat(b_b_s[:, :, 0], CH_B, axis=-1)[:, None, :]
    r = jnp.zeros(moe_router.shape[:2] + (LANES,), F32).at[:, :, :N_EXP].set(moe_router)
    r_hi = r.astype(BF16)
    r_lo = (r - r_hi.astype(F32)).astype(BF16)
    return a_w, alog_rows, dtb_rows, bs_b, w00_rows, b0_rows, r_hi, r_lo


def kernel(x_prompt, x_sample, state_gdn, state_conv, cache_mem_k, cache_mem_v, mem_prompt,
           a_w_in, a_conv_w, a_A_log, a_dt_bias, a_norm_w,
           b_w_in, b_ln_g, b_ln_b, b_w_s, b_b_s,
           w_mem_kv, w_out, ln1_g, ln1_b, ln2_g, ln2_b,
           ffn_w_gu, ffn_w_down, moe_router, moe_w_gu, moe_w_down):
    n_p, t_p, _ = x_prompt.shape
    n_s = x_sample.shape[0]
    n_a = a_w_in.shape[0]

    a_w, alog_rows, dtb_rows, bs_b, w00_rows, b0_rows, r_hi, r_lo = _prep_weights(
        a_w_in, a_A_log, a_dt_bias, b_w_s, b_b_s, moe_router)
    b_w = b_w_in.astype(BF16)
    w_out_b = w_out.astype(BF16)
    ffn_gu = ffn_w_gu.astype(BF16)
    ffn_dn = ffn_w_down.astype(BF16)
    moe_gu = moe_w_gu.astype(BF16)
    moe_dn = moe_w_down.astype(BF16)
    row = lambda v: v.reshape(1, -1).astype(F32)

    w_kv = jnp.transpose(w_mem_kv, (1, 0, 2)).reshape(D_MODEL, DEPTH * 2 * D_X).astype(BF16)
    kv = _matmul(mem_prompt.reshape(n_p * N_MEM, D_MODEL), w_kv, 1024, 1024, "mem_kv")
    kv = kv.reshape(n_p, N_MEM, DEPTH, 2, D_X)
    p_mem_k = jnp.transpose(kv[:, :, :, 0, :], (2, 0, 1, 3))
    p_mem_v = jnp.transpose(kv[:, :, :, 1, :], (2, 0, 1, 3))
    s_mem_k = cache_mem_k.reshape(DEPTH, n_s, N_MEM, D_X)
    s_mem_v = cache_mem_v.reshape(DEPTH, n_s, N_MEM, D_X)

    xp = x_prompt.reshape(n_p * t_p, D_MODEL)
    xs = x_sample.reshape(n_s, D_MODEL)
    p_hist0 = jnp.zeros((n_p, SUBLANES, CONV_DIM), F32)
    p_s0 = jnp.zeros((n_p, H_A, DK_A, DV_A), F32)
    s_hist_t = jnp.transpose(state_conv, (0, 2, 1, 3))

    p_gdn, p_conv, s_gdn, s_conv, s_sgu_v = [], [], [], [], []
    tm_p, tm_s = 1024, n_s
    for i in range(DEPTH):
        j = i // 2
        if i % 2 == 0:
            proj_p = _matmul(xp, a_w[j], tm_p, 1152, "in_proj_a")
            proj_s = _matmul(xs, a_w[j], tm_s, 1152, "in_proj_a_s")
            nw = row(a_norm_w[j])
            mix_p, sp, hp = _gdn_chunked(proj_p.reshape(n_p, t_p, A_COLS), a_conv_w[j],
                                         alog_rows[j], dtb_rows[j], nw, p_hist0, p_s0)
            mix_p = mix_p.reshape(n_p * t_p, V_A)
            p_gdn.append(sp)
            p_conv.append(hp[:, SUBLANES - (CONV_W - 1):, :])
            mix_s, ss, hs = _gdn_step(proj_s, a_conv_w[j], alog_rows[j], dtb_rows[j], nw,
                                      s_hist_t[j], state_gdn[j])
            s_gdn.append(ss)
            s_conv.append(jnp.transpose(hs, (1, 0, 2)))
            cols, xq_blk = A_COLS, A_XQ_OFF // D_X
        else:
            proj_p = _matmul(xp, b_w[j], tm_p, 896, "in_proj_b")
            proj_s = _matmul(xs, b_w[j], tm_s, 896, "in_proj_b_s")
            mix_p = _sgu_chunked(proj_p.reshape(n_p, t_p, B_COLS), row(b_ln_g[j]), row(b_ln_b[j]),
                                 b_w_s[j], bs_b[j]).reshape(n_p * t_p, D_B)
            mix_s, v_s = _sgu_first(proj_s, row(b_ln_g[j]), row(b_ln_b[j]), w00_rows[j], b0_rows[j])
            s_sgu_v.append(v_s.reshape(n_s, 1, D_B))
            cols, xq_blk = B_COLS, B_XQ_OFF // D_X
        xo_p = _xattn(proj_p.reshape(n_p, t_p, cols), xq_blk, p_mem_k[i], p_mem_v[i], 512)
        xo_s = _xattn_step(proj_s, xq_blk, s_mem_k[i], s_mem_v[i])
        g1, b1, g2, b2 = row(ln1_g[i]), row(ln1_b[i]), row(ln2_g[i]), row(ln2_b[i])
        xp = _outproj_ln(mix_p, xo_p.reshape(n_p * t_p, D_X), xp, w_out_b[i], g1, b1, 512)
        xs = _outproj_ln(mix_s, xo_s, xs, w_out_b[i], g1, b1, tm_s)
        if i % 2 == 0:
            xp = _ffn_ln(xp, None, ffn_gu[j][None], ffn_dn[j][None], g2, b2, tm_p, 512)
            xs = _ffn_ln(xs, None, ffn_gu[j][None], ffn_dn[j][None], g2, b2, tm_s, 512)
        else:
            _, gates_p, idx_p = _router(xp, r_hi[j], r_lo[j], tm_p)
            comb_s, _, _ = _router(xs, r_hi[j], r_lo[j], tm_s)
            xp = _moe_routed_ln(xp, gates_p, idx_p, moe_gu[j], moe_dn[j], g2, b2)
            xs = _ffn_ln(xs, comb_s, moe_gu[j], moe_dn[j], g2, b2, tm_s, 512)

    mem_shape = (DEPTH, n_p, N_MEM, H_X, DH_X)
    return (xp.reshape(n_p, t_p, D_MODEL),
            xs.reshape(n_s, 1, D_MODEL),
            jnp.stack(p_gdn),
            jnp.stack(p_conv),
            p_mem_k.reshape(mem_shape),
            p_mem_v.reshape(mem_shape),
            jnp.stack(s_gdn),
            jnp.stack(s_conv),
            jnp.stack(s_sgu_v))
```

```python
import functools

import jax
import jax.numpy as jnp
from jax import lax
from jax.experimental import pallas as pl
from jax.experimental.pallas import tpu as pltpu

F32 = jnp.float32
BF16 = jnp.bfloat16

D_MODEL = 1024
DEPTH = 4
H_A = 6
DK_A = 128
DV_A = 128
QK_A = H_A * DK_A
V_A = H_A * DV_A
CONV_W = 4
CONV_DIM = 2 * QK_A + V_A
GDN_CHUNK = 64
G_B = 6
CH_B = 128
D_B = G_B * CH_B
CHUNK_B = 128
N_MEM = 256
H_X = 4
DH_X = 64
D_X = H_X * DH_X
D_FF = 3584
N_EXP = 8
ALPHA = (2 * DEPTH) ** 0.25
LN_EPS = 1e-5
RMS_EPS = 1e-6

LANES = 128
SUBLANES = 8
VMEM_LIMIT_BYTES = 48 * 1024 * 1024

A_Z_OFF = CONV_DIM
A_XQ_OFF = CONV_DIM + V_A
A_AB_OFF = A_XQ_OFF + D_X
A_COLS = A_AB_OFF + LANES
B_LANE = 8
B_XQ_OFF = 2 * D_B
B_COLS = 2 * D_B + D_X


def _params(*sem):
    return pltpu.CompilerParams(dimension_semantics=sem, vmem_limit_bytes=VMEM_LIMIT_BYTES)


def _split2(x):
    hi = x.astype(BF16)
    lo = (x - hi.astype(F32)).astype(BF16)
    return hi, lo


_NN = (((1,), (0,)), ((), ()))
_NT = (((1,), (1,)), ((), ()))
_TN = (((0,), (0,)), ((), ()))


def _dot(a, b, dims=_NN, passes=1):
    if passes == 1:
        return lax.dot_general(a.astype(BF16), b.astype(BF16), dims, preferred_element_type=F32)
    ah, al = _split2(a)
    bh, bl = _split2(b)
    dg = functools.partial(lax.dot_general, dimension_numbers=dims, preferred_element_type=F32)
    return dg(ah, bh) + (dg(ah, bl) + dg(al, bh))


def _dot_exact_lhs(lhs_bf16, x):
    x1 = x.astype(BF16)
    r1 = x - x1.astype(F32)
    x2 = r1.astype(BF16)
    x3 = (r1 - x2.astype(F32)).astype(BF16)
    dg = functools.partial(lax.dot_general, dimension_numbers=_NN, preferred_element_type=F32)
    return dg(lhs_bf16, x1) + (dg(lhs_bf16, x2) + dg(lhs_bf16, x3))


def _sigmoid(x):
    return 1.0 / (1.0 + jnp.exp(-x))


def _silu(x):
    return x * _sigmoid(x)


def _softplus(x):
    return jnp.maximum(x, 0.0) + jnp.log1p(jnp.exp(-jnp.abs(x)))


def _gelu_tanh(x):
    c = 0.7978845608028654
    return 0.5 * x * (1.0 + jnp.tanh(c * (x + 0.044715 * (x * x * x))))


def _layer_norm(y, g, b):
    mu = jnp.mean(y, axis=-1, keepdims=True)
    d = y - mu
    var = jnp.mean(d * d, axis=-1, keepdims=True)
    return d * lax.rsqrt(var + LN_EPS) * g + b


def _lane_pick(x, lane_iota, idx):
    return jnp.sum(jnp.where(lane_iota == idx, x, 0.0), axis=-1, keepdims=True)


def _mm_kernel(x_ref, w_ref, o_ref):
    o_ref[...] = jnp.dot(x_ref[...].astype(BF16), w_ref[...],
                         preferred_element_type=F32).astype(o_ref.dtype)


def _matmul(x, w, tm, tn, name):
    m, k = x.shape
    n = w.shape[1]
    return pl.pallas_call(
        _mm_kernel,
        grid=(m // tm, n // tn),
        in_specs=[pl.BlockSpec((tm, k), lambda i, j: (i, 0)),
                  pl.BlockSpec((k, tn), lambda i, j: (0, j))],
        out_specs=pl.BlockSpec((tm, tn), lambda i, j: (i, j)),
        out_shape=jax.ShapeDtypeStruct((m, n), F32),
        compiler_params=_params("parallel", "parallel"),
        name=name,
    )(x, w)


def _outproj_ln_kernel(mix_ref, xo_ref, x_ref, wm_ref, wx_ref, g_ref, b_ref, o_ref):
    h = jnp.dot(mix_ref[...].astype(BF16), wm_ref[...], preferred_element_type=F32)
    h = h + jnp.dot(xo_ref[...].astype(BF16), wx_ref[...], preferred_element_type=F32)
    o_ref[...] = _layer_norm(ALPHA * x_ref[...] + h, g_ref[...], b_ref[...])


def _outproj_ln(mix, xo, x, w_out, g, b, tm):
    m = x.shape[0]
    d_mix = mix.shape[1]
    n_blk = d_mix // D_X
    return pl.pallas_call(
        _outproj_ln_kernel,
        grid=(m // tm,),
        in_specs=[pl.BlockSpec((tm, d_mix), lambda i: (i, 0)),
                  pl.BlockSpec((tm, D_X), lambda i: (i, 0)),
                  pl.BlockSpec((tm, D_MODEL), lambda i: (i, 0)),
                  pl.BlockSpec((d_mix, D_MODEL), lambda i: (0, 0)),
                  pl.BlockSpec((D_X, D_MODEL), lambda i: (n_blk, 0)),
                  pl.BlockSpec((1, D_MODEL), lambda i: (0, 0)),
                  pl.BlockSpec((1, D_MODEL), lambda i: (0, 0))],
        out_specs=pl.BlockSpec((tm, D_MODEL), lambda i: (i, 0)),
        out_shape=jax.ShapeDtypeStruct((m, D_MODEL), F32),
        compiler_params=_params("parallel"),
        name="outproj_ln",
    )(mix, xo, x, w_out, w_out, g, b)


def _ffn_kernel(*refs, moe):
    if moe:
        x_ref, comb_ref, wg_ref, wu_ref, wd_ref, g_ref, b_ref, o_ref, xb_ref, acc_ref = refs
    else:
        x_ref, wg_ref, wu_ref, wd_ref, g_ref, b_ref, o_ref, xb_ref, acc_ref = refs
    e = pl.program_id(1)
    f = pl.program_id(2)
    first = jnp.logical_and(e == 0, f == 0)
    last = jnp.logical_and(e == pl.num_programs(1) - 1, f == pl.num_programs(2) - 1)

    @pl.when(first)
    def _():
        xb_ref[...] = x_ref[...].astype(BF16)
        acc_ref[...] = jnp.zeros_like(acc_ref)

    xb = xb_ref[...]
    hg = jnp.dot(xb, wg_ref[...], preferred_element_type=F32)
    hu = jnp.dot(xb, wu_ref[...], preferred_element_type=F32)
    h = _silu(hg) * hu
    if moe:
        comb = comb_ref[...]
        lane = lax.broadcasted_iota(jnp.int32, comb.shape, 1)
        h = h * _lane_pick(comb, lane, e)
    acc_ref[...] += jnp.dot(h.astype(BF16), wd_ref[...], preferred_element_type=F32)

    @pl.when(last)
    def _():
        o_ref[...] = _layer_norm(ALPHA * x_ref[...] + acc_ref[...], g_ref[...], b_ref[...])


def _ffn_ln(x, comb, w_gu, w_down, g, b, tm, tf):
    m = x.shape[0]
    moe = comb is not None
    n_e = w_gu.shape[0]
    n_f = D_FF // tf
    in_specs = [pl.BlockSpec((tm, D_MODEL), lambda i, e, f: (i, 0))]
    args = [x]
    if moe:
        in_specs.append(pl.BlockSpec((tm, LANES), lambda i, e, f: (i, 0)))
        args.append(comb)
    in_specs += [pl.BlockSpec((None, D_MODEL, tf), lambda i, e, f: (e, 0, f)),
                 pl.BlockSpec((None, D_MODEL, tf), lambda i, e, f: (e, 0, n_f + f)),
                 pl.BlockSpec((None, tf, D_MODEL), lambda i, e, f: (e, f, 0)),
                 pl.BlockSpec((1, D_MODEL), lambda i, e, f: (0, 0)),
                 pl.BlockSpec((1, D_MODEL), lambda i, e, f: (0, 0))]
    args += [w_gu, w_gu, w_down, g, b]
    return pl.pallas_call(
        functools.partial(_ffn_kernel, moe=moe),
        grid=(m // tm, n_e, n_f),
        in_specs=in_specs,
        out_specs=pl.BlockSpec((tm, D_MODEL), lambda i, e, f: (i, 0)),
        out_shape=jax.ShapeDtypeStruct((m, D_MODEL), F32),
        scratch_shapes=[pltpu.VMEM((tm, D_MODEL), BF16), pltpu.VMEM((tm, D_MODEL), F32)],
        compiler_params=_params("parallel", "arbitrary", "arbitrary"),
        name="moe_ln" if moe else "ffn_ln",
    )(*args)


def _router_kernel(x_ref, wh_ref, wl_ref, comb_ref, gate_ref, idx_ref):
    x = x_ref[...]
    xh, xl = _split2(x)
    wh = wh_ref[...]
    lg = jnp.dot(xh, wh, preferred_element_type=F32)
    lg = lg + (jnp.dot(xh, wl_ref[...], preferred_element_type=F32)
               + jnp.dot(xl, wh, preferred_element_type=F32))
    lane = lax.broadcasted_iota(jnp.int32, lg.shape, 1)
    neg = -jnp.inf
    lg = jnp.where(lane < N_EXP, lg, neg)
    m1 = jnp.max(lg, axis=-1, keepdims=True)
    i1 = jnp.min(jnp.where(lg == m1, lane, LANES), axis=-1, keepdims=True)
    lg2 = jnp.where(lane == i1, neg, lg)
    m2 = jnp.max(lg2, axis=-1, keepdims=True)
    i2 = jnp.min(jnp.where(lg2 == m2, lane, LANES), axis=-1, keepdims=True)
    e2 = jnp.exp(m2 - m1)
    den = 1.0 + e2
    g1 = 1.0 / den
    g2 = e2 / den
    comb_ref[...] = jnp.where(lane == i1, g1, 0.0) + jnp.where(lane == i2, g2, 0.0)
    gate_ref[...] = jnp.where(lane == 0, g1, jnp.where(lane == 1, g2, 0.0))
    idx_ref[...] = jnp.where(lane == 0, i1, jnp.where(lane == 1, i2, 0))


def _router(x, w_hi, w_lo, tm):
    m = x.shape[0]
    out = pl.BlockSpec((tm, LANES), lambda i: (i, 0))
    return pl.pallas_call(
        _router_kernel,
        grid=(m // tm,),
        in_specs=[pl.BlockSpec((tm, D_MODEL), lambda i: (i, 0)),
                  pl.BlockSpec((D_MODEL, LANES), lambda i: (0, 0)),
                  pl.BlockSpec((D_MODEL, LANES), lambda i: (0, 0))],
        out_specs=[out, out, out],
        out_shape=[jax.ShapeDtypeStruct((m, LANES), F32),
                   jax.ShapeDtypeStruct((m, LANES), F32),
                   jax.ShapeDtypeStruct((m, LANES), jnp.int32)],
        compiler_params=_params("parallel"),
        name="router",
    )(x, w_hi, w_lo)


MOE_TM = 512
MOE_TF = 512
DISPATCH_TOKENS = 1024
COMBINE_TOKENS = 256


def _moe_plan(idx2, tm):
    t = idx2.shape[0]
    e_flat = idx2.reshape(-1)
    experts = jnp.arange(N_EXP, dtype=jnp.int32)
    onehot = (e_flat[:, None] == experts[None, :]).astype(jnp.int32)
    csum = jnp.cumsum(onehot, axis=0)
    rank = jnp.sum(csum * onehot, axis=1) - 1
    counts = csum[-1]
    ptiles = (counts + tm - 1) // tm
    tile_end = jnp.cumsum(ptiles)
    pstart = (tile_end - ptiles) * tm
    pos = jnp.sum(onehot * pstart[None, :], axis=1) + rank
    n_used = tile_end[-1]
    n_tiles = (2 * t) // tm + N_EXP
    jj = jnp.minimum(jnp.arange(n_tiles, dtype=jnp.int32), n_used - 1)
    tile_expert = jnp.sum((jj[:, None] >= tile_end[None, :]).astype(jnp.int32), axis=1)
    seg_base = jnp.concatenate([pstart + counts, (n_used * tm).reshape(1)])
    seg_len = jnp.concatenate([ptiles * tm - counts, ((n_tiles - n_used) * tm).reshape(1)])
    seg_end = jnp.cumsum(seg_len)
    j = jnp.arange(N_EXP * tm, dtype=jnp.int32)
    seg = jnp.sum((j[:, None] >= seg_end[None, :]).astype(jnp.int32), axis=1)
    seg_1h = (seg[:, None] == jnp.arange(N_EXP + 1, dtype=jnp.int32)[None, :]).astype(jnp.int32)
    pad_dst = j + jnp.sum(seg_1h * (seg_base - (seg_end - seg_len))[None, :], axis=1)
    return (pos.astype(jnp.int32), tile_expert.astype(jnp.int32), n_used.reshape(1).astype(jnp.int32),
            pad_dst.astype(jnp.int32), n_tiles)


def _row_copy(src_ref, src_row, dst_ref, dst_row, sem):
    return pltpu.make_async_copy(src_ref.at[pl.ds(src_row, 1), :], dst_ref.at[pl.ds(dst_row, 1), :], sem)


def _wait_rows(hbm_ref, n_rows, sem):
    rows = hbm_ref.at[pl.ds(0, n_rows), :]
    pltpu.make_async_copy(rows, rows, sem).wait()


def _moe_dispatch_kernel(pos_ref, pad_dst_ref, x_ref, xs_ref, sem):
    tq = x_ref.shape[0]
    n_pad = pad_dst_ref.shape[1]

    def send(grp, carry):
        r0 = pl.multiple_of(grp * SUBLANES, SUBLANES)
        for jr in range(SUBLANES):
            for k in range(2):
                _row_copy(x_ref, r0 + jr, xs_ref, pos_ref[0, 2 * (r0 + jr) + k], sem).start()
        return carry

    lax.fori_loop(0, tq // SUBLANES, send, 0)

    def pad(grp, carry):
        r0 = pl.multiple_of(grp * SUBLANES, SUBLANES)
        for jr in range(SUBLANES):
            _row_copy(x_ref, jr, xs_ref, pad_dst_ref[0, r0 + jr], sem).start()
        return carry

    lax.fori_loop(0, n_pad // SUBLANES, pad, 0)
    _wait_rows(xs_ref, 2 * tq + n_pad, sem)


def _moe_dispatch(x, pos, pad_dst, n_rows_out):
    m = x.shape[0]
    tq = DISPATCH_TOKENS
    n_steps = m // tq
    n_pad = pad_dst.shape[0] // n_steps
    smem = functools.partial(pl.BlockSpec, memory_space=pltpu.SMEM)
    return pl.pallas_call(
        _moe_dispatch_kernel,
        grid=(n_steps,),
        in_specs=[smem((None, 1, 2 * tq), lambda i: (i, 0, 0)),
                  smem((None, 1, n_pad), lambda i: (i, 0, 0)),
                  pl.BlockSpec((tq, D_MODEL), lambda i: (i, 0))],
        out_specs=pl.BlockSpec(memory_space=pl.ANY),
        out_shape=jax.ShapeDtypeStruct((n_rows_out, D_MODEL), F32),
        scratch_shapes=[pltpu.SemaphoreType.DMA(())],
        compiler_params=_params("arbitrary"),
        name="moe_dispatch",
    )(pos.reshape(n_steps, 1, 2 * tq), pad_dst.reshape(n_steps, 1, n_pad), x)


def _moe_ffn_kernel(te_ref, nu_ref, x_ref, wg_ref, wu_ref, wd_ref, o_ref, xb_ref):
    p = pl.program_id(0)
    f = pl.program_id(1)

    @pl.when(p < nu_ref[0])
    def _():
        @pl.when(f == 0)
        def _():
            xb_ref[...] = x_ref[...].astype(BF16)

        xb = xb_ref[...]
        hg = jnp.dot(xb, wg_ref[...], preferred_element_type=F32)
        hu = jnp.dot(xb, wu_ref[...], preferred_element_type=F32)
        y = jnp.dot((_silu(hg) * hu).astype(BF16), wd_ref[...], preferred_element_type=F32)

        @pl.when(f == 0)
        def _():
            o_ref[...] = y

        @pl.when(f != 0)
        def _():
            o_ref[...] += y

    @pl.when(jnp.logical_and(p >= nu_ref[0], f == 0))
    def _():
        o_ref[...] = jnp.zeros_like(o_ref)


def _moe_ffn(xs, tile_expert, n_used, w_gu, w_down, tm, tf):
    n_tiles = xs.shape[0] // tm
    n_f = D_FF // tf
    tile = lambda p, nu: jnp.minimum(p, nu[0] - 1)
    col = lambda p, f, nu: jnp.where(p < nu[0], f, n_f - 1)
    grid_spec = pltpu.PrefetchScalarGridSpec(
        num_scalar_prefetch=2,
        grid=(n_tiles, n_f),
        in_specs=[pl.BlockSpec((tm, D_MODEL), lambda p, f, te, nu: (tile(p, nu), 0)),
                  pl.BlockSpec((None, D_MODEL, tf), lambda p, f, te, nu: (te[p], 0, col(p, f, nu))),
                  pl.BlockSpec((None, D_MODEL, tf), lambda p, f, te, nu: (te[p], 0, n_f + col(p, f, nu))),
                  pl.BlockSpec((None, tf, D_MODEL), lambda p, f, te, nu: (te[p], col(p, f, nu), 0))],
        out_specs=pl.BlockSpec((tm, D_MODEL), lambda p, f, te, nu: (p, 0)),
        scratch_shapes=[pltpu.VMEM((tm, D_MODEL), BF16)],
    )
    return pl.pallas_call(
        _moe_ffn_kernel,
        grid_spec=grid_spec,
        out_shape=jax.ShapeDtypeStruct(xs.shape, F32),
        compiler_params=_params("arbitrary", "arbitrary"),
        name="moe_ffn",
    )(tile_expert, n_used, xs, w_gu, w_gu, w_down)


def _moe_combine_ln_kernel(pos_ref, gate_ref, x_ref, ye_ref, g_ref, b_ref, o_ref, buf_ref, sem):
    tq = x_ref.shape[0]

    def fetch(grp, carry):
        r0 = pl.multiple_of(grp * SUBLANES, SUBLANES)
        for jr in range(SUBLANES):
            for k in range(2):
                _row_copy(ye_ref, pos_ref[0, 2 * (r0 + jr) + k], buf_ref, k * tq + r0 + jr, sem).start()
        return carry

    lax.fori_loop(0, tq // SUBLANES, fetch, 0)
    _wait_rows(ye_ref, 2 * tq, sem)
    gates = gate_ref[...]
    lane = lax.broadcasted_iota(jnp.int32, gates.shape, 1)
    y = _lane_pick(gates, lane, 0) * buf_ref[0:tq, :] + _lane_pick(gates, lane, 1) * buf_ref[tq:2 * tq, :]
    o_ref[...] = _layer_norm(ALPHA * x_ref[...] + y, g_ref[...], b_ref[...])


def _moe_combine_ln(x, gates, pos, ye, g, b):
    m = x.shape[0]
    tq = COMBINE_TOKENS
    n_steps = m // tq
    return pl.pallas_call(
        _moe_combine_ln_kernel,
        grid=(n_steps,),
        in_specs=[pl.BlockSpec((None, 1, 2 * tq), lambda i: (i, 0, 0), memory_space=pltpu.SMEM),
                  pl.BlockSpec((tq, LANES), lambda i: (i, 0)),
                  pl.BlockSpec((tq, D_MODEL), lambda i: (i, 0)),
                  pl.BlockSpec(memory_space=pl.ANY),
                  pl.BlockSpec((1, D_MODEL), lambda i: (0, 0)),
                  pl.BlockSpec((1, D_MODEL), lambda i: (0, 0))],
        out_specs=pl.BlockSpec((tq, D_MODEL), lambda i: (i, 0)),
        out_shape=jax.ShapeDtypeStruct((m, D_MODEL), F32),
        scratch_shapes=[pltpu.VMEM((2 * tq, D_MODEL), F32), pltpu.SemaphoreType.DMA(())],
        compiler_params=_params("arbitrary"),
        name="moe_combine_ln",
    )(pos.reshape(n_steps, 1, 2 * tq), gates, x, ye, g, b)


def _moe_routed_ln(x, gates, idx, w_gu, w_down, g, b):
    pos, tile_expert, n_used, pad_dst, n_tiles = _moe_plan(idx[:, :2], MOE_TM)
    xs = _moe_dispatch(x, pos, pad_dst, n_tiles * MOE_TM)
    ye = _moe_ffn(xs, tile_expert, n_used, w_gu, w_down, MOE_TM, MOE_TF)
    return _moe_combine_ln(x, gates, pos, ye, g, b)


GDN_PASSES_QK = 1
GDN_PASSES_SOLVE = 3
GDN_PASSES_SOLVE_HIGH = 1
GDN_PASSES_STATE = 1
INV_BLOCK_SHIFT = 4
GDN_SEQS = 2


def _gdn_chunk_kernel(qkv_ref, z_ref, ab_ref, cw_ref, alog_ref, dtb_ref, nw_ref, hist0_ref, s0_ref,
                      mix_ref, sout_ref, hout_ref, xh_ref, s_ref):
    n = pl.program_id(1)
    c = GDN_CHUNK
    hrows = SUBLANES
    n_seq = qkv_ref.shape[0]
    items = [(b, h) for b in range(n_seq) for h in range(H_A)]

    @pl.when(n == 0)
    def _():
        xh_ref[:, 0:hrows, :] = hist0_ref[...]
        s_ref[...] = s0_ref[...]

    xh_ref[:, hrows:hrows + c, :] = qkv_ref[...]

    row = lax.broadcasted_iota(jnp.int32, (c, c), 0)
    col = lax.broadcasted_iota(jnp.int32, (c, c), 1)
    causal = row >= col
    strict = row > col
    eye = jnp.where(row == col, 1.0, 0.0)
    blockdiag = (row >> INV_BLOCK_SHIFT) == (col >> INV_BLOCK_SHIFT)
    tril_ones = jnp.where(causal, 1.0, 0.0).astype(BF16)
    lane = lax.broadcasted_iota(jnp.int32, (c, LANES), 1)
    mm_s = functools.partial(_dot, passes=GDN_PASSES_SOLVE)
    mm_h = functools.partial(_dot, passes=GDN_PASSES_SOLVE_HIGH)

    beta_all, gcum_all, gcum_t = [], [], []
    for b in range(n_seq):
        ab = ab_ref[b]
        g_all = -jnp.exp(alog_ref[...]) * _softplus(ab + dtb_ref[...])
        beta_all.append(_sigmoid(ab))
        gc = _dot_exact_lhs(tril_ones, g_all)
        gcum_all.append(gc)
        gcum_t.append(gc.T)

    def conv_silu(b, c0):
        acc = xh_ref[b, hrows - 3:hrows - 3 + c, c0:c0 + LANES] * cw_ref[0:1, c0:c0 + LANES]
        for j in range(1, CONV_W):
            acc = acc + (xh_ref[b, hrows - 3 + j:hrows - 3 + j + c, c0:c0 + LANES]
                         * cw_ref[j:j + 1, c0:c0 + LANES])
        return _silu(acc)

    q, k, v, kb, egc, gc_col, gc_last, beta_col, decay = {}, {}, {}, {}, {}, {}, {}, {}, {}
    for it in items:
        b, h = it
        qi = conv_silu(b, h * DK_A)
        ki = conv_silu(b, QK_A + h * DK_A)
        v[it] = conv_silu(b, 2 * QK_A + h * DV_A)
        q[it] = qi * lax.rsqrt(jnp.sum(qi * qi, axis=-1, keepdims=True) + RMS_EPS) * (DK_A ** -0.5)
        k[it] = ki * lax.rsqrt(jnp.sum(ki * ki, axis=-1, keepdims=True) + RMS_EPS)
        gc_col[it] = _lane_pick(gcum_all[b], lane, h)
        beta_col[it] = _lane_pick(beta_all[b], lane, B_LANE + h)
        gc_last[it] = gc_col[it][c - 1:c, :]
        decay[it] = jnp.where(causal, jnp.exp(gc_col[it] - gcum_t[b][h:h + 1, :c]), 0.0)
        egc[it] = jnp.exp(gc_col[it])
        kb[it] = k[it] * beta_col[it]

    kk = {it: _dot(jnp.concatenate([kb[it], q[it]], axis=0), k[it], _NT, GDN_PASSES_QK) for it in items}
    a_mat = {it: jnp.where(strict, kk[it][:c] * decay[it], 0.0) for it in items}
    attn = {it: kk[it][c:] * decay[it] for it in items}

    a_d = {it: jnp.where(blockdiag, a_mat[it], 0.0) for it in items}
    a_n = {it: a_mat[it] - a_d[it] for it in items}
    p2 = {it: mm_s(a_d[it], a_d[it]) for it in items}
    d_inv = {it: eye - a_d[it] for it in items}
    p4 = {it: mm_h(p2[it], p2[it]) for it in items}
    d_inv = {it: d_inv[it] + mm_s(d_inv[it], p2[it]) for it in items}
    p8 = {it: mm_h(p4[it], p4[it]) for it in items}
    d_inv = {it: d_inv[it] + mm_h(d_inv[it], p4[it]) for it in items}
    d_inv = {it: d_inv[it] + mm_h(d_inv[it], p8[it]) for it in items}
    m_blk = {it: mm_s(d_inv[it], a_n[it]) for it in items}
    rhs = {it: mm_s(d_inv[it], jnp.concatenate([v[it] * beta_col[it], kb[it] * egc[it]], axis=1))
           for it in items}
    m2 = {it: mm_s(m_blk[it], m_blk[it]) for it in items}
    y = {it: eye - m_blk[it] for it in items}
    y = {it: y[it] + mm_s(y[it], m2[it]) for it in items}
    sol = {it: mm_s(y[it], rhs[it]) for it in items}

    s_old = {it: s_ref[it[0], it[1]] for it in items}
    wq = {it: _dot(jnp.concatenate([sol[it][:, DV_A:], q[it] * egc[it]], axis=0), s_old[it],
                   _NN, GDN_PASSES_STATE) for it in items}
    v_new = {it: sol[it][:, :DV_A] - wq[it][:c] for it in items}
    o = {it: wq[it][c:] + _dot(attn[it], v_new[it], _NN, GDN_PASSES_STATE) for it in items}
    for it in items:
        k_dec = k[it] * jnp.exp(gc_last[it] - gc_col[it])
        s_ref[it[0], it[1]] = (s_old[it] * jnp.exp(gc_last[it])
                               + _dot(k_dec, v_new[it], _TN, GDN_PASSES_STATE))
    for it in items:
        b, h = it
        oi = o[it]
        oi = oi * lax.rsqrt(jnp.mean(oi * oi, axis=-1, keepdims=True) + RMS_EPS) * nw_ref[...]
        mix_ref[b, :, h * DV_A:(h + 1) * DV_A] = oi * _silu(z_ref[b, :, h * DV_A:(h + 1) * DV_A])

    xh_ref[:, 0:hrows, :] = xh_ref[:, c:c + hrows, :]

    @pl.when(n == pl.num_programs(1) - 1)
    def _():
        sout_ref[...] = s_ref[...]
        hout_ref[...] = xh_ref[:, c:c + hrows, :]


def _gdn_chunked(proj, conv_w, alog_row, dtb_row, nw_row, hist0, s0):
    bsz, t, _ = proj.shape
    c = GDN_CHUNK
    nb = GDN_SEQS
    n_chunks = t // c
    row_spec = lambda width, blk: pl.BlockSpec((nb, c, width), lambda b, n: (b, n, blk))
    full2 = lambda shape: pl.BlockSpec(shape, lambda b, n: (0, 0))
    return pl.pallas_call(
        _gdn_chunk_kernel,
        grid=(bsz // nb, n_chunks),
        in_specs=[row_spec(CONV_DIM, 0),
                  row_spec(V_A, A_Z_OFF // V_A),
                  row_spec(LANES, A_AB_OFF // LANES),
                  full2((CONV_W, CONV_DIM)),
                  full2((1, LANES)), full2((1, LANES)), full2((1, DV_A)),
                  pl.BlockSpec((nb, SUBLANES, CONV_DIM), lambda b, n: (b, 0, 0)),
                  pl.BlockSpec((nb, H_A, DK_A, DV_A), lambda b, n: (b, 0, 0, 0))],
        out_specs=[pl.BlockSpec((nb, c, V_A), lambda b, n: (b, n, 0)),
                   pl.BlockSpec((nb, H_A, DK_A, DV_A), lambda b, n: (b, 0, 0, 0)),
                   pl.BlockSpec((nb, SUBLANES, CONV_DIM), lambda b, n: (b, 0, 0))],
        out_shape=[jax.ShapeDtypeStruct((bsz, t, V_A), F32),
                   jax.ShapeDtypeStruct((bsz, H_A, DK_A, DV_A), F32),
                   jax.ShapeDtypeStruct((bsz, SUBLANES, CONV_DIM), F32)],
        scratch_shapes=[pltpu.VMEM((nb, SUBLANES + c, CONV_DIM), F32),
                        pltpu.VMEM((nb, H_A, DK_A, DV_A), F32)],
        compiler_params=_params("parallel", "arbitrary"),
        name="gdn_chunked",
    )(proj, proj, proj, conv_w, alog_row, dtb_row, nw_row, hist0, s0)


def _gdn_step_kernel(qkv_ref, z_ref, ab_ref, cw_ref, alog_ref, dtb_ref, nw_ref, hist_ref, s_ref,
                     mix_ref, sout_ref, hout_ref, o_scr):
    rows = qkv_ref.shape[0]
    new = qkv_ref[...]
    y = hist_ref[0] * cw_ref[0:1, :]
    for j in range(1, CONV_W - 1):
        y = y + hist_ref[j] * cw_ref[j:j + 1, :]
    y = y + new * cw_ref[CONV_W - 1:CONV_W, :]
    qkv = _silu(y)
    for j in range(CONV_W - 2):
        hout_ref[j] = hist_ref[j + 1]
    hout_ref[CONV_W - 2] = new

    ab = ab_ref[...]
    lane = lax.broadcasted_iota(jnp.int32, ab.shape, 1)
    g_all = -jnp.exp(alog_ref[...]) * _softplus(ab + dtb_ref[...])
    beta_all = _sigmoid(ab)
    ri = lax.broadcasted_iota(jnp.int32, (DK_A, DK_A), 0)
    ci = lax.broadcasted_iota(jnp.int32, (DK_A, DK_A), 1)
    diag = ri == ci

    def as_column(r):
        return jnp.sum(jnp.where(diag, r, 0.0), axis=-1, keepdims=True)

    for h in range(H_A):
        q = qkv[:, h * DK_A:(h + 1) * DK_A]
        k = qkv[:, QK_A + h * DK_A:QK_A + (h + 1) * DK_A]
        v = qkv[:, 2 * QK_A + h * DV_A:2 * QK_A + (h + 1) * DV_A]
        q = q * lax.rsqrt(jnp.sum(q * q, axis=-1, keepdims=True) + RMS_EPS) * (DK_A ** -0.5)
        k = k * lax.rsqrt(jnp.sum(k * k, axis=-1, keepdims=True) + RMS_EPS)
        eg = jnp.exp(_lane_pick(g_all, lane, h))
        beta = _lane_pick(beta_all, lane, B_LANE + h)
        for s in range(rows):
            k_col = as_column(k[s:s + 1, :])
            q_col = as_column(q[s:s + 1, :])
            st = s_ref[s, h] * eg[s:s + 1, :]
            ks = jnp.sum(k_col * st, axis=0, keepdims=True)
            u = beta[s:s + 1, :] * (v[s:s + 1, :] - ks)
            st = st + k_col * u
            sout_ref[s, h] = st
            o_scr[s:s + 1, h * DV_A:(h + 1) * DV_A] = jnp.sum(q_col * st, axis=0, keepdims=True)

    for h in range(H_A):
        o = o_scr[:, h * DV_A:(h + 1) * DV_A]
        o = o * lax.rsqrt(jnp.mean(o * o, axis=-1, keepdims=True) + RMS_EPS) * nw_ref[...]
        mix_ref[:, h * DV_A:(h + 1) * DV_A] = o * _silu(z_ref[:, h * DV_A:(h + 1) * DV_A])


def _gdn_step(proj, conv_w, alog_row, dtb_row, nw_row, hist_t, s0, layer):
    bsz = proj.shape[0]
    rows = SUBLANES
    full1 = lambda shape: pl.BlockSpec(shape, lambda i: (0, 0))
    return pl.pallas_call(
        _gdn_step_kernel,
        grid=(bsz // rows,),
        in_specs=[pl.BlockSpec((rows, CONV_DIM), lambda i: (i, 0)),
                  pl.BlockSpec((rows, V_A), lambda i: (i, A_Z_OFF // V_A)),
                  pl.BlockSpec((rows, LANES), lambda i: (i, A_AB_OFF // LANES)),
                  full1((CONV_W, CONV_DIM)),
                  full1((1, LANES)), full1((1, LANES)), full1((1, DV_A)),
                  pl.BlockSpec((None, CONV_W - 1, rows, CONV_DIM), lambda i: (layer, 0, i, 0)),
                  pl.BlockSpec((None, rows, H_A, DK_A, DV_A), lambda i: (layer, i, 0, 0, 0))],
        out_specs=[pl.BlockSpec((rows, V_A), lambda i: (i, 0)),
                   pl.BlockSpec((rows, H_A, DK_A, DV_A), lambda i: (i, 0, 0, 0)),
                   pl.BlockSpec((CONV_W - 1, rows, CONV_DIM), lambda i: (0, i, 0))],
        out_shape=[jax.ShapeDtypeStruct((bsz, V_A), F32),
                   jax.ShapeDtypeStruct((bsz, H_A, DK_A, DV_A), F32),
                   jax.ShapeDtypeStruct((CONV_W - 1, bsz, CONV_DIM), F32)],
        scratch_shapes=[pltpu.VMEM((rows, V_A), F32)],
        compiler_params=_params("parallel"),
        name="gdn_step",
    )(proj, proj, proj, conv_w, alog_row, dtb_row, nw_row, hist_t, s0)


def _sgu_chunk_kernel(u_ref, v_ref, g_ref, b_ref, ws_ref, bs_ref, mix_ref):
    c = CHUNK_B
    row = lax.broadcasted_iota(jnp.int32, (c, c), 0)
    col = lax.broadcasted_iota(jnp.int32, (c, c), 1)
    u = _gelu_tanh(u_ref[...])
    v = _layer_norm(_gelu_tanh(v_ref[...]), g_ref[...], b_ref[...])
    for gi in range(G_B):
        ws = jnp.where(row >= col, ws_ref[gi], 0.0)
        mixed = _dot(ws, v[:, gi * CH_B:(gi + 1) * CH_B]) + bs_ref[gi]
        mix_ref[:, gi * CH_B:(gi + 1) * CH_B] = u[:, gi * CH_B:(gi + 1) * CH_B] * mixed


def _sgu_chunked(proj, ln_g, ln_b, w_s, bs_b):
    bsz, t, _ = proj.shape
    c = CHUNK_B
    return pl.pallas_call(
        _sgu_chunk_kernel,
        grid=(bsz, t // c),
        in_specs=[pl.BlockSpec((None, c, D_B), lambda b, n: (b, n, 0)),
                  pl.BlockSpec((None, c, D_B), lambda b, n: (b, n, 1)),
                  pl.BlockSpec((1, D_B), lambda b, n: (0, 0)),
                  pl.BlockSpec((1, D_B), lambda b, n: (0, 0)),
                  pl.BlockSpec((G_B, c, c), lambda b, n: (0, 0, 0)),
                  pl.BlockSpec((G_B, c, CH_B), lambda b, n: (0, 0, 0))],
        out_specs=pl.BlockSpec((None, c, D_B), lambda b, n: (b, n, 0)),
        out_shape=jax.ShapeDtypeStruct((bsz, t, D_B), F32),
        compiler_params=_params("parallel", "parallel"),
        name="sgu_chunked",
    )(proj, proj, ln_g, ln_b, w_s, bs_b)


def _sgu_first_kernel(u_ref, v_ref, g_ref, b_ref, w00_ref, b0_ref, mix_ref, vout_ref):
    u = _gelu_tanh(u_ref[...])
    v = _layer_norm(_gelu_tanh(v_ref[...]), g_ref[...], b_ref[...])
    vout_ref[...] = v
    mix_ref[...] = u * (w00_ref[...] * v + b0_ref[...])


def _sgu_first(proj, ln_g, ln_b, w00_row, b0_row):
    bsz = proj.shape[0]
    row = lambda: pl.BlockSpec((1, D_B), lambda i: (0, 0))
    return pl.pallas_call(
        _sgu_first_kernel,
        grid=(1,),
        in_specs=[pl.BlockSpec((bsz, D_B), lambda i: (0, 0)),
                  pl.BlockSpec((bsz, D_B), lambda i: (0, 1)),
                  row(), row(), row(), row()],
        out_specs=[pl.BlockSpec((bsz, D_B), lambda i: (0, 0)),
                   pl.BlockSpec((bsz, D_B), lambda i: (0, 0))],
        out_shape=[jax.ShapeDtypeStruct((bsz, D_B), F32),
                   jax.ShapeDtypeStruct((bsz, D_B), F32)],
        compiler_params=_params("arbitrary"),
        name="sgu_first",
    )(proj, proj, ln_g, ln_b, w00_row, b0_row)


def _head_masks(shape):
    lane = lax.broadcasted_iota(jnp.int32, shape, len(shape) - 1)
    return [(lane // DH_X) == h for h in range(H_X)]


def _xattn_kernel(q_ref, k_ref, v_ref, o_ref):
    q = q_ref[...]
    kb = k_ref[...].astype(BF16)
    vb = v_ref[...].astype(BF16)
    masks = _head_masks(q.shape)
    out = jnp.zeros(q.shape, F32)
    for h in range(H_X):
        qh = jnp.where(masks[h], q, 0.0)
        s = _dot(qh, kb, _NT) * (DH_X ** -0.5)
        s = s - jnp.max(s, axis=-1, keepdims=True)
        p = jnp.exp(s)
        p = p / jnp.sum(p, axis=-1, keepdims=True)
        out = out + jnp.where(masks[h], _dot(p, vb), 0.0)
    o_ref[...] = out


def _xattn(proj, xq_blk, mem_k, mem_v, layer, tq):
    bsz, t, _ = proj.shape
    return pl.pallas_call(
        _xattn_kernel,
        grid=(bsz, t // tq),
        in_specs=[pl.BlockSpec((None, tq, D_X), lambda b, i: (b, i, xq_blk)),
                  pl.BlockSpec((None, None, N_MEM, D_X), lambda b, i: (layer, b, 0, 0)),
                  pl.BlockSpec((None, None, N_MEM, D_X), lambda b, i: (layer, b, 0, 0))],
        out_specs=pl.BlockSpec((None, tq, D_X), lambda b, i: (b, i, 0)),
        out_shape=jax.ShapeDtypeStruct((bsz, t, D_X), F32),
        compiler_params=_params("parallel", "parallel"),
        name="xattn",
    )(proj, mem_k, mem_v)


def _xattn_step_kernel(q_ref, k_ref, v_ref, o_ref):
    rows = q_ref.shape[0]
    q = q_ref[...]
    hrow = lax.broadcasted_iota(jnp.int32, (SUBLANES, D_X), 0)
    hlane = lax.broadcasted_iota(jnp.int32, (SUBLANES, D_X), 1) // DH_X
    sel = hrow == hlane
    for s in range(rows):
        qh = jnp.where(sel, q[s:s + 1, :], 0.0)
        sc = _dot(qh, k_ref[s], _NT) * (DH_X ** -0.5)
        sc = sc - jnp.max(sc, axis=-1, keepdims=True)
        p = jnp.exp(sc)
        p = p / jnp.sum(p, axis=-1, keepdims=True)
        pv = _dot(p, v_ref[s])
        o_ref[s:s + 1, :] = jnp.sum(jnp.where(sel, pv, 0.0), axis=0, keepdims=True)


def _xattn_step(proj, xq_blk, mem_k, mem_v, layer):
    bsz = proj.shape[0]
    rows = SUBLANES
    return pl.pallas_call(
        _xattn_step_kernel,
        grid=(bsz // rows,),
        in_specs=[pl.BlockSpec((rows, D_X), lambda i: (i, xq_blk)),
                  pl.BlockSpec((None, rows, N_MEM, D_X), lambda i: (layer, i, 0, 0)),
                  pl.BlockSpec((None, rows, N_MEM, D_X), lambda i: (layer, i, 0, 0))],
        out_specs=pl.BlockSpec((rows, D_X), lambda i: (i, 0)),
        out_shape=jax.ShapeDtypeStruct((bsz, D_X), F32),
        compiler_params=_params("parallel"),
        name="xattn_step",
    )(proj, mem_k, mem_v)


def _pad_lanes(v):
    return jnp.zeros((1, LANES), F32).at[0, :v.shape[0]].set(v.astype(F32))


def _prep_weights(a_w_in, a_A_log, a_dt_bias, b_w_s, b_b_s, moe_router):
    qkvz = a_w_in[:, :, :A_XQ_OFF]
    a_cols = a_w_in[:, :, A_XQ_OFF:A_XQ_OFF + H_A]
    b_cols = a_w_in[:, :, A_XQ_OFF + H_A:A_XQ_OFF + 2 * H_A]
    xq = a_w_in[:, :, A_XQ_OFF + 2 * H_A:]
    ab = jnp.zeros(a_w_in.shape[:2] + (LANES,), F32)
    ab = ab.at[:, :, :H_A].set(a_cols).at[:, :, B_LANE:B_LANE + H_A].set(b_cols)
    a_w = jnp.concatenate([qkvz, xq, ab], axis=-1).astype(BF16)
    alog_rows = [_pad_lanes(a_A_log[j]) for j in range(a_A_log.shape[0])]
    dtb_rows = [_pad_lanes(a_dt_bias[j]) for j in range(a_dt_bias.shape[0])]
    bs_b = jnp.broadcast_to(b_b_s[..., None], b_b_s.shape + (CH_B,)).astype(F32)
    w00_rows = jnp.repeat(b_w_s[:, :, 0, 0], CH_B, axis=-1)[:, None, :]
    b0_rows = jnp.repeat(b_b_s[:, :, 0], CH_B, axis=-1)[:, None, :]
    r = jnp.zeros(moe_router.shape[:2] + (LANES,), F32).at[:, :, :N_EXP].set(moe_router)
    r_hi = r.astype(BF16)
    r_lo = (r - r_hi.astype(F32)).astype(BF16)
    return a_w, alog_rows, dtb_rows, bs_b, w00_rows, b0_rows, r_hi, r_lo


def kernel(x_prompt, x_sample, state_gdn, state_conv, cache_mem_k, cache_mem_v, mem_prompt,
           a_w_in, a_conv_w, a_A_log, a_dt_bias, a_norm_w,
           b_w_in, b_ln_g, b_ln_b, b_w_s, b_b_s,
           w_mem_kv, w_out, ln1_g, ln1_b, ln2_g, ln2_b,
           ffn_w_gu, ffn_w_down, moe_router, moe_w_gu, moe_w_down):
    n_p, t_p, _ = x_prompt.shape
    n_s = x_sample.shape[0]
    n_a = a_w_in.shape[0]

    a_w, alog_rows, dtb_rows, bs_b, w00_rows, b0_rows, r_hi, r_lo = _prep_weights(
        a_w_in, a_A_log, a_dt_bias, b_w_s, b_b_s, moe_router)
    row = lambda v: v.reshape(1, -1).astype(F32)

    w_kv = jnp.transpose(w_mem_kv, (1, 0, 2)).reshape(D_MODEL, DEPTH * 2 * D_X).astype(BF16)
    kv = _matmul(mem_prompt.reshape(n_p * N_MEM, D_MODEL), w_kv, 1024, 1024, "mem_kv")
    kv = kv.reshape(n_p, N_MEM, DEPTH, 2, D_X)
    p_mem_k = jnp.transpose(kv[:, :, :, 0, :], (2, 0, 1, 3))
    p_mem_v = jnp.transpose(kv[:, :, :, 1, :], (2, 0, 1, 3))
    s_mem_k = cache_mem_k.reshape(DEPTH, n_s, N_MEM, D_X)
    s_mem_v = cache_mem_v.reshape(DEPTH, n_s, N_MEM, D_X)

    xp = x_prompt.reshape(n_p * t_p, D_MODEL)
    xs = x_sample.reshape(n_s, D_MODEL)
    p_hist0 = jnp.zeros((n_p, SUBLANES, CONV_DIM), F32)
    p_s0 = jnp.zeros((n_p, H_A, DK_A, DV_A), F32)
    s_hist_t = jnp.transpose(state_conv, (0, 2, 1, 3))

    p_gdn, p_conv, s_gdn, s_conv, s_sgu_v = [], [], [], [], []
    tm_p, tm_s = 1024, n_s
    for i in range(DEPTH):
        j = i // 2
        if i % 2 == 0:
            proj_p = _matmul(xp, a_w[j], tm_p, 1152, "in_proj_a")
            proj_s = _matmul(xs, a_w[j], tm_s, 1152, "in_proj_a_s")
            nw = row(a_norm_w[j])
            mix_p, sp, hp = _gdn_chunked(proj_p.reshape(n_p, t_p, A_COLS), a_conv_w[j],
                                         alog_rows[j], dtb_rows[j], nw, p_hist0, p_s0)
            mix_p = mix_p.reshape(n_p * t_p, V_A)
            p_gdn.append(sp)
            p_conv.append(hp[:, SUBLANES - (CONV_W - 1):, :])
            mix_s, ss, hs = _gdn_step(proj_s, a_conv_w[j], alog_rows[j], dtb_rows[j], nw,
                                      s_hist_t, state_gdn, j)
            s_gdn.append(ss)
            s_conv.append(jnp.transpose(hs, (1, 0, 2)))
            cols, xq_blk = A_COLS, A_XQ_OFF // D_X
        else:
            b_w = b_w_in[j].astype(BF16)
            proj_p = _matmul(xp, b_w, tm_p, 896, "in_proj_b")
            proj_s = _matmul(xs, b_w, tm_s, 896, "in_proj_b_s")
            mix_p = _sgu_chunked(proj_p.reshape(n_p, t_p, B_COLS), row(b_ln_g[j]), row(b_ln_b[j]),
                                 b_w_s[j], bs_b[j]).reshape(n_p * t_p, D_B)
            mix_s, v_s = _sgu_first(proj_s, row(b_ln_g[j]), row(b_ln_b[j]), w00_rows[j], b0_rows[j])
            s_sgu_v.append(v_s.reshape(n_s, 1, D_B))
            cols, xq_blk = B_COLS, B_XQ_OFF // D_X
        xo_p = _xattn(proj_p.reshape(n_p, t_p, cols), xq_blk, p_mem_k, p_mem_v, i, 512)
        xo_s = _xattn_step(proj_s, xq_blk, s_mem_k, s_mem_v, i)
        g1, b1, g2, b2 = row(ln1_g[i]), row(ln1_b[i]), row(ln2_g[i]), row(ln2_b[i])
        w_o = w_out[i].astype(BF16)
        xp = _outproj_ln(mix_p, xo_p.reshape(n_p * t_p, D_X), xp, w_o, g1, b1, 512)
        xs = _outproj_ln(mix_s, xo_s, xs, w_o, g1, b1, tm_s)
        if i % 2 == 0:
            w_gu = ffn_w_gu[j].astype(BF16)[None]
            w_dn = ffn_w_down[j].astype(BF16)[None]
            xp = _ffn_ln(xp, None, w_gu, w_dn, g2, b2, tm_p, 512)
            xs = _ffn_ln(xs, None, w_gu, w_dn, g2, b2, tm_s, 512)
        else:
            _, gates_p, idx_p = _router(xp, r_hi[j], r_lo[j], tm_p)
            comb_s, _, _ = _router(xs, r_hi[j], r_lo[j], tm_s)
            w_gu = moe_w_gu[j].astype(BF16)
            w_dn = moe_w_down[j].astype(BF16)
            xp = _moe_routed_ln(xp, gates_p, idx_p, w_gu, w_dn, g2, b2)
            xs = _ffn_ln(xs, comb_s, w_gu, w_dn, g2, b2, tm_s, 512)

    mem_shape = (DEPTH, n_p, N_MEM, H_X, DH_X)
    return (xp.reshape(n_p, t_p, D_MODEL),
            xs.reshape(n_s, 1, D_MODEL),
            jnp.stack(p_gdn),
            jnp.stack(p_conv),
            p_mem_k.reshape(mem_shape),
            p_mem_v.reshape(mem_shape),
            jnp.stack(s_gdn),
            jnp.stack(s_conv),
            jnp.stack(s_sgu_v))
```

```python
import functools

import jax
import jax.numpy as jnp
from jax import lax
from jax.experimental import pallas as pl
from jax.experimental.pallas import tpu as pltpu

F32 = jnp.float32
BF16 = jnp.bfloat16

D_MODEL = 1024
DEPTH = 4
H_A = 6
DK_A = 128
DV_A = 128
QK_A = H_A * DK_A
V_A = H_A * DV_A
CONV_W = 4
CONV_DIM = 2 * QK_A + V_A
GDN_CHUNK = 64
G_B = 6
CH_B = 128
D_B = G_B * CH_B
CHUNK_B = 128
N_MEM = 256
H_X = 4
DH_X = 64
D_X = H_X * DH_X
D_FF = 3584
N_EXP = 8
ALPHA = (2 * DEPTH) ** 0.25
LN_EPS = 1e-5
RMS_EPS = 1e-6

LANES = 128
SUBLANES = 8
VMEM_LIMIT_BYTES = 56 * 1024 * 1024

A_Z_OFF = CONV_DIM
A_XQ_OFF = CONV_DIM + V_A
A_AB_OFF = A_XQ_OFF + D_X
A_COLS = A_AB_OFF + LANES
B_LANE = 8
B_XQ_OFF = 2 * D_B
B_COLS = 2 * D_B + D_X


def _params(*sem):
    return pltpu.CompilerParams(dimension_semantics=sem, vmem_limit_bytes=VMEM_LIMIT_BYTES)


def _split2(x):
    hi = x.astype(BF16)
    lo = (x - hi.astype(F32)).astype(BF16)
    return hi, lo


_NN = (((1,), (0,)), ((), ()))
_NT = (((1,), (1,)), ((), ()))
_TN = (((0,), (0,)), ((), ()))


def _dot(a, b, dims=_NN, passes=1):
    if passes == 1:
        return lax.dot_general(a.astype(BF16), b.astype(BF16), dims, preferred_element_type=F32)
    ah, al = _split2(a)
    bh, bl = _split2(b)
    dg = functools.partial(lax.dot_general, dimension_numbers=dims, preferred_element_type=F32)
    return dg(ah, bh) + (dg(ah, bl) + dg(al, bh))


def _dot_exact_lhs(lhs_bf16, x):
    x1 = x.astype(BF16)
    r1 = x - x1.astype(F32)
    x2 = r1.astype(BF16)
    x3 = (r1 - x2.astype(F32)).astype(BF16)
    dg = functools.partial(lax.dot_general, dimension_numbers=_NN, preferred_element_type=F32)
    return dg(lhs_bf16, x1) + (dg(lhs_bf16, x2) + dg(lhs_bf16, x3))


def _sigmoid(x):
    return 1.0 / (1.0 + jnp.exp(-x))


def _silu(x):
    return x * _sigmoid(x)


def _softplus(x):
    return jnp.maximum(x, 0.0) + jnp.log1p(jnp.exp(-jnp.abs(x)))


def _gelu_tanh(x):
    c = 0.7978845608028654
    return 0.5 * x * (1.0 + jnp.tanh(c * (x + 0.044715 * (x * x * x))))


def _layer_norm(y, g, b):
    mu = jnp.mean(y, axis=-1, keepdims=True)
    d = y - mu
    var = jnp.mean(d * d, axis=-1, keepdims=True)
    return d * lax.rsqrt(var + LN_EPS) * g + b


def _lane_pick(x, lane_iota, idx):
    return jnp.sum(jnp.where(lane_iota == idx, x, 0.0), axis=-1, keepdims=True)


def _mm_kernel(x_ref, w_ref, o_ref):
    o_ref[...] = jnp.dot(x_ref[...].astype(BF16), w_ref[...],
                         preferred_element_type=F32).astype(o_ref.dtype)


def _matmul(x, w, tm, tn, name):
    m, k = x.shape
    n = w.shape[1]
    return pl.pallas_call(
        _mm_kernel,
        grid=(m // tm, n // tn),
        in_specs=[pl.BlockSpec((tm, k), lambda i, j: (i, 0)),
                  pl.BlockSpec((k, tn), lambda i, j: (0, j))],
        out_specs=pl.BlockSpec((tm, tn), lambda i, j: (i, j)),
        out_shape=jax.ShapeDtypeStruct((m, n), F32),
        compiler_params=_params("parallel", "parallel"),
        name=name,
    )(x, w)


def _outproj_ln_kernel(mix_ref, xo_ref, x_ref, wm_ref, wx_ref, g_ref, b_ref, o_ref):
    h = jnp.dot(mix_ref[...].astype(BF16), wm_ref[...], preferred_element_type=F32)
    h = h + jnp.dot(xo_ref[...].astype(BF16), wx_ref[...], preferred_element_type=F32)
    o_ref[...] = _layer_norm(ALPHA * x_ref[...] + h, g_ref[...], b_ref[...])


def _outproj_ln(mix, xo, x, w_out, g, b, tm):
    m = x.shape[0]
    d_mix = mix.shape[1]
    n_blk = d_mix // D_X
    return pl.pallas_call(
        _outproj_ln_kernel,
        grid=(m // tm,),
        in_specs=[pl.BlockSpec((tm, d_mix), lambda i: (i, 0)),
                  pl.BlockSpec((tm, D_X), lambda i: (i, 0)),
                  pl.BlockSpec((tm, D_MODEL), lambda i: (i, 0)),
                  pl.BlockSpec((d_mix, D_MODEL), lambda i: (0, 0)),
                  pl.BlockSpec((D_X, D_MODEL), lambda i: (n_blk, 0)),
                  pl.BlockSpec((1, D_MODEL), lambda i: (0, 0)),
                  pl.BlockSpec((1, D_MODEL), lambda i: (0, 0))],
        out_specs=pl.BlockSpec((tm, D_MODEL), lambda i: (i, 0)),
        out_shape=jax.ShapeDtypeStruct((m, D_MODEL), F32),
        compiler_params=_params("parallel"),
        name="outproj_ln",
    )(mix, xo, x, w_out, w_out, g, b)


def _ffn_kernel(*refs, moe):
    if moe:
        x_ref, comb_ref, wg_ref, wu_ref, wd_ref, g_ref, b_ref, o_ref, xb_ref, acc_ref = refs
    else:
        x_ref, wg_ref, wu_ref, wd_ref, g_ref, b_ref, o_ref, xb_ref, acc_ref = refs
    e = pl.program_id(1)
    f = pl.program_id(2)
    first = jnp.logical_and(e == 0, f == 0)
    last = jnp.logical_and(e == pl.num_programs(1) - 1, f == pl.num_programs(2) - 1)

    @pl.when(first)
    def _():
        xb_ref[...] = x_ref[...].astype(BF16)
        acc_ref[...] = jnp.zeros_like(acc_ref)

    xb = xb_ref[...]
    hg = jnp.dot(xb, wg_ref[...], preferred_element_type=F32)
    hu = jnp.dot(xb, wu_ref[...], preferred_element_type=F32)
    h = _silu(hg) * hu
    if moe:
        comb = comb_ref[...]
        lane = lax.broadcasted_iota(jnp.int32, comb.shape, 1)
        h = h * _lane_pick(comb, lane, e)
    acc_ref[...] += jnp.dot(h.astype(BF16), wd_ref[...], preferred_element_type=F32)

    @pl.when(last)
    def _():
        o_ref[...] = _layer_norm(ALPHA * x_ref[...] + acc_ref[...], g_ref[...], b_ref[...])


def _ffn_ln(x, comb, w_gu, w_down, layer, g, b, tm, tf):
    m = x.shape[0]
    moe = comb is not None
    n_e = w_gu.shape[1]
    n_f = D_FF // tf
    in_specs = [pl.BlockSpec((tm, D_MODEL), lambda i, e, f: (i, 0))]
    args = [x]
    if moe:
        in_specs.append(pl.BlockSpec((tm, LANES), lambda i, e, f: (i, 0)))
        args.append(comb)
    in_specs += [pl.BlockSpec((None, None, D_MODEL, tf), lambda i, e, f: (layer, e, 0, f)),
                 pl.BlockSpec((None, None, D_MODEL, tf), lambda i, e, f: (layer, e, 0, n_f + f)),
                 pl.BlockSpec((None, None, tf, D_MODEL), lambda i, e, f: (layer, e, f, 0)),
                 pl.BlockSpec((1, D_MODEL), lambda i, e, f: (0, 0)),
                 pl.BlockSpec((1, D_MODEL), lambda i, e, f: (0, 0))]
    args += [w_gu, w_gu, w_down, g, b]
    return pl.pallas_call(
        functools.partial(_ffn_kernel, moe=moe),
        grid=(m // tm, n_e, n_f),
        in_specs=in_specs,
        out_specs=pl.BlockSpec((tm, D_MODEL), lambda i, e, f: (i, 0)),
        out_shape=jax.ShapeDtypeStruct((m, D_MODEL), F32),
        scratch_shapes=[pltpu.VMEM((tm, D_MODEL), BF16), pltpu.VMEM((tm, D_MODEL), F32)],
        compiler_params=_params("parallel", "arbitrary", "arbitrary"),
        name="moe_ln" if moe else "ffn_ln",
    )(*args)


def _router_kernel(x_ref, wh_ref, wl_ref, comb_ref, gate_ref, idx_ref):
    x = x_ref[...]
    xh, xl = _split2(x)
    wh = wh_ref[...]
    lg = jnp.dot(xh, wh, preferred_element_type=F32)
    lg = lg + (jnp.dot(xh, wl_ref[...], preferred_element_type=F32)
               + jnp.dot(xl, wh, preferred_element_type=F32))
    lane = lax.broadcasted_iota(jnp.int32, lg.shape, 1)
    neg = -jnp.inf
    lg = jnp.where(lane < N_EXP, lg, neg)
    m1 = jnp.max(lg, axis=-1, keepdims=True)
    i1 = jnp.min(jnp.where(lg == m1, lane, LANES), axis=-1, keepdims=True)
    lg2 = jnp.where(lane == i1, neg, lg)
    m2 = jnp.max(lg2, axis=-1, keepdims=True)
    i2 = jnp.min(jnp.where(lg2 == m2, lane, LANES), axis=-1, keepdims=True)
    e2 = jnp.exp(m2 - m1)
    den = 1.0 + e2
    g1 = 1.0 / den
    g2 = e2 / den
    comb_ref[...] = jnp.where(lane == i1, g1, 0.0) + jnp.where(lane == i2, g2, 0.0)
    gate_ref[...] = jnp.where(lane == 0, g1, jnp.where(lane == 1, g2, 0.0))
    idx_ref[...] = jnp.where(lane == 0, i1, jnp.where(lane == 1, i2, 0))


def _router(x, w_hi, w_lo, tm):
    m = x.shape[0]
    out = pl.BlockSpec((tm, LANES), lambda i: (i, 0))
    return pl.pallas_call(
        _router_kernel,
        grid=(m // tm,),
        in_specs=[pl.BlockSpec((tm, D_MODEL), lambda i: (i, 0)),
                  pl.BlockSpec((D_MODEL, LANES), lambda i: (0, 0)),
                  pl.BlockSpec((D_MODEL, LANES), lambda i: (0, 0))],
        out_specs=[out, out, out],
        out_shape=[jax.ShapeDtypeStruct((m, LANES), F32),
                   jax.ShapeDtypeStruct((m, LANES), F32),
                   jax.ShapeDtypeStruct((m, LANES), jnp.int32)],
        compiler_params=_params("parallel"),
        name="router",
    )(x, w_hi, w_lo)


MOE_TM = 512
FFN_TF = 896
DISPATCH_TOKENS = 1024
COMBINE_TOKENS = 256


def _moe_plan(idx2, tm):
    t = idx2.shape[0]
    e_flat = idx2.reshape(-1)
    experts = jnp.arange(N_EXP, dtype=jnp.int32)
    onehot = (e_flat[:, None] == experts[None, :]).astype(jnp.int32)
    csum = jnp.cumsum(onehot, axis=0)
    rank = jnp.sum(csum * onehot, axis=1) - 1
    counts = csum[-1]
    ptiles = (counts + tm - 1) // tm
    tile_end = jnp.cumsum(ptiles)
    pstart = (tile_end - ptiles) * tm
    pos = jnp.sum(onehot * pstart[None, :], axis=1) + rank
    n_used = tile_end[-1]
    n_tiles = (2 * t) // tm + N_EXP
    jj = jnp.minimum(jnp.arange(n_tiles, dtype=jnp.int32), n_used - 1)
    tile_expert = jnp.sum((jj[:, None] >= tile_end[None, :]).astype(jnp.int32), axis=1)
    seg_base = jnp.concatenate([pstart + counts, (n_used * tm).reshape(1)])
    seg_len = jnp.concatenate([ptiles * tm - counts, ((n_tiles - n_used) * tm).reshape(1)])
    seg_end = jnp.cumsum(seg_len)
    j = jnp.arange(N_EXP * tm, dtype=jnp.int32)
    seg = jnp.sum((j[:, None] >= seg_end[None, :]).astype(jnp.int32), axis=1)
    seg_1h = (seg[:, None] == jnp.arange(N_EXP + 1, dtype=jnp.int32)[None, :]).astype(jnp.int32)
    pad_dst = j + jnp.sum(seg_1h * (seg_base - (seg_end - seg_len))[None, :], axis=1)
    return (pos.astype(jnp.int32), tile_expert.astype(jnp.int32), n_used.reshape(1).astype(jnp.int32),
            pad_dst.astype(jnp.int32), n_tiles)


def _row_copy(src_ref, src_row, dst_ref, dst_row, sem):
    return pltpu.make_async_copy(src_ref.at[pl.ds(src_row, 1), :], dst_ref.at[pl.ds(dst_row, 1), :], sem)


def _wait_rows(hbm_ref, n_rows, sem):
    rows = hbm_ref.at[pl.ds(0, n_rows), :]
    pltpu.make_async_copy(rows, rows, sem).wait()


def _moe_dispatch_kernel(pos_ref, pad_dst_ref, x_ref, xs_ref, sem):
    tq = x_ref.shape[0]
    n_pad = pad_dst_ref.shape[1]

    def send(grp, carry):
        r0 = pl.multiple_of(grp * SUBLANES, SUBLANES)
        for jr in range(SUBLANES):
            for k in range(2):
                _row_copy(x_ref, r0 + jr, xs_ref, pos_ref[0, 2 * (r0 + jr) + k], sem).start()
        return carry

    lax.fori_loop(0, tq // SUBLANES, send, 0)

    def pad(grp, carry):
        r0 = pl.multiple_of(grp * SUBLANES, SUBLANES)
        for jr in range(SUBLANES):
            _row_copy(x_ref, jr, xs_ref, pad_dst_ref[0, r0 + jr], sem).start()
        return carry

    lax.fori_loop(0, n_pad // SUBLANES, pad, 0)
    _wait_rows(xs_ref, 2 * tq + n_pad, sem)


def _moe_dispatch(x, pos, pad_dst, n_rows_out):
    m = x.shape[0]
    tq = DISPATCH_TOKENS
    n_steps = m // tq
    n_pad = pad_dst.shape[0] // n_steps
    smem = functools.partial(pl.BlockSpec, memory_space=pltpu.SMEM)
    return pl.pallas_call(
        _moe_dispatch_kernel,
        grid=(n_steps,),
        in_specs=[smem((None, 1, 2 * tq), lambda i: (i, 0, 0)),
                  smem((None, 1, n_pad), lambda i: (i, 0, 0)),
                  pl.BlockSpec((tq, D_MODEL), lambda i: (i, 0))],
        out_specs=pl.BlockSpec(memory_space=pl.ANY),
        out_shape=jax.ShapeDtypeStruct((n_rows_out, D_MODEL), F32),
        scratch_shapes=[pltpu.SemaphoreType.DMA(())],
        compiler_params=_params("arbitrary"),
        name="moe_dispatch",
    )(pos.reshape(n_steps, 1, 2 * tq), pad_dst.reshape(n_steps, 1, n_pad), x)


def _moe_ffn_kernel(te_ref, nu_ref, x_ref, wg_ref, wu_ref, wd_ref, o_ref, xb_ref):
    p = pl.program_id(0)
    f = pl.program_id(1)

    @pl.when(p < nu_ref[0])
    def _():
        @pl.when(f == 0)
        def _():
            xb_ref[...] = x_ref[...].astype(BF16)

        xb = xb_ref[...]
        hg = jnp.dot(xb, wg_ref[...], preferred_element_type=F32)
        hu = jnp.dot(xb, wu_ref[...], preferred_element_type=F32)
        y = jnp.dot((_silu(hg) * hu).astype(BF16), wd_ref[...], preferred_element_type=F32)

        @pl.when(f == 0)
        def _():
            o_ref[...] = y

        @pl.when(f != 0)
        def _():
            o_ref[...] += y

    @pl.when(jnp.logical_and(p >= nu_ref[0], f == 0))
    def _():
        o_ref[...] = jnp.zeros_like(o_ref)


def _moe_ffn(xs, tile_expert, n_used, w_gu, w_down, layer, tm, tf):
    n_tiles = xs.shape[0] // tm
    n_f = D_FF // tf
    tile = lambda p, nu: jnp.minimum(p, nu[0] - 1)
    col = lambda p, f, nu: jnp.where(p < nu[0], f, n_f - 1)
    grid_spec = pltpu.PrefetchScalarGridSpec(
        num_scalar_prefetch=2,
        grid=(n_tiles, n_f),
        in_specs=[pl.BlockSpec((tm, D_MODEL), lambda p, f, te, nu: (tile(p, nu), 0)),
                  pl.BlockSpec((None, None, D_MODEL, tf),
                               lambda p, f, te, nu: (layer, te[p], 0, col(p, f, nu))),
                  pl.BlockSpec((None, None, D_MODEL, tf),
                               lambda p, f, te, nu: (layer, te[p], 0, n_f + col(p, f, nu))),
                  pl.BlockSpec((None, None, tf, D_MODEL),
                               lambda p, f, te, nu: (layer, te[p], col(p, f, nu), 0))],
        out_specs=pl.BlockSpec((tm, D_MODEL), lambda p, f, te, nu: (p, 0)),
        scratch_shapes=[pltpu.VMEM((tm, D_MODEL), BF16)],
    )
    return pl.pallas_call(
        _moe_ffn_kernel,
        grid_spec=grid_spec,
        out_shape=jax.ShapeDtypeStruct(xs.shape, F32),
        compiler_params=_params("arbitrary", "arbitrary"),
        name="moe_ffn",
    )(tile_expert, n_used, xs, w_gu, w_gu, w_down)


def _moe_combine_ln_kernel(pos_ref, gate_ref, x_ref, ye_ref, g_ref, b_ref, o_ref, buf_ref, sem):
    tq = x_ref.shape[0]

    def fetch(grp, carry):
        r0 = pl.multiple_of(grp * SUBLANES, SUBLANES)
        for jr in range(SUBLANES):
            for k in range(2):
                _row_copy(ye_ref, pos_ref[0, 2 * (r0 + jr) + k], buf_ref, k * tq + r0 + jr, sem).start()
        return carry

    lax.fori_loop(0, tq // SUBLANES, fetch, 0)
    _wait_rows(ye_ref, 2 * tq, sem)
    gates = gate_ref[...]
    lane = lax.broadcasted_iota(jnp.int32, gates.shape, 1)
    y = _lane_pick(gates, lane, 0) * buf_ref[0:tq, :] + _lane_pick(gates, lane, 1) * buf_ref[tq:2 * tq, :]
    o_ref[...] = _layer_norm(ALPHA * x_ref[...] + y, g_ref[...], b_ref[...])


def _moe_combine_ln(x, gates, pos, ye, g, b):
    m = x.shape[0]
    tq = COMBINE_TOKENS
    n_steps = m // tq
    return pl.pallas_call(
        _moe_combine_ln_kernel,
        grid=(n_steps,),
        in_specs=[pl.BlockSpec((None, 1, 2 * tq), lambda i: (i, 0, 0), memory_space=pltpu.SMEM),
                  pl.BlockSpec((tq, LANES), lambda i: (i, 0)),
                  pl.BlockSpec((tq, D_MODEL), lambda i: (i, 0)),
                  pl.BlockSpec(memory_space=pl.ANY),
                  pl.BlockSpec((1, D_MODEL), lambda i: (0, 0)),
                  pl.BlockSpec((1, D_MODEL), lambda i: (0, 0))],
        out_specs=pl.BlockSpec((tq, D_MODEL), lambda i: (i, 0)),
        out_shape=jax.ShapeDtypeStruct((m, D_MODEL), F32),
        scratch_shapes=[pltpu.VMEM((2 * tq, D_MODEL), F32), pltpu.SemaphoreType.DMA(())],
        compiler_params=_params("arbitrary"),
        name="moe_combine_ln",
    )(pos.reshape(n_steps, 1, 2 * tq), gates, x, ye, g, b)


def _moe_routed_ln(x, gates, idx, w_gu, w_down, layer, g, b):
    pos, tile_expert, n_used, pad_dst, n_tiles = _moe_plan(idx[:, :2], MOE_TM)
    xs = _moe_dispatch(x, pos, pad_dst, n_tiles * MOE_TM)
    ye = _moe_ffn(xs, tile_expert, n_used, w_gu, w_down, layer, MOE_TM, FFN_TF)
    return _moe_combine_ln(x, gates, pos, ye, g, b)


GDN_PASSES_QK = 1
GDN_PASSES_SOLVE = 1
GDN_PASSES_STATE = 1
INV_BLOCK_SHIFT = 4
GDN_SEQS = 4


def _gdn_chunk_kernel(qkv_ref, z_ref, ab_ref, cw_ref, alog_ref, dtb_ref, nw_ref, hist0_ref, s0_ref,
                      mix_ref, sout_ref, hout_ref, xh_ref, s_ref):
    n = pl.program_id(1)
    c = GDN_CHUNK
    hrows = SUBLANES
    n_seq = qkv_ref.shape[0]
    items = [(b, h) for b in range(n_seq) for h in range(H_A)]

    @pl.when(n == 0)
    def _():
        xh_ref[:, 0:hrows, :] = hist0_ref[...]
        s_ref[...] = s0_ref[...]

    xh_ref[:, hrows:hrows + c, :] = qkv_ref[...]

    row = lax.broadcasted_iota(jnp.int32, (c, c), 0)
    col = lax.broadcasted_iota(jnp.int32, (c, c), 1)
    causal = row >= col
    strict = row > col
    eye = jnp.where(row == col, 1.0, 0.0)
    blockdiag = (row >> INV_BLOCK_SHIFT) == (col >> INV_BLOCK_SHIFT)
    tril_ones = jnp.where(causal, 1.0, 0.0).astype(BF16)
    lane = lax.broadcasted_iota(jnp.int32, (c, LANES), 1)
    mm_s = functools.partial(_dot, passes=GDN_PASSES_SOLVE)

    beta_all, gcum_all, gcum_t = [], [], []
    for b in range(n_seq):
        ab = ab_ref[b]
        g_all = -jnp.exp(alog_ref[...]) * _softplus(ab + dtb_ref[...])
        beta_all.append(_sigmoid(ab))
        gc = _dot_exact_lhs(tril_ones, g_all)
        gcum_all.append(gc)
        gcum_t.append(gc.T)

    def conv_silu(b, c0):
        acc = xh_ref[b, hrows - 3:hrows - 3 + c, c0:c0 + LANES] * cw_ref[0:1, c0:c0 + LANES]
        for j in range(1, CONV_W):
            acc = acc + (xh_ref[b, hrows - 3 + j:hrows - 3 + j + c, c0:c0 + LANES]
                         * cw_ref[j:j + 1, c0:c0 + LANES])
        return _silu(acc)

    q, k, v, kb, egc, gc_col, gc_last, beta_col, decay = {}, {}, {}, {}, {}, {}, {}, {}, {}
    for it in items:
        b, h = it
        qi = conv_silu(b, h * DK_A)
        ki = conv_silu(b, QK_A + h * DK_A)
        v[it] = conv_silu(b, 2 * QK_A + h * DV_A)
        q[it] = qi * lax.rsqrt(jnp.sum(qi * qi, axis=-1, keepdims=True) + RMS_EPS) * (DK_A ** -0.5)
        k[it] = ki * lax.rsqrt(jnp.sum(ki * ki, axis=-1, keepdims=True) + RMS_EPS)
        gc_col[it] = _lane_pick(gcum_all[b], lane, h)
        beta_col[it] = _lane_pick(beta_all[b], lane, B_LANE + h)
        gc_last[it] = gc_col[it][c - 1:c, :]
        decay[it] = jnp.where(causal, jnp.exp(gc_col[it] - gcum_t[b][h:h + 1, :c]), 0.0)
        egc[it] = jnp.exp(gc_col[it])
        kb[it] = k[it] * beta_col[it]

    kk = {it: _dot(jnp.concatenate([kb[it], q[it]], axis=0), k[it], _NT, GDN_PASSES_QK) for it in items}
    a_mat = {it: jnp.where(strict, kk[it][:c] * decay[it], 0.0) for it in items}
    attn = {it: kk[it][c:] * decay[it] for it in items}

    a_d = {it: jnp.where(blockdiag, a_mat[it], 0.0) for it in items}
    a_n = {it: jnp.where(blockdiag, 0.0, a_mat[it]) for it in items}
    p2 = {it: mm_s(a_d[it], a_d[it]) for it in items}
    e_d = {it: p2[it] - a_d[it] - mm_s(a_d[it], p2[it]) for it in items}
    p4 = {it: mm_s(p2[it], p2[it]) for it in items}
    e_d = {it: e_d[it] + p4[it] + mm_s(e_d[it], p4[it]) for it in items}
    p8 = {it: mm_s(p4[it], p4[it]) for it in items}
    e_d = {it: e_d[it] + p8[it] + mm_s(e_d[it], p8[it]) for it in items}
    m_blk = {it: a_n[it] + mm_s(e_d[it], a_n[it]) for it in items}
    rhs = {it: jnp.concatenate([v[it] * beta_col[it], kb[it] * egc[it]], axis=1) for it in items}
    rhs = {it: rhs[it] + mm_s(e_d[it], rhs[it]) for it in items}
    m2 = {it: mm_s(m_blk[it], m_blk[it]) for it in items}
    f_y = {it: m2[it] - m_blk[it] - mm_s(m_blk[it], m2[it]) for it in items}
    sol = {it: rhs[it] + mm_s(f_y[it], rhs[it]) for it in items}

    s_old = {it: s_ref[it[0], it[1]] for it in items}
    wq = {it: _dot(jnp.concatenate([sol[it][:, DV_A:], q[it] * egc[it]], axis=0), s_old[it],
                   _NN, GDN_PASSES_STATE) for it in items}
    v_new = {it: sol[it][:, :DV_A] - wq[it][:c] for it in items}
    o = {it: wq[it][c:] + _dot(attn[it], v_new[it], _NN, GDN_PASSES_STATE) for it in items}
    for it in items:
        k_dec = k[it] * jnp.exp(gc_last[it] - gc_col[it])
        s_ref[it[0], it[1]] = (s_old[it] * jnp.exp(gc_last[it])
                               + _dot(k_dec, v_new[it], _TN, GDN_PASSES_STATE))
    for it in items:
        b, h = it
        oi = o[it]
        oi = oi * lax.rsqrt(jnp.mean(oi * oi, axis=-1, keepdims=True) + RMS_EPS) * nw_ref[...]
        mix_ref[b, :, h * DV_A:(h + 1) * DV_A] = oi * _silu(z_ref[b, :, h * DV_A:(h + 1) * DV_A])

    xh_ref[:, 0:hrows, :] = xh_ref[:, c:c + hrows, :]

    @pl.when(n == pl.num_programs(1) - 1)
    def _():
        sout_ref[...] = s_ref[...]
        hout_ref[...] = xh_ref[:, c:c + hrows, :]


def _gdn_chunked(proj, conv_w, alog_row, dtb_row, nw_row, hist0, s0):
    bsz, t, _ = proj.shape
    c = GDN_CHUNK
    nb = GDN_SEQS
    n_chunks = t // c
    row_spec = lambda width, blk: pl.BlockSpec((nb, c, width), lambda b, n: (b, n, blk))
    full2 = lambda shape: pl.BlockSpec(shape, lambda b, n: (0, 0))
    return pl.pallas_call(
        _gdn_chunk_kernel,
        grid=(bsz // nb, n_chunks),
        in_specs=[row_spec(CONV_DIM, 0),
                  row_spec(V_A, A_Z_OFF // V_A),
                  row_spec(LANES, A_AB_OFF // LANES),
                  full2((CONV_W, CONV_DIM)),
                  full2((1, LANES)), full2((1, LANES)), full2((1, DV_A)),
                  pl.BlockSpec((nb, SUBLANES, CONV_DIM), lambda b, n: (b, 0, 0)),
                  pl.BlockSpec((nb, H_A, DK_A, DV_A), lambda b, n: (b, 0, 0, 0))],
        out_specs=[pl.BlockSpec((nb, c, V_A), lambda b, n: (b, n, 0)),
                   pl.BlockSpec((nb, H_A, DK_A, DV_A), lambda b, n: (b, 0, 0, 0)),
                   pl.BlockSpec((nb, SUBLANES, CONV_DIM), lambda b, n: (b, 0, 0))],
        out_shape=[jax.ShapeDtypeStruct((bsz, t, V_A), F32),
                   jax.ShapeDtypeStruct((bsz, H_A, DK_A, DV_A), F32),
                   jax.ShapeDtypeStruct((bsz, SUBLANES, CONV_DIM), F32)],
        scratch_shapes=[pltpu.VMEM((nb, SUBLANES + c, CONV_DIM), F32),
                        pltpu.VMEM((nb, H_A, DK_A, DV_A), F32)],
        compiler_params=_params("parallel", "arbitrary"),
        name="gdn_chunked",
    )(proj, proj, proj, conv_w, alog_row, dtb_row, nw_row, hist0, s0)


def _gdn_step_kernel(qkv_ref, z_ref, ab_ref, cw_ref, alog_ref, dtb_ref, nw_ref, hist_ref, s_ref,
                     mix_ref, sout_ref, hout_ref, o_scr):
    rows = qkv_ref.shape[0]
    new = qkv_ref[...]
    y = hist_ref[0] * cw_ref[0:1, :]
    for j in range(1, CONV_W - 1):
        y = y + hist_ref[j] * cw_ref[j:j + 1, :]
    y = y + new * cw_ref[CONV_W - 1:CONV_W, :]
    qkv = _silu(y)
    for j in range(CONV_W - 2):
        hout_ref[j] = hist_ref[j + 1]
    hout_ref[CONV_W - 2] = new

    ab = ab_ref[...]
    lane = lax.broadcasted_iota(jnp.int32, ab.shape, 1)
    g_all = -jnp.exp(alog_ref[...]) * _softplus(ab + dtb_ref[...])
    beta_all = _sigmoid(ab)
    ri = lax.broadcasted_iota(jnp.int32, (DK_A, DK_A), 0)
    ci = lax.broadcasted_iota(jnp.int32, (DK_A, DK_A), 1)
    diag = ri == ci

    def as_column(r):
        return jnp.sum(jnp.where(diag, r, 0.0), axis=-1, keepdims=True)

    for h in range(H_A):
        q = qkv[:, h * DK_A:(h + 1) * DK_A]
        k = qkv[:, QK_A + h * DK_A:QK_A + (h + 1) * DK_A]
        v = qkv[:, 2 * QK_A + h * DV_A:2 * QK_A + (h + 1) * DV_A]
        q = q * lax.rsqrt(jnp.sum(q * q, axis=-1, keepdims=True) + RMS_EPS) * (DK_A ** -0.5)
        k = k * lax.rsqrt(jnp.sum(k * k, axis=-1, keepdims=True) + RMS_EPS)
        eg = jnp.exp(_lane_pick(g_all, lane, h))
        beta = _lane_pick(beta_all, lane, B_LANE + h)
        for s in range(rows):
            k_col = as_column(k[s:s + 1, :])
            q_col = as_column(q[s:s + 1, :])
            st = s_ref[s, h] * eg[s:s + 1, :]
            ks = jnp.sum(k_col * st, axis=0, keepdims=True)
            u = beta[s:s + 1, :] * (v[s:s + 1, :] - ks)
            st = st + k_col * u
            sout_ref[s, h] = st
            o_scr[s:s + 1, h * DV_A:(h + 1) * DV_A] = jnp.sum(q_col * st, axis=0, keepdims=True)

    for h in range(H_A):
        o = o_scr[:, h * DV_A:(h + 1) * DV_A]
        o = o * lax.rsqrt(jnp.mean(o * o, axis=-1, keepdims=True) + RMS_EPS) * nw_ref[...]
        mix_ref[:, h * DV_A:(h + 1) * DV_A] = o * _silu(z_ref[:, h * DV_A:(h + 1) * DV_A])


def _gdn_step(proj, conv_w, alog_row, dtb_row, nw_row, hist_t, s0, layer):
    bsz = proj.shape[0]
    rows = SUBLANES
    full1 = lambda shape: pl.BlockSpec(shape, lambda i: (0, 0))
    return pl.pallas_call(
        _gdn_step_kernel,
        grid=(bsz // rows,),
        in_specs=[pl.BlockSpec((rows, CONV_DIM), lambda i: (i, 0)),
                  pl.BlockSpec((rows, V_A), lambda i: (i, A_Z_OFF // V_A)),
                  pl.BlockSpec((rows, LANES), lambda i: (i, A_AB_OFF // LANES)),
                  full1((CONV_W, CONV_DIM)),
                  full1((1, LANES)), full1((1, LANES)), full1((1, DV_A)),
                  pl.BlockSpec((None, CONV_W - 1, rows, CONV_DIM), lambda i: (layer, 0, i, 0)),
                  pl.BlockSpec((None, rows, H_A, DK_A, DV_A), lambda i: (layer, i, 0, 0, 0))],
        out_specs=[pl.BlockSpec((rows, V_A), lambda i: (i, 0)),
                   pl.BlockSpec((rows, H_A, DK_A, DV_A), lambda i: (i, 0, 0, 0)),
                   pl.BlockSpec((CONV_W - 1, rows, CONV_DIM), lambda i: (0, i, 0))],
        out_shape=[jax.ShapeDtypeStruct((bsz, V_A), F32),
                   jax.ShapeDtypeStruct((bsz, H_A, DK_A, DV_A), F32),
                   jax.ShapeDtypeStruct((CONV_W - 1, bsz, CONV_DIM), F32)],
        scratch_shapes=[pltpu.VMEM((rows, V_A), F32)],
        compiler_params=_params("parallel"),
        name="gdn_step",
    )(proj, proj, proj, conv_w, alog_row, dtb_row, nw_row, hist_t, s0)


def _sgu_chunk_kernel(u_ref, v_ref, g_ref, b_ref, ws_ref, bs_ref, mix_ref):
    c = CHUNK_B
    row = lax.broadcasted_iota(jnp.int32, (c, c), 0)
    col = lax.broadcasted_iota(jnp.int32, (c, c), 1)
    u = _gelu_tanh(u_ref[...])
    v = _layer_norm(_gelu_tanh(v_ref[...]), g_ref[...], b_ref[...])
    for gi in range(G_B):
        ws = jnp.where(row >= col, ws_ref[gi], 0.0)
        mixed = _dot(ws, v[:, gi * CH_B:(gi + 1) * CH_B]) + bs_ref[gi]
        mix_ref[:, gi * CH_B:(gi + 1) * CH_B] = u[:, gi * CH_B:(gi + 1) * CH_B] * mixed


def _sgu_chunked(proj, ln_g, ln_b, w_s, bs_b):
    bsz, t, _ = proj.shape
    c = CHUNK_B
    return pl.pallas_call(
        _sgu_chunk_kernel,
        grid=(bsz, t // c),
        in_specs=[pl.BlockSpec((None, c, D_B), lambda b, n: (b, n, 0)),
                  pl.BlockSpec((None, c, D_B), lambda b, n: (b, n, 1)),
                  pl.BlockSpec((1, D_B), lambda b, n: (0, 0)),
                  pl.BlockSpec((1, D_B), lambda b, n: (0, 0)),
                  pl.BlockSpec((G_B, c, c), lambda b, n: (0, 0, 0)),
                  pl.BlockSpec((G_B, c, CH_B), lambda b, n: (0, 0, 0))],
        out_specs=pl.BlockSpec((None, c, D_B), lambda b, n: (b, n, 0)),
        out_shape=jax.ShapeDtypeStruct((bsz, t, D_B), F32),
        compiler_params=_params("parallel", "parallel"),
        name="sgu_chunked",
    )(proj, proj, ln_g, ln_b, w_s, bs_b)


def _sgu_first_kernel(u_ref, v_ref, g_ref, b_ref, w00_ref, b0_ref, mix_ref, vout_ref):
    u = _gelu_tanh(u_ref[...])
    v = _layer_norm(_gelu_tanh(v_ref[...]), g_ref[...], b_ref[...])
    vout_ref[...] = v
    mix_ref[...] = u * (w00_ref[...] * v + b0_ref[...])


def _sgu_first(proj, ln_g, ln_b, w00_row, b0_row):
    bsz = proj.shape[0]
    row = lambda: pl.BlockSpec((1, D_B), lambda i: (0, 0))
    return pl.pallas_call(
        _sgu_first_kernel,
        grid=(1,),
        in_specs=[pl.BlockSpec((bsz, D_B), lambda i: (0, 0)),
                  pl.BlockSpec((bsz, D_B), lambda i: (0, 1)),
                  row(), row(), row(), row()],
        out_specs=[pl.BlockSpec((bsz, D_B), lambda i: (0, 0)),
                   pl.BlockSpec((bsz, D_B), lambda i: (0, 0))],
        out_shape=[jax.ShapeDtypeStruct((bsz, D_B), F32),
                   jax.ShapeDtypeStruct((bsz, D_B), F32)],
        compiler_params=_params("arbitrary"),
        name="sgu_first",
    )(proj, proj, ln_g, ln_b, w00_row, b0_row)


def _head_masks(shape):
    lane = lax.broadcasted_iota(jnp.int32, shape, len(shape) - 1)
    return [(lane // DH_X) == h for h in range(H_X)]


def _xattn_kernel(q_ref, k_ref, v_ref, o_ref):
    q = q_ref[...]
    kb = k_ref[...].astype(BF16)
    vb = v_ref[...].astype(BF16)
    masks = _head_masks(q.shape)
    out = jnp.zeros(q.shape, F32)
    for h in range(H_X):
        qh = jnp.where(masks[h], q, 0.0)
        s = _dot(qh, kb, _NT) * (DH_X ** -0.5)
        s = s - jnp.max(s, axis=-1, keepdims=True)
        p = jnp.exp(s)
        p = p / jnp.sum(p, axis=-1, keepdims=True)
        out = out + jnp.where(masks[h], _dot(p, vb), 0.0)
    o_ref[...] = out


def _xattn(proj, xq_blk, mem_k, mem_v, layer, tq):
    bsz, t, _ = proj.shape
    return pl.pallas_call(
        _xattn_kernel,
        grid=(bsz, t // tq),
        in_specs=[pl.BlockSpec((None, tq, D_X), lambda b, i: (b, i, xq_blk)),
                  pl.BlockSpec((None, None, N_MEM, D_X), lambda b, i: (layer, b, 0, 0)),
                  pl.BlockSpec((None, None, N_MEM, D_X), lambda b, i: (layer, b, 0, 0))],
        out_specs=pl.BlockSpec((None, tq, D_X), lambda b, i: (b, i, 0)),
        out_shape=jax.ShapeDtypeStruct((bsz, t, D_X), F32),
        compiler_params=_params("parallel", "parallel"),
        name="xattn",
    )(proj, mem_k, mem_v)


def _xattn_step_kernel(q_ref, k_ref, v_ref, o_ref):
    rows = q_ref.shape[0]
    q = q_ref[...]
    hrow = lax.broadcasted_iota(jnp.int32, (SUBLANES, D_X), 0)
    hlane = lax.broadcasted_iota(jnp.int32, (SUBLANES, D_X), 1) // DH_X
    sel = hrow == hlane
    for s in range(rows):
        qh = jnp.where(sel, q[s:s + 1, :], 0.0)
        sc = _dot(qh, k_ref[s], _NT) * (DH_X ** -0.5)
        sc = sc - jnp.max(sc, axis=-1, keepdims=True)
        p = jnp.exp(sc)
        p = p / jnp.sum(p, axis=-1, keepdims=True)
        pv = _dot(p, v_ref[s])
        o_ref[s:s + 1, :] = jnp.sum(jnp.where(sel, pv, 0.0), axis=0, keepdims=True)


def _xattn_step(proj, xq_blk, mem_k, mem_v, layer):
    bsz = proj.shape[0]
    rows = SUBLANES
    return pl.pallas_call(
        _xattn_step_kernel,
        grid=(bsz // rows,),
        in_specs=[pl.BlockSpec((rows, D_X), lambda i: (i, xq_blk)),
                  pl.BlockSpec((None, rows, N_MEM, D_X), lambda i: (layer, i, 0, 0)),
                  pl.BlockSpec((None, rows, N_MEM, D_X), lambda i: (layer, i, 0, 0))],
        out_specs=pl.BlockSpec((rows, D_X), lambda i: (i, 0)),
        out_shape=jax.ShapeDtypeStruct((bsz, D_X), F32),
        compiler_params=_params("parallel"),
        name="xattn_step",
    )(proj, mem_k, mem_v)


def _pad_lanes(v):
    return jnp.zeros((1, LANES), F32).at[0, :v.shape[0]].set(v.astype(F32))


def _prep_weights(a_w_in, a_A_log, a_dt_bias, b_w_s, b_b_s, moe_router):
    qkvz = a_w_in[:, :, :A_XQ_OFF]
    a_cols = a_w_in[:, :, A_XQ_OFF:A_XQ_OFF + H_A]
    b_cols = a_w_in[:, :, A_XQ_OFF + H_A:A_XQ_OFF + 2 * H_A]
    xq = a_w_in[:, :, A_XQ_OFF + 2 * H_A:]
    ab = jnp.zeros(a_w_in.shape[:2] + (LANES,), F32)
    ab = ab.at[:, :, :H_A].set(a_cols).at[:, :, B_LANE:B_LANE + H_A].set(b_cols)
    a_w = jnp.concatenate([qkvz, xq, ab], axis=-1).astype(BF16)
    alog_rows = [_pad_lanes(a_A_log[j]) for j in range(a_A_log.shape[0])]
    dtb_rows = [_pad_lanes(a_dt_bias[j]) for j in range(a_dt_bias.shape[0])]
    bs_b = jnp.broadcast_to(b_b_s[..., None], b_b_s.shape + (CH_B,)).astype(F32)
    w00_rows = jnp.repeat(b_w_s[:, :, 0, 0], CH_B, axis=-1)[:, None, :]
    b0_rows = jnp.repeat(b_b_s[:, :, 0], CH_B, axis=-1)[:, None, :]
    r = jnp.zeros(moe_router.shape[:2] + (LANES,), F32).at[:, :, :N_EXP].set(moe_router)
    r_hi = r.astype(BF16)
    r_lo = (r - r_hi.astype(F32)).astype(BF16)
    return a_w, alog_rows, dtb_rows, bs_b, w00_rows, b0_rows, r_hi, r_lo


def kernel(x_prompt, x_sample, state_gdn, state_conv, cache_mem_k, cache_mem_v, mem_prompt,
           a_w_in, a_conv_w, a_A_log, a_dt_bias, a_norm_w,
           b_w_in, b_ln_g, b_ln_b, b_w_s, b_b_s,
           w_mem_kv, w_out, ln1_g, ln1_b, ln2_g, ln2_b,
           ffn_w_gu, ffn_w_down, moe_router, moe_w_gu, moe_w_down):
    n_p, t_p, _ = x_prompt.shape
    n_s = x_sample.shape[0]
    n_a = a_w_in.shape[0]

    a_w, alog_rows, dtb_rows, bs_b, w00_rows, b0_rows, r_hi, r_lo = _prep_weights(
        a_w_in, a_A_log, a_dt_bias, b_w_s, b_b_s, moe_router)
    ffn_gu = ffn_w_gu.astype(BF16)[:, None]
    ffn_dn = ffn_w_down.astype(BF16)[:, None]
    moe_gu = moe_w_gu.astype(BF16)
    moe_dn = moe_w_down.astype(BF16)
    row = lambda v: v.reshape(1, -1).astype(F32)

    w_kv = jnp.transpose(w_mem_kv, (1, 0, 2)).reshape(D_MODEL, DEPTH * 2 * D_X).astype(BF16)
    kv = _matmul(mem_prompt.reshape(n_p * N_MEM, D_MODEL), w_kv, 1024, 1024, "mem_kv")
    kv = kv.reshape(n_p, N_MEM, DEPTH, 2, D_X)
    p_mem_k = jnp.transpose(kv[:, :, :, 0, :], (2, 0, 1, 3))
    p_mem_v = jnp.transpose(kv[:, :, :, 1, :], (2, 0, 1, 3))
    s_mem_k = cache_mem_k.reshape(DEPTH, n_s, N_MEM, D_X)
    s_mem_v = cache_mem_v.reshape(DEPTH, n_s, N_MEM, D_X)

    xp = x_prompt.reshape(n_p * t_p, D_MODEL)
    xs = x_sample.reshape(n_s, D_MODEL)
    p_hist0 = jnp.zeros((n_p, SUBLANES, CONV_DIM), F32)
    p_s0 = jnp.zeros((n_p, H_A, DK_A, DV_A), F32)
    s_hist_t = jnp.transpose(state_conv, (0, 2, 1, 3))

    p_gdn, p_conv, s_gdn, s_conv, s_sgu_v = [], [], [], [], []
    tm_p, tm_s = 1024, n_s
    for i in range(DEPTH):
        j = i // 2
        if i % 2 == 0:
            proj_p = _matmul(xp, a_w[j], tm_p, 1152, "in_proj_a")
            proj_s = _matmul(xs, a_w[j], tm_s, 1152, "in_proj_a_s")
            nw = row(a_norm_w[j])
            mix_p, sp, hp = _gdn_chunked(proj_p.reshape(n_p, t_p, A_COLS), a_conv_w[j],
                                         alog_rows[j], dtb_rows[j], nw, p_hist0, p_s0)
            mix_p = mix_p.reshape(n_p * t_p, V_A)
            p_gdn.append(sp)
            p_conv.append(hp[:, SUBLANES - (CONV_W - 1):, :])
            mix_s, ss, hs = _gdn_step(proj_s, a_conv_w[j], alog_rows[j], dtb_rows[j], nw,
                                      s_hist_t, state_gdn, j)
            s_gdn.append(ss)
            s_conv.append(jnp.transpose(hs, (1, 0, 2)))
            cols, xq_blk = A_COLS, A_XQ_OFF // D_X
        else:
            b_w = b_w_in[j].astype(BF16)
            proj_p = _matmul(xp, b_w, tm_p, 896, "in_proj_b")
            proj_s = _matmul(xs, b_w, tm_s, 896, "in_proj_b_s")
            mix_p = _sgu_chunked(proj_p.reshape(n_p, t_p, B_COLS), row(b_ln_g[j]), row(b_ln_b[j]),
                                 b_w_s[j], bs_b[j]).reshape(n_p * t_p, D_B)
            mix_s, v_s = _sgu_first(proj_s, row(b_ln_g[j]), row(b_ln_b[j]), w00_rows[j], b0_rows[j])
            s_sgu_v.append(v_s.reshape(n_s, 1, D_B))
            cols, xq_blk = B_COLS, B_XQ_OFF // D_X
        xo_p = _xattn(proj_p.reshape(n_p, t_p, cols), xq_blk, p_mem_k, p_mem_v, i, 512)
        xo_s = _xattn_step(proj_s, xq_blk, s_mem_k, s_mem_v, i)
        g1, b1, g2, b2 = row(ln1_g[i]), row(ln1_b[i]), row(ln2_g[i]), row(ln2_b[i])
        w_o = w_out[i].astype(BF16)
        xp = _outproj_ln(mix_p, xo_p.reshape(n_p * t_p, D_X), xp, w_o, g1, b1, 512)
        xs = _outproj_ln(mix_s, xo_s, xs, w_o, g1, b1, tm_s)
        if i % 2 == 0:
            xp = _ffn_ln(xp, None, ffn_gu, ffn_dn, j, g2, b2, tm_p, FFN_TF)
            xs = _ffn_ln(xs, None, ffn_gu, ffn_dn, j, g2, b2, tm_s, FFN_TF)
        else:
            _, gates_p, idx_p = _router(xp, r_hi[j], r_lo[j], tm_p)
            comb_s, _, _ = _router(xs, r_hi[j], r_lo[j], tm_s)
            xp = _moe_routed_ln(xp, gates_p, idx_p, moe_gu, moe_dn, j, g2, b2)
            xs = _ffn_ln(xs, comb_s, moe_gu, moe_dn, j, g2, b2, tm_s, FFN_TF)

    mem_shape = (DEPTH, n_p, N_MEM, H_X, DH_X)
    return (xp.reshape(n_p, t_p, D_MODEL),
            xs.reshape(n_s, 1, D_MODEL),
            jnp.stack(p_gdn),
            jnp.stack(p_conv),
            p_mem_k.reshape(mem_shape),
            p_mem_v.reshape(mem_shape),
            jnp.stack(s_gdn),
            jnp.stack(s_conv),
            jnp.stack(s_sgu_v))
```

```python
import functools

import jax
import jax.numpy as jnp
from jax import lax
from jax.experimental import pallas as pl
from jax.experimental.pallas import tpu as pltpu

F32 = jnp.float32
BF16 = jnp.bfloat16

D_MODEL = 1024
DEPTH = 4
H_A = 6
DK_A = 128
DV_A = 128
QK_A = H_A * DK_A
V_A = H_A * DV_A
CONV_W = 4
CONV_DIM = 2 * QK_A + V_A
GDN_CHUNK = 64
G_B = 6
CH_B = 128
D_B = G_B * CH_B
CHUNK_B = 128
N_MEM = 256
H_X = 4
DH_X = 64
D_X = H_X * DH_X
D_FF = 3584
N_EXP = 8
ALPHA = (2 * DEPTH) ** 0.25
LN_EPS = 1e-5
RMS_EPS = 1e-6

LANES = 128
SUBLANES = 8
VMEM_LIMIT_BYTES = 56 * 1024 * 1024

A_Z_OFF = CONV_DIM
A_XQ_OFF = CONV_DIM + V_A
A_AB_OFF = A_XQ_OFF + D_X
A_COLS = A_AB_OFF + LANES
B_LANE = 8
B_XQ_OFF = 2 * D_B
B_COLS = 2 * D_B + D_X


def _params(*sem):
    return pltpu.CompilerParams(dimension_semantics=sem, vmem_limit_bytes=VMEM_LIMIT_BYTES)


def _split2(x):
    hi = x.astype(BF16)
    lo = (x - hi.astype(F32)).astype(BF16)
    return hi, lo


_NN = (((1,), (0,)), ((), ()))
_NT = (((1,), (1,)), ((), ()))
_TN = (((0,), (0,)), ((), ()))


def _dot(a, b, dims=_NN, passes=1):
    if passes == 1:
        return lax.dot_general(a.astype(BF16), b.astype(BF16), dims, preferred_element_type=F32)
    ah, al = _split2(a)
    bh, bl = _split2(b)
    dg = functools.partial(lax.dot_general, dimension_numbers=dims, preferred_element_type=F32)
    return dg(ah, bh) + (dg(ah, bl) + dg(al, bh))


def _dot_exact_lhs(lhs_bf16, x):
    x1 = x.astype(BF16)
    r1 = x - x1.astype(F32)
    x2 = r1.astype(BF16)
    x3 = (r1 - x2.astype(F32)).astype(BF16)
    dg = functools.partial(lax.dot_general, dimension_numbers=_NN, preferred_element_type=F32)
    return dg(lhs_bf16, x1) + (dg(lhs_bf16, x2) + dg(lhs_bf16, x3))


def _sigmoid(x):
    return 1.0 / (1.0 + jnp.exp(-x))


def _silu(x):
    return x * _sigmoid(x)


def _softplus(x):
    return jnp.maximum(x, 0.0) + jnp.log1p(jnp.exp(-jnp.abs(x)))


def _gelu_tanh(x):
    c = 0.7978845608028654
    return 0.5 * x * (1.0 + jnp.tanh(c * (x + 0.044715 * (x * x * x))))


def _layer_norm(y, g, b):
    mu = jnp.mean(y, axis=-1, keepdims=True)
    d = y - mu
    var = jnp.mean(d * d, axis=-1, keepdims=True)
    return d * lax.rsqrt(var + LN_EPS) * g + b


def _lane_pick(x, lane_iota, idx):
    return jnp.sum(jnp.where(lane_iota == idx, x, 0.0), axis=-1, keepdims=True)


def _mm_kernel(x_ref, w_ref, o_ref):
    o_ref[...] = jnp.dot(x_ref[...].astype(BF16), w_ref[...],
                         preferred_element_type=F32).astype(o_ref.dtype)


def _matmul(x, w, tm, tn, name):
    m, k = x.shape
    n = w.shape[1]
    return pl.pallas_call(
        _mm_kernel,
        grid=(m // tm, n // tn),
        in_specs=[pl.BlockSpec((tm, k), lambda i, j: (i, 0)),
                  pl.BlockSpec((k, tn), lambda i, j: (0, j))],
        out_specs=pl.BlockSpec((tm, tn), lambda i, j: (i, j)),
        out_shape=jax.ShapeDtypeStruct((m, n), F32),
        compiler_params=_params("parallel", "parallel"),
        name=name,
    )(x, w)


def _outproj_ln_kernel(mix_ref, xo_ref, x_ref, wm_ref, wx_ref, g_ref, b_ref, o_ref):
    h = jnp.dot(mix_ref[...].astype(BF16), wm_ref[...], preferred_element_type=F32)
    h = h + jnp.dot(xo_ref[...].astype(BF16), wx_ref[...], preferred_element_type=F32)
    o_ref[...] = _layer_norm(ALPHA * x_ref[...] + h, g_ref[...], b_ref[...])


def _outproj_ln(mix, xo, x, w_out, g, b, tm):
    m = x.shape[0]
    d_mix = mix.shape[1]
    n_blk = d_mix // D_X
    return pl.pallas_call(
        _outproj_ln_kernel,
        grid=(m // tm,),
        in_specs=[pl.BlockSpec((tm, d_mix), lambda i: (i, 0)),
                  pl.BlockSpec((tm, D_X), lambda i: (i, 0)),
                  pl.BlockSpec((tm, D_MODEL), lambda i: (i, 0)),
                  pl.BlockSpec((d_mix, D_MODEL), lambda i: (0, 0)),
                  pl.BlockSpec((D_X, D_MODEL), lambda i: (n_blk, 0)),
                  pl.BlockSpec((1, D_MODEL), lambda i: (0, 0)),
                  pl.BlockSpec((1, D_MODEL), lambda i: (0, 0))],
        out_specs=pl.BlockSpec((tm, D_MODEL), lambda i: (i, 0)),
        out_shape=jax.ShapeDtypeStruct((m, D_MODEL), F32),
        compiler_params=_params("parallel"),
        name="outproj_ln",
    )(mix, xo, x, w_out, w_out, g, b)


FFN_CHUNK = 256


def _swiglu_partial(xb, wg_ref, wu_ref, wd_ref, row_scale=None):
    y = None
    for c0 in range(0, wg_ref.shape[-1], FFN_CHUNK):
        hg = jnp.dot(xb, wg_ref[:, c0:c0 + FFN_CHUNK], preferred_element_type=F32)
        hu = jnp.dot(xb, wu_ref[:, c0:c0 + FFN_CHUNK], preferred_element_type=F32)
        h = _silu(hg) * hu
        if row_scale is not None:
            h = h * row_scale
        part = jnp.dot(h.astype(BF16), wd_ref[c0:c0 + FFN_CHUNK, :], preferred_element_type=F32)
        y = part if y is None else y + part
    return y


def _ffn_kernel(*refs, moe):
    if moe:
        x_ref, comb_ref, wg_ref, wu_ref, wd_ref, g_ref, b_ref, o_ref, xb_ref, acc_ref = refs
    else:
        x_ref, wg_ref, wu_ref, wd_ref, g_ref, b_ref, o_ref, xb_ref, acc_ref = refs
    e = pl.program_id(1)
    f = pl.program_id(2)
    first = jnp.logical_and(e == 0, f == 0)
    last = jnp.logical_and(e == pl.num_programs(1) - 1, f == pl.num_programs(2) - 1)

    @pl.when(first)
    def _():
        xb_ref[...] = x_ref[...].astype(BF16)
        acc_ref[...] = jnp.zeros_like(acc_ref)

    row_scale = None
    if moe:
        comb = comb_ref[...]
        lane = lax.broadcasted_iota(jnp.int32, comb.shape, 1)
        row_scale = _lane_pick(comb, lane, e)
    acc_ref[...] += _swiglu_partial(xb_ref[...], wg_ref, wu_ref, wd_ref, row_scale)

    @pl.when(last)
    def _():
        o_ref[...] = _layer_norm(ALPHA * x_ref[...] + acc_ref[...], g_ref[...], b_ref[...])


def _ffn_ln(x, comb, w_gu, w_down, layer, g, b, tm, tf):
    m = x.shape[0]
    moe = comb is not None
    n_e = w_gu.shape[1]
    n_f = D_FF // tf
    in_specs = [pl.BlockSpec((tm, D_MODEL), lambda i, e, f: (i, 0))]
    args = [x]
    if moe:
        in_specs.append(pl.BlockSpec((tm, LANES), lambda i, e, f: (i, 0)))
        args.append(comb)
    in_specs += [pl.BlockSpec((None, None, D_MODEL, tf), lambda i, e, f: (layer, e, 0, f)),
                 pl.BlockSpec((None, None, D_MODEL, tf), lambda i, e, f: (layer, e, 0, n_f + f)),
                 pl.BlockSpec((None, None, tf, D_MODEL), lambda i, e, f: (layer, e, f, 0)),
                 pl.BlockSpec((1, D_MODEL), lambda i, e, f: (0, 0)),
                 pl.BlockSpec((1, D_MODEL), lambda i, e, f: (0, 0))]
    args += [w_gu, w_gu, w_down, g, b]
    return pl.pallas_call(
        functools.partial(_ffn_kernel, moe=moe),
        grid=(m // tm, n_e, n_f),
        in_specs=in_specs,
        out_specs=pl.BlockSpec((tm, D_MODEL), lambda i, e, f: (i, 0)),
        out_shape=jax.ShapeDtypeStruct((m, D_MODEL), F32),
        scratch_shapes=[pltpu.VMEM((tm, D_MODEL), BF16), pltpu.VMEM((tm, D_MODEL), F32)],
        compiler_params=_params("parallel", "arbitrary", "arbitrary"),
        name="moe_ln" if moe else "ffn_ln",
    )(*args)


def _router_kernel(x_ref, wh_ref, wl_ref, comb_ref, gate_ref, idx_ref):
    x = x_ref[...]
    xh, xl = _split2(x)
    wh = wh_ref[...]
    lg = jnp.dot(xh, wh, preferred_element_type=F32)
    lg = lg + (jnp.dot(xh, wl_ref[...], preferred_element_type=F32)
               + jnp.dot(xl, wh, preferred_element_type=F32))
    lane = lax.broadcasted_iota(jnp.int32, lg.shape, 1)
    neg = -jnp.inf
    lg = jnp.where(lane < N_EXP, lg, neg)
    m1 = jnp.max(lg, axis=-1, keepdims=True)
    i1 = jnp.min(jnp.where(lg == m1, lane, LANES), axis=-1, keepdims=True)
    lg2 = jnp.where(lane == i1, neg, lg)
    m2 = jnp.max(lg2, axis=-1, keepdims=True)
    i2 = jnp.min(jnp.where(lg2 == m2, lane, LANES), axis=-1, keepdims=True)
    e2 = jnp.exp(m2 - m1)
    den = 1.0 + e2
    g1 = 1.0 / den
    g2 = e2 / den
    comb_ref[...] = jnp.where(lane == i1, g1, 0.0) + jnp.where(lane == i2, g2, 0.0)
    gate_ref[...] = jnp.where(lane == 0, g1, jnp.where(lane == 1, g2, 0.0))
    idx_ref[...] = jnp.where(lane == 0, i1, jnp.where(lane == 1, i2, 0))


def _router(x, w_hi, w_lo, tm):
    m = x.shape[0]
    out = pl.BlockSpec((tm, LANES), lambda i: (i, 0))
    return pl.pallas_call(
        _router_kernel,
        grid=(m // tm,),
        in_specs=[pl.BlockSpec((tm, D_MODEL), lambda i: (i, 0)),
                  pl.BlockSpec((D_MODEL, LANES), lambda i: (0, 0)),
                  pl.BlockSpec((D_MODEL, LANES), lambda i: (0, 0))],
        out_specs=[out, out, out],
        out_shape=[jax.ShapeDtypeStruct((m, LANES), F32),
                   jax.ShapeDtypeStruct((m, LANES), F32),
                   jax.ShapeDtypeStruct((m, LANES), jnp.int32)],
        compiler_params=_params("parallel"),
        name="router",
    )(x, w_hi, w_lo)


MOE_TM = 512
FFN_TF = 1792
DISPATCH_TOKENS = 1024
COMBINE_TOKENS = 256


def _moe_plan(idx2, tm):
    t = idx2.shape[0]
    e_flat = idx2.reshape(-1)
    experts = jnp.arange(N_EXP, dtype=jnp.int32)
    onehot = (e_flat[:, None] == experts[None, :]).astype(jnp.int32)
    csum = jnp.cumsum(onehot, axis=0)
    rank = jnp.sum(csum * onehot, axis=1) - 1
    counts = csum[-1]
    ptiles = (counts + tm - 1) // tm
    tile_end = jnp.cumsum(ptiles)
    pstart = (tile_end - ptiles) * tm
    pos = jnp.sum(onehot * pstart[None, :], axis=1) + rank
    n_used = tile_end[-1]
    n_tiles = (2 * t) // tm + N_EXP
    jj = jnp.minimum(jnp.arange(n_tiles, dtype=jnp.int32), n_used - 1)
    tile_expert = jnp.sum((jj[:, None] >= tile_end[None, :]).astype(jnp.int32), axis=1)
    seg_base = jnp.concatenate([pstart + counts, (n_used * tm).reshape(1)])
    seg_len = jnp.concatenate([ptiles * tm - counts, ((n_tiles - n_used) * tm).reshape(1)])
    seg_end = jnp.cumsum(seg_len)
    j = jnp.arange(N_EXP * tm, dtype=jnp.int32)
    seg = jnp.sum((j[:, None] >= seg_end[None, :]).astype(jnp.int32), axis=1)
    seg_1h = (seg[:, None] == jnp.arange(N_EXP + 1, dtype=jnp.int32)[None, :]).astype(jnp.int32)
    pad_dst = j + jnp.sum(seg_1h * (seg_base - (seg_end - seg_len))[None, :], axis=1)
    return (pos.astype(jnp.int32), tile_expert.astype(jnp.int32), n_used.reshape(1).astype(jnp.int32),
            pad_dst.astype(jnp.int32), n_tiles)


def _wait_rows(hbm_ref, n_rows, sem):
    rows = hbm_ref.at[pl.ds(0, n_rows), :]
    pltpu.make_async_copy(rows, rows, sem).wait()


def _tile_row_copy(tiles_ref, tile, row, hbm_ref, hbm_row, sem, to_hbm):
    vm = tiles_ref.at[tile, pl.ds(row, 1), :]
    hb = hbm_ref.at[pl.ds(hbm_row, 1), :]
    return pltpu.make_async_copy(vm, hb, sem) if to_hbm else pltpu.make_async_copy(hb, vm, sem)


def _moe_dispatch_kernel(pos_ref, pad_dst_ref, x_ref, xs_ref, sem):
    n_grp = x_ref.shape[0]
    n_pad = pad_dst_ref.shape[1]

    def send(grp, carry):
        for jr in range(SUBLANES):
            for k in range(2):
                dst = pos_ref[0, 2 * SUBLANES * grp + 2 * jr + k]
                _tile_row_copy(x_ref, grp, jr, xs_ref, dst, sem, True).start()
        return carry

    lax.fori_loop(0, n_grp, send, 0)

    def pad(grp, carry):
        for jr in range(SUBLANES):
            _tile_row_copy(x_ref, 0, jr, xs_ref, pad_dst_ref[0, SUBLANES * grp + jr], sem, True).start()
        return carry

    lax.fori_loop(0, n_pad // SUBLANES, pad, 0)
    _wait_rows(xs_ref, 2 * SUBLANES * n_grp + n_pad, sem)


def _moe_dispatch(x, pos, pad_dst, n_rows_out):
    m = x.shape[0]
    tq = DISPATCH_TOKENS
    n_steps = m // tq
    n_pad = pad_dst.shape[0] // n_steps
    smem = functools.partial(pl.BlockSpec, memory_space=pltpu.SMEM)
    return pl.pallas_call(
        _moe_dispatch_kernel,
        grid=(n_steps,),
        in_specs=[smem((None, 1, 2 * tq), lambda i: (i, 0, 0)),
                  smem((None, 1, n_pad), lambda i: (i, 0, 0)),
                  pl.BlockSpec((tq // SUBLANES, SUBLANES, D_MODEL), lambda i: (i, 0, 0))],
        out_specs=pl.BlockSpec(memory_space=pl.ANY),
        out_shape=jax.ShapeDtypeStruct((n_rows_out, D_MODEL), F32),
        scratch_shapes=[pltpu.SemaphoreType.DMA(())],
        compiler_params=_params("arbitrary"),
        name="moe_dispatch",
    )(pos.reshape(n_steps, 1, 2 * tq), pad_dst.reshape(n_steps, 1, n_pad),
      x.reshape(m // SUBLANES, SUBLANES, D_MODEL))


def _moe_ffn_kernel(te_ref, nu_ref, x_ref, wg_ref, wu_ref, wd_ref, o_ref, xb_ref):
    p = pl.program_id(0)
    f = pl.program_id(1)

    @pl.when(p < nu_ref[0])
    def _():
        @pl.when(f == 0)
        def _():
            xb_ref[...] = x_ref[...].astype(BF16)

        y = _swiglu_partial(xb_ref[...], wg_ref, wu_ref, wd_ref)

        @pl.when(f == 0)
        def _():
            o_ref[...] = y

        @pl.when(f != 0)
        def _():
            o_ref[...] += y

    @pl.when(jnp.logical_and(p >= nu_ref[0], f == 0))
    def _():
        o_ref[...] = jnp.zeros_like(o_ref)


def _moe_ffn(xs, tile_expert, n_used, w_gu, w_down, layer, tm, tf):
    n_tiles = xs.shape[0] // tm
    n_f = D_FF // tf
    tile = lambda p, nu: jnp.minimum(p, nu[0] - 1)
    col = lambda p, f, nu: jnp.where(p < nu[0], f, n_f - 1)
    grid_spec = pltpu.PrefetchScalarGridSpec(
        num_scalar_prefetch=2,
        grid=(n_tiles, n_f),
        in_specs=[pl.BlockSpec((tm, D_MODEL), lambda p, f, te, nu: (tile(p, nu), 0)),
                  pl.BlockSpec((None, None, D_MODEL, tf),
                               lambda p, f, te, nu: (layer, te[p], 0, col(p, f, nu))),
                  pl.BlockSpec((None, None, D_MODEL, tf),
                               lambda p, f, te, nu: (layer, te[p], 0, n_f + col(p, f, nu))),
                  pl.BlockSpec((None, None, tf, D_MODEL),
                               lambda p, f, te, nu: (layer, te[p], col(p, f, nu), 0))],
        out_specs=pl.BlockSpec((tm, D_MODEL), lambda p, f, te, nu: (p, 0)),
        scratch_shapes=[pltpu.VMEM((tm, D_MODEL), BF16)],
    )
    return pl.pallas_call(
        _moe_ffn_kernel,
        grid_spec=grid_spec,
        out_shape=jax.ShapeDtypeStruct(xs.shape, F32),
        compiler_params=_params("arbitrary", "arbitrary"),
        name="moe_ffn",
    )(tile_expert, n_used, xs, w_gu, w_gu, w_down)


def _moe_combine_ln_kernel(pos_ref, gate_ref, x_ref, ye_ref, g_ref, b_ref, o_ref, buf_ref, sem):
    tq = x_ref.shape[0]
    n_grp = tq // SUBLANES

    def fetch(grp, carry):
        for jr in range(SUBLANES):
            for k in range(2):
                src = pos_ref[0, 2 * SUBLANES * grp + 2 * jr + k]
                _tile_row_copy(buf_ref, k * n_grp + grp, jr, ye_ref, src, sem, False).start()
        return carry

    lax.fori_loop(0, n_grp, fetch, 0)
    _wait_rows(ye_ref, 2 * tq, sem)
    gates = gate_ref[...]
    lane = lax.broadcasted_iota(jnp.int32, gates.shape, 1)
    y0 = buf_ref[0:n_grp].reshape(tq, D_MODEL)
    y1 = buf_ref[n_grp:2 * n_grp].reshape(tq, D_MODEL)
    y = _lane_pick(gates, lane, 0) * y0 + _lane_pick(gates, lane, 1) * y1
    o_ref[...] = _layer_norm(ALPHA * x_ref[...] + y, g_ref[...], b_ref[...])


def _moe_combine_ln(x, gates, pos, ye, g, b):
    m = x.shape[0]
    tq = COMBINE_TOKENS
    n_steps = m // tq
    return pl.pallas_call(
        _moe_combine_ln_kernel,
        grid=(n_steps,),
        in_specs=[pl.BlockSpec((None, 1, 2 * tq), lambda i: (i, 0, 0), memory_space=pltpu.SMEM),
                  pl.BlockSpec((tq, LANES), lambda i: (i, 0)),
                  pl.BlockSpec((tq, D_MODEL), lambda i: (i, 0)),
                  pl.BlockSpec(memory_space=pl.ANY),
                  pl.BlockSpec((1, D_MODEL), lambda i: (0, 0)),
                  pl.BlockSpec((1, D_MODEL), lambda i: (0, 0))],
        out_specs=pl.BlockSpec((tq, D_MODEL), lambda i: (i, 0)),
        out_shape=jax.ShapeDtypeStruct((m, D_MODEL), F32),
        scratch_shapes=[pltpu.VMEM((2 * tq // SUBLANES, SUBLANES, D_MODEL), F32),
                        pltpu.SemaphoreType.DMA(())],
        compiler_params=_params("arbitrary"),
        name="moe_combine_ln",
    )(pos.reshape(n_steps, 1, 2 * tq), gates, x, ye, g, b)


def _moe_routed_ln(x, gates, idx, w_gu, w_down, layer, g, b):
    pos, tile_expert, n_used, pad_dst, n_tiles = _moe_plan(idx[:, :2], MOE_TM)
    xs = _moe_dispatch(x, pos, pad_dst, n_tiles * MOE_TM)
    ye = _moe_ffn(xs, tile_expert, n_used, w_gu, w_down, layer, MOE_TM, FFN_TF)
    return _moe_combine_ln(x, gates, pos, ye, g, b)


GDN_PASSES_QK = 1
GDN_PASSES_SOLVE = 1
GDN_PASSES_STATE = 1
INV_BLOCK_SHIFT = 4
GDN_SEQS = 4


def _gdn_chunk_kernel(qkv_ref, z_ref, ab_ref, cw_ref, alog_ref, dtb_ref, nw_ref, hist0_ref, s0_ref,
                      mix_ref, sout_ref, hout_ref, xh_ref, s_ref):
    n = pl.program_id(1)
    c = GDN_CHUNK
    hrows = SUBLANES
    n_seq = qkv_ref.shape[0]
    items = [(b, h) for b in range(n_seq) for h in range(H_A)]

    @pl.when(n == 0)
    def _():
        xh_ref[:, 0:hrows, :] = hist0_ref[...]
        s_ref[...] = s0_ref[...]

    xh_ref[:, hrows:hrows + c, :] = qkv_ref[...]

    row = lax.broadcasted_iota(jnp.int32, (c, c), 0)
    col = lax.broadcasted_iota(jnp.int32, (c, c), 1)
    causal = row >= col
    strict = row > col
    eye = jnp.where(row == col, 1.0, 0.0)
    blockdiag = (row >> INV_BLOCK_SHIFT) == (col >> INV_BLOCK_SHIFT)
    tril_ones = jnp.where(causal, 1.0, 0.0).astype(BF16)
    lane = lax.broadcasted_iota(jnp.int32, (c, LANES), 1)
    mm_s = functools.partial(_dot, passes=GDN_PASSES_SOLVE)

    beta_all, gcum_all, gcum_t = [], [], []
    for b in range(n_seq):
        ab = ab_ref[b]
        g_all = -jnp.exp(alog_ref[...]) * _softplus(ab + dtb_ref[...])
        beta_all.append(_sigmoid(ab))
        gc = _dot_exact_lhs(tril_ones, g_all)
        gcum_all.append(gc)
        gcum_t.append(gc.T)

    def conv_silu(b, c0):
        acc = xh_ref[b, hrows - 3:hrows - 3 + c, c0:c0 + LANES] * cw_ref[0:1, c0:c0 + LANES]
        for j in range(1, CONV_W):
            acc = acc + (xh_ref[b, hrows - 3 + j:hrows - 3 + j + c, c0:c0 + LANES]
                         * cw_ref[j:j + 1, c0:c0 + LANES])
        return _silu(acc)

    q, k, v, kb, egc, gc_col, gc_last, beta_col, decay = {}, {}, {}, {}, {}, {}, {}, {}, {}
    for it in items:
        b, h = it
        qi = conv_silu(b, h * DK_A)
        ki = conv_silu(b, QK_A + h * DK_A)
        v[it] = conv_silu(b, 2 * QK_A + h * DV_A)
        q[it] = qi * lax.rsqrt(jnp.sum(qi * qi, axis=-1, keepdims=True) + RMS_EPS) * (DK_A ** -0.5)
        k[it] = ki * lax.rsqrt(jnp.sum(ki * ki, axis=-1, keepdims=True) + RMS_EPS)
        gc_col[it] = _lane_pick(gcum_all[b], lane, h)
        beta_col[it] = _lane_pick(beta_all[b], lane, B_LANE + h)
        gc_last[it] = gc_col[it][c - 1:c, :]
        decay[it] = jnp.where(causal, jnp.exp(gc_col[it] - gcum_t[b][h:h + 1, :c]), 0.0)
        egc[it] = jnp.exp(gc_col[it])
        kb[it] = k[it] * beta_col[it]

    kk = {it: _dot(jnp.concatenate([kb[it], q[it]], axis=0), k[it], _NT, GDN_PASSES_QK) for it in items}
    a_mat = {it: jnp.where(strict, kk[it][:c] * decay[it], 0.0) for it in items}
    attn = {it: kk[it][c:] * decay[it] for it in items}

    a_d = {it: jnp.where(blockdiag, a_mat[it], 0.0) for it in items}
    a_n = {it: jnp.where(blockdiag, 0.0, a_mat[it]) for it in items}
    p2 = {it: mm_s(a_d[it], a_d[it]) for it in items}
    e_d = {it: p2[it] - a_d[it] - mm_s(a_d[it], p2[it]) for it in items}
    p4 = {it: mm_s(p2[it], p2[it]) for it in items}
    e_d = {it: e_d[it] + p4[it] + mm_s(e_d[it], p4[it]) for it in items}
    p8 = {it: mm_s(p4[it], p4[it]) for it in items}
    e_d = {it: e_d[it] + p8[it] + mm_s(e_d[it], p8[it]) for it in items}
    m_blk = {it: a_n[it] + mm_s(e_d[it], a_n[it]) for it in items}
    rhs = {it: jnp.concatenate([v[it] * beta_col[it], kb[it] * egc[it]], axis=1) for it in items}
    rhs = {it: rhs[it] + mm_s(e_d[it], rhs[it]) for it in items}
    m2 = {it: mm_s(m_blk[it], m_blk[it]) for it in items}
    f_y = {it: m2[it] - m_blk[it] - mm_s(m_blk[it], m2[it]) for it in items}
    sol = {it: rhs[it] + mm_s(f_y[it], rhs[it]) for it in items}

    s_old = {it: s_ref[it[0], it[1]] for it in items}
    wq = {it: _dot(jnp.concatenate([sol[it][:, DV_A:], q[it] * egc[it]], axis=0), s_old[it],
                   _NN, GDN_PASSES_STATE) for it in items}
    v_new = {it: sol[it][:, :DV_A] - wq[it][:c] for it in items}
    o = {it: wq[it][c:] + _dot(attn[it], v_new[it], _NN, GDN_PASSES_STATE) for it in items}
    for it in items:
        k_dec = k[it] * jnp.exp(gc_last[it] - gc_col[it])
        s_ref[it[0], it[1]] = (s_old[it] * jnp.exp(gc_last[it])
                               + _dot(k_dec, v_new[it], _TN, GDN_PASSES_STATE))
    for it in items:
        b, h = it
        oi = o[it]
        oi = oi * lax.rsqrt(jnp.mean(oi * oi, axis=-1, keepdims=True) + RMS_EPS) * nw_ref[...]
        mix_ref[b, :, h * DV_A:(h + 1) * DV_A] = oi * _silu(z_ref[b, :, h * DV_A:(h + 1) * DV_A])

    xh_ref[:, 0:hrows, :] = xh_ref[:, c:c + hrows, :]

    @pl.when(n == pl.num_programs(1) - 1)
    def _():
        sout_ref[...] = s_ref[...]
        hout_ref[...] = xh_ref[:, c:c + hrows, :]


def _gdn_chunked(proj, conv_w, alog_row, dtb_row, nw_row, hist0, s0):
    bsz, t, _ = proj.shape
    c = GDN_CHUNK
    nb = GDN_SEQS
    n_chunks = t // c
    row_spec = lambda width, blk: pl.BlockSpec((nb, c, width), lambda b, n: (b, n, blk))
    full2 = lambda shape: pl.BlockSpec(shape, lambda b, n: (0, 0))
    return pl.pallas_call(
        _gdn_chunk_kernel,
        grid=(bsz // nb, n_chunks),
        in_specs=[row_spec(CONV_DIM, 0),
                  row_spec(V_A, A_Z_OFF // V_A),
                  row_spec(LANES, A_AB_OFF // LANES),
                  full2((CONV_W, CONV_DIM)),
                  full2((1, LANES)), full2((1, LANES)), full2((1, DV_A)),
                  pl.BlockSpec((nb, SUBLANES, CONV_DIM), lambda b, n: (b, 0, 0)),
                  pl.BlockSpec((nb, H_A, DK_A, DV_A), lambda b, n: (b, 0, 0, 0))],
        out_specs=[pl.BlockSpec((nb, c, V_A), lambda b, n: (b, n, 0)),
                   pl.BlockSpec((nb, H_A, DK_A, DV_A), lambda b, n: (b, 0, 0, 0)),
                   pl.BlockSpec((nb, SUBLANES, CONV_DIM), lambda b, n: (b, 0, 0))],
        out_shape=[jax.ShapeDtypeStruct((bsz, t, V_A), F32),
                   jax.ShapeDtypeStruct((bsz, H_A, DK_A, DV_A), F32),
                   jax.ShapeDtypeStruct((bsz, SUBLANES, CONV_DIM), F32)],
        scratch_shapes=[pltpu.VMEM((nb, SUBLANES + c, CONV_DIM), F32),
                        pltpu.VMEM((nb, H_A, DK_A, DV_A), F32)],
        compiler_params=_params("parallel", "arbitrary"),
        name="gdn_chunked",
    )(proj, proj, proj, conv_w, alog_row, dtb_row, nw_row, hist0, s0)


def _gdn_step_kernel(qkv_ref, z_ref, ab_ref, cw_ref, alog_ref, dtb_ref, nw_ref, hist_ref, s_ref,
                     mix_ref, sout_ref, hout_ref, o_scr):
    rows = qkv_ref.shape[0]
    new = qkv_ref[...]
    y = hist_ref[0] * cw_ref[0:1, :]
    for j in range(1, CONV_W - 1):
        y = y + hist_ref[j] * cw_ref[j:j + 1, :]
    y = y + new * cw_ref[CONV_W - 1:CONV_W, :]
    qkv = _silu(y)
    for j in range(CONV_W - 2):
        hout_ref[j] = hist_ref[j + 1]
    hout_ref[CONV_W - 2] = new

    ab = ab_ref[...]
    lane = lax.broadcasted_iota(jnp.int32, ab.shape, 1)
    g_all = -jnp.exp(alog_ref[...]) * _softplus(ab + dtb_ref[...])
    beta_all = _sigmoid(ab)
    ri = lax.broadcasted_iota(jnp.int32, (DK_A, DK_A), 0)
    ci = lax.broadcasted_iota(jnp.int32, (DK_A, DK_A), 1)
    diag = ri == ci

    def as_column(r):
        return jnp.sum(jnp.where(diag, r, 0.0), axis=-1, keepdims=True)

    for h in range(H_A):
        q = qkv[:, h * DK_A:(h + 1) * DK_A]
        k = qkv[:, QK_A + h * DK_A:QK_A + (h + 1) * DK_A]
        v = qkv[:, 2 * QK_A + h * DV_A:2 * QK_A + (h + 1) * DV_A]
        q = q * lax.rsqrt(jnp.sum(q * q, axis=-1, keepdims=True) + RMS_EPS) * (DK_A ** -0.5)
        k = k * lax.rsqrt(jnp.sum(k * k, axis=-1, keepdims=True) + RMS_EPS)
        eg = jnp.exp(_lane_pick(g_all, lane, h))
        beta = _lane_pick(beta_all, lane, B_LANE + h)
        for s in range(rows):
            k_col = as_column(k[s:s + 1, :])
            q_col = as_column(q[s:s + 1, :])
            st = s_ref[s, h] * eg[s:s + 1, :]
            ks = jnp.sum(k_col * st, axis=0, keepdims=True)
            u = beta[s:s + 1, :] * (v[s:s + 1, :] - ks)
            st = st + k_col * u
            sout_ref[s, h] = st
            o_scr[s:s + 1, h * DV_A:(h + 1) * DV_A] = jnp.sum(q_col * st, axis=0, keepdims=True)

    for h in range(H_A):
        o = o_scr[:, h * DV_A:(h + 1) * DV_A]
        o = o * lax.rsqrt(jnp.mean(o * o, axis=-1, keepdims=True) + RMS_EPS) * nw_ref[...]
        mix_ref[:, h * DV_A:(h + 1) * DV_A] = o * _silu(z_ref[:, h * DV_A:(h + 1) * DV_A])


def _gdn_step(proj, conv_w, alog_row, dtb_row, nw_row, hist_t, s0, layer):
    bsz = proj.shape[0]
    rows = SUBLANES
    full1 = lambda shape: pl.BlockSpec(shape, lambda i: (0, 0))
    return pl.pallas_call(
        _gdn_step_kernel,
        grid=(bsz // rows,),
        in_specs=[pl.BlockSpec((rows, CONV_DIM), lambda i: (i, 0)),
                  pl.BlockSpec((rows, V_A), lambda i: (i, A_Z_OFF // V_A)),
                  pl.BlockSpec((rows, LANES), lambda i: (i, A_AB_OFF // LANES)),
                  full1((CONV_W, CONV_DIM)),
                  full1((1, LANES)), full1((1, LANES)), full1((1, DV_A)),
                  pl.BlockSpec((None, CONV_W - 1, rows, CONV_DIM), lambda i: (layer, 0, i, 0)),
                  pl.BlockSpec((None, rows, H_A, DK_A, DV_A), lambda i: (layer, i, 0, 0, 0))],
        out_specs=[pl.BlockSpec((rows, V_A), lambda i: (i, 0)),
                   pl.BlockSpec((rows, H_A, DK_A, DV_A), lambda i: (i, 0, 0, 0)),
                   pl.BlockSpec((CONV_W - 1, rows, CONV_DIM), lambda i: (0, i, 0))],
        out_shape=[jax.ShapeDtypeStruct((bsz, V_A), F32),
                   jax.ShapeDtypeStruct((bsz, H_A, DK_A, DV_A), F32),
                   jax.ShapeDtypeStruct((CONV_W - 1, bsz, CONV_DIM), F32)],
        scratch_shapes=[pltpu.VMEM((rows, V_A), F32)],
        compiler_params=_params("parallel"),
        name="gdn_step",
    )(proj, proj, proj, conv_w, alog_row, dtb_row, nw_row, hist_t, s0)


SGU_CHUNKS = 4


def _sgu_chunk_kernel(u_ref, v_ref, g_ref, b_ref, ws_ref, bs_ref, mix_ref):
    cb = CHUNK_B
    row = lax.broadcasted_iota(jnp.int32, (cb, cb), 0)
    col = lax.broadcasted_iota(jnp.int32, (cb, cb), 1)
    ws = [jnp.where(row >= col, ws_ref[gi], 0.0).astype(BF16) for gi in range(G_B)]
    for r0 in range(0, u_ref.shape[0], cb):
        u = _gelu_tanh(u_ref[r0:r0 + cb, :])
        v = _layer_norm(_gelu_tanh(v_ref[r0:r0 + cb, :]), g_ref[...], b_ref[...])
        for gi in range(G_B):
            mixed = _dot(ws[gi], v[:, gi * CH_B:(gi + 1) * CH_B]) + bs_ref[gi]
            mix_ref[r0:r0 + cb, gi * CH_B:(gi + 1) * CH_B] = u[:, gi * CH_B:(gi + 1) * CH_B] * mixed


def _sgu_chunked(proj, ln_g, ln_b, w_s, bs_b):
    bsz, t, _ = proj.shape
    c = CHUNK_B * SGU_CHUNKS
    return pl.pallas_call(
        _sgu_chunk_kernel,
        grid=(bsz, t // c),
        in_specs=[pl.BlockSpec((None, c, D_B), lambda b, n: (b, n, 0)),
                  pl.BlockSpec((None, c, D_B), lambda b, n: (b, n, 1)),
                  pl.BlockSpec((1, D_B), lambda b, n: (0, 0)),
                  pl.BlockSpec((1, D_B), lambda b, n: (0, 0)),
                  pl.BlockSpec((G_B, CHUNK_B, CHUNK_B), lambda b, n: (0, 0, 0)),
                  pl.BlockSpec((G_B, CHUNK_B, CH_B), lambda b, n: (0, 0, 0))],
        out_specs=pl.BlockSpec((None, c, D_B), lambda b, n: (b, n, 0)),
        out_shape=jax.ShapeDtypeStruct((bsz, t, D_B), F32),
        compiler_params=_params("parallel", "parallel"),
        name="sgu_chunked",
    )(proj, proj, ln_g, ln_b, w_s, bs_b)


def _sgu_first_kernel(u_ref, v_ref, g_ref, b_ref, w00_ref, b0_ref, mix_ref, vout_ref):
    u = _gelu_tanh(u_ref[...])
    v = _layer_norm(_gelu_tanh(v_ref[...]), g_ref[...], b_ref[...])
    vout_ref[...] = v
    mix_ref[...] = u * (w00_ref[...] * v + b0_ref[...])


def _sgu_first(proj, ln_g, ln_b, w00_row, b0_row):
    bsz = proj.shape[0]
    row = lambda: pl.BlockSpec((1, D_B), lambda i: (0, 0))
    return pl.pallas_call(
        _sgu_first_kernel,
        grid=(1,),
        in_specs=[pl.BlockSpec((bsz, D_B), lambda i: (0, 0)),
                  pl.BlockSpec((bsz, D_B), lambda i: (0, 1)),
                  row(), row(), row(), row()],
        out_specs=[pl.BlockSpec((bsz, D_B), lambda i: (0, 0)),
                   pl.BlockSpec((bsz, D_B), lambda i: (0, 0))],
        out_shape=[jax.ShapeDtypeStruct((bsz, D_B), F32),
                   jax.ShapeDtypeStruct((bsz, D_B), F32)],
        compiler_params=_params("arbitrary"),
        name="sgu_first",
    )(proj, proj, ln_g, ln_b, w00_row, b0_row)


def _head_masks(shape):
    lane = lax.broadcasted_iota(jnp.int32, shape, len(shape) - 1)
    return [(lane // DH_X) == h for h in range(H_X)]


def _xattn_kernel(q_ref, k_ref, v_ref, o_ref):
    q = q_ref[...]
    kb = k_ref[...].astype(BF16)
    vb = v_ref[...].astype(BF16)
    masks = _head_masks(q.shape)
    out = jnp.zeros(q.shape, F32)
    for h in range(H_X):
        qh = jnp.where(masks[h], q, 0.0)
        s = _dot(qh, kb, _NT) * (DH_X ** -0.5)
        s = s - jnp.max(s, axis=-1, keepdims=True)
        p = jnp.exp(s)
        p = p / jnp.sum(p, axis=-1, keepdims=True)
        out = out + jnp.where(masks[h], _dot(p, vb), 0.0)
    o_ref[...] = out


def _xattn(proj, xq_blk, mem_k, mem_v, layer, tq):
    bsz, t, _ = proj.shape
    return pl.pallas_call(
        _xattn_kernel,
        grid=(bsz, t // tq),
        in_specs=[pl.BlockSpec((None, tq, D_X), lambda b, i: (b, i, xq_blk)),
                  pl.BlockSpec((None, None, N_MEM, D_X), lambda b, i: (layer, b, 0, 0)),
                  pl.BlockSpec((None, None, N_MEM, D_X), lambda b, i: (layer, b, 0, 0))],
        out_specs=pl.BlockSpec((None, tq, D_X), lambda b, i: (b, i, 0)),
        out_shape=jax.ShapeDtypeStruct((bsz, t, D_X), F32),
        compiler_params=_params("parallel", "parallel"),
        name="xattn",
    )(proj, mem_k, mem_v)


def _xattn_step_kernel(q_ref, k_ref, v_ref, o_ref):
    rows = q_ref.shape[0]
    q = q_ref[...]
    hrow = lax.broadcasted_iota(jnp.int32, (SUBLANES, D_X), 0)
    hlane = lax.broadcasted_iota(jnp.int32, (SUBLANES, D_X), 1) // DH_X
    sel = hrow == hlane
    for s in range(rows):
        qh = jnp.where(sel, q[s:s + 1, :], 0.0)
        sc = _dot(qh, k_ref[s], _NT) * (DH_X ** -0.5)
        sc = sc - jnp.max(sc, axis=-1, keepdims=True)
        p = jnp.exp(sc)
        p = p / jnp.sum(p, axis=-1, keepdims=True)
        pv = _dot(p, v_ref[s])
        o_ref[s:s + 1, :] = jnp.sum(jnp.where(sel, pv, 0.0), axis=0, keepdims=True)


def _xattn_step(proj, xq_blk, mem_k, mem_v, layer):
    bsz = proj.shape[0]
    rows = SUBLANES
    return pl.pallas_call(
        _xattn_step_kernel,
        grid=(bsz // rows,),
        in_specs=[pl.BlockSpec((rows, D_X), lambda i: (i, xq_blk)),
                  pl.BlockSpec((None, rows, N_MEM, D_X), lambda i: (layer, i, 0, 0)),
                  pl.BlockSpec((None, rows, N_MEM, D_X), lambda i: (layer, i, 0, 0))],
        out_specs=pl.BlockSpec((rows, D_X), lambda i: (i, 0)),
        out_shape=jax.ShapeDtypeStruct((bsz, D_X), F32),
        compiler_params=_params("parallel"),
        name="xattn_step",
    )(proj, mem_k, mem_v)


def _pad_lanes(v):
    return jnp.zeros((1, LANES), F32).at[0, :v.shape[0]].set(v.astype(F32))


def _prep_weights(a_w_in, a_A_log, a_dt_bias, b_w_s, b_b_s, moe_router):
    qkvz = a_w_in[:, :, :A_XQ_OFF]
    a_cols = a_w_in[:, :, A_XQ_OFF:A_XQ_OFF + H_A]
    b_cols = a_w_in[:, :, A_XQ_OFF + H_A:A_XQ_OFF + 2 * H_A]
    xq = a_w_in[:, :, A_XQ_OFF + 2 * H_A:]
    ab = jnp.zeros(a_w_in.shape[:2] + (LANES,), F32)
    ab = ab.at[:, :, :H_A].set(a_cols).at[:, :, B_LANE:B_LANE + H_A].set(b_cols)
    a_w = jnp.concatenate([qkvz, xq, ab], axis=-1).astype(BF16)
    alog_rows = [_pad_lanes(a_A_log[j]) for j in range(a_A_log.shape[0])]
    dtb_rows = [_pad_lanes(a_dt_bias[j]) for j in range(a_dt_bias.shape[0])]
    bs_b = jnp.broadcast_to(b_b_s[..., None], b_b_s.shape + (CH_B,)).astype(F32)
    w00_rows = jnp.repeat(b_w_s[:, :, 0, 0], CH_B, axis=-1)[:, None, :]
    b0_rows = jnp.repeat(b_b_s[:, :, 0], CH_B, axis=-1)[:, None, :]
    r = jnp.zeros(moe_router.shape[:2] + (LANES,), F32).at[:, :, :N_EXP].set(moe_router)
    r_hi = r.astype(BF16)
    r_lo = (r - r_hi.astype(F32)).astype(BF16)
    return a_w, alog_rows, dtb_rows, bs_b, w00_rows, b0_rows, r_hi, r_lo


def kernel(x_prompt, x_sample, state_gdn, state_conv, cache_mem_k, cache_mem_v, mem_prompt,
           a_w_in, a_conv_w, a_A_log, a_dt_bias, a_norm_w,
           b_w_in, b_ln_g, b_ln_b, b_w_s, b_b_s,
           w_mem_kv, w_out, ln1_g, ln1_b, ln2_g, ln2_b,
           ffn_w_gu, ffn_w_down, moe_router, moe_w_gu, moe_w_down):
    n_p, t_p, _ = x_prompt.shape
    n_s = x_sample.shape[0]
    n_a = a_w_in.shape[0]

    a_w, alog_rows, dtb_rows, bs_b, w00_rows, b0_rows, r_hi, r_lo = _prep_weights(
        a_w_in, a_A_log, a_dt_bias, b_w_s, b_b_s, moe_router)
    ffn_gu = ffn_w_gu.astype(BF16)[:, None]
    ffn_dn = ffn_w_down.astype(BF16)[:, None]
    moe_gu = moe_w_gu.astype(BF16)
    moe_dn = moe_w_down.astype(BF16)
    row = lambda v: v.reshape(1, -1).astype(F32)

    w_kv = jnp.transpose(w_mem_kv, (1, 0, 2)).reshape(D_MODEL, DEPTH * 2 * D_X).astype(BF16)
    kv = _matmul(mem_prompt.reshape(n_p * N_MEM, D_MODEL), w_kv, 1024, 1024, "mem_kv")
    kv = kv.reshape(n_p, N_MEM, DEPTH, 2, D_X)
    p_mem_k = jnp.transpose(kv[:, :, :, 0, :], (2, 0, 1, 3))
    p_mem_v = jnp.transpose(kv[:, :, :, 1, :], (2, 0, 1, 3))
    s_mem_k = cache_mem_k.reshape(DEPTH, n_s, N_MEM, D_X)
    s_mem_v = cache_mem_v.reshape(DEPTH, n_s, N_MEM, D_X)

    xp = x_prompt.reshape(n_p * t_p, D_MODEL)
    xs = x_sample.reshape(n_s, D_MODEL)
    p_hist0 = jnp.zeros((n_p, SUBLANES, CONV_DIM), F32)
    p_s0 = jnp.zeros((n_p, H_A, DK_A, DV_A), F32)
    s_hist_t = jnp.transpose(state_conv, (0, 2, 1, 3))

    p_gdn, p_conv, s_gdn, s_conv, s_sgu_v = [], [], [], [], []
    tm_p, tm_s = 1024, n_s
    for i in range(DEPTH):
        j = i // 2
        if i % 2 == 0:
            proj_p = _matmul(xp, a_w[j], tm_p, 1152, "in_proj_a")
            proj_s = _matmul(xs, a_w[j], tm_s, 1152, "in_proj_a_s")
            nw = row(a_norm_w[j])
            mix_p, sp, hp = _gdn_chunked(proj_p.reshape(n_p, t_p, A_COLS), a_conv_w[j],
                                         alog_rows[j], dtb_rows[j], nw, p_hist0, p_s0)
            mix_p = mix_p.reshape(n_p * t_p, V_A)
            p_gdn.append(sp)
            p_conv.append(hp[:, SUBLANES - (CONV_W - 1):, :])
            mix_s, ss, hs = _gdn_step(proj_s, a_conv_w[j], alog_rows[j], dtb_rows[j], nw,
                                      s_hist_t, state_gdn, j)
            s_gdn.append(ss)
            s_conv.append(jnp.transpose(hs, (1, 0, 2)))
            cols, xq_blk = A_COLS, A_XQ_OFF // D_X
        else:
            b_w = b_w_in[j].astype(BF16)
            proj_p = _matmul(xp, b_w, tm_p, 896, "in_proj_b")
            proj_s = _matmul(xs, b_w, tm_s, 896, "in_proj_b_s")
            mix_p = _sgu_chunked(proj_p.reshape(n_p, t_p, B_COLS), row(b_ln_g[j]), row(b_ln_b[j]),
                                 b_w_s[j], bs_b[j]).reshape(n_p * t_p, D_B)
            mix_s, v_s = _sgu_first(proj_s, row(b_ln_g[j]), row(b_ln_b[j]), w00_rows[j], b0_rows[j])
            s_sgu_v.append(v_s.reshape(n_s, 1, D_B))
            cols, xq_blk = B_COLS, B_XQ_OFF // D_X
        xo_p = _xattn(proj_p.reshape(n_p, t_p, cols), xq_blk, p_mem_k, p_mem_v, i, 512)
        xo_s = _xattn_step(proj_s, xq_blk, s_mem_k, s_mem_v, i)
        g1, b1, g2, b2 = row(ln1_g[i]), row(ln1_b[i]), row(ln2_g[i]), row(ln2_b[i])
        w_o = w_out[i].astype(BF16)
        xp = _outproj_ln(mix_p, xo_p.reshape(n_p * t_p, D_X), xp, w_o, g1, b1, 512)
        xs = _outproj_ln(mix_s, xo_s, xs, w_o, g1, b1, tm_s)
        if i % 2 == 0:
            xp = _ffn_ln(xp, None, ffn_gu, ffn_dn, j, g2, b2, tm_p, FFN_TF)
            xs = _ffn_ln(xs, None, ffn_gu, ffn_dn, j, g2, b2, tm_s, FFN_TF)
        else:
            _, gates_p, idx_p = _router(xp, r_hi[j], r_lo[j], tm_p)
            comb_s, _, _ = _router(xs, r_hi[j], r_lo[j], tm_s)
            xp = _moe_routed_ln(xp, gates_p, idx_p, moe_gu, moe_dn, j, g2, b2)
            xs = _ffn_ln(xs, comb_s, moe_gu, moe_dn, j, g2, b2, tm_s, FFN_TF)

    mem_shape = (DEPTH, n_p, N_MEM, H_X, DH_X)
    return (xp.reshape(n_p, t_p, D_MODEL),
            xs.reshape(n_s, 1, D_MODEL),
            jnp.stack(p_gdn),
            jnp.stack(p_conv),
            p_mem_k.reshape(mem_shape),
            p_mem_v.reshape(mem_shape),
            jnp.stack(s_gdn),
            jnp.stack(s_conv),
            jnp.stack(s_sgu_v))
```

```python
import functools

import jax
import jax.numpy as jnp
from jax import lax
from jax.experimental import pallas as pl
from jax.experimental.pallas import tpu as pltpu

F32 = jnp.float32
BF16 = jnp.bfloat16

D_MODEL = 1024
DEPTH = 4
H_A = 6
DK_A = 128
DV_A = 128
QK_A = H_A * DK_A
V_A = H_A * DV_A
CONV_W = 4
CONV_DIM = 2 * QK_A + V_A
GDN_CHUNK = 64
G_B = 6
CH_B = 128
D_B = G_B * CH_B
CHUNK_B = 128
N_MEM = 256
H_X = 4
DH_X = 64
D_X = H_X * DH_X
D_FF = 3584
N_EXP = 8
ALPHA = (2 * DEPTH) ** 0.25
LN_EPS = 1e-5
RMS_EPS = 1e-6

LANES = 128
SUBLANES = 8
VMEM_LIMIT_BYTES = 56 * 1024 * 1024

A_Z_OFF = CONV_DIM
A_XQ_OFF = CONV_DIM + V_A
A_AB_OFF = A_XQ_OFF + D_X
A_COLS = A_AB_OFF + LANES
B_LANE = 8
B_XQ_OFF = 2 * D_B
B_COLS = 2 * D_B + D_X


def _params(*sem):
    return pltpu.CompilerParams(dimension_semantics=sem, vmem_limit_bytes=VMEM_LIMIT_BYTES)


def _split2(x):
    hi = x.astype(BF16)
    lo = (x - hi.astype(F32)).astype(BF16)
    return hi, lo


_NN = (((1,), (0,)), ((), ()))
_NT = (((1,), (1,)), ((), ()))
_TN = (((0,), (0,)), ((), ()))


def _dot(a, b, dims=_NN, passes=1):
    if passes == 1:
        return lax.dot_general(a.astype(BF16), b.astype(BF16), dims, preferred_element_type=F32)
    ah, al = _split2(a)
    bh, bl = _split2(b)
    dg = functools.partial(lax.dot_general, dimension_numbers=dims, preferred_element_type=F32)
    return dg(ah, bh) + (dg(ah, bl) + dg(al, bh))


def _dot_exact_lhs(lhs_bf16, x):
    x1 = x.astype(BF16)
    r1 = x - x1.astype(F32)
    x2 = r1.astype(BF16)
    x3 = (r1 - x2.astype(F32)).astype(BF16)
    dg = functools.partial(lax.dot_general, dimension_numbers=_NN, preferred_element_type=F32)
    return dg(lhs_bf16, x1) + (dg(lhs_bf16, x2) + dg(lhs_bf16, x3))


def _sigmoid(x):
    return 1.0 / (1.0 + jnp.exp(-x))


def _silu(x):
    return x * _sigmoid(x)


def _softplus(x):
    return jnp.maximum(x, 0.0) + jnp.log1p(jnp.exp(-jnp.abs(x)))


def _gelu_tanh(x):
    c = 0.7978845608028654
    return 0.5 * x * (1.0 + jnp.tanh(c * (x + 0.044715 * (x * x * x))))


def _layer_norm(y, g, b):
    mu = jnp.mean(y, axis=-1, keepdims=True)
    d = y - mu
    var = jnp.mean(d * d, axis=-1, keepdims=True)
    return d * lax.rsqrt(var + LN_EPS) * g + b


def _lane_pick(x, lane_iota, idx):
    return jnp.sum(jnp.where(lane_iota == idx, x, 0.0), axis=-1, keepdims=True)


def _mm_kernel(x_ref, w_ref, o_ref, *, chunk):
    xb = x_ref[...].astype(BF16)
    for c0 in range(0, o_ref.shape[1], chunk):
        o_ref[:, c0:c0 + chunk] = jnp.dot(xb, w_ref[:, c0:c0 + chunk], preferred_element_type=F32)


def _matmul(x, w, tm, chunk, name):
    m, k = x.shape
    n = w.shape[1]
    return pl.pallas_call(
        functools.partial(_mm_kernel, chunk=chunk),
        grid=(m // tm,),
        in_specs=[pl.BlockSpec((tm, k), lambda i: (i, 0)),
                  pl.BlockSpec((k, n), lambda i: (0, 0))],
        out_specs=pl.BlockSpec((tm, n), lambda i: (i, 0)),
        out_shape=jax.ShapeDtypeStruct((m, n), F32),
        compiler_params=_params("parallel"),
        name=name,
    )(x, w)


def _outproj_ln_kernel(mix_ref, xo_ref, x_ref, wm_ref, wx_ref, g_ref, b_ref, *rest):
    h = jnp.dot(mix_ref[...].astype(BF16), wm_ref[...], preferred_element_type=F32)
    h = h + jnp.dot(xo_ref[...].astype(BF16), wx_ref[...], preferred_element_type=F32)
    y = _layer_norm(ALPHA * x_ref[...] + h, g_ref[...], b_ref[...])
    if len(rest) == 1:
        (o_ref,) = rest
    else:
        rh_ref, rl_ref, o_ref, comb_ref, gate_ref, idx_ref = rest
        comb_ref[...], gate_ref[...], idx_ref[...] = _route_top2(y, rh_ref[...], rl_ref[...])
    o_ref[...] = y


def _outproj_ln(mix, xo, x, w_out, g, b, tm, router=None):
    m = x.shape[0]
    d_mix = mix.shape[1]
    n_blk = d_mix // D_X
    row_blk = lambda width: pl.BlockSpec((tm, width), lambda i: (i, 0))
    const = lambda shape, blk=0: pl.BlockSpec(shape, lambda i: (blk, 0))
    in_specs = [row_blk(d_mix), row_blk(D_X), row_blk(D_MODEL),
                const((d_mix, D_MODEL)), const((D_X, D_MODEL), n_blk),
                const((1, D_MODEL)), const((1, D_MODEL))]
    args = [mix, xo, x, w_out, w_out, g, b]
    out_specs = [row_blk(D_MODEL)]
    out_shape = [jax.ShapeDtypeStruct((m, D_MODEL), F32)]
    if router is not None:
        in_specs += [const((D_MODEL, LANES)), const((D_MODEL, LANES))]
        args += list(router)
        out_specs += [row_blk(LANES)] * 3
        out_shape += [jax.ShapeDtypeStruct((m, LANES), F32), jax.ShapeDtypeStruct((m, LANES), F32),
                      jax.ShapeDtypeStruct((m, LANES), jnp.int32)]
    out = pl.pallas_call(
        _outproj_ln_kernel,
        grid=(m // tm,),
        in_specs=in_specs,
        out_specs=out_specs,
        out_shape=out_shape,
        compiler_params=_params("parallel"),
        name="outproj_ln",
    )(*args)
    return out[0] if router is None else out


FFN_CHUNK = 256


def _swiglu_partial(xb, wg_ref, wu_ref, wd_ref, row_scale=None):
    y = None
    for c0 in range(0, wg_ref.shape[-1], FFN_CHUNK):
        hg = jnp.dot(xb, wg_ref[:, c0:c0 + FFN_CHUNK], preferred_element_type=F32)
        hu = jnp.dot(xb, wu_ref[:, c0:c0 + FFN_CHUNK], preferred_element_type=F32)
        h = _silu(hg) * hu
        if row_scale is not None:
            h = h * row_scale
        part = jnp.dot(h.astype(BF16), wd_ref[c0:c0 + FFN_CHUNK, :], preferred_element_type=F32)
        y = part if y is None else y + part
    return y


def _ffn_kernel(*refs, moe):
    if moe:
        x_ref, comb_ref, wg_ref, wu_ref, wd_ref, g_ref, b_ref, o_ref, xb_ref, acc_ref = refs
    else:
        x_ref, wg_ref, wu_ref, wd_ref, g_ref, b_ref, o_ref, xb_ref, acc_ref = refs
    e = pl.program_id(1)
    f = pl.program_id(2)
    first = jnp.logical_and(e == 0, f == 0)
    last = jnp.logical_and(e == pl.num_programs(1) - 1, f == pl.num_programs(2) - 1)

    @pl.when(first)
    def _():
        xb_ref[...] = x_ref[...].astype(BF16)
        acc_ref[...] = jnp.zeros_like(acc_ref)

    row_scale = None
    if moe:
        comb = comb_ref[...]
        lane = lax.broadcasted_iota(jnp.int32, comb.shape, 1)
        row_scale = _lane_pick(comb, lane, e)
    acc_ref[...] += _swiglu_partial(xb_ref[...], wg_ref, wu_ref, wd_ref, row_scale)

    @pl.when(last)
    def _():
        o_ref[...] = _layer_norm(ALPHA * x_ref[...] + acc_ref[...], g_ref[...], b_ref[...])


def _ffn_ln(x, comb, w_gu, w_down, layer, g, b, tm, tf):
    m = x.shape[0]
    moe = comb is not None
    n_e = w_gu.shape[1]
    n_f = D_FF // tf
    in_specs = [pl.BlockSpec((tm, D_MODEL), lambda i, e, f: (i, 0))]
    args = [x]
    if moe:
        in_specs.append(pl.BlockSpec((tm, LANES), lambda i, e, f: (i, 0)))
        args.append(comb)
    in_specs += [pl.BlockSpec((None, None, D_MODEL, tf), lambda i, e, f: (layer, e, 0, f)),
                 pl.BlockSpec((None, None, D_MODEL, tf), lambda i, e, f: (layer, e, 0, n_f + f)),
                 pl.BlockSpec((None, None, tf, D_MODEL), lambda i, e, f: (layer, e, f, 0)),
                 pl.BlockSpec((1, D_MODEL), lambda i, e, f: (0, 0)),
                 pl.BlockSpec((1, D_MODEL), lambda i, e, f: (0, 0))]
    args += [w_gu, w_gu, w_down, g, b]
    return pl.pallas_call(
        functools.partial(_ffn_kernel, moe=moe),
        grid=(m // tm, n_e, n_f),
        in_specs=in_specs,
        out_specs=pl.BlockSpec((tm, D_MODEL), lambda i, e, f: (i, 0)),
        out_shape=jax.ShapeDtypeStruct((m, D_MODEL), F32),
        scratch_shapes=[pltpu.VMEM((tm, D_MODEL), BF16), pltpu.VMEM((tm, D_MODEL), F32)],
        compiler_params=_params("parallel", "arbitrary", "arbitrary"),
        name="moe_ln" if moe else "ffn_ln",
    )(*args)


def _route_top2(x, wh, wl):
    xh, xl = _split2(x)
    lg = jnp.dot(xh, wh, preferred_element_type=F32)
    lg = lg + (jnp.dot(xh, wl, preferred_element_type=F32) + jnp.dot(xl, wh, preferred_element_type=F32))
    lane = lax.broadcasted_iota(jnp.int32, lg.shape, 1)
    neg = -jnp.inf
    lg = jnp.where(lane < N_EXP, lg, neg)
    m1 = jnp.max(lg, axis=-1, keepdims=True)
    i1 = jnp.min(jnp.where(lg == m1, lane, LANES), axis=-1, keepdims=True)
    lg2 = jnp.where(lane == i1, neg, lg)
    m2 = jnp.max(lg2, axis=-1, keepdims=True)
    i2 = jnp.min(jnp.where(lg2 == m2, lane, LANES), axis=-1, keepdims=True)
    e2 = jnp.exp(m2 - m1)
    den = 1.0 + e2
    g1 = 1.0 / den
    g2 = e2 / den
    comb = jnp.where(lane == i1, g1, 0.0) + jnp.where(lane == i2, g2, 0.0)
    gates = jnp.where(lane == 0, g1, jnp.where(lane == 1, g2, 0.0))
    idx = jnp.where(lane == 0, i1, jnp.where(lane == 1, i2, 0))
    return comb, gates, idx


MOE_TM = 512
FFN_TF = 1792
DISPATCH_TOKENS = 1024
COMBINE_TOKENS = 256


def _moe_plan(idx2, tm):
    t = idx2.shape[0]
    e_flat = idx2.reshape(-1)
    experts = jnp.arange(N_EXP, dtype=jnp.int32)
    onehot = (e_flat[:, None] == experts[None, :]).astype(jnp.int32)
    csum = jnp.cumsum(onehot, axis=0)
    rank = jnp.sum(csum * onehot, axis=1) - 1
    counts = csum[-1]
    ptiles = (counts + tm - 1) // tm
    tile_end = jnp.cumsum(ptiles)
    pstart = (tile_end - ptiles) * tm
    pos = jnp.sum(onehot * pstart[None, :], axis=1) + rank
    n_used = tile_end[-1]
    n_tiles = (2 * t) // tm + N_EXP
    jj = jnp.minimum(jnp.arange(n_tiles, dtype=jnp.int32), n_used - 1)
    tile_expert = jnp.sum((jj[:, None] >= tile_end[None, :]).astype(jnp.int32), axis=1)
    seg_base = jnp.concatenate([pstart + counts, (n_used * tm).reshape(1)])
    seg_len = jnp.concatenate([ptiles * tm - counts, ((n_tiles - n_used) * tm).reshape(1)])
    seg_end = jnp.cumsum(seg_len)
    j = jnp.arange(N_EXP * tm, dtype=jnp.int32)
    seg = jnp.sum((j[:, None] >= seg_end[None, :]).astype(jnp.int32), axis=1)
    seg_1h = (seg[:, None] == jnp.arange(N_EXP + 1, dtype=jnp.int32)[None, :]).astype(jnp.int32)
    pad_dst = j + jnp.sum(seg_1h * (seg_base - (seg_end - seg_len))[None, :], axis=1)
    return (pos.astype(jnp.int32), tile_expert.astype(jnp.int32), n_used.reshape(1).astype(jnp.int32),
            pad_dst.astype(jnp.int32), n_tiles)


def _wait_rows(hbm_ref, n_rows, sem):
    rows = hbm_ref.at[pl.ds(0, n_rows), :]
    pltpu.make_async_copy(rows, rows, sem).wait()


def _tile_row_copy(tiles_ref, tile, row, hbm_ref, hbm_row, sem, to_hbm):
    vm = tiles_ref.at[tile, pl.ds(row, 1), :]
    hb = hbm_ref.at[pl.ds(hbm_row, 1), :]
    return pltpu.make_async_copy(vm, hb, sem) if to_hbm else pltpu.make_async_copy(hb, vm, sem)


def _moe_dispatch_kernel(pos_ref, pad_dst_ref, x_ref, xs_ref, sem):
    n_grp = x_ref.shape[0]
    n_pad = pad_dst_ref.shape[1]

    def send(grp, carry):
        for jr in range(SUBLANES):
            for k in range(2):
                dst = pos_ref[0, 2 * SUBLANES * grp + 2 * jr + k]
                _tile_row_copy(x_ref, grp, jr, xs_ref, dst, sem, True).start()
        return carry

    lax.fori_loop(0, n_grp, send, 0)

    def pad(grp, carry):
        for jr in range(SUBLANES):
            _tile_row_copy(x_ref, 0, jr, xs_ref, pad_dst_ref[0, SUBLANES * grp + jr], sem, True).start()
        return carry

    lax.fori_loop(0, n_pad // SUBLANES, pad, 0)
    _wait_rows(xs_ref, 2 * SUBLANES * n_grp + n_pad, sem)


def _moe_dispatch(x, pos, pad_dst, n_rows_out):
    m = x.shape[0]
    tq = DISPATCH_TOKENS
    n_steps = m // tq
    n_pad = pad_dst.shape[0] // n_steps
    smem = functools.partial(pl.BlockSpec, memory_space=pltpu.SMEM)
    return pl.pallas_call(
        _moe_dispatch_kernel,
        grid=(n_steps,),
        in_specs=[smem((None, 1, 2 * tq), lambda i: (i, 0, 0)),
                  smem((None, 1, n_pad), lambda i: (i, 0, 0)),
                  pl.BlockSpec((tq // SUBLANES, SUBLANES, D_MODEL), lambda i: (i, 0, 0))],
        out_specs=pl.BlockSpec(memory_space=pl.ANY),
        out_shape=jax.ShapeDtypeStruct((n_rows_out, D_MODEL), F32),
        scratch_shapes=[pltpu.SemaphoreType.DMA(())],
        compiler_params=_params("arbitrary"),
        name="moe_dispatch",
    )(pos.reshape(n_steps, 1, 2 * tq), pad_dst.reshape(n_steps, 1, n_pad),
      x.reshape(m // SUBLANES, SUBLANES, D_MODEL))


def _moe_ffn_kernel(te_ref, nu_ref, x_ref, wg_ref, wu_ref, wd_ref, o_ref, xb_ref):
    p = pl.program_id(0)
    f = pl.program_id(1)

    @pl.when(p < nu_ref[0])
    def _():
        @pl.when(f == 0)
        def _():
            xb_ref[...] = x_ref[...].astype(BF16)

        y = _swiglu_partial(xb_ref[...], wg_ref, wu_ref, wd_ref)

        @pl.when(f == 0)
        def _():
            o_ref[...] = y

        @pl.when(f != 0)
        def _():
            o_ref[...] += y

    @pl.when(jnp.logical_and(p >= nu_ref[0], f == 0))
    def _():
        o_ref[...] = jnp.zeros_like(o_ref)


def _moe_ffn(xs, tile_expert, n_used, w_gu, w_down, layer, tm, tf):
    n_tiles = xs.shape[0] // tm
    n_f = D_FF // tf
    tile = lambda p, nu: jnp.minimum(p, nu[0] - 1)
    col = lambda p, f, nu: jnp.where(p < nu[0], f, n_f - 1)
    grid_spec = pltpu.PrefetchScalarGridSpec(
        num_scalar_prefetch=2,
        grid=(n_tiles, n_f),
        in_specs=[pl.BlockSpec((tm, D_MODEL), lambda p, f, te, nu: (tile(p, nu), 0)),
                  pl.BlockSpec((None, None, D_MODEL, tf),
                               lambda p, f, te, nu: (layer, te[p], 0, col(p, f, nu))),
                  pl.BlockSpec((None, None, D_MODEL, tf),
                               lambda p, f, te, nu: (layer, te[p], 0, n_f + col(p, f, nu))),
                  pl.BlockSpec((None, None, tf, D_MODEL),
                               lambda p, f, te, nu: (layer, te[p], col(p, f, nu), 0))],
        out_specs=pl.BlockSpec((tm, D_MODEL), lambda p, f, te, nu: (p, 0)),
        scratch_shapes=[pltpu.VMEM((tm, D_MODEL), BF16)],
    )
    return pl.pallas_call(
        _moe_ffn_kernel,
        grid_spec=grid_spec,
        out_shape=jax.ShapeDtypeStruct(xs.shape, F32),
        compiler_params=_params("arbitrary", "arbitrary"),
        name="moe_ffn",
    )(tile_expert, n_used, xs, w_gu, w_gu, w_down)


def _moe_combine_ln_kernel(pos_ref, pos_next_ref, gate_ref, x_ref, ye_ref, g_ref, b_ref, o_ref,
                           buf_ref, sem):
    i = pl.program_id(0)
    tq = x_ref.shape[0]
    n_grp = tq // SUBLANES
    slot = lax.rem(i, 2)

    def start_fetch(rows_ref, to_slot):
        def fetch(grp, carry):
            for jr in range(SUBLANES):
                for k in range(2):
                    src = rows_ref[0, 2 * SUBLANES * grp + 2 * jr + k]
                    _tile_row_copy(buf_ref.at[to_slot], k * n_grp + grp, jr, ye_ref, src,
                                   sem.at[to_slot], False).start()
            return carry

        lax.fori_loop(0, n_grp, fetch, 0)

    @pl.when(i == 0)
    def _():
        start_fetch(pos_ref, 0)

    @pl.when(i + 1 < pl.num_programs(0))
    def _():
        start_fetch(pos_next_ref, 1 - slot)

    _wait_rows(ye_ref, 2 * tq, sem.at[slot])
    gates = gate_ref[...]
    lane = lax.broadcasted_iota(jnp.int32, gates.shape, 1)
    y0 = buf_ref[slot, 0:n_grp].reshape(tq, D_MODEL)
    y1 = buf_ref[slot, n_grp:2 * n_grp].reshape(tq, D_MODEL)
    y = _lane_pick(gates, lane, 0) * y0 + _lane_pick(gates, lane, 1) * y1
    o_ref[...] = _layer_norm(ALPHA * x_ref[...] + y, g_ref[...], b_ref[...])


def _moe_combine_ln(x, gates, pos, ye, g, b):
    m = x.shape[0]
    tq = COMBINE_TOKENS
    n_steps = m // tq
    pos3 = pos.reshape(n_steps, 1, 2 * tq)
    smem = functools.partial(pl.BlockSpec, memory_space=pltpu.SMEM)
    return pl.pallas_call(
        _moe_combine_ln_kernel,
        grid=(n_steps,),
        in_specs=[smem((None, 1, 2 * tq), lambda i: (i, 0, 0)),
                  smem((None, 1, 2 * tq), lambda i: (jnp.minimum(i + 1, n_steps - 1), 0, 0)),
                  pl.BlockSpec((tq, LANES), lambda i: (i, 0)),
                  pl.BlockSpec((tq, D_MODEL), lambda i: (i, 0)),
                  pl.BlockSpec(memory_space=pl.ANY),
                  pl.BlockSpec((1, D_MODEL), lambda i: (0, 0)),
                  pl.BlockSpec((1, D_MODEL), lambda i: (0, 0))],
        out_specs=pl.BlockSpec((tq, D_MODEL), lambda i: (i, 0)),
        out_shape=jax.ShapeDtypeStruct((m, D_MODEL), F32),
        scratch_shapes=[pltpu.VMEM((2, 2 * tq // SUBLANES, SUBLANES, D_MODEL), F32),
                        pltpu.SemaphoreType.DMA((2,))],
        compiler_params=_params("arbitrary"),
        name="moe_combine_ln",
    )(pos3, pos3, gates, x, ye, g, b)


def _moe_routed_ln(x, gates, idx, w_gu, w_down, layer, g, b):
    pos, tile_expert, n_used, pad_dst, n_tiles = _moe_plan(idx[:, :2], MOE_TM)
    xs = _moe_dispatch(x, pos, pad_dst, n_tiles * MOE_TM)
    ye = _moe_ffn(xs, tile_expert, n_used, w_gu, w_down, layer, MOE_TM, FFN_TF)
    return _moe_combine_ln(x, gates, pos, ye, g, b)


GDN_PASSES_QK = 1
GDN_PASSES_SOLVE = 1
GDN_PASSES_STATE = 1
INV_BLOCK_SHIFT = 4
GDN_SEQS = 4


def _gdn_chunk_kernel(qkv_ref, z_ref, ab_ref, cw_ref, alog_ref, dtb_ref, nw_ref, hist0_ref, s0_ref,
                      mix_ref, sout_ref, hout_ref, xh_ref, s_ref):
    n = pl.program_id(1)
    c = GDN_CHUNK
    hrows = SUBLANES
    n_seq = qkv_ref.shape[0]
    items = [(b, h) for b in range(n_seq) for h in range(H_A)]

    @pl.when(n == 0)
    def _():
        xh_ref[:, 0:hrows, :] = hist0_ref[...]
        s_ref[...] = s0_ref[...]

    xh_ref[:, hrows:hrows + c, :] = qkv_ref[...]

    row = lax.broadcasted_iota(jnp.int32, (c, c), 0)
    col = lax.broadcasted_iota(jnp.int32, (c, c), 1)
    causal = row >= col
    strict = row > col
    eye = jnp.where(row == col, 1.0, 0.0)
    blockdiag = (row >> INV_BLOCK_SHIFT) == (col >> INV_BLOCK_SHIFT)
    tril_ones = jnp.where(causal, 1.0, 0.0).astype(BF16)
    lane = lax.broadcasted_iota(jnp.int32, (c, LANES), 1)
    mm_s = functools.partial(_dot, passes=GDN_PASSES_SOLVE)

    beta_all, gcum_all, gcum_t = [], [], []
    for b in range(n_seq):
        ab = ab_ref[b]
        g_all = -jnp.exp(alog_ref[...]) * _softplus(ab + dtb_ref[...])
        beta_all.append(_sigmoid(ab))
        gc = _dot_exact_lhs(tril_ones, g_all)
        gcum_all.append(gc)
        gcum_t.append(gc.T)

    def conv_silu(b, c0):
        acc = xh_ref[b, hrows - 3:hrows - 3 + c, c0:c0 + LANES] * cw_ref[0:1, c0:c0 + LANES]
        for j in range(1, CONV_W):
            acc = acc + (xh_ref[b, hrows - 3 + j:hrows - 3 + j + c, c0:c0 + LANES]
                         * cw_ref[j:j + 1, c0:c0 + LANES])
        return _silu(acc)

    q, k, v, kb, egc, gc_col, gc_last, beta_col, decay = {}, {}, {}, {}, {}, {}, {}, {}, {}
    for it in items:
        b, h = it
        qi = conv_silu(b, h * DK_A)
        ki = conv_silu(b, QK_A + h * DK_A)
        v[it] = conv_silu(b, 2 * QK_A + h * DV_A)
        q[it] = qi * lax.rsqrt(jnp.sum(qi * qi, axis=-1, keepdims=True) + RMS_EPS) * (DK_A ** -0.5)
        k[it] = ki * lax.rsqrt(jnp.sum(ki * ki, axis=-1, keepdims=True) + RMS_EPS)
        gc_col[it] = _lane_pick(gcum_all[b], lane, h)
        beta_col[it] = _lane_pick(beta_all[b], lane, B_LANE + h)
        gc_last[it] = gc_col[it][c - 1:c, :]
        decay[it] = jnp.where(causal, jnp.exp(gc_col[it] - gcum_t[b][h:h + 1, :c]), 0.0)
        egc[it] = jnp.exp(gc_col[it])
        kb[it] = k[it] * beta_col[it]

    kk = {it: _dot(jnp.concatenate([kb[it], q[it]], axis=0), k[it], _NT, GDN_PASSES_QK) for it in items}
    a_mat = {it: jnp.where(strict, kk[it][:c] * decay[it], 0.0) for it in items}
    attn = {it: kk[it][c:] * decay[it] for it in items}

    a_d = {it: jnp.where(blockdiag, a_mat[it], 0.0) for it in items}
    a_n = {it: jnp.where(blockdiag, 0.0, a_mat[it]) for it in items}
    p2 = {it: mm_s(a_d[it], a_d[it]) for it in items}
    e_d = {it: p2[it] - a_d[it] - mm_s(a_d[it], p2[it]) for it in items}
    p4 = {it: mm_s(p2[it], p2[it]) for it in items}
    e_d = {it: e_d[it] + p4[it] + mm_s(e_d[it], p4[it]) for it in items}
    p8 = {it: mm_s(p4[it], p4[it]) for it in items}
    e_d = {it: e_d[it] + p8[it] + mm_s(e_d[it], p8[it]) for it in items}
    m_blk = {it: a_n[it] + mm_s(e_d[it], a_n[it]) for it in items}
    rhs = {it: jnp.concatenate([v[it] * beta_col[it], kb[it] * egc[it]], axis=1) for it in items}
    rhs = {it: rhs[it] + mm_s(e_d[it], rhs[it]) for it in items}
    m2 = {it: mm_s(m_blk[it], m_blk[it]) for it in items}
    f_y = {it: m2[it] - m_blk[it] - mm_s(m_blk[it], m2[it]) for it in items}
    sol = {it: rhs[it] + mm_s(f_y[it], rhs[it]) for it in items}

    s_old = {it: s_ref[it[0], it[1]] for it in items}
    wq = {it: _dot(jnp.concatenate([sol[it][:, DV_A:], q[it] * egc[it]], axis=0), s_old[it],
                   _NN, GDN_PASSES_STATE) for it in items}
    v_new = {it: sol[it][:, :DV_A] - wq[it][:c] for it in items}
    o = {it: wq[it][c:] + _dot(attn[it], v_new[it], _NN, GDN_PASSES_STATE) for it in items}
    for it in items:
        k_dec = k[it] * jnp.exp(gc_last[it] - gc_col[it])
        s_ref[it[0], it[1]] = (s_old[it] * jnp.exp(gc_last[it])
                               + _dot(k_dec, v_new[it], _TN, GDN_PASSES_STATE))
    for it in items:
        b, h = it
        oi = o[it]
        oi = oi * lax.rsqrt(jnp.mean(oi * oi, axis=-1, keepdims=True) + RMS_EPS) * nw_ref[...]
        mix_ref[b, :, h * DV_A:(h + 1) * DV_A] = oi * _silu(z_ref[b, :, h * DV_A:(h + 1) * DV_A])

    xh_ref[:, 0:hrows, :] = xh_ref[:, c:c + hrows, :]

    @pl.when(n == pl.num_programs(1) - 1)
    def _():
        sout_ref[...] = s_ref[...]
        hout_ref[...] = xh_ref[:, c:c + hrows, :]


def _gdn_chunked(proj, conv_w, alog_row, dtb_row, nw_row, hist0, s0):
    bsz, t, _ = proj.shape
    c = GDN_CHUNK
    nb = GDN_SEQS
    n_chunks = t // c
    row_spec = lambda width, blk: pl.BlockSpec((nb, c, width), lambda b, n: (b, n, blk))
    full2 = lambda shape: pl.BlockSpec(shape, lambda b, n: (0, 0))
    return pl.pallas_call(
        _gdn_chunk_kernel,
        grid=(bsz // nb, n_chunks),
        in_specs=[row_spec(CONV_DIM, 0),
                  row_spec(V_A, A_Z_OFF // V_A),
                  row_spec(LANES, A_AB_OFF // LANES),
                  full2((CONV_W, CONV_DIM)),
                  full2((1, LANES)), full2((1, LANES)), full2((1, DV_A)),
                  pl.BlockSpec((nb, SUBLANES, CONV_DIM), lambda b, n: (b, 0, 0)),
                  pl.BlockSpec((nb, H_A, DK_A, DV_A), lambda b, n: (b, 0, 0, 0))],
        out_specs=[pl.BlockSpec((nb, c, V_A), lambda b, n: (b, n, 0)),
                   pl.BlockSpec((nb, H_A, DK_A, DV_A), lambda b, n: (b, 0, 0, 0)),
                   pl.BlockSpec((nb, SUBLANES, CONV_DIM), lambda b, n: (b, 0, 0))],
        out_shape=[jax.ShapeDtypeStruct((bsz, t, V_A), F32),
                   jax.ShapeDtypeStruct((bsz, H_A, DK_A, DV_A), F32),
                   jax.ShapeDtypeStruct((bsz, SUBLANES, CONV_DIM), F32)],
        scratch_shapes=[pltpu.VMEM((nb, SUBLANES + c, CONV_DIM), F32),
                        pltpu.VMEM((nb, H_A, DK_A, DV_A), F32)],
        compiler_params=_params("parallel", "arbitrary"),
        name="gdn_chunked",
    )(proj, proj, proj, conv_w, alog_row, dtb_row, nw_row, hist0, s0)


def _gdn_step_kernel(qkv_ref, z_ref, ab_ref, cw_ref, alog_ref, dtb_ref, nw_ref, hist_ref, s_ref,
                     mix_ref, sout_ref, hout_ref, o_scr):
    rows = qkv_ref.shape[0]
    new = qkv_ref[...]
    y = hist_ref[0] * cw_ref[0:1, :]
    for j in range(1, CONV_W - 1):
        y = y + hist_ref[j] * cw_ref[j:j + 1, :]
    y = y + new * cw_ref[CONV_W - 1:CONV_W, :]
    qkv = _silu(y)
    for j in range(CONV_W - 2):
        hout_ref[j] = hist_ref[j + 1]
    hout_ref[CONV_W - 2] = new

    ab = ab_ref[...]
    lane = lax.broadcasted_iota(jnp.int32, ab.shape, 1)
    g_all = -jnp.exp(alog_ref[...]) * _softplus(ab + dtb_ref[...])
    beta_all = _sigmoid(ab)
    ri = lax.broadcasted_iota(jnp.int32, (DK_A, DK_A), 0)
    ci = lax.broadcasted_iota(jnp.int32, (DK_A, DK_A), 1)
    diag = ri == ci

    def as_column(r):
        return jnp.sum(jnp.where(diag, r, 0.0), axis=-1, keepdims=True)

    for h in range(H_A):
        q = qkv[:, h * DK_A:(h + 1) * DK_A]
        k = qkv[:, QK_A + h * DK_A:QK_A + (h + 1) * DK_A]
        v = qkv[:, 2 * QK_A + h * DV_A:2 * QK_A + (h + 1) * DV_A]
        q = q * lax.rsqrt(jnp.sum(q * q, axis=-1, keepdims=True) + RMS_EPS) * (DK_A ** -0.5)
        k = k * lax.rsqrt(jnp.sum(k * k, axis=-1, keepdims=True) + RMS_EPS)
        eg = jnp.exp(_lane_pick(g_all, lane, h))
        beta = _lane_pick(beta_all, lane, B_LANE + h)
        for s in range(rows):
            k_col = as_column(k[s:s + 1, :])
            q_col = as_column(q[s:s + 1, :])
            st = s_ref[s, h] * eg[s:s + 1, :]
            ks = jnp.sum(k_col * st, axis=0, keepdims=True)
            u = beta[s:s + 1, :] * (v[s:s + 1, :] - ks)
            st = st + k_col * u
            sout_ref[s, h] = st
            o_scr[s:s + 1, h * DV_A:(h + 1) * DV_A] = jnp.sum(q_col * st, axis=0, keepdims=True)

    for h in range(H_A):
        o = o_scr[:, h * DV_A:(h + 1) * DV_A]
        o = o * lax.rsqrt(jnp.mean(o * o, axis=-1, keepdims=True) + RMS_EPS) * nw_ref[...]
        mix_ref[:, h * DV_A:(h + 1) * DV_A] = o * _silu(z_ref[:, h * DV_A:(h + 1) * DV_A])


def _gdn_step(proj, conv_w, alog_row, dtb_row, nw_row, hist_t, s0, layer):
    bsz = proj.shape[0]
    rows = SUBLANES
    full1 = lambda shape: pl.BlockSpec(shape, lambda i: (0, 0))
    return pl.pallas_call(
        _gdn_step_kernel,
        grid=(bsz // rows,),
        in_specs=[pl.BlockSpec((rows, CONV_DIM), lambda i: (i, 0)),
                  pl.BlockSpec((rows, V_A), lambda i: (i, A_Z_OFF // V_A)),
                  pl.BlockSpec((rows, LANES), lambda i: (i, A_AB_OFF // LANES)),
                  full1((CONV_W, CONV_DIM)),
                  full1((1, LANES)), full1((1, LANES)), full1((1, DV_A)),
                  pl.BlockSpec((None, CONV_W - 1, rows, CONV_DIM), lambda i: (layer, 0, i, 0)),
                  pl.BlockSpec((None, rows, H_A, DK_A, DV_A), lambda i: (layer, i, 0, 0, 0))],
        out_specs=[pl.BlockSpec((rows, V_A), lambda i: (i, 0)),
                   pl.BlockSpec((rows, H_A, DK_A, DV_A), lambda i: (i, 0, 0, 0)),
                   pl.BlockSpec((CONV_W - 1, rows, CONV_DIM), lambda i: (0, i, 0))],
        out_shape=[jax.ShapeDtypeStruct((bsz, V_A), F32),
                   jax.ShapeDtypeStruct((bsz, H_A, DK_A, DV_A), F32),
                   jax.ShapeDtypeStruct((CONV_W - 1, bsz, CONV_DIM), F32)],
        scratch_shapes=[pltpu.VMEM((rows, V_A), F32)],
        compiler_params=_params("parallel"),
        name="gdn_step",
    )(proj, proj, proj, conv_w, alog_row, dtb_row, nw_row, hist_t, s0)


SGU_CHUNKS = 4


def _sgu_chunk_kernel(u_ref, v_ref, g_ref, b_ref, ws_ref, bs_ref, mix_ref):
    cb = CHUNK_B
    row = lax.broadcasted_iota(jnp.int32, (cb, cb), 0)
    col = lax.broadcasted_iota(jnp.int32, (cb, cb), 1)
    ws = [jnp.where(row >= col, ws_ref[gi], 0.0).astype(BF16) for gi in range(G_B)]
    for r0 in range(0, u_ref.shape[0], cb):
        u = _gelu_tanh(u_ref[r0:r0 + cb, :])
        v = _layer_norm(_gelu_tanh(v_ref[r0:r0 + cb, :]), g_ref[...], b_ref[...])
        for gi in range(G_B):
            mixed = _dot(ws[gi], v[:, gi * CH_B:(gi + 1) * CH_B]) + bs_ref[gi]
            mix_ref[r0:r0 + cb, gi * CH_B:(gi + 1) * CH_B] = u[:, gi * CH_B:(gi + 1) * CH_B] * mixed


def _sgu_chunked(proj, ln_g, ln_b, w_s, bs_b):
    bsz, t, _ = proj.shape
    c = CHUNK_B * SGU_CHUNKS
    return pl.pallas_call(
        _sgu_chunk_kernel,
        grid=(bsz, t // c),
        in_specs=[pl.BlockSpec((None, c, D_B), lambda b, n: (b, n, 0)),
                  pl.BlockSpec((None, c, D_B), lambda b, n: (b, n, 1)),
                  pl.BlockSpec((1, D_B), lambda b, n: (0, 0)),
                  pl.BlockSpec((1, D_B), lambda b, n: (0, 0)),
                  pl.BlockSpec((G_B, CHUNK_B, CHUNK_B), lambda b, n: (0, 0, 0)),
                  pl.BlockSpec((G_B, CHUNK_B, CH_B), lambda b, n: (0, 0, 0))],
        out_specs=pl.BlockSpec((None, c, D_B), lambda b, n: (b, n, 0)),
        out_shape=jax.ShapeDtypeStruct((bsz, t, D_B), F32),
        compiler_params=_params("parallel", "parallel"),
        name="sgu_chunked",
    )(proj, proj, ln_g, ln_b, w_s, bs_b)


def _sgu_first_kernel(u_ref, v_ref, g_ref, b_ref, w00_ref, b0_ref, mix_ref, vout_ref):
    u = _gelu_tanh(u_ref[...])
    v = _layer_norm(_gelu_tanh(v_ref[...]), g_ref[...], b_ref[...])
    vout_ref[...] = v
    mix_ref[...] = u * (w00_ref[...] * v + b0_ref[...])


def _sgu_first(proj, ln_g, ln_b, w00_row, b0_row):
    bsz = proj.shape[0]
    row = lambda: pl.BlockSpec((1, D_B), lambda i: (0, 0))
    return pl.pallas_call(
        _sgu_first_kernel,
        grid=(1,),
        in_specs=[pl.BlockSpec((bsz, D_B), lambda i: (0, 0)),
                  pl.BlockSpec((bsz, D_B), lambda i: (0, 1)),
                  row(), row(), row(), row()],
        out_specs=[pl.BlockSpec((bsz, D_B), lambda i: (0, 0)),
                   pl.BlockSpec((bsz, D_B), lambda i: (0, 0))],
        out_shape=[jax.ShapeDtypeStruct((bsz, D_B), F32),
                   jax.ShapeDtypeStruct((bsz, D_B), F32)],
        compiler_params=_params("arbitrary"),
        name="sgu_first",
    )(proj, proj, ln_g, ln_b, w00_row, b0_row)


def _head_masks(shape):
    lane = lax.broadcasted_iota(jnp.int32, shape, len(shape) - 1)
    return [(lane // DH_X) == h for h in range(H_X)]


def _xattn_kernel(q_ref, k_ref, v_ref, o_ref):
    q = q_ref[...]
    kb = k_ref[...].astype(BF16)
    vb = v_ref[...].astype(BF16)
    masks = _head_masks(q.shape)
    out = jnp.zeros(q.shape, F32)
    for h in range(H_X):
        qh = jnp.where(masks[h], q, 0.0)
        s = _dot(qh, kb, _NT) * (DH_X ** -0.5)
        s = s - jnp.max(s, axis=-1, keepdims=True)
        p = jnp.exp(s)
        p = p / jnp.sum(p, axis=-1, keepdims=True)
        out = out + jnp.where(masks[h], _dot(p, vb), 0.0)
    o_ref[...] = out


def _xattn(proj, xq_blk, mem_k, mem_v, layer, tq):
    bsz, t, _ = proj.shape
    return pl.pallas_call(
        _xattn_kernel,
        grid=(bsz, t // tq),
        in_specs=[pl.BlockSpec((None, tq, D_X), lambda b, i: (b, i, xq_blk)),
                  pl.BlockSpec((None, None, N_MEM, D_X), lambda b, i: (layer, b, 0, 0)),
                  pl.BlockSpec((None, None, N_MEM, D_X), lambda b, i: (layer, b, 0, 0))],
        out_specs=pl.BlockSpec((None, tq, D_X), lambda b, i: (b, i, 0)),
        out_shape=jax.ShapeDtypeStruct((bsz, t, D_X), F32),
        compiler_params=_params("parallel", "parallel"),
        name="xattn",
    )(proj, mem_k, mem_v)


def _xattn_step_kernel(q_ref, k_ref, v_ref, o_ref):
    rows = q_ref.shape[0]
    q = q_ref[...]
    hrow = lax.broadcasted_iota(jnp.int32, (SUBLANES, D_X), 0)
    hlane = lax.broadcasted_iota(jnp.int32, (SUBLANES, D_X), 1) // DH_X
    sel = hrow == hlane
    seqs = range(rows)
    sc = [_dot(jnp.where(sel, q[s:s + 1, :], 0.0), k_ref[s], _NT) * (DH_X ** -0.5) for s in seqs]
    p = [jnp.exp(sc[s] - jnp.max(sc[s], axis=-1, keepdims=True)) for s in seqs]
    p = [p[s] / jnp.sum(p[s], axis=-1, keepdims=True) for s in seqs]
    pv = [_dot(p[s], v_ref[s]) for s in seqs]
    for s in seqs:
        o_ref[s:s + 1, :] = jnp.sum(jnp.where(sel, pv[s], 0.0), axis=0, keepdims=True)


def _xattn_step(proj, xq_blk, mem_k, mem_v, layer):
    bsz = proj.shape[0]
    rows = SUBLANES
    return pl.pallas_call(
        _xattn_step_kernel,
        grid=(bsz // rows,),
        in_specs=[pl.BlockSpec((rows, D_X), lambda i: (i, xq_blk)),
                  pl.BlockSpec((None, rows, N_MEM, D_X), lambda i: (layer, i, 0, 0)),
                  pl.BlockSpec((None, rows, N_MEM, D_X), lambda i: (layer, i, 0, 0))],
        out_specs=pl.BlockSpec((rows, D_X), lambda i: (i, 0)),
        out_shape=jax.ShapeDtypeStruct((bsz, D_X), F32),
        compiler_params=_params("parallel"),
        name="xattn_step",
    )(proj, mem_k, mem_v)


def _pad_lanes(v):
    return jnp.zeros((1, LANES), F32).at[0, :v.shape[0]].set(v.astype(F32))


def _prep_weights(a_w_in, a_A_log, a_dt_bias, b_w_s, b_b_s, moe_router):
    qkvz = a_w_in[:, :, :A_XQ_OFF]
    a_cols = a_w_in[:, :, A_XQ_OFF:A_XQ_OFF + H_A]
    b_cols = a_w_in[:, :, A_XQ_OFF + H_A:A_XQ_OFF + 2 * H_A]
    xq = a_w_in[:, :, A_XQ_OFF + 2 * H_A:]
    ab = jnp.zeros(a_w_in.shape[:2] + (LANES,), F32)
    ab = ab.at[:, :, :H_A].set(a_cols).at[:, :, B_LANE:B_LANE + H_A].set(b_cols)
    a_w = jnp.concatenate([qkvz, xq, ab], axis=-1).astype(BF16)
    alog_rows = [_pad_lanes(a_A_log[j]) for j in range(a_A_log.shape[0])]
    dtb_rows = [_pad_lanes(a_dt_bias[j]) for j in range(a_dt_bias.shape[0])]
    bs_b = jnp.broadcast_to(b_b_s[..., None], b_b_s.shape + (CH_B,)).astype(F32)
    w00_rows = jnp.repeat(b_w_s[:, :, 0, 0], CH_B, axis=-1)[:, None, :]
    b0_rows = jnp.repeat(b_b_s[:, :, 0], CH_B, axis=-1)[:, None, :]
    r = jnp.zeros(moe_router.shape[:2] + (LANES,), F32).at[:, :, :N_EXP].set(moe_router)
    r_hi = r.astype(BF16)
    r_lo = (r - r_hi.astype(F32)).astype(BF16)
    return a_w, alog_rows, dtb_rows, bs_b, w00_rows, b0_rows, r_hi, r_lo


def kernel(x_prompt, x_sample, state_gdn, state_conv, cache_mem_k, cache_mem_v, mem_prompt,
           a_w_in, a_conv_w, a_A_log, a_dt_bias, a_norm_w,
           b_w_in, b_ln_g, b_ln_b, b_w_s, b_b_s,
           w_mem_kv, w_out, ln1_g, ln1_b, ln2_g, ln2_b,
           ffn_w_gu, ffn_w_down, moe_router, moe_w_gu, moe_w_down):
    n_p, t_p, _ = x_prompt.shape
    n_s = x_sample.shape[0]
    n_a = a_w_in.shape[0]

    a_w, alog_rows, dtb_rows, bs_b, w00_rows, b0_rows, r_hi, r_lo = _prep_weights(
        a_w_in, a_A_log, a_dt_bias, b_w_s, b_b_s, moe_router)
    ffn_gu = ffn_w_gu.astype(BF16)[:, None]
    ffn_dn = ffn_w_down.astype(BF16)[:, None]
    moe_gu = moe_w_gu.astype(BF16)
    moe_dn = moe_w_down.astype(BF16)
    row = lambda v: v.reshape(1, -1).astype(F32)

    w_kv = jnp.transpose(w_mem_kv, (1, 0, 2)).reshape(D_MODEL, DEPTH * 2 * D_X).astype(BF16)
    kv = _matmul(mem_prompt.reshape(n_p * N_MEM, D_MODEL), w_kv, 512, 1024, "mem_kv")
    kv = kv.reshape(n_p, N_MEM, DEPTH, 2, D_X)
    p_mem_k = jnp.transpose(kv[:, :, :, 0, :], (2, 0, 1, 3))
    p_mem_v = jnp.transpose(kv[:, :, :, 1, :], (2, 0, 1, 3))
    s_mem_k = cache_mem_k.reshape(DEPTH, n_s, N_MEM, D_X)
    s_mem_v = cache_mem_v.reshape(DEPTH, n_s, N_MEM, D_X)

    xp = x_prompt.reshape(n_p * t_p, D_MODEL)
    xs = x_sample.reshape(n_s, D_MODEL)
    p_hist0 = jnp.zeros((n_p, SUBLANES, CONV_DIM), F32)
    p_s0 = jnp.zeros((n_p, H_A, DK_A, DV_A), F32)
    s_hist_t = jnp.transpose(state_conv, (0, 2, 1, 3))

    p_gdn, p_conv, s_gdn, s_conv, s_sgu_v = [], [], [], [], []
    tm_p, tm_s = 1024, n_s
    for i in range(DEPTH):
        j = i // 2
        if i % 2 == 0:
            proj_p = _matmul(xp, a_w[j], 512, 1152, "in_proj_a")
            proj_s = _matmul(xs, a_w[j], tm_s, 1152, "in_proj_a_s")
            nw = row(a_norm_w[j])
            mix_p, sp, hp = _gdn_chunked(proj_p.reshape(n_p, t_p, A_COLS), a_conv_w[j],
                                         alog_rows[j], dtb_rows[j], nw, p_hist0, p_s0)
            mix_p = mix_p.reshape(n_p * t_p, V_A)
            p_gdn.append(sp)
            p_conv.append(hp[:, SUBLANES - (CONV_W - 1):, :])
            mix_s, ss, hs = _gdn_step(proj_s, a_conv_w[j], alog_rows[j], dtb_rows[j], nw,
                                      s_hist_t, state_gdn, j)
            s_gdn.append(ss)
            s_conv.append(jnp.transpose(hs, (1, 0, 2)))
            cols, xq_blk = A_COLS, A_XQ_OFF // D_X
        else:
            b_w = b_w_in[j].astype(BF16)
            proj_p = _matmul(xp, b_w, 512, 896, "in_proj_b")
            proj_s = _matmul(xs, b_w, tm_s, 896, "in_proj_b_s")
            mix_p = _sgu_chunked(proj_p.reshape(n_p, t_p, B_COLS), row(b_ln_g[j]), row(b_ln_b[j]),
                                 b_w_s[j], bs_b[j]).reshape(n_p * t_p, D_B)
            mix_s, v_s = _sgu_first(proj_s, row(b_ln_g[j]), row(b_ln_b[j]), w00_rows[j], b0_rows[j])
            s_sgu_v.append(v_s.reshape(n_s, 1, D_B))
            cols, xq_blk = B_COLS, B_XQ_OFF // D_X
        xo_p = _xattn(proj_p.reshape(n_p, t_p, cols), xq_blk, p_mem_k, p_mem_v, i, 512)
        xo_s = _xattn_step(proj_s, xq_blk, s_mem_k, s_mem_v, i)
        g1, b1, g2, b2 = row(ln1_g[i]), row(ln1_b[i]), row(ln2_g[i]), row(ln2_b[i])
        w_o = w_out[i].astype(BF16)
        xo_p = xo_p.reshape(n_p * t_p, D_X)
        if i % 2 == 0:
            xp = _outproj_ln(mix_p, xo_p, xp, w_o, g1, b1, 512)
            xs = _outproj_ln(mix_s, xo_s, xs, w_o, g1, b1, tm_s)
            xp = _ffn_ln(xp, None, ffn_gu, ffn_dn, j, g2, b2, tm_p, FFN_TF)
            xs = _ffn_ln(xs, None, ffn_gu, ffn_dn, j, g2, b2, tm_s, FFN_TF)
        else:
            router = (r_hi[j], r_lo[j])
            xp, _, gates_p, idx_p = _outproj_ln(mix_p, xo_p, xp, w_o, g1, b1, 512, router)
            xs, comb_s, _, _ = _outproj_ln(mix_s, xo_s, xs, w_o, g1, b1, tm_s, router)
            xp = _moe_routed_ln(xp, gates_p, idx_p, moe_gu, moe_dn, j, g2, b2)
            xs = _ffn_ln(xs, comb_s, moe_gu, moe_dn, j, g2, b2, tm_s, FFN_TF)

    mem_shape = (DEPTH, n_p, N_MEM, H_X, DH_X)
    return (xp.reshape(n_p, t_p, D_MODEL),
            xs.reshape(n_s, 1, D_MODEL),
            jnp.stack(p_gdn),
            jnp.stack(p_conv),
            p_mem_k.reshape(mem_shape),
            p_mem_v.reshape(mem_shape),
            jnp.stack(s_gdn),
            jnp.stack(s_conv),
            jnp.stack(s_sgu_v))
```

```python
import functools

import jax
import jax.numpy as jnp
from jax import lax
from jax.experimental import pallas as pl
from jax.experimental.pallas import tpu as pltpu

F32 = jnp.float32
BF16 = jnp.bfloat16

D_MODEL = 1024
DEPTH = 4
H_A = 6
DK_A = 128
DV_A = 128
QK_A = H_A * DK_A
V_A = H_A * DV_A
CONV_W = 4
CONV_DIM = 2 * QK_A + V_A
GDN_CHUNK = 64
G_B = 6
CH_B = 128
D_B = G_B * CH_B
CHUNK_B = 128
N_MEM = 256
H_X = 4
DH_X = 64
D_X = H_X * DH_X
D_FF = 3584
N_EXP = 8
ALPHA = (2 * DEPTH) ** 0.25
LN_EPS = 1e-5
RMS_EPS = 1e-6

LANES = 128
SUBLANES = 8
VMEM_LIMIT_BYTES = 56 * 1024 * 1024

A_Z_OFF = CONV_DIM
A_XQ_OFF = CONV_DIM + V_A
A_AB_OFF = A_XQ_OFF + D_X
A_COLS = A_AB_OFF + LANES
B_LANE = 8
B_XQ_OFF = 2 * D_B
B_COLS = 2 * D_B + D_X


def _params(*sem):
    return pltpu.CompilerParams(dimension_semantics=sem, vmem_limit_bytes=VMEM_LIMIT_BYTES)


def _split2(x):
    hi = x.astype(BF16)
    lo = (x - hi.astype(F32)).astype(BF16)
    return hi, lo


_NN = (((1,), (0,)), ((), ()))
_NT = (((1,), (1,)), ((), ()))
_TN = (((0,), (0,)), ((), ()))


def _dot(a, b, dims=_NN, passes=1):
    if passes == 1:
        return lax.dot_general(a.astype(BF16), b.astype(BF16), dims, preferred_element_type=F32)
    ah, al = _split2(a)
    bh, bl = _split2(b)
    dg = functools.partial(lax.dot_general, dimension_numbers=dims, preferred_element_type=F32)
    return dg(ah, bh) + (dg(ah, bl) + dg(al, bh))


def _dot_exact_lhs(lhs_bf16, x):
    x1 = x.astype(BF16)
    r1 = x - x1.astype(F32)
    x2 = r1.astype(BF16)
    x3 = (r1 - x2.astype(F32)).astype(BF16)
    dg = functools.partial(lax.dot_general, dimension_numbers=_NN, preferred_element_type=F32)
    return dg(lhs_bf16, x1) + (dg(lhs_bf16, x2) + dg(lhs_bf16, x3))


def _sigmoid(x):
    return 1.0 / (1.0 + jnp.exp(-x))


def _silu(x):
    return x * _sigmoid(x)


def _softplus(x):
    return jnp.maximum(x, 0.0) + jnp.log1p(jnp.exp(-jnp.abs(x)))


def _gelu_tanh(x):
    c = 0.7978845608028654
    return 0.5 * x * (1.0 + jnp.tanh(c * (x + 0.044715 * (x * x * x))))


def _layer_norm(y, g, b):
    mu = jnp.mean(y, axis=-1, keepdims=True)
    d = y - mu
    var = jnp.mean(d * d, axis=-1, keepdims=True)
    return d * lax.rsqrt(var + LN_EPS) * g + b


def _lane_pick(x, lane_iota, idx):
    return jnp.sum(jnp.where(lane_iota == idx, x, 0.0), axis=-1, keepdims=True)


def _mm_kernel(x_ref, w_ref, o_ref, *, chunk):
    xb = x_ref[...].astype(BF16)
    for c0 in range(0, o_ref.shape[1], chunk):
        o_ref[:, c0:c0 + chunk] = jnp.dot(xb, w_ref[:, c0:c0 + chunk], preferred_element_type=F32)


def _matmul(x, w, tm, chunk, name):
    m, k = x.shape
    n = w.shape[1]
    return pl.pallas_call(
        functools.partial(_mm_kernel, chunk=chunk),
        grid=(m // tm,),
        in_specs=[pl.BlockSpec((tm, k), lambda i: (i, 0)),
                  pl.BlockSpec((k, n), lambda i: (0, 0))],
        out_specs=pl.BlockSpec((tm, n), lambda i: (i, 0)),
        out_shape=jax.ShapeDtypeStruct((m, n), F32),
        compiler_params=_params("parallel"),
        name=name,
    )(x, w)


def _outproj_ln_kernel(mix_ref, xo_ref, x_ref, wm_ref, wx_ref, g_ref, b_ref, *rest):
    h = jnp.dot(mix_ref[...].astype(BF16), wm_ref[...], preferred_element_type=F32)
    h = h + jnp.dot(xo_ref[...].astype(BF16), wx_ref[...], preferred_element_type=F32)
    y = _layer_norm(ALPHA * x_ref[...] + h, g_ref[...], b_ref[...])
    if len(rest) == 1:
        (o_ref,) = rest
    else:
        rh_ref, rl_ref, o_ref, comb_ref, gate_ref, idx_ref = rest
        comb_ref[...], gate_ref[...], idx_ref[...] = _route_top2(y, rh_ref[...], rl_ref[...])
    o_ref[...] = y


def _outproj_ln(mix, xo, x, w_out, g, b, tm, router=None):
    m = x.shape[0]
    d_mix = mix.shape[1]
    n_blk = d_mix // D_X
    row_blk = lambda width: pl.BlockSpec((tm, width), lambda i: (i, 0))
    const = lambda shape, blk=0: pl.BlockSpec(shape, lambda i: (blk, 0))
    in_specs = [row_blk(d_mix), row_blk(D_X), row_blk(D_MODEL),
                const((d_mix, D_MODEL)), const((D_X, D_MODEL), n_blk),
                const((1, D_MODEL)), const((1, D_MODEL))]
    args = [mix, xo, x, w_out, w_out, g, b]
    out_specs = [row_blk(D_MODEL)]
    out_shape = [jax.ShapeDtypeStruct((m, D_MODEL), F32)]
    if router is not None:
        in_specs += [const((D_MODEL, LANES)), const((D_MODEL, LANES))]
        args += list(router)
        out_specs += [row_blk(LANES)] * 3
        out_shape += [jax.ShapeDtypeStruct((m, LANES), F32), jax.ShapeDtypeStruct((m, LANES), F32),
                      jax.ShapeDtypeStruct((m, LANES), jnp.int32)]
    out = pl.pallas_call(
        _outproj_ln_kernel,
        grid=(m // tm,),
        in_specs=in_specs,
        out_specs=out_specs,
        out_shape=out_shape,
        compiler_params=_params("parallel"),
        name="outproj_ln",
    )(*args)
    return out[0] if router is None else out


FFN_CHUNK = 256


def _swiglu_partial(xb, wg_ref, wu_ref, wd_ref, row_scale=None):
    y = None
    for c0 in range(0, wg_ref.shape[-1], FFN_CHUNK):
        hg = jnp.dot(xb, wg_ref[:, c0:c0 + FFN_CHUNK], preferred_element_type=F32)
        hu = jnp.dot(xb, wu_ref[:, c0:c0 + FFN_CHUNK], preferred_element_type=F32)
        h = _silu(hg) * hu
        if row_scale is not None:
            h = h * row_scale
        part = jnp.dot(h.astype(BF16), wd_ref[c0:c0 + FFN_CHUNK, :], preferred_element_type=F32)
        y = part if y is None else y + part
    return y


def _ffn_kernel(*refs, moe):
    if moe:
        x_ref, comb_ref, wg_ref, wu_ref, wd_ref, g_ref, b_ref, o_ref, xb_ref, acc_ref = refs
    else:
        x_ref, wg_ref, wu_ref, wd_ref, g_ref, b_ref, o_ref, xb_ref, acc_ref = refs
    e = pl.program_id(1)
    f = pl.program_id(2)
    first = jnp.logical_and(e == 0, f == 0)
    last = jnp.logical_and(e == pl.num_programs(1) - 1, f == pl.num_programs(2) - 1)

    @pl.when(first)
    def _():
        xb_ref[...] = x_ref[...].astype(BF16)
        acc_ref[...] = jnp.zeros_like(acc_ref)

    row_scale = None
    if moe:
        comb = comb_ref[...]
        lane = lax.broadcasted_iota(jnp.int32, comb.shape, 1)
        row_scale = _lane_pick(comb, lane, e)
    acc_ref[...] += _swiglu_partial(xb_ref[...], wg_ref, wu_ref, wd_ref, row_scale)

    @pl.when(last)
    def _():
        o_ref[...] = _layer_norm(ALPHA * x_ref[...] + acc_ref[...], g_ref[...], b_ref[...])


def _ffn_ln(x, comb, w_gu, w_down, layer, g, b, tm, tf):
    m = x.shape[0]
    moe = comb is not None
    n_e = w_gu.shape[1]
    n_f = D_FF // tf
    in_specs = [pl.BlockSpec((tm, D_MODEL), lambda i, e, f: (i, 0))]
    args = [x]
    if moe:
        in_specs.append(pl.BlockSpec((tm, LANES), lambda i, e, f: (i, 0)))
        args.append(comb)
    in_specs += [pl.BlockSpec((None, None, D_MODEL, tf), lambda i, e, f: (layer, e, 0, f)),
                 pl.BlockSpec((None, None, D_MODEL, tf), lambda i, e, f: (layer, e, 0, n_f + f)),
                 pl.BlockSpec((None, None, tf, D_MODEL), lambda i, e, f: (layer, e, f, 0)),
                 pl.BlockSpec((1, D_MODEL), lambda i, e, f: (0, 0)),
                 pl.BlockSpec((1, D_MODEL), lambda i, e, f: (0, 0))]
    args += [w_gu, w_gu, w_down, g, b]
    return pl.pallas_call(
        functools.partial(_ffn_kernel, moe=moe),
        grid=(m // tm, n_e, n_f),
        in_specs=in_specs,
        out_specs=pl.BlockSpec((tm, D_MODEL), lambda i, e, f: (i, 0)),
        out_shape=jax.ShapeDtypeStruct((m, D_MODEL), F32),
        scratch_shapes=[pltpu.VMEM((tm, D_MODEL), BF16), pltpu.VMEM((tm, D_MODEL), F32)],
        compiler_params=_params("parallel", "arbitrary", "arbitrary"),
        name="moe_ln" if moe else "ffn_ln",
    )(*args)


def _route_top2(x, wh, wl):
    xb = x.astype(BF16)
    lg = jnp.dot(xb, wh, preferred_element_type=F32) + jnp.dot(xb, wl, preferred_element_type=F32)
    lane = lax.broadcasted_iota(jnp.int32, lg.shape, 1)
    neg = -jnp.inf
    lg = jnp.where(lane < N_EXP, lg, neg)
    m1 = jnp.max(lg, axis=-1, keepdims=True)
    i1 = jnp.min(jnp.where(lg == m1, lane, LANES), axis=-1, keepdims=True)
    lg2 = jnp.where(lane == i1, neg, lg)
    m2 = jnp.max(lg2, axis=-1, keepdims=True)
    i2 = jnp.min(jnp.where(lg2 == m2, lane, LANES), axis=-1, keepdims=True)
    e2 = jnp.exp(m2 - m1)
    den = 1.0 + e2
    g1 = 1.0 / den
    g2 = e2 / den
    comb = jnp.where(lane == i1, g1, 0.0) + jnp.where(lane == i2, g2, 0.0)
    gates = jnp.where(lane == 0, g1, jnp.where(lane == 1, g2, 0.0))
    idx = jnp.where(lane == 0, i1, jnp.where(lane == 1, i2, 0))
    return comb, gates, idx


MOE_TM = 512
FFN_TF = 1792
DISPATCH_TOKENS = 1024
COMBINE_TOKENS = 256


def _moe_plan(idx2, tm):
    t = idx2.shape[0]
    e_flat = idx2.reshape(-1)
    experts = jnp.arange(N_EXP, dtype=jnp.int32)
    onehot = (e_flat[:, None] == experts[None, :]).astype(jnp.int32)
    csum = jnp.cumsum(onehot, axis=0)
    rank = jnp.sum(csum * onehot, axis=1) - 1
    counts = csum[-1]
    ptiles = (counts + tm - 1) // tm
    tile_end = jnp.cumsum(ptiles)
    pstart = (tile_end - ptiles) * tm
    pos = jnp.sum(onehot * pstart[None, :], axis=1) + rank
    n_used = tile_end[-1]
    n_tiles = (2 * t) // tm + N_EXP
    jj = jnp.minimum(jnp.arange(n_tiles, dtype=jnp.int32), n_used - 1)
    tile_expert = jnp.sum((jj[:, None] >= tile_end[None, :]).astype(jnp.int32), axis=1)
    seg_base = jnp.concatenate([pstart + counts, (n_used * tm).reshape(1)])
    seg_len = jnp.concatenate([ptiles * tm - counts, ((n_tiles - n_used) * tm).reshape(1)])
    seg_end = jnp.cumsum(seg_len)
    j = jnp.arange(N_EXP * tm, dtype=jnp.int32)
    seg = jnp.sum((j[:, None] >= seg_end[None, :]).astype(jnp.int32), axis=1)
    seg_1h = (seg[:, None] == jnp.arange(N_EXP + 1, dtype=jnp.int32)[None, :]).astype(jnp.int32)
    pad_dst = j + jnp.sum(seg_1h * (seg_base - (seg_end - seg_len))[None, :], axis=1)
    return (pos.astype(jnp.int32), tile_expert.astype(jnp.int32), n_used.reshape(1).astype(jnp.int32),
            pad_dst.astype(jnp.int32), n_tiles)


def _wait_rows(hbm_ref, n_rows, sem):
    rows = hbm_ref.at[pl.ds(0, n_rows), :]
    pltpu.make_async_copy(rows, rows, sem).wait()


def _tile_row_copy(tiles_ref, tile, row, hbm_ref, hbm_row, sem, to_hbm):
    vm = tiles_ref.at[tile, pl.ds(row, 1), :]
    hb = hbm_ref.at[pl.ds(hbm_row, 1), :]
    return pltpu.make_async_copy(vm, hb, sem) if to_hbm else pltpu.make_async_copy(hb, vm, sem)


def _moe_dispatch_kernel(pos_ref, pad_dst_ref, x_ref, xs_ref, sem):
    n_grp = x_ref.shape[0]
    n_pad = pad_dst_ref.shape[1]

    def send(grp, carry):
        for jr in range(SUBLANES):
            for k in range(2):
                dst = pos_ref[0, 2 * SUBLANES * grp + 2 * jr + k]
                _tile_row_copy(x_ref, grp, jr, xs_ref, dst, sem, True).start()
        return carry

    lax.fori_loop(0, n_grp, send, 0)

    def pad(grp, carry):
        for jr in range(SUBLANES):
            _tile_row_copy(x_ref, 0, jr, xs_ref, pad_dst_ref[0, SUBLANES * grp + jr], sem, True).start()
        return carry

    lax.fori_loop(0, n_pad // SUBLANES, pad, 0)
    _wait_rows(xs_ref, 2 * SUBLANES * n_grp + n_pad, sem)


def _moe_dispatch(x, pos, pad_dst, n_rows_out):
    m = x.shape[0]
    tq = DISPATCH_TOKENS
    n_steps = m // tq
    n_pad = pad_dst.shape[0] // n_steps
    smem = functools.partial(pl.BlockSpec, memory_space=pltpu.SMEM)
    return pl.pallas_call(
        _moe_dispatch_kernel,
        grid=(n_steps,),
        in_specs=[smem((None, 1, 2 * tq), lambda i: (i, 0, 0)),
                  smem((None, 1, n_pad), lambda i: (i, 0, 0)),
                  pl.BlockSpec((tq // SUBLANES, SUBLANES, D_MODEL), lambda i: (i, 0, 0))],
        out_specs=pl.BlockSpec(memory_space=pl.ANY),
        out_shape=jax.ShapeDtypeStruct((n_rows_out, D_MODEL), F32),
        scratch_shapes=[pltpu.SemaphoreType.DMA(())],
        compiler_params=_params("arbitrary"),
        name="moe_dispatch",
    )(pos.reshape(n_steps, 1, 2 * tq), pad_dst.reshape(n_steps, 1, n_pad),
      x.reshape(m // SUBLANES, SUBLANES, D_MODEL))


def _moe_ffn_kernel(te_ref, nu_ref, x_ref, wg_ref, wu_ref, wd_ref, o_ref, xb_ref):
    p = pl.program_id(0)
    f = pl.program_id(1)

    @pl.when(p < nu_ref[0])
    def _():
        @pl.when(f == 0)
        def _():
            xb_ref[...] = x_ref[...].astype(BF16)

        y = _swiglu_partial(xb_ref[...], wg_ref, wu_ref, wd_ref)

        @pl.when(f == 0)
        def _():
            o_ref[...] = y

        @pl.when(f != 0)
        def _():
            o_ref[...] += y

    @pl.when(jnp.logical_and(p >= nu_ref[0], f == 0))
    def _():
        o_ref[...] = jnp.zeros_like(o_ref)


def _moe_ffn(xs, tile_expert, n_used, w_gu, w_down, layer, tm, tf):
    n_tiles = xs.shape[0] // tm
    n_f = D_FF // tf
    tile = lambda p, nu: jnp.minimum(p, nu[0] - 1)
    col = lambda p, f, nu: jnp.where(p < nu[0], f, n_f - 1)
    grid_spec = pltpu.PrefetchScalarGridSpec(
        num_scalar_prefetch=2,
        grid=(n_tiles, n_f),
        in_specs=[pl.BlockSpec((tm, D_MODEL), lambda p, f, te, nu: (tile(p, nu), 0)),
                  pl.BlockSpec((None, None, D_MODEL, tf),
                               lambda p, f, te, nu: (layer, te[p], 0, col(p, f, nu))),
                  pl.BlockSpec((None, None, D_MODEL, tf),
                               lambda p, f, te, nu: (layer, te[p], 0, n_f + col(p, f, nu))),
                  pl.BlockSpec((None, None, tf, D_MODEL),
                               lambda p, f, te, nu: (layer, te[p], col(p, f, nu), 0))],
        out_specs=pl.BlockSpec((tm, D_MODEL), lambda p, f, te, nu: (p, 0)),
        scratch_shapes=[pltpu.VMEM((tm, D_MODEL), BF16)],
    )
    return pl.pallas_call(
        _moe_ffn_kernel,
        grid_spec=grid_spec,
        out_shape=jax.ShapeDtypeStruct(xs.shape, F32),
        compiler_params=_params("arbitrary", "arbitrary"),
        name="moe_ffn",
    )(tile_expert, n_used, xs, w_gu, w_gu, w_down)


def _moe_combine_ln_kernel(pos_ref, pos_next_ref, gate_ref, x_ref, ye_ref, g_ref, b_ref, o_ref,
                           buf_ref, sem):
    i = pl.program_id(0)
    tq = x_ref.shape[0]
    n_grp = tq // SUBLANES
    slot = lax.rem(i, 2)

    def start_fetch(rows_ref, to_slot):
        def fetch(grp, carry):
            for jr in range(SUBLANES):
                for k in range(2):
                    src = rows_ref[0, 2 * SUBLANES * grp + 2 * jr + k]
                    _tile_row_copy(buf_ref.at[to_slot], k * n_grp + grp, jr, ye_ref, src,
                                   sem.at[to_slot], False).start()
            return carry

        lax.fori_loop(0, n_grp, fetch, 0)

    @pl.when(i == 0)
    def _():
        start_fetch(pos_ref, 0)

    @pl.when(i + 1 < pl.num_programs(0))
    def _():
        start_fetch(pos_next_ref, 1 - slot)

    _wait_rows(ye_ref, 2 * tq, sem.at[slot])
    gates = gate_ref[...]
    lane = lax.broadcasted_iota(jnp.int32, gates.shape, 1)
    y0 = buf_ref[slot, 0:n_grp].reshape(tq, D_MODEL)
    y1 = buf_ref[slot, n_grp:2 * n_grp].reshape(tq, D_MODEL)
    y = _lane_pick(gates, lane, 0) * y0 + _lane_pick(gates, lane, 1) * y1
    o_ref[...] = _layer_norm(ALPHA * x_ref[...] + y, g_ref[...], b_ref[...])


def _moe_combine_ln(x, gates, pos, ye, g, b):
    m = x.shape[0]
    tq = COMBINE_TOKENS
    n_steps = m // tq
    pos3 = pos.reshape(n_steps, 1, 2 * tq)
    smem = functools.partial(pl.BlockSpec, memory_space=pltpu.SMEM)
    return pl.pallas_call(
        _moe_combine_ln_kernel,
        grid=(n_steps,),
        in_specs=[smem((None, 1, 2 * tq), lambda i: (i, 0, 0)),
                  smem((None, 1, 2 * tq), lambda i: (jnp.minimum(i + 1, n_steps - 1), 0, 0)),
                  pl.BlockSpec((tq, LANES), lambda i: (i, 0)),
                  pl.BlockSpec((tq, D_MODEL), lambda i: (i, 0)),
                  pl.BlockSpec(memory_space=pl.ANY),
                  pl.BlockSpec((1, D_MODEL), lambda i: (0, 0)),
                  pl.BlockSpec((1, D_MODEL), lambda i: (0, 0))],
        out_specs=pl.BlockSpec((tq, D_MODEL), lambda i: (i, 0)),
        out_shape=jax.ShapeDtypeStruct((m, D_MODEL), F32),
        scratch_shapes=[pltpu.VMEM((2, 2 * tq // SUBLANES, SUBLANES, D_MODEL), F32),
                        pltpu.SemaphoreType.DMA((2,))],
        compiler_params=_params("arbitrary"),
        name="moe_combine_ln",
    )(pos3, pos3, gates, x, ye, g, b)


def _moe_routed_ln(x, gates, idx, w_gu, w_down, layer, g, b):
    pos, tile_expert, n_used, pad_dst, n_tiles = _moe_plan(idx[:, :2], MOE_TM)
    xs = _moe_dispatch(x, pos, pad_dst, n_tiles * MOE_TM)
    ye = _moe_ffn(xs, tile_expert, n_used, w_gu, w_down, layer, MOE_TM, FFN_TF)
    return _moe_combine_ln(x, gates, pos, ye, g, b)


GDN_PASSES_QK = 1
GDN_PASSES_SOLVE = 1
GDN_PASSES_STATE = 1
INV_BLOCK_SHIFT = 4
GDN_SEQS = 4


def _gdn_chunk_kernel(qkv_ref, z_ref, ab_ref, cw_ref, alog_ref, dtb_ref, nw_ref, hist0_ref, s0_ref,
                      mix_ref, sout_ref, hout_ref, xh_ref, s_ref):
    n = pl.program_id(1)
    c = GDN_CHUNK
    hrows = SUBLANES
    n_seq = qkv_ref.shape[0]
    items = [(b, h) for b in range(n_seq) for h in range(H_A)]

    @pl.when(n == 0)
    def _():
        xh_ref[:, 0:hrows, :] = hist0_ref[...]
        s_ref[...] = s0_ref[...]

    xh_ref[:, hrows:hrows + c, :] = qkv_ref[...]

    row = lax.broadcasted_iota(jnp.int32, (c, c), 0)
    col = lax.broadcasted_iota(jnp.int32, (c, c), 1)
    causal = row >= col
    strict = row > col
    eye = jnp.where(row == col, 1.0, 0.0)
    blockdiag = (row >> INV_BLOCK_SHIFT) == (col >> INV_BLOCK_SHIFT)
    tril_ones = jnp.where(causal, 1.0, 0.0).astype(BF16)
    lane = lax.broadcasted_iota(jnp.int32, (c, LANES), 1)
    mm_s = functools.partial(_dot, passes=GDN_PASSES_SOLVE)

    beta_all, gcum_all, gcum_t = [], [], []
    for b in range(n_seq):
        ab = ab_ref[b]
        g_all = -jnp.exp(alog_ref[...]) * _softplus(ab + dtb_ref[...])
        beta_all.append(_sigmoid(ab))
        gc = _dot_exact_lhs(tril_ones, g_all)
        gcum_all.append(gc)
        gcum_t.append(gc.T)

    def conv_silu(b, c0):
        acc = xh_ref[b, hrows - 3:hrows - 3 + c, c0:c0 + LANES] * cw_ref[0:1, c0:c0 + LANES]
        for j in range(1, CONV_W):
            acc = acc + (xh_ref[b, hrows - 3 + j:hrows - 3 + j + c, c0:c0 + LANES]
                         * cw_ref[j:j + 1, c0:c0 + LANES])
        return _silu(acc)

    q, k, v, kb, egc, gc_col, gc_last, beta_col, decay = {}, {}, {}, {}, {}, {}, {}, {}, {}
    for it in items:
        b, h = it
        qi = conv_silu(b, h * DK_A)
        ki = conv_silu(b, QK_A + h * DK_A)
        v[it] = conv_silu(b, 2 * QK_A + h * DV_A)
        q[it] = qi * lax.rsqrt(jnp.sum(qi * qi, axis=-1, keepdims=True) + RMS_EPS) * (DK_A ** -0.5)
        k[it] = ki * lax.rsqrt(jnp.sum(ki * ki, axis=-1, keepdims=True) + RMS_EPS)
        gc_col[it] = _lane_pick(gcum_all[b], lane, h)
        beta_col[it] = _lane_pick(beta_all[b], lane, B_LANE + h)
        gc_last[it] = gc_col[it][c - 1:c, :]
        decay[it] = jnp.where(causal, jnp.exp(gc_col[it] - gcum_t[b][h:h + 1, :c]), 0.0)
        egc[it] = jnp.exp(gc_col[it])
        kb[it] = k[it] * beta_col[it]

    kk = {it: _dot(jnp.concatenate([kb[it], q[it]], axis=0), k[it], _NT, GDN_PASSES_QK) for it in items}
    a_mat = {it: jnp.where(strict, kk[it][:c] * decay[it], 0.0) for it in items}
    attn = {it: kk[it][c:] * decay[it] for it in items}

    a_d = {it: jnp.where(blockdiag, a_mat[it], 0.0) for it in items}
    a_n = {it: jnp.where(blockdiag, 0.0, a_mat[it]) for it in items}
    p2 = {it: mm_s(a_d[it], a_d[it]) for it in items}
    e_d = {it: p2[it] - a_d[it] - mm_s(a_d[it], p2[it]) for it in items}
    p4 = {it: mm_s(p2[it], p2[it]) for it in items}
    e_d = {it: e_d[it] + p4[it] + mm_s(e_d[it], p4[it]) for it in items}
    p8 = {it: mm_s(p4[it], p4[it]) for it in items}
    e_d = {it: e_d[it] + p8[it] + mm_s(e_d[it], p8[it]) for it in items}
    m_blk = {it: a_n[it] + mm_s(e_d[it], a_n[it]) for it in items}
    rhs = {it: jnp.concatenate([v[it] * beta_col[it], kb[it] * egc[it]], axis=1) for it in items}
    rhs = {it: rhs[it] + mm_s(e_d[it], rhs[it]) for it in items}
    m2 = {it: mm_s(m_blk[it], m_blk[it]) for it in items}
    f_y = {it: m2[it] - m_blk[it] - mm_s(m_blk[it], m2[it]) for it in items}
    sol = {it: rhs[it] + mm_s(f_y[it], rhs[it]) for it in items}

    s_old = {it: s_ref[it[0], it[1]] for it in items}
    wq = {it: _dot(jnp.concatenate([sol[it][:, DV_A:], q[it] * egc[it]], axis=0), s_old[it],
                   _NN, GDN_PASSES_STATE) for it in items}
    v_new = {it: sol[it][:, :DV_A] - wq[it][:c] for it in items}
    o = {it: wq[it][c:] + _dot(attn[it], v_new[it], _NN, GDN_PASSES_STATE) for it in items}
    for it in items:
        k_dec = k[it] * jnp.exp(gc_last[it] - gc_col[it])
        s_ref[it[0], it[1]] = (s_old[it] * jnp.exp(gc_last[it])
                               + _dot(k_dec, v_new[it], _TN, GDN_PASSES_STATE))
    for it in items:
        b, h = it
        oi = o[it]
        oi = oi * lax.rsqrt(jnp.mean(oi * oi, axis=-1, keepdims=True) + RMS_EPS) * nw_ref[...]
        mix_ref[b, :, h * DV_A:(h + 1) * DV_A] = oi * _silu(z_ref[b, :, h * DV_A:(h + 1) * DV_A])

    xh_ref[:, 0:hrows, :] = xh_ref[:, c:c + hrows, :]

    @pl.when(n == pl.num_programs(1) - 1)
    def _():
        sout_ref[...] = s_ref[...]
        hout_ref[...] = xh_ref[:, c:c + hrows, :]


def _gdn_chunked(proj, conv_w, alog_row, dtb_row, nw_row, hist0, s0):
    bsz, t, _ = proj.shape
    c = GDN_CHUNK
    nb = GDN_SEQS
    n_chunks = t // c
    row_spec = lambda width, blk: pl.BlockSpec((nb, c, width), lambda b, n: (b, n, blk))
    full2 = lambda shape: pl.BlockSpec(shape, lambda b, n: (0, 0))
    return pl.pallas_call(
        _gdn_chunk_kernel,
        grid=(bsz // nb, n_chunks),
        in_specs=[row_spec(CONV_DIM, 0),
                  row_spec(V_A, A_Z_OFF // V_A),
                  row_spec(LANES, A_AB_OFF // LANES),
                  full2((CONV_W, CONV_DIM)),
                  full2((1, LANES)), full2((1, LANES)), full2((1, DV_A)),
                  pl.BlockSpec((nb, SUBLANES, CONV_DIM), lambda b, n: (b, 0, 0)),
                  pl.BlockSpec((nb, H_A, DK_A, DV_A), lambda b, n: (b, 0, 0, 0))],
        out_specs=[pl.BlockSpec((nb, c, V_A), lambda b, n: (b, n, 0)),
                   pl.BlockSpec((nb, H_A, DK_A, DV_A), lambda b, n: (b, 0, 0, 0)),
                   pl.BlockSpec((nb, SUBLANES, CONV_DIM), lambda b, n: (b, 0, 0))],
        out_shape=[jax.ShapeDtypeStruct((bsz, t, V_A), F32),
                   jax.ShapeDtypeStruct((bsz, H_A, DK_A, DV_A), F32),
                   jax.ShapeDtypeStruct((bsz, SUBLANES, CONV_DIM), F32)],
        scratch_shapes=[pltpu.VMEM((nb, SUBLANES + c, CONV_DIM), F32),
                        pltpu.VMEM((nb, H_A, DK_A, DV_A), F32)],
        compiler_params=_params("parallel", "arbitrary"),
        name="gdn_chunked",
    )(proj, proj, proj, conv_w, alog_row, dtb_row, nw_row, hist0, s0)


def _gdn_step_kernel(qkv_ref, z_ref, ab_ref, cw_ref, alog_ref, dtb_ref, nw_ref, hist_ref, s_ref,
                     mix_ref, sout_ref, hout_ref, o_scr):
    rows = qkv_ref.shape[0]
    new = qkv_ref[...]
    y = hist_ref[0] * cw_ref[0:1, :]
    for j in range(1, CONV_W - 1):
        y = y + hist_ref[j] * cw_ref[j:j + 1, :]
    y = y + new * cw_ref[CONV_W - 1:CONV_W, :]
    qkv = _silu(y)
    for j in range(CONV_W - 2):
        hout_ref[j] = hist_ref[j + 1]
    hout_ref[CONV_W - 2] = new

    ab = ab_ref[...]
    lane = lax.broadcasted_iota(jnp.int32, ab.shape, 1)
    g_all = -jnp.exp(alog_ref[...]) * _softplus(ab + dtb_ref[...])
    beta_all = _sigmoid(ab)
    ri = lax.broadcasted_iota(jnp.int32, (DK_A, DK_A), 0)
    ci = lax.broadcasted_iota(jnp.int32, (DK_A, DK_A), 1)
    diag = ri == ci

    def as_column(r):
        return jnp.sum(jnp.where(diag, r, 0.0), axis=-1, keepdims=True)

    for h in range(H_A):
        q = qkv[:, h * DK_A:(h + 1) * DK_A]
        k = qkv[:, QK_A + h * DK_A:QK_A + (h + 1) * DK_A]
        v = qkv[:, 2 * QK_A + h * DV_A:2 * QK_A + (h + 1) * DV_A]
        q = q * lax.rsqrt(jnp.sum(q * q, axis=-1, keepdims=True) + RMS_EPS) * (DK_A ** -0.5)
        k = k * lax.rsqrt(jnp.sum(k * k, axis=-1, keepdims=True) + RMS_EPS)
        eg = jnp.exp(_lane_pick(g_all, lane, h))
        beta = _lane_pick(beta_all, lane, B_LANE + h)
        for s in range(rows):
            k_col = as_column(k[s:s + 1, :])
            q_col = as_column(q[s:s + 1, :])
            st = s_ref[s, h] * eg[s:s + 1, :]
            ks = jnp.sum(k_col * st, axis=0, keepdims=True)
            u = beta[s:s + 1, :] * (v[s:s + 1, :] - ks)
            st = st + k_col * u
            sout_ref[s, h] = st
            o_scr[s:s + 1, h * DV_A:(h + 1) * DV_A] = jnp.sum(q_col * st, axis=0, keepdims=True)

    for h in range(H_A):
        o = o_scr[:, h * DV_A:(h + 1) * DV_A]
        o = o * lax.rsqrt(jnp.mean(o * o, axis=-1, keepdims=True) + RMS_EPS) * nw_ref[...]
        mix_ref[:, h * DV_A:(h + 1) * DV_A] = o * _silu(z_ref[:, h * DV_A:(h + 1) * DV_A])


def _gdn_step(proj, conv_w, alog_row, dtb_row, nw_row, hist_t, s0, layer):
    bsz = proj.shape[0]
    rows = SUBLANES
    full1 = lambda shape: pl.BlockSpec(shape, lambda i: (0, 0))
    return pl.pallas_call(
        _gdn_step_kernel,
        grid=(bsz // rows,),
        in_specs=[pl.BlockSpec((rows, CONV_DIM), lambda i: (i, 0)),
                  pl.BlockSpec((rows, V_A), lambda i: (i, A_Z_OFF // V_A)),
                  pl.BlockSpec((rows, LANES), lambda i: (i, A_AB_OFF // LANES)),
                  full1((CONV_W, CONV_DIM)),
                  full1((1, LANES)), full1((1, LANES)), full1((1, DV_A)),
                  pl.BlockSpec((None, CONV_W - 1, rows, CONV_DIM), lambda i: (layer, 0, i, 0)),
                  pl.BlockSpec((None, rows, H_A, DK_A, DV_A), lambda i: (layer, i, 0, 0, 0))],
        out_specs=[pl.BlockSpec((rows, V_A), lambda i: (i, 0)),
                   pl.BlockSpec((rows, H_A, DK_A, DV_A), lambda i: (i, 0, 0, 0)),
                   pl.BlockSpec((CONV_W - 1, rows, CONV_DIM), lambda i: (0, i, 0))],
        out_shape=[jax.ShapeDtypeStruct((bsz, V_A), F32),
                   jax.ShapeDtypeStruct((bsz, H_A, DK_A, DV_A), F32),
                   jax.ShapeDtypeStruct((CONV_W - 1, bsz, CONV_DIM), F32)],
        scratch_shapes=[pltpu.VMEM((rows, V_A), F32)],
        compiler_params=_params("parallel"),
        name="gdn_step",
    )(proj, proj, proj, conv_w, alog_row, dtb_row, nw_row, hist_t, s0)


SGU_CHUNKS = 4


def _sgu_chunk_kernel(u_ref, v_ref, g_ref, b_ref, ws_ref, bs_ref, mix_ref):
    cb = CHUNK_B
    row = lax.broadcasted_iota(jnp.int32, (cb, cb), 0)
    col = lax.broadcasted_iota(jnp.int32, (cb, cb), 1)
    ws = [jnp.where(row >= col, ws_ref[gi], 0.0).astype(BF16) for gi in range(G_B)]
    for r0 in range(0, u_ref.shape[0], cb):
        u = _gelu_tanh(u_ref[r0:r0 + cb, :])
        v = _layer_norm(_gelu_tanh(v_ref[r0:r0 + cb, :]), g_ref[...], b_ref[...])
        for gi in range(G_B):
            mixed = _dot(ws[gi], v[:, gi * CH_B:(gi + 1) * CH_B]) + bs_ref[gi]
            mix_ref[r0:r0 + cb, gi * CH_B:(gi + 1) * CH_B] = u[:, gi * CH_B:(gi + 1) * CH_B] * mixed


def _sgu_chunked(proj, ln_g, ln_b, w_s, bs_b):
    bsz, t, _ = proj.shape
    c = CHUNK_B * SGU_CHUNKS
    return pl.pallas_call(
        _sgu_chunk_kernel,
        grid=(bsz, t // c),
        in_specs=[pl.BlockSpec((None, c, D_B), lambda b, n: (b, n, 0)),
                  pl.BlockSpec((None, c, D_B), lambda b, n: (b, n, 1)),
                  pl.BlockSpec((1, D_B), lambda b, n: (0, 0)),
                  pl.BlockSpec((1, D_B), lambda b, n: (0, 0)),
                  pl.BlockSpec((G_B, CHUNK_B, CHUNK_B), lambda b, n: (0, 0, 0)),
                  pl.BlockSpec((G_B, CHUNK_B, CH_B), lambda b, n: (0, 0, 0))],
        out_specs=pl.BlockSpec((None, c, D_B), lambda b, n: (b, n, 0)),
        out_shape=jax.ShapeDtypeStruct((bsz, t, D_B), F32),
        compiler_params=_params("parallel", "parallel"),
        name="sgu_chunked",
    )(proj, proj, ln_g, ln_b, w_s, bs_b)


def _sgu_first_kernel(u_ref, v_ref, g_ref, b_ref, w00_ref, b0_ref, mix_ref, vout_ref):
    u = _gelu_tanh(u_ref[...])
    v = _layer_norm(_gelu_tanh(v_ref[...]), g_ref[...], b_ref[...])
    vout_ref[...] = v
    mix_ref[...] = u * (w00_ref[...] * v + b0_ref[...])


def _sgu_first(proj, ln_g, ln_b, w00_row, b0_row):
    bsz = proj.shape[0]
    row = lambda: pl.BlockSpec((1, D_B), lambda i: (0, 0))
    return pl.pallas_call(
        _sgu_first_kernel,
        grid=(1,),
        in_specs=[pl.BlockSpec((bsz, D_B), lambda i: (0, 0)),
                  pl.BlockSpec((bsz, D_B), lambda i: (0, 1)),
                  row(), row(), row(), row()],
        out_specs=[pl.BlockSpec((bsz, D_B), lambda i: (0, 0)),
                   pl.BlockSpec((bsz, D_B), lambda i: (0, 0))],
        out_shape=[jax.ShapeDtypeStruct((bsz, D_B), F32),
                   jax.ShapeDtypeStruct((bsz, D_B), F32)],
        compiler_params=_params("arbitrary"),
        name="sgu_first",
    )(proj, proj, ln_g, ln_b, w00_row, b0_row)


def _head_masks(shape):
    lane = lax.broadcasted_iota(jnp.int32, shape, len(shape) - 1)
    return [(lane // DH_X) == h for h in range(H_X)]


def _xattn_kernel(q_ref, k_ref, v_ref, o_ref):
    q = q_ref[...]
    kb = k_ref[...].astype(BF16)
    vb = v_ref[...].astype(BF16)
    masks = _head_masks(q.shape)
    heads = range(H_X)
    s = [_dot(jnp.where(masks[h], q, 0.0), kb, _NT) * (DH_X ** -0.5) for h in heads]
    p = [jnp.exp(s[h] - jnp.max(s[h], axis=-1, keepdims=True)) for h in heads]
    p = [p[h] / jnp.sum(p[h], axis=-1, keepdims=True) for h in heads]
    pv = [_dot(p[h], vb) for h in heads]
    out = jnp.where(masks[0], pv[0], 0.0)
    for h in range(1, H_X):
        out = jnp.where(masks[h], pv[h], out)
    o_ref[...] = out


def _xattn(proj, xq_blk, mem_k, mem_v, layer, tq):
    bsz, t, _ = proj.shape
    return pl.pallas_call(
        _xattn_kernel,
        grid=(bsz, t // tq),
        in_specs=[pl.BlockSpec((None, tq, D_X), lambda b, i: (b, i, xq_blk)),
                  pl.BlockSpec((None, None, N_MEM, D_X), lambda b, i: (layer, b, 0, 0)),
                  pl.BlockSpec((None, None, N_MEM, D_X), lambda b, i: (layer, b, 0, 0))],
        out_specs=pl.BlockSpec((None, tq, D_X), lambda b, i: (b, i, 0)),
        out_shape=jax.ShapeDtypeStruct((bsz, t, D_X), F32),
        compiler_params=_params("parallel", "parallel"),
        name="xattn",
    )(proj, mem_k, mem_v)


def _xattn_step_kernel(q_ref, k_ref, v_ref, o_ref):
    rows = q_ref.shape[0]
    q = q_ref[...]
    hrow = lax.broadcasted_iota(jnp.int32, (SUBLANES, D_X), 0)
    hlane = lax.broadcasted_iota(jnp.int32, (SUBLANES, D_X), 1) // DH_X
    sel = hrow == hlane
    seqs = range(rows)
    sc = [_dot(jnp.where(sel, q[s:s + 1, :], 0.0), k_ref[s], _NT) * (DH_X ** -0.5) for s in seqs]
    p = [jnp.exp(sc[s] - jnp.max(sc[s], axis=-1, keepdims=True)) for s in seqs]
    p = [p[s] / jnp.sum(p[s], axis=-1, keepdims=True) for s in seqs]
    pv = [_dot(p[s], v_ref[s]) for s in seqs]
    for s in seqs:
        o_ref[s:s + 1, :] = jnp.sum(jnp.where(sel, pv[s], 0.0), axis=0, keepdims=True)


def _xattn_step(proj, xq_blk, mem_k, mem_v, layer):
    bsz = proj.shape[0]
    rows = SUBLANES
    return pl.pallas_call(
        _xattn_step_kernel,
        grid=(bsz // rows,),
        in_specs=[pl.BlockSpec((rows, D_X), lambda i: (i, xq_blk)),
                  pl.BlockSpec((None, rows, N_MEM, D_X), lambda i: (layer, i, 0, 0)),
                  pl.BlockSpec((None, rows, N_MEM, D_X), lambda i: (layer, i, 0, 0))],
        out_specs=pl.BlockSpec((rows, D_X), lambda i: (i, 0)),
        out_shape=jax.ShapeDtypeStruct((bsz, D_X), F32),
        compiler_params=_params("parallel"),
        name="xattn_step",
    )(proj, mem_k, mem_v)


def _pad_lanes(v):
    return jnp.zeros((1, LANES), F32).at[0, :v.shape[0]].set(v.astype(F32))


def _prep_weights(a_w_in, a_A_log, a_dt_bias, b_w_s, b_b_s, moe_router):
    qkvz = a_w_in[:, :, :A_XQ_OFF]
    a_cols = a_w_in[:, :, A_XQ_OFF:A_XQ_OFF + H_A]
    b_cols = a_w_in[:, :, A_XQ_OFF + H_A:A_XQ_OFF + 2 * H_A]
    xq = a_w_in[:, :, A_XQ_OFF + 2 * H_A:]
    ab = jnp.zeros(a_w_in.shape[:2] + (LANES,), F32)
    ab = ab.at[:, :, :H_A].set(a_cols).at[:, :, B_LANE:B_LANE + H_A].set(b_cols)
    a_w = jnp.concatenate([qkvz, xq, ab], axis=-1).astype(BF16)
    alog_rows = [_pad_lanes(a_A_log[j]) for j in range(a_A_log.shape[0])]
    dtb_rows = [_pad_lanes(a_dt_bias[j]) for j in range(a_dt_bias.shape[0])]
    bs_b = jnp.broadcast_to(b_b_s[..., None], b_b_s.shape + (CH_B,)).astype(F32)
    w00_rows = jnp.repeat(b_w_s[:, :, 0, 0], CH_B, axis=-1)[:, None, :]
    b0_rows = jnp.repeat(b_b_s[:, :, 0], CH_B, axis=-1)[:, None, :]
    r = jnp.zeros(moe_router.shape[:2] + (LANES,), F32).at[:, :, :N_EXP].set(moe_router)
    r_hi = r.astype(BF16)
    r_lo = (r - r_hi.astype(F32)).astype(BF16)
    return a_w, alog_rows, dtb_rows, bs_b, w00_rows, b0_rows, r_hi, r_lo


def kernel(x_prompt, x_sample, state_gdn, state_conv, cache_mem_k, cache_mem_v, mem_prompt,
           a_w_in, a_conv_w, a_A_log, a_dt_bias, a_norm_w,
           b_w_in, b_ln_g, b_ln_b, b_w_s, b_b_s,
           w_mem_kv, w_out, ln1_g, ln1_b, ln2_g, ln2_b,
           ffn_w_gu, ffn_w_down, moe_router, moe_w_gu, moe_w_down):
    n_p, t_p, _ = x_prompt.shape
    n_s = x_sample.shape[0]
    n_a = a_w_in.shape[0]

    a_w, alog_rows, dtb_rows, bs_b, w00_rows, b0_rows, r_hi, r_lo = _prep_weights(
        a_w_in, a_A_log, a_dt_bias, b_w_s, b_b_s, moe_router)
    ffn_gu = ffn_w_gu.astype(BF16)[:, None]
    ffn_dn = ffn_w_down.astype(BF16)[:, None]
    moe_gu = moe_w_gu.astype(BF16)
    moe_dn = moe_w_down.astype(BF16)
    row = lambda v: v.reshape(1, -1).astype(F32)

    w_kv = jnp.transpose(w_mem_kv, (1, 0, 2)).reshape(D_MODEL, DEPTH * 2 * D_X).astype(BF16)
    kv = _matmul(mem_prompt.reshape(n_p * N_MEM, D_MODEL), w_kv, 512, 1024, "mem_kv")
    kv = kv.reshape(n_p, N_MEM, DEPTH, 2, D_X)
    p_mem_k = jnp.transpose(kv[:, :, :, 0, :], (2, 0, 1, 3))
    p_mem_v = jnp.transpose(kv[:, :, :, 1, :], (2, 0, 1, 3))
    s_mem_k = cache_mem_k.reshape(DEPTH, n_s, N_MEM, D_X)
    s_mem_v = cache_mem_v.reshape(DEPTH, n_s, N_MEM, D_X)

    xp = x_prompt.reshape(n_p * t_p, D_MODEL)
    xs = x_sample.reshape(n_s, D_MODEL)
    p_hist0 = jnp.zeros((n_p, SUBLANES, CONV_DIM), F32)
    p_s0 = jnp.zeros((n_p, H_A, DK_A, DV_A), F32)
    s_hist_t = jnp.transpose(state_conv, (0, 2, 1, 3))

    p_gdn, p_conv, s_gdn, s_conv, s_sgu_v = [], [], [], [], []
    tm_p, tm_s = 1024, n_s
    for i in range(DEPTH):
        j = i // 2
        if i % 2 == 0:
            proj_p = _matmul(xp, a_w[j], 512, 1152, "in_proj_a")
            proj_s = _matmul(xs, a_w[j], tm_s, 1152, "in_proj_a_s")
            nw = row(a_norm_w[j])
            mix_p, sp, hp = _gdn_chunked(proj_p.reshape(n_p, t_p, A_COLS), a_conv_w[j],
                                         alog_rows[j], dtb_rows[j], nw, p_hist0, p_s0)
            mix_p = mix_p.reshape(n_p * t_p, V_A)
            p_gdn.append(sp)
            p_conv.append(hp[:, SUBLANES - (CONV_W - 1):, :])
            mix_s, ss, hs = _gdn_step(proj_s, a_conv_w[j], alog_rows[j], dtb_rows[j], nw,
                                      s_hist_t, state_gdn, j)
            s_gdn.append(ss)
            s_conv.append(jnp.transpose(hs, (1, 0, 2)))
            cols, xq_blk = A_COLS, A_XQ_OFF // D_X
        else:
            b_w = b_w_in[j].astype(BF16)
            proj_p = _matmul(xp, b_w, 512, 896, "in_proj_b")
            proj_s = _matmul(xs, b_w, tm_s, 896, "in_proj_b_s")
            mix_p = _sgu_chunked(proj_p.reshape(n_p, t_p, B_COLS), row(b_ln_g[j]), row(b_ln_b[j]),
                                 b_w_s[j], bs_b[j]).reshape(n_p * t_p, D_B)
            mix_s, v_s = _sgu_first(proj_s, row(b_ln_g[j]), row(b_ln_b[j]), w00_rows[j], b0_rows[j])
            s_sgu_v.append(v_s.reshape(n_s, 1, D_B))
            cols, xq_blk = B_COLS, B_XQ_OFF // D_X
        xo_p = _xattn(proj_p.reshape(n_p, t_p, cols), xq_blk, p_mem_k, p_mem_v, i, 512)
        xo_s = _xattn_step(proj_s, xq_blk, s_mem_k, s_mem_v, i)
        g1, b1, g2, b2 = row(ln1_g[i]), row(ln1_b[i]), row(ln2_g[i]), row(ln2_b[i])
        w_o = w_out[i].astype(BF16)
        xo_p = xo_p.reshape(n_p * t_p, D_X)
        if i % 2 == 0:
            xp = _outproj_ln(mix_p, xo_p, xp, w_o, g1, b1, 512)
            xs = _outproj_ln(mix_s, xo_s, xs, w_o, g1, b1, tm_s)
            xp = _ffn_ln(xp, None, ffn_gu, ffn_dn, j, g2, b2, tm_p, FFN_TF)
            xs = _ffn_ln(xs, None, ffn_gu, ffn_dn, j, g2, b2, tm_s, FFN_TF)
        else:
            router = (r_hi[j], r_lo[j])
            xp, _, gates_p, idx_p = _outproj_ln(mix_p, xo_p, xp, w_o, g1, b1, 512, router)
            xs, comb_s, _, _ = _outproj_ln(mix_s, xo_s, xs, w_o, g1, b1, tm_s, router)
            xp = _moe_routed_ln(xp, gates_p, idx_p, moe_gu, moe_dn, j, g2, b2)
            xs = _ffn_ln(xs, comb_s, moe_gu, moe_dn, j, g2, b2, tm_s, FFN_TF)

    mem_shape = (DEPTH, n_p, N_MEM, H_X, DH_X)
    return (xp.reshape(n_p, t_p, D_MODEL),
            xs.reshape(n_s, 1, D_MODEL),
            jnp.stack(p_gdn),
            jnp.stack(p_conv),
            p_mem_k.reshape(mem_shape),
            p_mem_v.reshape(mem_shape),
            jnp.stack(s_gdn),
            jnp.stack(s_conv),
            jnp.stack(s_sgu_v))
```

```python
import functools

import jax
import jax.numpy as jnp
from jax import lax
from jax.experimental import pallas as pl
from jax.experimental.pallas import tpu as pltpu

F32 = jnp.float32
BF16 = jnp.bfloat16

D_MODEL = 1024
DEPTH = 4
H_A = 6
DK_A = 128
DV_A = 128
QK_A = H_A * DK_A
V_A = H_A * DV_A
CONV_W = 4
CONV_DIM = 2 * QK_A + V_A
GDN_CHUNK = 64
G_B = 6
CH_B = 128
D_B = G_B * CH_B
CHUNK_B = 128
N_MEM = 256
H_X = 4
DH_X = 64
D_X = H_X * DH_X
D_FF = 3584
N_EXP = 8
ALPHA = (2 * DEPTH) ** 0.25
LN_EPS = 1e-5
RMS_EPS = 1e-6

LANES = 128
SUBLANES = 8
VMEM_LIMIT_BYTES = 56 * 1024 * 1024

A_Z_OFF = CONV_DIM
A_XQ_OFF = CONV_DIM + V_A
A_AB_OFF = A_XQ_OFF + D_X
A_COLS = A_AB_OFF + LANES
B_LANE = 8
B_XQ_OFF = 2 * D_B
B_COLS = 2 * D_B + D_X


def _params(*sem):
    return pltpu.CompilerParams(dimension_semantics=sem, vmem_limit_bytes=VMEM_LIMIT_BYTES)


def _split2(x):
    hi = x.astype(BF16)
    lo = (x - hi.astype(F32)).astype(BF16)
    return hi, lo


_NN = (((1,), (0,)), ((), ()))
_NT = (((1,), (1,)), ((), ()))
_TN = (((0,), (0,)), ((), ()))


def _dot(a, b, dims=_NN, passes=1):
    if passes == 1:
        return lax.dot_general(a.astype(BF16), b.astype(BF16), dims, preferred_element_type=F32)
    ah, al = _split2(a)
    bh, bl = _split2(b)
    dg = functools.partial(lax.dot_general, dimension_numbers=dims, preferred_element_type=F32)
    return dg(ah, bh) + (dg(ah, bl) + dg(al, bh))


def _dot_exact_lhs(lhs_bf16, x):
    x1 = x.astype(BF16)
    r1 = x - x1.astype(F32)
    x2 = r1.astype(BF16)
    x3 = (r1 - x2.astype(F32)).astype(BF16)
    dg = functools.partial(lax.dot_general, dimension_numbers=_NN, preferred_element_type=F32)
    return dg(lhs_bf16, x1) + (dg(lhs_bf16, x2) + dg(lhs_bf16, x3))


def _sigmoid(x):
    return 1.0 / (1.0 + jnp.exp(-x))


def _silu(x):
    return x * _sigmoid(x)


def _softplus(x):
    return jnp.maximum(x, 0.0) + jnp.log1p(jnp.exp(-jnp.abs(x)))


def _gelu_tanh(x):
    c = 0.7978845608028654
    return 0.5 * x * (1.0 + jnp.tanh(c * (x + 0.044715 * (x * x * x))))


def _layer_norm(y, g, b):
    mu = jnp.mean(y, axis=-1, keepdims=True)
    d = y - mu
    var = jnp.mean(d * d, axis=-1, keepdims=True)
    return d * lax.rsqrt(var + LN_EPS) * g + b


def _lane_pick(x, lane_iota, idx):
    return jnp.sum(jnp.where(lane_iota == idx, x, 0.0), axis=-1, keepdims=True)


def _mm_kernel(x_ref, w_ref, o_ref, *, chunk):
    xb = x_ref[...].astype(BF16)
    for c0 in range(0, o_ref.shape[1], chunk):
        o_ref[:, c0:c0 + chunk] = jnp.dot(xb, w_ref[:, c0:c0 + chunk], preferred_element_type=F32)


def _matmul(x, w, tm, chunk, name):
    m, k = x.shape
    n = w.shape[1]
    return pl.pallas_call(
        functools.partial(_mm_kernel, chunk=chunk),
        grid=(m // tm,),
        in_specs=[pl.BlockSpec((tm, k), lambda i: (i, 0)),
                  pl.BlockSpec((k, n), lambda i: (0, 0))],
        out_specs=pl.BlockSpec((tm, n), lambda i: (i, 0)),
        out_shape=jax.ShapeDtypeStruct((m, n), F32),
        compiler_params=_params("parallel"),
        name=name,
    )(x, w)


def _outproj_ln_kernel(mix_ref, xo_ref, x_ref, wm_ref, wx_ref, g_ref, b_ref, *rest):
    h = jnp.dot(mix_ref[...].astype(BF16), wm_ref[...], preferred_element_type=F32)
    h = h + jnp.dot(xo_ref[...].astype(BF16), wx_ref[...], preferred_element_type=F32)
    y = _layer_norm(ALPHA * x_ref[...] + h, g_ref[...], b_ref[...])
    if len(rest) == 1:
        (o_ref,) = rest
    else:
        rh_ref, rl_ref, o_ref, comb_ref, gate_ref, idx_ref = rest
        comb_ref[...], gate_ref[...], idx_ref[...] = _route_top2(y, rh_ref[...], rl_ref[...])
    o_ref[...] = y


def _outproj_ln(mix, xo, x, w_out, g, b, tm, router=None):
    m = x.shape[0]
    d_mix = mix.shape[1]
    n_blk = d_mix // D_X
    row_blk = lambda width: pl.BlockSpec((tm, width), lambda i: (i, 0))
    const = lambda shape, blk=0: pl.BlockSpec(shape, lambda i: (blk, 0))
    in_specs = [row_blk(d_mix), row_blk(D_X), row_blk(D_MODEL),
                const((d_mix, D_MODEL)), const((D_X, D_MODEL), n_blk),
                const((1, D_MODEL)), const((1, D_MODEL))]
    args = [mix, xo, x, w_out, w_out, g, b]
    out_specs = [row_blk(D_MODEL)]
    out_shape = [jax.ShapeDtypeStruct((m, D_MODEL), F32)]
    if router is not None:
        in_specs += [const((D_MODEL, LANES)), const((D_MODEL, LANES))]
        args += list(router)
        out_specs += [row_blk(LANES)] * 3
        out_shape += [jax.ShapeDtypeStruct((m, LANES), F32), jax.ShapeDtypeStruct((m, LANES), F32),
                      jax.ShapeDtypeStruct((m, LANES), jnp.int32)]
    out = pl.pallas_call(
        _outproj_ln_kernel,
        grid=(m // tm,),
        in_specs=in_specs,
        out_specs=out_specs,
        out_shape=out_shape,
        compiler_params=_params("parallel"),
        name="outproj_ln",
    )(*args)
    return out[0] if router is None else out


FFN_CHUNK = 256


def _swiglu_partial(xb, wg_ref, wu_ref, wd_ref, row_scale=None):
    y = None
    for c0 in range(0, wg_ref.shape[-1], FFN_CHUNK):
        hg = jnp.dot(xb, wg_ref[:, c0:c0 + FFN_CHUNK], preferred_element_type=F32)
        hu = jnp.dot(xb, wu_ref[:, c0:c0 + FFN_CHUNK], preferred_element_type=F32)
        h = _silu(hg) * hu
        if row_scale is not None:
            h = h * row_scale
        part = jnp.dot(h.astype(BF16), wd_ref[c0:c0 + FFN_CHUNK, :], preferred_element_type=F32)
        y = part if y is None else y + part
    return y


def _ffn_kernel(*refs, moe):
    if moe:
        x_ref, comb_ref, wg_ref, wu_ref, wd_ref, g_ref, b_ref, o_ref, xb_ref, acc_ref = refs
    else:
        x_ref, wg_ref, wu_ref, wd_ref, g_ref, b_ref, o_ref, xb_ref, acc_ref = refs
    e = pl.program_id(1)
    f = pl.program_id(2)
    first = jnp.logical_and(e == 0, f == 0)
    last = jnp.logical_and(e == pl.num_programs(1) - 1, f == pl.num_programs(2) - 1)

    @pl.when(first)
    def _():
        xb_ref[...] = x_ref[...].astype(BF16)
        acc_ref[...] = jnp.zeros_like(acc_ref)

    row_scale = None
    if moe:
        comb = comb_ref[...]
        lane = lax.broadcasted_iota(jnp.int32, comb.shape, 1)
        row_scale = _lane_pick(comb, lane, e)
    acc_ref[...] += _swiglu_partial(xb_ref[...], wg_ref, wu_ref, wd_ref, row_scale)

    @pl.when(last)
    def _():
        o_ref[...] = _layer_norm(ALPHA * x_ref[...] + acc_ref[...], g_ref[...], b_ref[...])


def _ffn_ln(x, comb, w_gu, w_down, layer, g, b, tm, tf):
    m = x.shape[0]
    moe = comb is not None
    n_e = w_gu.shape[1]
    n_f = D_FF // tf
    in_specs = [pl.BlockSpec((tm, D_MODEL), lambda i, e, f: (i, 0))]
    args = [x]
    if moe:
        in_specs.append(pl.BlockSpec((tm, LANES), lambda i, e, f: (i, 0)))
        args.append(comb)
    in_specs += [pl.BlockSpec((None, None, D_MODEL, tf), lambda i, e, f: (layer, e, 0, f)),
                 pl.BlockSpec((None, None, D_MODEL, tf), lambda i, e, f: (layer, e, 0, n_f + f)),
                 pl.BlockSpec((None, None, tf, D_MODEL), lambda i, e, f: (layer, e, f, 0)),
                 pl.BlockSpec((1, D_MODEL), lambda i, e, f: (0, 0)),
                 pl.BlockSpec((1, D_MODEL), lambda i, e, f: (0, 0))]
    args += [w_gu, w_gu, w_down, g, b]
    return pl.pallas_call(
        functools.partial(_ffn_kernel, moe=moe),
        grid=(m // tm, n_e, n_f),
        in_specs=in_specs,
        out_specs=pl.BlockSpec((tm, D_MODEL), lambda i, e, f: (i, 0)),
        out_shape=jax.ShapeDtypeStruct((m, D_MODEL), F32),
        scratch_shapes=[pltpu.VMEM((tm, D_MODEL), BF16), pltpu.VMEM((tm, D_MODEL), F32)],
        compiler_params=_params("parallel", "arbitrary", "arbitrary"),
        name="moe_ln" if moe else "ffn_ln",
    )(*args)


def _route_top2(x, wh, wl):
    xb = x.astype(BF16)
    lg = jnp.dot(xb, wh, preferred_element_type=F32) + jnp.dot(xb, wl, preferred_element_type=F32)
    lane = lax.broadcasted_iota(jnp.int32, lg.shape, 1)
    neg = -jnp.inf
    lg = jnp.where(lane < N_EXP, lg, neg)
    m1 = jnp.max(lg, axis=-1, keepdims=True)
    i1 = jnp.min(jnp.where(lg == m1, lane, LANES), axis=-1, keepdims=True)
    lg2 = jnp.where(lane == i1, neg, lg)
    m2 = jnp.max(lg2, axis=-1, keepdims=True)
    i2 = jnp.min(jnp.where(lg2 == m2, lane, LANES), axis=-1, keepdims=True)
    e2 = jnp.exp(m2 - m1)
    den = 1.0 + e2
    g1 = 1.0 / den
    g2 = e2 / den
    comb = jnp.where(lane == i1, g1, 0.0) + jnp.where(lane == i2, g2, 0.0)
    gates = jnp.where(lane == 0, g1, jnp.where(lane == 1, g2, 0.0))
    idx = jnp.where(lane == 0, i1, jnp.where(lane == 1, i2, 0))
    return comb, gates, idx


MOE_TM = 512
FFN_TF = 1792
DISPATCH_TOKENS = 1024
COMBINE_TOKENS = 256


def _moe_plan(idx2, tm):
    t = idx2.shape[0]
    e_flat = idx2.reshape(-1)
    experts = jnp.arange(N_EXP, dtype=jnp.int32)
    onehot = (e_flat[:, None] == experts[None, :]).astype(jnp.int32)
    csum = jnp.cumsum(onehot, axis=0)
    rank = jnp.sum(csum * onehot, axis=1) - 1
    counts = csum[-1]
    ptiles = (counts + tm - 1) // tm
    tile_end = jnp.cumsum(ptiles)
    pstart = (tile_end - ptiles) * tm
    pos = jnp.sum(onehot * pstart[None, :], axis=1) + rank
    n_used = tile_end[-1]
    n_tiles = (2 * t) // tm + N_EXP
    jj = jnp.minimum(jnp.arange(n_tiles, dtype=jnp.int32), n_used - 1)
    tile_expert = jnp.sum((jj[:, None] >= tile_end[None, :]).astype(jnp.int32), axis=1)
    seg_base = jnp.concatenate([pstart + counts, (n_used * tm).reshape(1)])
    seg_len = jnp.concatenate([ptiles * tm - counts, ((n_tiles - n_used) * tm).reshape(1)])
    seg_end = jnp.cumsum(seg_len)
    j = jnp.arange(N_EXP * tm, dtype=jnp.int32)
    seg = jnp.sum((j[:, None] >= seg_end[None, :]).astype(jnp.int32), axis=1)
    seg_1h = (seg[:, None] == jnp.arange(N_EXP + 1, dtype=jnp.int32)[None, :]).astype(jnp.int32)
    pad_dst = j + jnp.sum(seg_1h * (seg_base - (seg_end - seg_len))[None, :], axis=1)
    return (pos.astype(jnp.int32), tile_expert.astype(jnp.int32), n_used.reshape(1).astype(jnp.int32),
            pad_dst.astype(jnp.int32), n_tiles)


ROW_TILE = (SUBLANES, LANES)
assert SUBLANES * LANES == D_MODEL


def _wait_rows(tiles_hbm_ref, n_rows, sem):
    rows = tiles_hbm_ref.at[pl.ds(0, n_rows)]
    pltpu.make_async_copy(rows, rows, sem).wait()


def _moe_dispatch_kernel(pos_ref, pad_dst_ref, x_ref, xs_ref, xr_ref, sem):
    tq = x_ref.shape[0]
    n_pad = pad_dst_ref.shape[1]
    xr_ref[...] = x_ref[...].reshape((tq,) + ROW_TILE)

    def send(grp, carry):
        for jr in range(SUBLANES):
            t = SUBLANES * grp + jr
            for k in range(2):
                pltpu.make_async_copy(xr_ref.at[t], xs_ref.at[pos_ref[0, 2 * t + k]], sem).start()
        return carry

    lax.fori_loop(0, tq // SUBLANES, send, 0)

    def pad(grp, carry):
        for jr in range(SUBLANES):
            dst = pad_dst_ref[0, SUBLANES * grp + jr]
            pltpu.make_async_copy(xr_ref.at[jr], xs_ref.at[dst], sem).start()
        return carry

    lax.fori_loop(0, n_pad // SUBLANES, pad, 0)
    _wait_rows(xs_ref, 2 * tq + n_pad, sem)


def _moe_dispatch(x, pos, pad_dst, n_rows_out):
    m = x.shape[0]
    tq = DISPATCH_TOKENS
    n_steps = m // tq
    n_pad = pad_dst.shape[0] // n_steps
    smem = functools.partial(pl.BlockSpec, memory_space=pltpu.SMEM)
    return pl.pallas_call(
        _moe_dispatch_kernel,
        grid=(n_steps,),
        in_specs=[smem((None, 1, 2 * tq), lambda i: (i, 0, 0)),
                  smem((None, 1, n_pad), lambda i: (i, 0, 0)),
                  pl.BlockSpec((tq, D_MODEL), lambda i: (i, 0))],
        out_specs=pl.BlockSpec(memory_space=pl.ANY),
        out_shape=jax.ShapeDtypeStruct((n_rows_out,) + ROW_TILE, F32),
        scratch_shapes=[pltpu.VMEM((tq,) + ROW_TILE, F32), pltpu.SemaphoreType.DMA(())],
        compiler_params=_params("arbitrary"),
        name="moe_dispatch",
    )(pos.reshape(n_steps, 1, 2 * tq), pad_dst.reshape(n_steps, 1, n_pad), x)


def _moe_ffn_kernel(te_ref, nu_ref, x_ref, wg_ref, wu_ref, wd_ref, o_ref, xb_ref, acc_ref):
    p = pl.program_id(0)
    f = pl.program_id(1)
    tm = xb_ref.shape[0]

    @pl.when(p < nu_ref[0])
    def _():
        @pl.when(f == 0)
        def _():
            xb_ref[...] = x_ref[...].reshape(tm, D_MODEL).astype(BF16)

        y = _swiglu_partial(xb_ref[...], wg_ref, wu_ref, wd_ref)
        last = pl.num_programs(1) - 1

        @pl.when(f == 0)
        def _():
            acc_ref[...] = y

        @pl.when(jnp.logical_and(f != 0, f != last))
        def _():
            acc_ref[...] += y

        @pl.when(f == last)
        def _():
            o_ref[...] = (acc_ref[...] + y).reshape((tm,) + ROW_TILE)

    @pl.when(jnp.logical_and(p >= nu_ref[0], f == 0))
    def _():
        o_ref[...] = jnp.zeros_like(o_ref)


def _moe_ffn(xs, tile_expert, n_used, w_gu, w_down, layer, tm, tf):
    n_tiles = xs.shape[0] // tm
    n_f = D_FF // tf
    assert n_f >= 2
    tile = lambda p, nu: jnp.minimum(p, nu[0] - 1)
    col = lambda p, f, nu: jnp.where(p < nu[0], f, n_f - 1)
    grid_spec = pltpu.PrefetchScalarGridSpec(
        num_scalar_prefetch=2,
        grid=(n_tiles, n_f),
        in_specs=[pl.BlockSpec((tm,) + ROW_TILE, lambda p, f, te, nu: (tile(p, nu), 0, 0)),
                  pl.BlockSpec((None, None, D_MODEL, tf),
                               lambda p, f, te, nu: (layer, te[p], 0, col(p, f, nu))),
                  pl.BlockSpec((None, None, D_MODEL, tf),
                               lambda p, f, te, nu: (layer, te[p], 0, n_f + col(p, f, nu))),
                  pl.BlockSpec((None, None, tf, D_MODEL),
                               lambda p, f, te, nu: (layer, te[p], col(p, f, nu), 0))],
        out_specs=pl.BlockSpec((tm,) + ROW_TILE, lambda p, f, te, nu: (p, 0, 0)),
        scratch_shapes=[pltpu.VMEM((tm, D_MODEL), BF16), pltpu.VMEM((tm, D_MODEL), F32)],
    )
    return pl.pallas_call(
        _moe_ffn_kernel,
        grid_spec=grid_spec,
        out_shape=jax.ShapeDtypeStruct(xs.shape, F32),
        compiler_params=_params("arbitrary", "arbitrary"),
        name="moe_ffn",
    )(tile_expert, n_used, xs, w_gu, w_gu, w_down)


def _moe_combine_ln_kernel(pos_ref, pos_next_ref, gate_ref, x_ref, ye_ref, g_ref, b_ref, o_ref,
                           buf_ref, sem):
    i = pl.program_id(0)
    tq = x_ref.shape[0]
    slot = lax.rem(i, 2)

    def start_fetch(rows_ref, to_slot):
        def fetch(grp, carry):
            for jr in range(SUBLANES):
                t = SUBLANES * grp + jr
                for k in range(2):
                    pltpu.make_async_copy(ye_ref.at[rows_ref[0, 2 * t + k]], buf_ref.at[to_slot, k * tq + t],
                                          sem.at[to_slot]).start()
            return carry

        lax.fori_loop(0, tq // SUBLANES, fetch, 0)

    @pl.when(i == 0)
    def _():
        start_fetch(pos_ref, 0)

    @pl.when(i + 1 < pl.num_programs(0))
    def _():
        start_fetch(pos_next_ref, 1 - slot)

    _wait_rows(ye_ref, 2 * tq, sem.at[slot])
    gates = gate_ref[...]
    lane = lax.broadcasted_iota(jnp.int32, gates.shape, 1)
    y0 = buf_ref[slot, 0:tq].reshape(tq, D_MODEL)
    y1 = buf_ref[slot, tq:2 * tq].reshape(tq, D_MODEL)
    y = _lane_pick(gates, lane, 0) * y0 + _lane_pick(gates, lane, 1) * y1
    o_ref[...] = _layer_norm(ALPHA * x_ref[...] + y, g_ref[...], b_ref[...])


def _moe_combine_ln(x, gates, pos, ye, g, b):
    m = x.shape[0]
    tq = COMBINE_TOKENS
    n_steps = m // tq
    pos3 = pos.reshape(n_steps, 1, 2 * tq)
    smem = functools.partial(pl.BlockSpec, memory_space=pltpu.SMEM)
    return pl.pallas_call(
        _moe_combine_ln_kernel,
        grid=(n_steps,),
        in_specs=[smem((None, 1, 2 * tq), lambda i: (i, 0, 0)),
                  smem((None, 1, 2 * tq), lambda i: (jnp.minimum(i + 1, n_steps - 1), 0, 0)),
                  pl.BlockSpec((tq, LANES), lambda i: (i, 0)),
                  pl.BlockSpec((tq, D_MODEL), lambda i: (i, 0)),
                  pl.BlockSpec(memory_space=pl.ANY),
                  pl.BlockSpec((1, D_MODEL), lambda i: (0, 0)),
                  pl.BlockSpec((1, D_MODEL), lambda i: (0, 0))],
        out_specs=pl.BlockSpec((tq, D_MODEL), lambda i: (i, 0)),
        out_shape=jax.ShapeDtypeStruct((m, D_MODEL), F32),
        scratch_shapes=[pltpu.VMEM((2, 2 * tq) + ROW_TILE, F32), pltpu.SemaphoreType.DMA((2,))],
        compiler_params=_params("arbitrary"),
        name="moe_combine_ln",
    )(pos3, pos3, gates, x, ye, g, b)


def _moe_routed_ln(x, gates, idx, w_gu, w_down, layer, g, b):
    pos, tile_expert, n_used, pad_dst, n_tiles = _moe_plan(idx[:, :2], MOE_TM)
    xs = _moe_dispatch(x, pos, pad_dst, n_tiles * MOE_TM)
    ye = _moe_ffn(xs, tile_expert, n_used, w_gu, w_down, layer, MOE_TM, FFN_TF)
    return _moe_combine_ln(x, gates, pos, ye, g, b)


GDN_PASSES_QK = 1
GDN_PASSES_SOLVE = 1
GDN_PASSES_STATE = 1
INV_BLOCK_SHIFT = 4
GDN_SEQS = 4


def _gdn_chunk_kernel(qkv_ref, z_ref, ab_ref, cw_ref, alog_ref, dtb_ref, nw_ref, hist0_ref, s0_ref,
                      mix_ref, sout_ref, hout_ref, xh_ref, s_ref):
    n = pl.program_id(1)
    c = GDN_CHUNK
    hrows = SUBLANES
    n_seq = qkv_ref.shape[0]
    items = [(b, h) for b in range(n_seq) for h in range(H_A)]

    @pl.when(n == 0)
    def _():
        xh_ref[:, 0:hrows, :] = hist0_ref[...]
        s_ref[...] = s0_ref[...]

    xh_ref[:, hrows:hrows + c, :] = qkv_ref[...]

    row = lax.broadcasted_iota(jnp.int32, (c, c), 0)
    col = lax.broadcasted_iota(jnp.int32, (c, c), 1)
    causal = row >= col
    strict = row > col
    eye = jnp.where(row == col, 1.0, 0.0)
    blockdiag = (row >> INV_BLOCK_SHIFT) == (col >> INV_BLOCK_SHIFT)
    tril_ones = jnp.where(causal, 1.0, 0.0).astype(BF16)
    lane = lax.broadcasted_iota(jnp.int32, (c, LANES), 1)
    mm_s = functools.partial(_dot, passes=GDN_PASSES_SOLVE)

    beta_all, gcum_all, gcum_t = [], [], []
    for b in range(n_seq):
        ab = ab_ref[b]
        g_all = -jnp.exp(alog_ref[...]) * _softplus(ab + dtb_ref[...])
        beta_all.append(_sigmoid(ab))
        gc = _dot_exact_lhs(tril_ones, g_all)
        gcum_all.append(gc)
        gcum_t.append(gc.T)

    def conv_silu(b, c0):
        acc = xh_ref[b, hrows - 3:hrows - 3 + c, c0:c0 + LANES] * cw_ref[0:1, c0:c0 + LANES]
        for j in range(1, CONV_W):
            acc = acc + (xh_ref[b, hrows - 3 + j:hrows - 3 + j + c, c0:c0 + LANES]
                         * cw_ref[j:j + 1, c0:c0 + LANES])
        return _silu(acc)

    q, k, v, kb, egc, gc_col, gc_last, beta_col, decay = {}, {}, {}, {}, {}, {}, {}, {}, {}
    for it in items:
        b, h = it
        qi = conv_silu(b, h * DK_A)
        ki = conv_silu(b, QK_A + h * DK_A)
        v[it] = conv_silu(b, 2 * QK_A + h * DV_A)
        q[it] = qi * lax.rsqrt(jnp.sum(qi * qi, axis=-1, keepdims=True) + RMS_EPS) * (DK_A ** -0.5)
        k[it] = ki * lax.rsqrt(jnp.sum(ki * ki, axis=-1, keepdims=True) + RMS_EPS)
        gc_col[it] = _lane_pick(gcum_all[b], lane, h)
        beta_col[it] = _lane_pick(beta_all[b], lane, B_LANE + h)
        gc_last[it] = gc_col[it][c - 1:c, :]
        decay[it] = jnp.where(causal, jnp.exp(gc_col[it] - gcum_t[b][h:h + 1, :c]), 0.0)
        egc[it] = jnp.exp(gc_col[it])
        kb[it] = k[it] * beta_col[it]

    kk = {it: _dot(jnp.concatenate([kb[it], q[it]], axis=0), k[it], _NT, GDN_PASSES_QK) for it in items}
    a_mat = {it: jnp.where(strict, kk[it][:c] * decay[it], 0.0) for it in items}
    attn = {it: kk[it][c:] * decay[it] for it in items}

    a_d = {it: jnp.where(blockdiag, a_mat[it], 0.0) for it in items}
    a_n = {it: jnp.where(blockdiag, 0.0, a_mat[it]) for it in items}
    p2 = {it: mm_s(a_d[it], a_d[it]) for it in items}
    e_d = {it: p2[it] - a_d[it] - mm_s(a_d[it], p2[it]) for it in items}
    p4 = {it: mm_s(p2[it], p2[it]) for it in items}
    e_d = {it: e_d[it] + p4[it] + mm_s(e_d[it], p4[it]) for it in items}
    p8 = {it: mm_s(p4[it], p4[it]) for it in items}
    e_d = {it: e_d[it] + p8[it] + mm_s(e_d[it], p8[it]) for it in items}
    m_blk = {it: a_n[it] + mm_s(e_d[it], a_n[it]) for it in items}
    rhs = {it: jnp.concatenate([v[it] * beta_col[it], kb[it] * egc[it]], axis=1) for it in items}
    rhs = {it: rhs[it] + mm_s(e_d[it], rhs[it]) for it in items}
    m2 = {it: mm_s(m_blk[it], m_blk[it]) for it in items}
    f_y = {it: m2[it] - m_blk[it] - mm_s(m_blk[it], m2[it]) for it in items}
    sol = {it: rhs[it] + mm_s(f_y[it], rhs[it]) for it in items}

    s_old = {it: s_ref[it[0], it[1]] for it in items}
    wq = {it: _dot(jnp.concatenate([sol[it][:, DV_A:], q[it] * egc[it]], axis=0), s_old[it],
                   _NN, GDN_PASSES_STATE) for it in items}
    v_new = {it: sol[it][:, :DV_A] - wq[it][:c] for it in items}
    o = {it: wq[it][c:] + _dot(attn[it], v_new[it], _NN, GDN_PASSES_STATE) for it in items}
    for it in items:
        k_dec = k[it] * jnp.exp(gc_last[it] - gc_col[it])
        s_ref[it[0], it[1]] = (s_old[it] * jnp.exp(gc_last[it])
                               + _dot(k_dec, v_new[it], _TN, GDN_PASSES_STATE))
    for it in items:
        b, h = it
        oi = o[it]
        oi = oi * lax.rsqrt(jnp.mean(oi * oi, axis=-1, keepdims=True) + RMS_EPS) * nw_ref[...]
        mix_ref[b, :, h * DV_A:(h + 1) * DV_A] = oi * _silu(z_ref[b, :, h * DV_A:(h + 1) * DV_A])

    xh_ref[:, 0:hrows, :] = xh_ref[:, c:c + hrows, :]

    @pl.when(n == pl.num_programs(1) - 1)
    def _():
        sout_ref[...] = s_ref[...]
        hout_ref[...] = xh_ref[:, c:c + hrows, :]


def _gdn_chunked(proj, conv_w, alog_row, dtb_row, nw_row, hist0, s0):
    bsz, t, _ = proj.shape
    c = GDN_CHUNK
    nb = GDN_SEQS
    n_chunks = t // c
    row_spec = lambda width, blk: pl.BlockSpec((nb, c, width), lambda b, n: (b, n, blk))
    full2 = lambda shape: pl.BlockSpec(shape, lambda b, n: (0, 0))
    return pl.pallas_call(
        _gdn_chunk_kernel,
        grid=(bsz // nb, n_chunks),
        in_specs=[row_spec(CONV_DIM, 0),
                  row_spec(V_A, A_Z_OFF // V_A),
                  row_spec(LANES, A_AB_OFF // LANES),
                  full2((CONV_W, CONV_DIM)),
                  full2((1, LANES)), full2((1, LANES)), full2((1, DV_A)),
                  pl.BlockSpec((nb, SUBLANES, CONV_DIM), lambda b, n: (b, 0, 0)),
                  pl.BlockSpec((nb, H_A, DK_A, DV_A), lambda b, n: (b, 0, 0, 0))],
        out_specs=[pl.BlockSpec((nb, c, V_A), lambda b, n: (b, n, 0)),
                   pl.BlockSpec((nb, H_A, DK_A, DV_A), lambda b, n: (b, 0, 0, 0)),
                   pl.BlockSpec((nb, SUBLANES, CONV_DIM), lambda b, n: (b, 0, 0))],
        out_shape=[jax.ShapeDtypeStruct((bsz, t, V_A), F32),
                   jax.ShapeDtypeStruct((bsz, H_A, DK_A, DV_A), F32),
                   jax.ShapeDtypeStruct((bsz, SUBLANES, CONV_DIM), F32)],
        scratch_shapes=[pltpu.VMEM((nb, SUBLANES + c, CONV_DIM), F32),
                        pltpu.VMEM((nb, H_A, DK_A, DV_A), F32)],
        compiler_params=_params("parallel", "arbitrary"),
        name="gdn_chunked",
    )(proj, proj, proj, conv_w, alog_row, dtb_row, nw_row, hist0, s0)


def _gdn_step_kernel(qkv_ref, z_ref, ab_ref, cw_ref, alog_ref, dtb_ref, nw_ref, hist_ref, s_ref,
                     mix_ref, sout_ref, hout_ref, o_scr):
    rows = qkv_ref.shape[0]
    new = qkv_ref[...]
    y = hist_ref[0] * cw_ref[0:1, :]
    for j in range(1, CONV_W - 1):
        y = y + hist_ref[j] * cw_ref[j:j + 1, :]
    y = y + new * cw_ref[CONV_W - 1:CONV_W, :]
    qkv = _silu(y)
    for j in range(CONV_W - 2):
        hout_ref[j] = hist_ref[j + 1]
    hout_ref[CONV_W - 2] = new

    ab = ab_ref[...]
    lane = lax.broadcasted_iota(jnp.int32, ab.shape, 1)
    g_all = -jnp.exp(alog_ref[...]) * _softplus(ab + dtb_ref[...])
    beta_all = _sigmoid(ab)
    ri = lax.broadcasted_iota(jnp.int32, (DK_A, DK_A), 0)
    ci = lax.broadcasted_iota(jnp.int32, (DK_A, DK_A), 1)
    diag = ri == ci

    def as_column(r):
        return jnp.sum(jnp.where(diag, r, 0.0), axis=-1, keepdims=True)

    for h in range(H_A):
        q = qkv[:, h * DK_A:(h + 1) * DK_A]
        k = qkv[:, QK_A + h * DK_A:QK_A + (h + 1) * DK_A]
        v = qkv[:, 2 * QK_A + h * DV_A:2 * QK_A + (h + 1) * DV_A]
        q = q * lax.rsqrt(jnp.sum(q * q, axis=-1, keepdims=True) + RMS_EPS) * (DK_A ** -0.5)
        k = k * lax.rsqrt(jnp.sum(k * k, axis=-1, keepdims=True) + RMS_EPS)
        eg = jnp.exp(_lane_pick(g_all, lane, h))
        beta = _lane_pick(beta_all, lane, B_LANE + h)
        for s in range(rows):
            k_col = as_column(k[s:s + 1, :])
            q_col = as_column(q[s:s + 1, :])
            st = s_ref[s, h] * eg[s:s + 1, :]
            ks = jnp.sum(k_col * st, axis=0, keepdims=True)
            u = beta[s:s + 1, :] * (v[s:s + 1, :] - ks)
            st = st + k_col * u
            sout_ref[s, h] = st
            o_scr[s:s + 1, h * DV_A:(h + 1) * DV_A] = jnp.sum(q_col * st, axis=0, keepdims=True)

    for h in range(H_A):
        o = o_scr[:, h * DV_A:(h + 1) * DV_A]
        o = o * lax.rsqrt(jnp.mean(o * o, axis=-1, keepdims=True) + RMS_EPS) * nw_ref[...]
        mix_ref[:, h * DV_A:(h + 1) * DV_A] = o * _silu(z_ref[:, h * DV_A:(h + 1) * DV_A])


def _gdn_step(proj, conv_w, alog_row, dtb_row, nw_row, hist_t, s0, layer):
    bsz = proj.shape[0]
    rows = SUBLANES
    full1 = lambda shape: pl.BlockSpec(shape, lambda i: (0, 0))
    return pl.pallas_call(
        _gdn_step_kernel,
        grid=(bsz // rows,),
        in_specs=[pl.BlockSpec((rows, CONV_DIM), lambda i: (i, 0)),
                  pl.BlockSpec((rows, V_A), lambda i: (i, A_Z_OFF // V_A)),
                  pl.BlockSpec((rows, LANES), lambda i: (i, A_AB_OFF // LANES)),
                  full1((CONV_W, CONV_DIM)),
                  full1((1, LANES)), full1((1, LANES)), full1((1, DV_A)),
                  pl.BlockSpec((None, CONV_W - 1, rows, CONV_DIM), lambda i: (layer, 0, i, 0)),
                  pl.BlockSpec((None, rows, H_A, DK_A, DV_A), lambda i: (layer, i, 0, 0, 0))],
        out_specs=[pl.BlockSpec((rows, V_A), lambda i: (i, 0)),
                   pl.BlockSpec((rows, H_A, DK_A, DV_A), lambda i: (i, 0, 0, 0)),
                   pl.BlockSpec((CONV_W - 1, rows, CONV_DIM), lambda i: (0, i, 0))],
        out_shape=[jax.ShapeDtypeStruct((bsz, V_A), F32),
                   jax.ShapeDtypeStruct((bsz, H_A, DK_A, DV_A), F32),
                   jax.ShapeDtypeStruct((CONV_W - 1, bsz, CONV_DIM), F32)],
        scratch_shapes=[pltpu.VMEM((rows, V_A), F32)],
        compiler_params=_params("parallel"),
        name="gdn_step",
    )(proj, proj, proj, conv_w, alog_row, dtb_row, nw_row, hist_t, s0)


SGU_CHUNKS = 4


def _sgu_chunk_kernel(u_ref, v_ref, g_ref, b_ref, ws_ref, bs_ref, mix_ref):
    cb = CHUNK_B
    row = lax.broadcasted_iota(jnp.int32, (cb, cb), 0)
    col = lax.broadcasted_iota(jnp.int32, (cb, cb), 1)
    ws = [jnp.where(row >= col, ws_ref[gi], 0.0).astype(BF16) for gi in range(G_B)]
    for r0 in range(0, u_ref.shape[0], cb):
        u = _gelu_tanh(u_ref[r0:r0 + cb, :])
        v = _layer_norm(_gelu_tanh(v_ref[r0:r0 + cb, :]), g_ref[...], b_ref[...])
        for gi in range(G_B):
            mixed = _dot(ws[gi], v[:, gi * CH_B:(gi + 1) * CH_B]) + bs_ref[gi]
            mix_ref[r0:r0 + cb, gi * CH_B:(gi + 1) * CH_B] = u[:, gi * CH_B:(gi + 1) * CH_B] * mixed


def _sgu_chunked(proj, ln_g, ln_b, w_s, bs_b):
    bsz, t, _ = proj.shape
    c = CHUNK_B * SGU_CHUNKS
    return pl.pallas_call(
        _sgu_chunk_kernel,
        grid=(bsz, t // c),
        in_specs=[pl.BlockSpec((None, c, D_B), lambda b, n: (b, n, 0)),
                  pl.BlockSpec((None, c, D_B), lambda b, n: (b, n, 1)),
                  pl.BlockSpec((1, D_B), lambda b, n: (0, 0)),
                  pl.BlockSpec((1, D_B), lambda b, n: (0, 0)),
                  pl.BlockSpec((G_B, CHUNK_B, CHUNK_B), lambda b, n: (0, 0, 0)),
                  pl.BlockSpec((G_B, CHUNK_B, CH_B), lambda b, n: (0, 0, 0))],
        out_specs=pl.BlockSpec((None, c, D_B), lambda b, n: (b, n, 0)),
        out_shape=jax.ShapeDtypeStruct((bsz, t, D_B), F32),
        compiler_params=_params("parallel", "parallel"),
        name="sgu_chunked",
    )(proj, proj, ln_g, ln_b, w_s, bs_b)


def _sgu_first_kernel(u_ref, v_ref, g_ref, b_ref, w00_ref, b0_ref, mix_ref, vout_ref):
    u = _gelu_tanh(u_ref[...])
    v = _layer_norm(_gelu_tanh(v_ref[...]), g_ref[...], b_ref[...])
    vout_ref[...] = v
    mix_ref[...] = u * (w00_ref[...] * v + b0_ref[...])


def _sgu_first(proj, ln_g, ln_b, w00_row, b0_row):
    bsz = proj.shape[0]
    row = lambda: pl.BlockSpec((1, D_B), lambda i: (0, 0))
    return pl.pallas_call(
        _sgu_first_kernel,
        grid=(1,),
        in_specs=[pl.BlockSpec((bsz, D_B), lambda i: (0, 0)),
                  pl.BlockSpec((bsz, D_B), lambda i: (0, 1)),
                  row(), row(), row(), row()],
        out_specs=[pl.BlockSpec((bsz, D_B), lambda i: (0, 0)),
                   pl.BlockSpec((bsz, D_B), lambda i: (0, 0))],
        out_shape=[jax.ShapeDtypeStruct((bsz, D_B), F32),
                   jax.ShapeDtypeStruct((bsz, D_B), F32)],
        compiler_params=_params("arbitrary"),
        name="sgu_first",
    )(proj, proj, ln_g, ln_b, w00_row, b0_row)


def _head_masks(shape):
    lane = lax.broadcasted_iota(jnp.int32, shape, len(shape) - 1)
    return [(lane // DH_X) == h for h in range(H_X)]


def _xattn_kernel(q_ref, k_ref, v_ref, o_ref):
    q = q_ref[...]
    kb = k_ref[...].astype(BF16)
    vb = v_ref[...].astype(BF16)
    masks = _head_masks(q.shape)
    heads = range(H_X)
    s = [_dot(jnp.where(masks[h], q, 0.0), kb, _NT) * (DH_X ** -0.5) for h in heads]
    p = [jnp.exp(s[h] - jnp.max(s[h], axis=-1, keepdims=True)) for h in heads]
    p = [p[h] / jnp.sum(p[h], axis=-1, keepdims=True) for h in heads]
    pv = [_dot(p[h], vb) for h in heads]
    out = jnp.where(masks[0], pv[0], 0.0)
    for h in range(1, H_X):
        out = jnp.where(masks[h], pv[h], out)
    o_ref[...] = out


def _xattn(proj, xq_blk, mem_k, mem_v, layer, tq):
    bsz, t, _ = proj.shape
    return pl.pallas_call(
        _xattn_kernel,
        grid=(bsz, t // tq),
        in_specs=[pl.BlockSpec((None, tq, D_X), lambda b, i: (b, i, xq_blk)),
                  pl.BlockSpec((None, None, N_MEM, D_X), lambda b, i: (layer, b, 0, 0)),
                  pl.BlockSpec((None, None, N_MEM, D_X), lambda b, i: (layer, b, 0, 0))],
        out_specs=pl.BlockSpec((None, tq, D_X), lambda b, i: (b, i, 0)),
        out_shape=jax.ShapeDtypeStruct((bsz, t, D_X), F32),
        compiler_params=_params("parallel", "parallel"),
        name="xattn",
    )(proj, mem_k, mem_v)


def _xattn_step_kernel(q_ref, k_ref, v_ref, o_ref):
    rows = q_ref.shape[0]
    q = q_ref[...]
    hrow = lax.broadcasted_iota(jnp.int32, (SUBLANES, D_X), 0)
    hlane = lax.broadcasted_iota(jnp.int32, (SUBLANES, D_X), 1) // DH_X
    sel = hrow == hlane
    seqs = range(rows)
    sc = [_dot(jnp.where(sel, q[s:s + 1, :], 0.0), k_ref[s], _NT) * (DH_X ** -0.5) for s in seqs]
    p = [jnp.exp(sc[s] - jnp.max(sc[s], axis=-1, keepdims=True)) for s in seqs]
    p = [p[s] / jnp.sum(p[s], axis=-1, keepdims=True) for s in seqs]
    pv = [_dot(p[s], v_ref[s]) for s in seqs]
    for s in seqs:
        o_ref[s:s + 1, :] = jnp.sum(jnp.where(sel, pv[s], 0.0), axis=0, keepdims=True)


def _xattn_step(proj, xq_blk, mem_k, mem_v, layer):
    bsz = proj.shape[0]
    rows = SUBLANES
    return pl.pallas_call(
        _xattn_step_kernel,
        grid=(bsz // rows,),
        in_specs=[pl.BlockSpec((rows, D_X), lambda i: (i, xq_blk)),
                  pl.BlockSpec((None, rows, N_MEM, D_X), lambda i: (layer, i, 0, 0)),
                  pl.BlockSpec((None, rows, N_MEM, D_X), lambda i: (layer, i, 0, 0))],
        out_specs=pl.BlockSpec((rows, D_X), lambda i: (i, 0)),
        out_shape=jax.ShapeDtypeStruct((bsz, D_X), F32),
        compiler_params=_params("parallel"),
        name="xattn_step",
    )(proj, mem_k, mem_v)


def _pad_lanes(v):
    return jnp.zeros((1, LANES), F32).at[0, :v.shape[0]].set(v.astype(F32))


def _prep_weights(a_w_in, a_A_log, a_dt_bias, b_w_s, b_b_s, moe_router):
    qkvz = a_w_in[:, :, :A_XQ_OFF]
    a_cols = a_w_in[:, :, A_XQ_OFF:A_XQ_OFF + H_A]
    b_cols = a_w_in[:, :, A_XQ_OFF + H_A:A_XQ_OFF + 2 * H_A]
    xq = a_w_in[:, :, A_XQ_OFF + 2 * H_A:]
    ab = jnp.zeros(a_w_in.shape[:2] + (LANES,), F32)
    ab = ab.at[:, :, :H_A].set(a_cols).at[:, :, B_LANE:B_LANE + H_A].set(b_cols)
    a_w = jnp.concatenate([qkvz, xq, ab], axis=-1).astype(BF16)
    alog_rows = [_pad_lanes(a_A_log[j]) for j in range(a_A_log.shape[0])]
    dtb_rows = [_pad_lanes(a_dt_bias[j]) for j in range(a_dt_bias.shape[0])]
    bs_b = jnp.broadcast_to(b_b_s[..., None], b_b_s.shape + (CH_B,)).astype(F32)
    w00_rows = jnp.repeat(b_w_s[:, :, 0, 0], CH_B, axis=-1)[:, None, :]
    b0_rows = jnp.repeat(b_b_s[:, :, 0], CH_B, axis=-1)[:, None, :]
    r = jnp.zeros(moe_router.shape[:2] + (LANES,), F32).at[:, :, :N_EXP].set(moe_router)
    r_hi = r.astype(BF16)
    r_lo = (r - r_hi.astype(F32)).astype(BF16)
    return a_w, alog_rows, dtb_rows, bs_b, w00_rows, b0_rows, r_hi, r_lo


def kernel(x_prompt, x_sample, state_gdn, state_conv, cache_mem_k, cache_mem_v, mem_prompt,
           a_w_in, a_conv_w, a_A_log, a_dt_bias, a_norm_w,
           b_w_in, b_ln_g, b_ln_b, b_w_s, b_b_s,
           w_mem_kv, w_out, ln1_g, ln1_b, ln2_g, ln2_b,
           ffn_w_gu, ffn_w_down, moe_router, moe_w_gu, moe_w_down):
    n_p, t_p, _ = x_prompt.shape
    n_s = x_sample.shape[0]
    n_a = a_w_in.shape[0]

    a_w, alog_rows, dtb_rows, bs_b, w00_rows, b0_rows, r_hi, r_lo = _prep_weights(
        a_w_in, a_A_log, a_dt_bias, b_w_s, b_b_s, moe_router)
    ffn_gu = ffn_w_gu.astype(BF16)[:, None]
    ffn_dn = ffn_w_down.astype(BF16)[:, None]
    moe_gu = moe_w_gu.astype(BF16)
    moe_dn = moe_w_down.astype(BF16)
    row = lambda v: v.reshape(1, -1).astype(F32)

    w_kv = jnp.transpose(w_mem_kv, (1, 0, 2)).reshape(D_MODEL, DEPTH * 2 * D_X).astype(BF16)
    kv = _matmul(mem_prompt.reshape(n_p * N_MEM, D_MODEL), w_kv, 512, 1024, "mem_kv")
    kv = kv.reshape(n_p, N_MEM, DEPTH, 2, D_X)
    p_mem_k = jnp.transpose(kv[:, :, :, 0, :], (2, 0, 1, 3))
    p_mem_v = jnp.transpose(kv[:, :, :, 1, :], (2, 0, 1, 3))
    s_mem_k = cache_mem_k.reshape(DEPTH, n_s, N_MEM, D_X)
    s_mem_v = cache_mem_v.reshape(DEPTH, n_s, N_MEM, D_X)

    xp = x_prompt.reshape(n_p * t_p, D_MODEL)
    xs = x_sample.reshape(n_s, D_MODEL)
    p_hist0 = jnp.zeros((n_p, SUBLANES, CONV_DIM), F32)
    p_s0 = jnp.zeros((n_p, H_A, DK_A, DV_A), F32)
    s_hist_t = jnp.transpose(state_conv, (0, 2, 1, 3))

    p_gdn, p_conv, s_gdn, s_conv, s_sgu_v = [], [], [], [], []
    tm_p, tm_s = 1024, n_s
    for i in range(DEPTH):
        j = i // 2
        if i % 2 == 0:
            proj_p = _matmul(xp, a_w[j], 512, 1152, "in_proj_a")
            proj_s = _matmul(xs, a_w[j], tm_s, 1152, "in_proj_a_s")
            nw = row(a_norm_w[j])
            mix_p, sp, hp = _gdn_chunked(proj_p.reshape(n_p, t_p, A_COLS), a_conv_w[j],
                                         alog_rows[j], dtb_rows[j], nw, p_hist0, p_s0)
            mix_p = mix_p.reshape(n_p * t_p, V_A)
            p_gdn.append(sp)
            p_conv.append(hp[:, SUBLANES - (CONV_W - 1):, :])
            mix_s, ss, hs = _gdn_step(proj_s, a_conv_w[j], alog_rows[j], dtb_rows[j], nw,
                                      s_hist_t, state_gdn, j)
            s_gdn.append(ss)
            s_conv.append(jnp.transpose(hs, (1, 0, 2)))
            cols, xq_blk = A_COLS, A_XQ_OFF // D_X
        else:
            b_w = b_w_in[j].astype(BF16)
            proj_p = _matmul(xp, b_w, 512, 896, "in_proj_b")
            proj_s = _matmul(xs, b_w, tm_s, 896, "in_proj_b_s")
            mix_p = _sgu_chunked(proj_p.reshape(n_p, t_p, B_COLS), row(b_ln_g[j]), row(b_ln_b[j]),
                                 b_w_s[j], bs_b[j]).reshape(n_p * t_p, D_B)
            mix_s, v_s = _sgu_first(proj_s, row(b_ln_g[j]), row(b_ln_b[j]), w00_rows[j], b0_rows[j])
            s_sgu_v.append(v_s.reshape(n_s, 1, D_B))
            cols, xq_blk = B_COLS, B_XQ_OFF // D_X
        xo_p = _xattn(proj_p.reshape(n_p, t_p, cols), xq_blk, p_mem_k, p_mem_v, i, 512)
        xo_s = _xattn_step(proj_s, xq_blk, s_mem_k, s_mem_v, i)
        g1, b1, g2, b2 = row(ln1_g[i]), row(ln1_b[i]), row(ln2_g[i]), row(ln2_b[i])
        w_o = w_out[i].astype(BF16)
        xo_p = xo_p.reshape(n_p * t_p, D_X)
        if i % 2 == 0:
            xp = _outproj_ln(mix_p, xo_p, xp, w_o, g1, b1, 512)
            xs = _outproj_ln(mix_s, xo_s, xs, w_o, g1, b1, tm_s)
            xp = _ffn_ln(xp, None, ffn_gu, ffn_dn, j, g2, b2, tm_p, FFN_TF)
            xs = _ffn_ln(xs, None, ffn_gu, ffn_dn, j, g2, b2, tm_s, FFN_TF)
        else:
            router = (r_hi[j], r_lo[j])
            xp, _, gates_p, idx_p = _outproj_ln(mix_p, xo_p, xp, w_o, g1, b1, 512, router)
            xs, comb_s, _, _ = _outproj_ln(mix_s, xo_s, xs, w_o, g1, b1, tm_s, router)
            xp = _moe_routed_ln(xp, gates_p, idx_p, moe_gu, moe_dn, j, g2, b2)
            xs = _ffn_ln(xs, comb_s, moe_gu, moe_dn, j, g2, b2, tm_s, FFN_TF)

    mem_shape = (DEPTH, n_p, N_MEM, H_X, DH_X)
    return (xp.reshape(n_p, t_p, D_MODEL),
            xs.reshape(n_s, 1, D_MODEL),
            jnp.stack(p_gdn),
            jnp.stack(p_conv),
            p_mem_k.reshape(mem_shape),
            p_mem_v.reshape(mem_shape),
            jnp.stack(s_gdn),
            jnp.stack(s_conv),
            jnp.stack(s_sgu_v))
```

```python
import functools

import jax
import jax.numpy as jnp
from jax import lax
from jax.experimental import pallas as pl
from jax.experimental.pallas import tpu as pltpu

F32 = jnp.float32
BF16 = jnp.bfloat16

D_MODEL = 1024
DEPTH = 4
H_A = 6
DK_A = 128
DV_A = 128
QK_A = H_A * DK_A
V_A = H_A * DV_A
CONV_W = 4
CONV_DIM = 2 * QK_A + V_A
GDN_CHUNK = 64
G_B = 6
CH_B = 128
D_B = G_B * CH_B
CHUNK_B = 128
N_MEM = 256
H_X = 4
DH_X = 64
D_X = H_X * DH_X
D_FF = 3584
N_EXP = 8
ALPHA = (2 * DEPTH) ** 0.25
LN_EPS = 1e-5
RMS_EPS = 1e-6

LANES = 128
SUBLANES = 8
VMEM_LIMIT_BYTES = 56 * 1024 * 1024

A_Z_OFF = CONV_DIM
A_XQ_OFF = CONV_DIM + V_A
A_AB_OFF = A_XQ_OFF + D_X
A_COLS = A_AB_OFF + LANES
B_LANE = 8
B_XQ_OFF = 2 * D_B
B_COLS = 2 * D_B + D_X


def _params(*sem):
    return pltpu.CompilerParams(dimension_semantics=sem, vmem_limit_bytes=VMEM_LIMIT_BYTES)


def _split2(x):
    hi = x.astype(BF16)
    lo = (x - hi.astype(F32)).astype(BF16)
    return hi, lo


_NN = (((1,), (0,)), ((), ()))
_NT = (((1,), (1,)), ((), ()))
_TN = (((0,), (0,)), ((), ()))


def _dot(a, b, dims=_NN, passes=1):
    if passes == 1:
        return lax.dot_general(a.astype(BF16), b.astype(BF16), dims, preferred_element_type=F32)
    ah, al = _split2(a)
    bh, bl = _split2(b)
    dg = functools.partial(lax.dot_general, dimension_numbers=dims, preferred_element_type=F32)
    return dg(ah, bh) + (dg(ah, bl) + dg(al, bh))


def _dot_exact_lhs(lhs_bf16, x):
    x1 = x.astype(BF16)
    r1 = x - x1.astype(F32)
    x2 = r1.astype(BF16)
    x3 = (r1 - x2.astype(F32)).astype(BF16)
    dg = functools.partial(lax.dot_general, dimension_numbers=_NN, preferred_element_type=F32)
    return dg(lhs_bf16, x1) + (dg(lhs_bf16, x2) + dg(lhs_bf16, x3))


def _sigmoid(x):
    return 1.0 / (1.0 + jnp.exp(-x))


def _silu(x):
    return x * _sigmoid(x)


def _softplus(x):
    return jnp.maximum(x, 0.0) + jnp.log1p(jnp.exp(-jnp.abs(x)))


def _gelu_tanh(x):
    c = 0.7978845608028654
    return 0.5 * x * (1.0 + jnp.tanh(c * (x + 0.044715 * (x * x * x))))


def _layer_norm(y, g, b):
    mu = jnp.mean(y, axis=-1, keepdims=True)
    d = y - mu
    var = jnp.mean(d * d, axis=-1, keepdims=True)
    return d * lax.rsqrt(var + LN_EPS) * g + b


def _lane_pick(x, lane_iota, idx):
    return jnp.sum(jnp.where(lane_iota == idx, x, 0.0), axis=-1, keepdims=True)


def _mm_kernel(x_ref, w_ref, o_ref, *, chunk):
    xb = x_ref[...].astype(BF16)
    for c0 in range(0, o_ref.shape[1], chunk):
        o_ref[:, c0:c0 + chunk] = jnp.dot(xb, w_ref[:, c0:c0 + chunk], preferred_element_type=F32)


def _matmul(x, w, tm, chunk, name):
    m, k = x.shape
    n = w.shape[1]
    return pl.pallas_call(
        functools.partial(_mm_kernel, chunk=chunk),
        grid=(m // tm,),
        in_specs=[pl.BlockSpec((tm, k), lambda i: (i, 0)),
                  pl.BlockSpec((k, n), lambda i: (0, 0))],
        out_specs=pl.BlockSpec((tm, n), lambda i: (i, 0)),
        out_shape=jax.ShapeDtypeStruct((m, n), F32),
        compiler_params=_params("parallel"),
        name=name,
    )(x, w)


def _outproj_ln_kernel(mix_ref, xo_ref, x_ref, wm_ref, wx_ref, g_ref, b_ref, *rest):
    h = jnp.dot(mix_ref[...].astype(BF16), wm_ref[...], preferred_element_type=F32)
    h = h + jnp.dot(xo_ref[...].astype(BF16), wx_ref[...], preferred_element_type=F32)
    y = _layer_norm(ALPHA * x_ref[...] + h, g_ref[...], b_ref[...])
    if len(rest) == 1:
        (o_ref,) = rest
    else:
        rh_ref, rl_ref, o_ref, comb_ref, gate_ref, idx_ref = rest
        comb_ref[...], gate_ref[...], idx_ref[...] = _route_top2(y, rh_ref[...], rl_ref[...])
    o_ref[...] = y


def _outproj_ln(mix, xo, x, w_out, g, b, tm, router=None):
    m = x.shape[0]
    d_mix = mix.shape[1]
    n_blk = d_mix // D_X
    row_blk = lambda width: pl.BlockSpec((tm, width), lambda i: (i, 0))
    const = lambda shape, blk=0: pl.BlockSpec(shape, lambda i: (blk, 0))
    in_specs = [row_blk(d_mix), row_blk(D_X), row_blk(D_MODEL),
                const((d_mix, D_MODEL)), const((D_X, D_MODEL), n_blk),
                const((1, D_MODEL)), const((1, D_MODEL))]
    args = [mix, xo, x, w_out, w_out, g, b]
    out_specs = [row_blk(D_MODEL)]
    out_shape = [jax.ShapeDtypeStruct((m, D_MODEL), F32)]
    if router is not None:
        in_specs += [const((D_MODEL, LANES)), const((D_MODEL, LANES))]
        args += list(router)
        out_specs += [row_blk(LANES)] * 3
        out_shape += [jax.ShapeDtypeStruct((m, LANES), F32), jax.ShapeDtypeStruct((m, LANES), F32),
                      jax.ShapeDtypeStruct((m, LANES), jnp.int32)]
    out = pl.pallas_call(
        _outproj_ln_kernel,
        grid=(m // tm,),
        in_specs=in_specs,
        out_specs=out_specs,
        out_shape=out_shape,
        compiler_params=_params("parallel"),
        name="outproj_ln",
    )(*args)
    return out[0] if router is None else out


FFN_CHUNK = 256


def _swiglu_partial(xb, wg_ref, wu_ref, wd_ref, row_scale=None):
    y = None
    for c0 in range(0, wg_ref.shape[-1], FFN_CHUNK):
        hg = jnp.dot(xb, wg_ref[:, c0:c0 + FFN_CHUNK], preferred_element_type=F32)
        hu = jnp.dot(xb, wu_ref[:, c0:c0 + FFN_CHUNK], preferred_element_type=F32)
        h = _silu(hg) * hu
        if row_scale is not None:
            h = h * row_scale
        part = jnp.dot(h.astype(BF16), wd_ref[c0:c0 + FFN_CHUNK, :], preferred_element_type=F32)
        y = part if y is None else y + part
    return y


def _ffn_kernel(*refs, moe):
    if moe:
        x_ref, comb_ref, wg_ref, wu_ref, wd_ref, g_ref, b_ref, o_ref, xb_ref, acc_ref = refs
    else:
        x_ref, wg_ref, wu_ref, wd_ref, g_ref, b_ref, o_ref, xb_ref, acc_ref = refs
    e = pl.program_id(1)
    f = pl.program_id(2)
    first = jnp.logical_and(e == 0, f == 0)
    last = jnp.logical_and(e == pl.num_programs(1) - 1, f == pl.num_programs(2) - 1)

    @pl.when(first)
    def _():
        xb_ref[...] = x_ref[...].astype(BF16)
        acc_ref[...] = jnp.zeros_like(acc_ref)

    row_scale = None
    if moe:
        comb = comb_ref[...]
        lane = lax.broadcasted_iota(jnp.int32, comb.shape, 1)
        row_scale = _lane_pick(comb, lane, e)
    acc_ref[...] += _swiglu_partial(xb_ref[...], wg_ref, wu_ref, wd_ref, row_scale)

    @pl.when(last)
    def _():
        o_ref[...] = _layer_norm(ALPHA * x_ref[...] + acc_ref[...], g_ref[...], b_ref[...])


def _ffn_ln(x, comb, w_gu, w_down, layer, g, b, tm, tf):
    m = x.shape[0]
    moe = comb is not None
    n_e = w_gu.shape[1]
    n_f = D_FF // tf
    in_specs = [pl.BlockSpec((tm, D_MODEL), lambda i, e, f: (i, 0))]
    args = [x]
    if moe:
        in_specs.append(pl.BlockSpec((tm, LANES), lambda i, e, f: (i, 0)))
        args.append(comb)
    in_specs += [pl.BlockSpec((None, None, D_MODEL, tf), lambda i, e, f: (layer, e, 0, f)),
                 pl.BlockSpec((None, None, D_MODEL, tf), lambda i, e, f: (layer, e, 0, n_f + f)),
                 pl.BlockSpec((None, None, tf, D_MODEL), lambda i, e, f: (layer, e, f, 0)),
                 pl.BlockSpec((1, D_MODEL), lambda i, e, f: (0, 0)),
                 pl.BlockSpec((1, D_MODEL), lambda i, e, f: (0, 0))]
    args += [w_gu, w_gu, w_down, g, b]
    return pl.pallas_call(
        functools.partial(_ffn_kernel, moe=moe),
        grid=(m // tm, n_e, n_f),
        in_specs=in_specs,
        out_specs=pl.BlockSpec((tm, D_MODEL), lambda i, e, f: (i, 0)),
        out_shape=jax.ShapeDtypeStruct((m, D_MODEL), F32),
        scratch_shapes=[pltpu.VMEM((tm, D_MODEL), BF16), pltpu.VMEM((tm, D_MODEL), F32)],
        compiler_params=_params("parallel", "arbitrary", "arbitrary"),
        name="moe_ln" if moe else "ffn_ln",
    )(*args)


def _route_top2(x, wh, wl):
    xb = x.astype(BF16)
    lg = jnp.dot(xb, wh, preferred_element_type=F32) + jnp.dot(xb, wl, preferred_element_type=F32)
    lane = lax.broadcasted_iota(jnp.int32, lg.shape, 1)
    neg = -jnp.inf
    lg = jnp.where(lane < N_EXP, lg, neg)
    m1 = jnp.max(lg, axis=-1, keepdims=True)
    i1 = jnp.min(jnp.where(lg == m1, lane, LANES), axis=-1, keepdims=True)
    lg2 = jnp.where(lane == i1, neg, lg)
    m2 = jnp.max(lg2, axis=-1, keepdims=True)
    i2 = jnp.min(jnp.where(lg2 == m2, lane, LANES), axis=-1, keepdims=True)
    e2 = jnp.exp(m2 - m1)
    den = 1.0 + e2
    g1 = 1.0 / den
    g2 = e2 / den
    comb = jnp.where(lane == i1, g1, 0.0) + jnp.where(lane == i2, g2, 0.0)
    gates = jnp.where(lane == 0, g1, jnp.where(lane == 1, g2, 0.0))
    idx = jnp.where(lane == 0, i1, jnp.where(lane == 1, i2, 0))
    return comb, gates, idx


MOE_TM = 512
FFN_TF = 1792
DISPATCH_TOKENS = 1024
COMBINE_TOKENS = 256


def _moe_plan(idx2, tm):
    t = idx2.shape[0]
    e_flat = idx2.reshape(-1)
    experts = jnp.arange(N_EXP, dtype=jnp.int32)
    onehot = (e_flat[:, None] == experts[None, :]).astype(jnp.int32)
    csum = jnp.cumsum(onehot, axis=0)
    rank = jnp.sum(csum * onehot, axis=1) - 1
    counts = csum[-1]
    ptiles = (counts + tm - 1) // tm
    tile_end = jnp.cumsum(ptiles)
    pstart = (tile_end - ptiles) * tm
    pos = jnp.sum(onehot * pstart[None, :], axis=1) + rank
    n_used = tile_end[-1]
    n_tiles = (2 * t) // tm + N_EXP
    jj = jnp.minimum(jnp.arange(n_tiles, dtype=jnp.int32), n_used - 1)
    tile_expert = jnp.sum((jj[:, None] >= tile_end[None, :]).astype(jnp.int32), axis=1)
    seg_base = jnp.concatenate([pstart + counts, (n_used * tm).reshape(1)])
    seg_len = jnp.concatenate([ptiles * tm - counts, ((n_tiles - n_used) * tm).reshape(1)])
    seg_end = jnp.cumsum(seg_len)
    j = jnp.arange(N_EXP * tm, dtype=jnp.int32)
    seg = jnp.sum((j[:, None] >= seg_end[None, :]).astype(jnp.int32), axis=1)
    seg_1h = (seg[:, None] == jnp.arange(N_EXP + 1, dtype=jnp.int32)[None, :]).astype(jnp.int32)
    pad_dst = j + jnp.sum(seg_1h * (seg_base - (seg_end - seg_len))[None, :], axis=1)
    return (pos.astype(jnp.int32), tile_expert.astype(jnp.int32), n_used.reshape(1).astype(jnp.int32),
            pad_dst.astype(jnp.int32), n_tiles)


ROW_TILE = (SUBLANES, LANES)
assert SUBLANES * LANES == D_MODEL


def _wait_rows(tiles_hbm_ref, n_rows, sem):
    rows = tiles_hbm_ref.at[pl.ds(0, n_rows)]
    pltpu.make_async_copy(rows, rows, sem).wait()


def _moe_dispatch_kernel(pos_ref, pad_dst_ref, x_ref, xs_ref, xr_ref, sem):
    tq = x_ref.shape[0]
    n_pad = pad_dst_ref.shape[1]
    xr_ref[...] = x_ref[...].reshape((tq,) + ROW_TILE)

    def send(grp, carry):
        for jr in range(SUBLANES):
            t = SUBLANES * grp + jr
            for k in range(2):
                pltpu.make_async_copy(xr_ref.at[t], xs_ref.at[pos_ref[0, 2 * t + k]], sem).start(priority=k)
        return carry

    lax.fori_loop(0, tq // SUBLANES, send, 0)

    def pad(grp, carry):
        for jr in range(SUBLANES):
            dst = pad_dst_ref[0, SUBLANES * grp + jr]
            pltpu.make_async_copy(xr_ref.at[jr], xs_ref.at[dst], sem).start(priority=jr % 2)
        return carry

    lax.fori_loop(0, n_pad // SUBLANES, pad, 0)
    _wait_rows(xs_ref, 2 * tq + n_pad, sem)


def _moe_dispatch(x, pos, pad_dst, n_rows_out):
    m = x.shape[0]
    tq = DISPATCH_TOKENS
    n_steps = m // tq
    n_pad = pad_dst.shape[0] // n_steps
    smem = functools.partial(pl.BlockSpec, memory_space=pltpu.SMEM)
    return pl.pallas_call(
        _moe_dispatch_kernel,
        grid=(n_steps,),
        in_specs=[smem((None, 1, 2 * tq), lambda i: (i, 0, 0)),
                  smem((None, 1, n_pad), lambda i: (i, 0, 0)),
                  pl.BlockSpec((tq, D_MODEL), lambda i: (i, 0))],
        out_specs=pl.BlockSpec(memory_space=pl.ANY),
        out_shape=jax.ShapeDtypeStruct((n_rows_out,) + ROW_TILE, F32),
        scratch_shapes=[pltpu.VMEM((tq,) + ROW_TILE, F32), pltpu.SemaphoreType.DMA(())],
        compiler_params=_params("arbitrary"),
        name="moe_dispatch",
    )(pos.reshape(n_steps, 1, 2 * tq), pad_dst.reshape(n_steps, 1, n_pad), x)


def _moe_ffn_kernel(te_ref, nu_ref, x_ref, wg_ref, wu_ref, wd_ref, o_ref, xb_ref, acc_ref):
    p = pl.program_id(0)
    f = pl.program_id(1)
    tm = xb_ref.shape[0]

    @pl.when(p < nu_ref[0])
    def _():
        @pl.when(f == 0)
        def _():
            xb_ref[...] = x_ref[...].reshape(tm, D_MODEL).astype(BF16)

        y = _swiglu_partial(xb_ref[...], wg_ref, wu_ref, wd_ref)
        last = pl.num_programs(1) - 1

        @pl.when(f == 0)
        def _():
            acc_ref[...] = y

        @pl.when(jnp.logical_and(f != 0, f != last))
        def _():
            acc_ref[...] += y

        @pl.when(f == last)
        def _():
            o_ref[...] = (acc_ref[...] + y).reshape((tm,) + ROW_TILE)

    @pl.when(jnp.logical_and(p >= nu_ref[0], f == 0))
    def _():
        o_ref[...] = jnp.zeros_like(o_ref)


def _moe_ffn(xs, tile_expert, n_used, w_gu, w_down, layer, tm, tf):
    n_tiles = xs.shape[0] // tm
    n_f = D_FF // tf
    assert n_f >= 2
    tile = lambda p, nu: jnp.minimum(p, nu[0] - 1)
    col = lambda p, f, nu: jnp.where(p < nu[0], f, n_f - 1)
    grid_spec = pltpu.PrefetchScalarGridSpec(
        num_scalar_prefetch=2,
        grid=(n_tiles, n_f),
        in_specs=[pl.BlockSpec((tm,) + ROW_TILE, lambda p, f, te, nu: (tile(p, nu), 0, 0)),
                  pl.BlockSpec((None, None, D_MODEL, tf),
                               lambda p, f, te, nu: (layer, te[p], 0, col(p, f, nu))),
                  pl.BlockSpec((None, None, D_MODEL, tf),
                               lambda p, f, te, nu: (layer, te[p], 0, n_f + col(p, f, nu))),
                  pl.BlockSpec((None, None, tf, D_MODEL),
                               lambda p, f, te, nu: (layer, te[p], col(p, f, nu), 0))],
        out_specs=pl.BlockSpec((tm,) + ROW_TILE, lambda p, f, te, nu: (p, 0, 0)),
        scratch_shapes=[pltpu.VMEM((tm, D_MODEL), BF16), pltpu.VMEM((tm, D_MODEL), F32)],
    )
    return pl.pallas_call(
        _moe_ffn_kernel,
        grid_spec=grid_spec,
        out_shape=jax.ShapeDtypeStruct(xs.shape, F32),
        compiler_params=_params("arbitrary", "arbitrary"),
        name="moe_ffn",
    )(tile_expert, n_used, xs, w_gu, w_gu, w_down)


def _moe_combine_ln_kernel(pos_ref, pos_next_ref, gate_ref, x_ref, ye_ref, g_ref, b_ref, o_ref,
                           buf_ref, sem):
    i = pl.program_id(0)
    tq = x_ref.shape[0]
    slot = lax.rem(i, 2)

    def start_fetch(rows_ref, to_slot):
        def fetch(grp, carry):
            for jr in range(SUBLANES):
                t = SUBLANES * grp + jr
                for k in range(2):
                    pltpu.make_async_copy(ye_ref.at[rows_ref[0, 2 * t + k]], buf_ref.at[to_slot, k * tq + t],
                                          sem.at[to_slot]).start(priority=k)
            return carry

        lax.fori_loop(0, tq // SUBLANES, fetch, 0)

    @pl.when(i == 0)
    def _():
        start_fetch(pos_ref, 0)

    @pl.when(i + 1 < pl.num_programs(0))
    def _():
        start_fetch(pos_next_ref, 1 - slot)

    _wait_rows(ye_ref, 2 * tq, sem.at[slot])
    gates = gate_ref[...]
    lane = lax.broadcasted_iota(jnp.int32, gates.shape, 1)
    y0 = buf_ref[slot, 0:tq].reshape(tq, D_MODEL)
    y1 = buf_ref[slot, tq:2 * tq].reshape(tq, D_MODEL)
    y = _lane_pick(gates, lane, 0) * y0 + _lane_pick(gates, lane, 1) * y1
    o_ref[...] = _layer_norm(ALPHA * x_ref[...] + y, g_ref[...], b_ref[...])


def _moe_combine_ln(x, gates, pos, ye, g, b):
    m = x.shape[0]
    tq = COMBINE_TOKENS
    n_steps = m // tq
    pos3 = pos.reshape(n_steps, 1, 2 * tq)
    smem = functools.partial(pl.BlockSpec, memory_space=pltpu.SMEM)
    return pl.pallas_call(
        _moe_combine_ln_kernel,
        grid=(n_steps,),
        in_specs=[smem((None, 1, 2 * tq), lambda i: (i, 0, 0)),
                  smem((None, 1, 2 * tq), lambda i: (jnp.minimum(i + 1, n_steps - 1), 0, 0)),
                  pl.BlockSpec((tq, LANES), lambda i: (i, 0)),
                  pl.BlockSpec((tq, D_MODEL), lambda i: (i, 0)),
                  pl.BlockSpec(memory_space=pl.ANY),
                  pl.BlockSpec((1, D_MODEL), lambda i: (0, 0)),
                  pl.BlockSpec((1, D_MODEL), lambda i: (0, 0))],
        out_specs=pl.BlockSpec((tq, D_MODEL), lambda i: (i, 0)),
        out_shape=jax.ShapeDtypeStruct((m, D_MODEL), F32),
        scratch_shapes=[pltpu.VMEM((2, 2 * tq) + ROW_TILE, F32), pltpu.SemaphoreType.DMA((2,))],
        compiler_params=_params("arbitrary"),
        name="moe_combine_ln",
    )(pos3, pos3, gates, x, ye, g, b)


def _moe_routed_ln(x, gates, idx, w_gu, w_down, layer, g, b):
    pos, tile_expert, n_used, pad_dst, n_tiles = _moe_plan(idx[:, :2], MOE_TM)
    xs = _moe_dispatch(x, pos, pad_dst, n_tiles * MOE_TM)
    ye = _moe_ffn(xs, tile_expert, n_used, w_gu, w_down, layer, MOE_TM, FFN_TF)
    return _moe_combine_ln(x, gates, pos, ye, g, b)


GDN_PASSES_QK = 1
GDN_PASSES_SOLVE = 1
GDN_PASSES_STATE = 1
INV_BLOCK_SHIFT = 4
GDN_SEQS = 4


def _gdn_chunk_kernel(qkv_ref, z_ref, ab_ref, cw_ref, alog_ref, dtb_ref, nw_ref, hist0_ref, s0_ref,
                      mix_ref, sout_ref, hout_ref, xh_ref, s_ref):
    n = pl.program_id(1)
    c = GDN_CHUNK
    hrows = SUBLANES
    n_seq = qkv_ref.shape[0]
    items = [(b, h) for b in range(n_seq) for h in range(H_A)]

    @pl.when(n == 0)
    def _():
        xh_ref[:, 0:hrows, :] = hist0_ref[...]
        s_ref[...] = s0_ref[...]

    xh_ref[:, hrows:hrows + c, :] = qkv_ref[...]

    row = lax.broadcasted_iota(jnp.int32, (c, c), 0)
    col = lax.broadcasted_iota(jnp.int32, (c, c), 1)
    causal = row >= col
    strict = row > col
    eye = jnp.where(row == col, 1.0, 0.0)
    blockdiag = (row >> INV_BLOCK_SHIFT) == (col >> INV_BLOCK_SHIFT)
    tril_ones = jnp.where(causal, 1.0, 0.0).astype(BF16)
    lane = lax.broadcasted_iota(jnp.int32, (c, LANES), 1)
    mm_s = functools.partial(_dot, passes=GDN_PASSES_SOLVE)

    beta_all, gcum_all, gcum_t = [], [], []
    for b in range(n_seq):
        ab = ab_ref[b]
        g_all = -jnp.exp(alog_ref[...]) * _softplus(ab + dtb_ref[...])
        beta_all.append(_sigmoid(ab))
        gc = _dot_exact_lhs(tril_ones, g_all)
        gcum_all.append(gc)
        gcum_t.append(gc.T)

    def conv_silu(b, c0):
        acc = xh_ref[b, hrows - 3:hrows - 3 + c, c0:c0 + LANES] * cw_ref[0:1, c0:c0 + LANES]
        for j in range(1, CONV_W):
            acc = acc + (xh_ref[b, hrows - 3 + j:hrows - 3 + j + c, c0:c0 + LANES]
                         * cw_ref[j:j + 1, c0:c0 + LANES])
        return _silu(acc)

    q, k, v, kb, egc, gc_col, gc_last, beta_col, decay = {}, {}, {}, {}, {}, {}, {}, {}, {}
    for it in items:
        b, h = it
        qi = conv_silu(b, h * DK_A)
        ki = conv_silu(b, QK_A + h * DK_A)
        v[it] = conv_silu(b, 2 * QK_A + h * DV_A)
        q[it] = qi * lax.rsqrt(jnp.sum(qi * qi, axis=-1, keepdims=True) + RMS_EPS) * (DK_A ** -0.5)
        k[it] = ki * lax.rsqrt(jnp.sum(ki * ki, axis=-1, keepdims=True) + RMS_EPS)
        gc_col[it] = _lane_pick(gcum_all[b], lane, h)
        beta_col[it] = _lane_pick(beta_all[b], lane, B_LANE + h)
        gc_last[it] = gc_col[it][c - 1:c, :]
        decay[it] = jnp.where(causal, jnp.exp(gc_col[it] - gcum_t[b][h:h + 1, :c]), 0.0)
        egc[it] = jnp.exp(gc_col[it])
        kb[it] = k[it] * beta_col[it]

    kk = {it: _dot(jnp.concatenate([kb[it], q[it]], axis=0), k[it], _NT, GDN_PASSES_QK) for it in items}
    a_mat = {it: jnp.where(strict, kk[it][:c] * decay[it], 0.0) for it in items}
    attn = {it: kk[it][c:] * decay[it] for it in items}

    a_d = {it: jnp.where(blockdiag, a_mat[it], 0.0) for it in items}
    a_n = {it: jnp.where(blockdiag, 0.0, a_mat[it]) for it in items}
    p2 = {it: mm_s(a_d[it], a_d[it]) for it in items}
    e_d = {it: p2[it] - a_d[it] - mm_s(a_d[it], p2[it]) for it in items}
    p4 = {it: mm_s(p2[it], p2[it]) for it in items}
    e_d = {it: e_d[it] + p4[it] + mm_s(e_d[it], p4[it]) for it in items}
    p8 = {it: mm_s(p4[it], p4[it]) for it in items}
    e_d = {it: e_d[it] + p8[it] + mm_s(e_d[it], p8[it]) for it in items}
    m_blk = {it: a_n[it] + mm_s(e_d[it], a_n[it]) for it in items}
    rhs = {it: jnp.concatenate([v[it] * beta_col[it], kb[it] * egc[it]], axis=1) for it in items}
    rhs = {it: rhs[it] + mm_s(e_d[it], rhs[it]) for it in items}
    m2 = {it: mm_s(m_blk[it], m_blk[it]) for it in items}
    f_y = {it: m2[it] - m_blk[it] - mm_s(m_blk[it], m2[it]) for it in items}
    sol = {it: rhs[it] + mm_s(f_y[it], rhs[it]) for it in items}

    s_old = {it: s_ref[it[0], it[1]] for it in items}
    wq = {it: _dot(jnp.concatenate([sol[it][:, DV_A:], q[it] * egc[it]], axis=0), s_old[it],
                   _NN, GDN_PASSES_STATE) for it in items}
    v_new = {it: sol[it][:, :DV_A] - wq[it][:c] for it in items}
    o = {it: wq[it][c:] + _dot(attn[it], v_new[it], _NN, GDN_PASSES_STATE) for it in items}
    for it in items:
        k_dec = k[it] * jnp.exp(gc_last[it] - gc_col[it])
        s_ref[it[0], it[1]] = (s_old[it] * jnp.exp(gc_last[it])
                               + _dot(k_dec, v_new[it], _TN, GDN_PASSES_STATE))
    for it in items:
        b, h = it
        oi = o[it]
        oi = oi * lax.rsqrt(jnp.mean(oi * oi, axis=-1, keepdims=True) + RMS_EPS) * nw_ref[...]
        mix_ref[b, :, h * DV_A:(h + 1) * DV_A] = oi * _silu(z_ref[b, :, h * DV_A:(h + 1) * DV_A])

    xh_ref[:, 0:hrows, :] = xh_ref[:, c:c + hrows, :]

    @pl.when(n == pl.num_programs(1) - 1)
    def _():
        sout_ref[...] = s_ref[...]
        hout_ref[...] = xh_ref[:, c:c + hrows, :]


def _gdn_chunked(proj, conv_w, alog_row, dtb_row, nw_row, hist0, s0):
    bsz, t, _ = proj.shape
    c = GDN_CHUNK
    nb = GDN_SEQS
    n_chunks = t // c
    row_spec = lambda width, blk: pl.BlockSpec((nb, c, width), lambda b, n: (b, n, blk))
    full2 = lambda shape: pl.BlockSpec(shape, lambda b, n: (0, 0))
    return pl.pallas_call(
        _gdn_chunk_kernel,
        grid=(bsz // nb, n_chunks),
        in_specs=[row_spec(CONV_DIM, 0),
                  row_spec(V_A, A_Z_OFF // V_A),
                  row_spec(LANES, A_AB_OFF // LANES),
                  full2((CONV_W, CONV_DIM)),
                  full2((1, LANES)), full2((1, LANES)), full2((1, DV_A)),
                  pl.BlockSpec((nb, SUBLANES, CONV_DIM), lambda b, n: (b, 0, 0)),
                  pl.BlockSpec((nb, H_A, DK_A, DV_A), lambda b, n: (b, 0, 0, 0))],
        out_specs=[pl.BlockSpec((nb, c, V_A), lambda b, n: (b, n, 0)),
                   pl.BlockSpec((nb, H_A, DK_A, DV_A), lambda b, n: (b, 0, 0, 0)),
                   pl.BlockSpec((nb, SUBLANES, CONV_DIM), lambda b, n: (b, 0, 0))],
        out_shape=[jax.ShapeDtypeStruct((bsz, t, V_A), F32),
                   jax.ShapeDtypeStruct((bsz, H_A, DK_A, DV_A), F32),
                   jax.ShapeDtypeStruct((bsz, SUBLANES, CONV_DIM), F32)],
        scratch_shapes=[pltpu.VMEM((nb, SUBLANES + c, CONV_DIM), F32),
                        pltpu.VMEM((nb, H_A, DK_A, DV_A), F32)],
        compiler_params=_params("parallel", "arbitrary"),
        name="gdn_chunked",
    )(proj, proj, proj, conv_w, alog_row, dtb_row, nw_row, hist0, s0)


def _gdn_step_kernel(qkv_ref, z_ref, ab_ref, cw_ref, alog_ref, dtb_ref, nw_ref, hist_ref, s_ref,
                     mix_ref, sout_ref, hout_ref, o_scr):
    rows = qkv_ref.shape[0]
    new = qkv_ref[...]
    y = hist_ref[0] * cw_ref[0:1, :]
    for j in range(1, CONV_W - 1):
        y = y + hist_ref[j] * cw_ref[j:j + 1, :]
    y = y + new * cw_ref[CONV_W - 1:CONV_W, :]
    qkv = _silu(y)
    for j in range(CONV_W - 2):
        hout_ref[j] = hist_ref[j + 1]
    hout_ref[CONV_W - 2] = new

    ab = ab_ref[...]
    lane = lax.broadcasted_iota(jnp.int32, ab.shape, 1)
    g_all = -jnp.exp(alog_ref[...]) * _softplus(ab + dtb_ref[...])
    beta_all = _sigmoid(ab)
    ri = lax.broadcasted_iota(jnp.int32, (DK_A, DK_A), 0)
    ci = lax.broadcasted_iota(jnp.int32, (DK_A, DK_A), 1)
    diag = ri == ci

    def as_column(r):
        return jnp.sum(jnp.where(diag, r, 0.0), axis=-1, keepdims=True)

    for h in range(H_A):
        q = qkv[:, h * DK_A:(h + 1) * DK_A]
        k = qkv[:, QK_A + h * DK_A:QK_A + (h + 1) * DK_A]
        v = qkv[:, 2 * QK_A + h * DV_A:2 * QK_A + (h + 1) * DV_A]
        q = q * lax.rsqrt(jnp.sum(q * q, axis=-1, keepdims=True) + RMS_EPS) * (DK_A ** -0.5)
        k = k * lax.rsqrt(jnp.sum(k * k, axis=-1, keepdims=True) + RMS_EPS)
        eg = jnp.exp(_lane_pick(g_all, lane, h))
        beta = _lane_pick(beta_all, lane, B_LANE + h)
        for s in range(rows):
            k_col = as_column(k[s:s + 1, :])
            q_col = as_column(q[s:s + 1, :])
            st = s_ref[s, h] * eg[s:s + 1, :]
            ks = jnp.sum(k_col * st, axis=0, keepdims=True)
            u = beta[s:s + 1, :] * (v[s:s + 1, :] - ks)
            st = st + k_col * u
            sout_ref[s, h] = st
            o_scr[s:s + 1, h * DV_A:(h + 1) * DV_A] = jnp.sum(q_col * st, axis=0, keepdims=True)

    for h in range(H_A):
        o = o_scr[:, h * DV_A:(h + 1) * DV_A]
        o = o * lax.rsqrt(jnp.mean(o * o, axis=-1, keepdims=True) + RMS_EPS) * nw_ref[...]
        mix_ref[:, h * DV_A:(h + 1) * DV_A] = o * _silu(z_ref[:, h * DV_A:(h + 1) * DV_A])


def _gdn_step(proj, conv_w, alog_row, dtb_row, nw_row, hist_t, s0, layer):
    bsz = proj.shape[0]
    rows = SUBLANES
    full1 = lambda shape: pl.BlockSpec(shape, lambda i: (0, 0))
    return pl.pallas_call(
        _gdn_step_kernel,
        grid=(bsz // rows,),
        in_specs=[pl.BlockSpec((rows, CONV_DIM), lambda i: (i, 0)),
                  pl.BlockSpec((rows, V_A), lambda i: (i, A_Z_OFF // V_A)),
                  pl.BlockSpec((rows, LANES), lambda i: (i, A_AB_OFF // LANES)),
                  full1((CONV_W, CONV_DIM)),
                  full1((1, LANES)), full1((1, LANES)), full1((1, DV_A)),
                  pl.BlockSpec((None, CONV_W - 1, rows, CONV_DIM), lambda i: (layer, 0, i, 0)),
                  pl.BlockSpec((None, rows, H_A, DK_A, DV_A), lambda i: (layer, i, 0, 0, 0))],
        out_specs=[pl.BlockSpec((rows, V_A), lambda i: (i, 0)),
                   pl.BlockSpec((rows, H_A, DK_A, DV_A), lambda i: (i, 0, 0, 0)),
                   pl.BlockSpec((CONV_W - 1, rows, CONV_DIM), lambda i: (0, i, 0))],
        out_shape=[jax.ShapeDtypeStruct((bsz, V_A), F32),
                   jax.ShapeDtypeStruct((bsz, H_A, DK_A, DV_A), F32),
                   jax.ShapeDtypeStruct((CONV_W - 1, bsz, CONV_DIM), F32)],
        scratch_shapes=[pltpu.VMEM((rows, V_A), F32)],
        compiler_params=_params("parallel"),
        name="gdn_step",
    )(proj, proj, proj, conv_w, alog_row, dtb_row, nw_row, hist_t, s0)


SGU_CHUNKS = 4


def _sgu_chunk_kernel(u_ref, v_ref, g_ref, b_ref, ws_ref, bs_ref, mix_ref):
    cb = CHUNK_B
    row = lax.broadcasted_iota(jnp.int32, (cb, cb), 0)
    col = lax.broadcasted_iota(jnp.int32, (cb, cb), 1)
    ws = [jnp.where(row >= col, ws_ref[gi], 0.0).astype(BF16) for gi in range(G_B)]
    for r0 in range(0, u_ref.shape[0], cb):
        u = _gelu_tanh(u_ref[r0:r0 + cb, :])
        v = _layer_norm(_gelu_tanh(v_ref[r0:r0 + cb, :]), g_ref[...], b_ref[...])
        for gi in range(G_B):
            mixed = _dot(ws[gi], v[:, gi * CH_B:(gi + 1) * CH_B]) + bs_ref[gi]
            mix_ref[r0:r0 + cb, gi * CH_B:(gi + 1) * CH_B] = u[:, gi * CH_B:(gi + 1) * CH_B] * mixed


def _sgu_chunked(proj, ln_g, ln_b, w_s, bs_b):
    bsz, t, _ = proj.shape
    c = CHUNK_B * SGU_CHUNKS
    return pl.pallas_call(
        _sgu_chunk_kernel,
        grid=(bsz, t // c),
        in_specs=[pl.BlockSpec((None, c, D_B), lambda b, n: (b, n, 0)),
                  pl.BlockSpec((None, c, D_B), lambda b, n: (b, n, 1)),
                  pl.BlockSpec((1, D_B), lambda b, n: (0, 0)),
                  pl.BlockSpec((1, D_B), lambda b, n: (0, 0)),
                  pl.BlockSpec((G_B, CHUNK_B, CHUNK_B), lambda b, n: (0, 0, 0)),
                  pl.BlockSpec((G_B, CHUNK_B, CH_B), lambda b, n: (0, 0, 0))],
        out_specs=pl.BlockSpec((None, c, D_B), lambda b, n: (b, n, 0)),
        out_shape=jax.ShapeDtypeStruct((bsz, t, D_B), F32),
        compiler_params=_params("parallel", "parallel"),
        name="sgu_chunked",
    )(proj, proj, ln_g, ln_b, w_s, bs_b)


def _sgu_first_kernel(u_ref, v_ref, g_ref, b_ref, w00_ref, b0_ref, mix_ref, vout_ref):
    u = _gelu_tanh(u_ref[...])
    v = _layer_norm(_gelu_tanh(v_ref[...]), g_ref[...], b_ref[...])
    vout_ref[...] = v
    mix_ref[...] = u * (w00_ref[...] * v + b0_ref[...])


def _sgu_first(proj, ln_g, ln_b, w00_row, b0_row):
    bsz = proj.shape[0]
    row = lambda: pl.BlockSpec((1, D_B), lambda i: (0, 0))
    return pl.pallas_call(
        _sgu_first_kernel,
        grid=(1,),
        in_specs=[pl.BlockSpec((bsz, D_B), lambda i: (0, 0)),
                  pl.BlockSpec((bsz, D_B), lambda i: (0, 1)),
                  row(), row(), row(), row()],
        out_specs=[pl.BlockSpec((bsz, D_B), lambda i: (0, 0)),
                   pl.BlockSpec((bsz, D_B), lambda i: (0, 0))],
        out_shape=[jax.ShapeDtypeStruct((bsz, D_B), F32),
                   jax.ShapeDtypeStruct((bsz, D_B), F32)],
        compiler_params=_params("arbitrary"),
        name="sgu_first",
    )(proj, proj, ln_g, ln_b, w00_row, b0_row)


def _head_masks(shape):
    lane = lax.broadcasted_iota(jnp.int32, shape, len(shape) - 1)
    return [(lane // DH_X) == h for h in range(H_X)]


def _xattn_kernel(q_ref, k_ref, v_ref, o_ref):
    q = q_ref[...]
    kb = k_ref[...].astype(BF16)
    vb = v_ref[...].astype(BF16)
    masks = _head_masks(q.shape)
    heads = range(H_X)
    s = [_dot(jnp.where(masks[h], q, 0.0), kb, _NT) * (DH_X ** -0.5) for h in heads]
    p = [jnp.exp(s[h] - jnp.max(s[h], axis=-1, keepdims=True)) for h in heads]
    p = [p[h] / jnp.sum(p[h], axis=-1, keepdims=True) for h in heads]
    pv = [_dot(p[h], vb) for h in heads]
    out = jnp.where(masks[0], pv[0], 0.0)
    for h in range(1, H_X):
        out = jnp.where(masks[h], pv[h], out)
    o_ref[...] = out


def _xattn(proj, xq_blk, mem_k, mem_v, layer, tq):
    bsz, t, _ = proj.shape
    return pl.pallas_call(
        _xattn_kernel,
        grid=(bsz, t // tq),
        in_specs=[pl.BlockSpec((None, tq, D_X), lambda b, i: (b, i, xq_blk)),
                  pl.BlockSpec((None, None, N_MEM, D_X), lambda b, i: (layer, b, 0, 0)),
                  pl.BlockSpec((None, None, N_MEM, D_X), lambda b, i: (layer, b, 0, 0))],
        out_specs=pl.BlockSpec((None, tq, D_X), lambda b, i: (b, i, 0)),
        out_shape=jax.ShapeDtypeStruct((bsz, t, D_X), F32),
        compiler_params=_params("parallel", "parallel"),
        name="xattn",
    )(proj, mem_k, mem_v)


def _xattn_step_kernel(q_ref, k_ref, v_ref, o_ref):
    rows = q_ref.shape[0]
    q = q_ref[...]
    hrow = lax.broadcasted_iota(jnp.int32, (SUBLANES, D_X), 0)
    hlane = lax.broadcasted_iota(jnp.int32, (SUBLANES, D_X), 1) // DH_X
    sel = hrow == hlane
    seqs = range(rows)
    sc = [_dot(jnp.where(sel, q[s:s + 1, :], 0.0), k_ref[s], _NT) * (DH_X ** -0.5) for s in seqs]
    p = [jnp.exp(sc[s] - jnp.max(sc[s], axis=-1, keepdims=True)) for s in seqs]
    p = [p[s] / jnp.sum(p[s], axis=-1, keepdims=True) for s in seqs]
    pv = [_dot(p[s], v_ref[s]) for s in seqs]
    for s in seqs:
        o_ref[s:s + 1, :] = jnp.sum(jnp.where(sel, pv[s], 0.0), axis=0, keepdims=True)


def _xattn_step(proj, xq_blk, mem_k, mem_v, layer):
    bsz = proj.shape[0]
    rows = SUBLANES
    return pl.pallas_call(
        _xattn_step_kernel,
        grid=(bsz // rows,),
        in_specs=[pl.BlockSpec((rows, D_X), lambda i: (i, xq_blk)),
                  pl.BlockSpec((None, rows, N_MEM, D_X), lambda i: (layer, i, 0, 0)),
                  pl.BlockSpec((None, rows, N_MEM, D_X), lambda i: (layer, i, 0, 0))],
        out_specs=pl.BlockSpec((rows, D_X), lambda i: (i, 0)),
        out_shape=jax.ShapeDtypeStruct((bsz, D_X), F32),
        compiler_params=_params("parallel"),
        name="xattn_step",
    )(proj, mem_k, mem_v)


def _pad_lanes(v):
    return jnp.zeros((1, LANES), F32).at[0, :v.shape[0]].set(v.astype(F32))


def _prep_weights(a_w_in, a_A_log, a_dt_bias, b_w_s, b_b_s, moe_router):
    qkvz = a_w_in[:, :, :A_XQ_OFF]
    a_cols = a_w_in[:, :, A_XQ_OFF:A_XQ_OFF + H_A]
    b_cols = a_w_in[:, :, A_XQ_OFF + H_A:A_XQ_OFF + 2 * H_A]
    xq = a_w_in[:, :, A_XQ_OFF + 2 * H_A:]
    ab = jnp.zeros(a_w_in.shape[:2] + (LANES,), F32)
    ab = ab.at[:, :, :H_A].set(a_cols).at[:, :, B_LANE:B_LANE + H_A].set(b_cols)
    a_w = jnp.concatenate([qkvz, xq, ab], axis=-1).astype(BF16)
    alog_rows = [_pad_lanes(a_A_log[j]) for j in range(a_A_log.shape[0])]
    dtb_rows = [_pad_lanes(a_dt_bias[j]) for j in range(a_dt_bias.shape[0])]
    bs_b = jnp.broadcast_to(b_b_s[..., None], b_b_s.shape + (CH_B,)).astype(F32)
    w00_rows = jnp.repeat(b_w_s[:, :, 0, 0], CH_B, axis=-1)[:, None, :]
    b0_rows = jnp.repeat(b_b_s[:, :, 0], CH_B, axis=-1)[:, None, :]
    r = jnp.zeros(moe_router.shape[:2] + (LANES,), F32).at[:, :, :N_EXP].set(moe_router)
    r_hi = r.astype(BF16)
    r_lo = (r - r_hi.astype(F32)).astype(BF16)
    return a_w, alog_rows, dtb_rows, bs_b, w00_rows, b0_rows, r_hi, r_lo


def kernel(x_prompt, x_sample, state_gdn, state_conv, cache_mem_k, cache_mem_v, mem_prompt,
           a_w_in, a_conv_w, a_A_log, a_dt_bias, a_norm_w,
           b_w_in, b_ln_g, b_ln_b, b_w_s, b_b_s,
           w_mem_kv, w_out, ln1_g, ln1_b, ln2_g, ln2_b,
           ffn_w_gu, ffn_w_down, moe_router, moe_w_gu, moe_w_down):
    n_p, t_p, _ = x_prompt.shape
    n_s = x_sample.shape[0]
    n_a = a_w_in.shape[0]

    a_w, alog_rows, dtb_rows, bs_b, w00_rows, b0_rows, r_hi, r_lo = _prep_weights(
        a_w_in, a_A_log, a_dt_bias, b_w_s, b_b_s, moe_router)
    ffn_gu = ffn_w_gu.astype(BF16)[:, None]
    ffn_dn = ffn_w_down.astype(BF16)[:, None]
    moe_gu = moe_w_gu.astype(BF16)
    moe_dn = moe_w_down.astype(BF16)
    row = lambda v: v.reshape(1, -1).astype(F32)

    w_kv = jnp.transpose(w_mem_kv, (1, 0, 2)).reshape(D_MODEL, DEPTH * 2 * D_X).astype(BF16)
    kv = _matmul(mem_prompt.reshape(n_p * N_MEM, D_MODEL), w_kv, 512, 1024, "mem_kv")
    kv = kv.reshape(n_p, N_MEM, DEPTH, 2, D_X)
    p_mem_k = jnp.transpose(kv[:, :, :, 0, :], (2, 0, 1, 3))
    p_mem_v = jnp.transpose(kv[:, :, :, 1, :], (2, 0, 1, 3))
    s_mem_k = cache_mem_k.reshape(DEPTH, n_s, N_MEM, D_X)
    s_mem_v = cache_mem_v.reshape(DEPTH, n_s, N_MEM, D_X)

    xp = x_prompt.reshape(n_p * t_p, D_MODEL)
    xs = x_sample.reshape(n_s, D_MODEL)
    p_hist0 = jnp.zeros((n_p, SUBLANES, CONV_DIM), F32)
    p_s0 = jnp.zeros((n_p, H_A, DK_A, DV_A), F32)
    s_hist_t = jnp.transpose(state_conv, (0, 2, 1, 3))

    p_gdn, p_conv, s_gdn, s_conv, s_sgu_v = [], [], [], [], []
    tm_p, tm_s = 1024, n_s
    for i in range(DEPTH):
        j = i // 2
        if i % 2 == 0:
            proj_p = _matmul(xp, a_w[j], 512, 1152, "in_proj_a")
            proj_s = _matmul(xs, a_w[j], tm_s, 1152, "in_proj_a_s")
            nw = row(a_norm_w[j])
            mix_p, sp, hp = _gdn_chunked(proj_p.reshape(n_p, t_p, A_COLS), a_conv_w[j],
                                         alog_rows[j], dtb_rows[j], nw, p_hist0, p_s0)
            mix_p = mix_p.reshape(n_p * t_p, V_A)
            p_gdn.append(sp)
            p_conv.append(hp[:, SUBLANES - (CONV_W - 1):, :])
            mix_s, ss, hs = _gdn_step(proj_s, a_conv_w[j], alog_rows[j], dtb_rows[j], nw,
                                      s_hist_t, state_gdn, j)
            s_gdn.append(ss)
            s_conv.append(jnp.transpose(hs, (1, 0, 2)))
            cols, xq_blk = A_COLS, A_XQ_OFF // D_X
        else:
            b_w = b_w_in[j].astype(BF16)
            proj_p = _matmul(xp, b_w, 512, 896, "in_proj_b")
            proj_s = _matmul(xs, b_w, tm_s, 896, "in_proj_b_s")
            mix_p = _sgu_chunked(proj_p.reshape(n_p, t_p, B_COLS), row(b_ln_g[j]), row(b_ln_b[j]),
                                 b_w_s[j], bs_b[j]).reshape(n_p * t_p, D_B)
            mix_s, v_s = _sgu_first(proj_s, row(b_ln_g[j]), row(b_ln_b[j]), w00_rows[j], b0_rows[j])
            s_sgu_v.append(v_s.reshape(n_s, 1, D_B))
            cols, xq_blk = B_COLS, B_XQ_OFF // D_X
        xo_p = _xattn(proj_p.reshape(n_p, t_p, cols), xq_blk, p_mem_k, p_mem_v, i, 512)
        xo_s = _xattn_step(proj_s, xq_blk, s_mem_k, s_mem_v, i)
        g1, b1, g2, b2 = row(ln1_g[i]), row(ln1_b[i]), row(ln2_g[i]), row(ln2_b[i])
        w_o = w_out[i].astype(BF16)
        xo_p = xo_p.reshape(n_p * t_p, D_X)
        if i % 2 == 0:
            xp = _outproj_ln(mix_p, xo_p, xp, w_o, g1, b1, 512)
            xs = _outproj_ln(mix_s, xo_s, xs, w_o, g1, b1, tm_s)
            xp = _ffn_ln(xp, None, ffn_gu, ffn_dn, j, g2, b2, tm_p, FFN_TF)
            xs = _ffn_ln(xs, None, ffn_gu, ffn_dn, j, g2, b2, tm_s, FFN_TF)
        else:
            router = (r_hi[j], r_lo[j])
            xp, _, gates_p, idx_p = _outproj_ln(mix_p, xo_p, xp, w_o, g1, b1, 512, router)
            xs, comb_s, _, _ = _outproj_ln(mix_s, xo_s, xs, w_o, g1, b1, tm_s, router)
            xp = _moe_routed_ln(xp, gates_p, idx_p, moe_gu, moe_dn, j, g2, b2)
            xs = _ffn_ln(xs, comb_s, moe_gu, moe_dn, j, g2, b2, tm_s, FFN_TF)

    mem_shape = (DEPTH, n_p, N_MEM, H_X, DH_X)
    return (xp.reshape(n_p, t_p, D_MODEL),
            xs.reshape(n_s, 1, D_MODEL),
            jnp.stack(p_gdn),
            jnp.stack(p_conv),
            p_mem_k.reshape(mem_shape),
            p_mem_v.reshape(mem_shape),
            jnp.stack(s_gdn),
            jnp.stack(s_conv),
            jnp.stack(s_sgu_v))
```

```python
import functools

import jax
import jax.numpy as jnp
from jax import lax
from jax.experimental import pallas as pl
from jax.experimental.pallas import tpu as pltpu

F32 = jnp.float32
BF16 = jnp.bfloat16

D_MODEL = 1024
DEPTH = 4
H_A = 6
DK_A = 128
DV_A = 128
QK_A = H_A * DK_A
V_A = H_A * DV_A
CONV_W = 4
CONV_DIM = 2 * QK_A + V_A
GDN_CHUNK = 64
G_B = 6
CH_B = 128
D_B = G_B * CH_B
CHUNK_B = 128
N_MEM = 256
H_X = 4
DH_X = 64
D_X = H_X * DH_X
D_FF = 3584
N_EXP = 8
ALPHA = (2 * DEPTH) ** 0.25
LN_EPS = 1e-5
RMS_EPS = 1e-6

LANES = 128
SUBLANES = 8
VMEM_LIMIT_BYTES = 56 * 1024 * 1024

A_Z_OFF = CONV_DIM
A_XQ_OFF = CONV_DIM + V_A
A_AB_OFF = A_XQ_OFF + D_X
A_COLS = A_AB_OFF + LANES
B_LANE = 8
B_XQ_OFF = 2 * D_B
B_COLS = 2 * D_B + D_X


def _params(*sem):
    return pltpu.CompilerParams(dimension_semantics=sem, vmem_limit_bytes=VMEM_LIMIT_BYTES)


def _split2(x):
    hi = x.astype(BF16)
    lo = (x - hi.astype(F32)).astype(BF16)
    return hi, lo


_NN = (((1,), (0,)), ((), ()))
_NT = (((1,), (1,)), ((), ()))
_TN = (((0,), (0,)), ((), ()))


def _dot(a, b, dims=_NN, passes=1):
    if passes == 1:
        return lax.dot_general(a.astype(BF16), b.astype(BF16), dims, preferred_element_type=F32)
    ah, al = _split2(a)
    bh, bl = _split2(b)
    dg = functools.partial(lax.dot_general, dimension_numbers=dims, preferred_element_type=F32)
    return dg(ah, bh) + (dg(ah, bl) + dg(al, bh))


def _dot_exact_lhs(lhs_bf16, x):
    x1 = x.astype(BF16)
    r1 = x - x1.astype(F32)
    x2 = r1.astype(BF16)
    x3 = (r1 - x2.astype(F32)).astype(BF16)
    dg = functools.partial(lax.dot_general, dimension_numbers=_NN, preferred_element_type=F32)
    return dg(lhs_bf16, x1) + (dg(lhs_bf16, x2) + dg(lhs_bf16, x3))


def _sigmoid(x):
    return 1.0 / (1.0 + jnp.exp(-x))


def _silu(x):
    return x * _sigmoid(x)


def _softplus(x):
    return jnp.maximum(x, 0.0) + jnp.log1p(jnp.exp(-jnp.abs(x)))


def _gelu_tanh(x):
    c = 0.7978845608028654
    return 0.5 * x * (1.0 + jnp.tanh(c * (x + 0.044715 * (x * x * x))))


def _layer_norm(y, g, b):
    mu = jnp.mean(y, axis=-1, keepdims=True)
    d = y - mu
    var = jnp.mean(d * d, axis=-1, keepdims=True)
    return d * lax.rsqrt(var + LN_EPS) * g + b


def _lane_pick(x, lane_iota, idx):
    return jnp.sum(jnp.where(lane_iota == idx, x, 0.0), axis=-1, keepdims=True)


def _mm_kernel(x_ref, w_ref, o_ref, *, chunk):
    xb = x_ref[...].astype(BF16)
    for c0 in range(0, o_ref.shape[1], chunk):
        o_ref[:, c0:c0 + chunk] = jnp.dot(xb, w_ref[:, c0:c0 + chunk], preferred_element_type=F32)


def _matmul(x, w, tm, chunk, name):
    m, k = x.shape
    n = w.shape[1]
    return pl.pallas_call(
        functools.partial(_mm_kernel, chunk=chunk),
        grid=(m // tm,),
        in_specs=[pl.BlockSpec((tm, k), lambda i: (i, 0)),
                  pl.BlockSpec((k, n), lambda i: (0, 0))],
        out_specs=pl.BlockSpec((tm, n), lambda i: (i, 0)),
        out_shape=jax.ShapeDtypeStruct((m, n), F32),
        compiler_params=_params("parallel"),
        name=name,
    )(x, w)


def _outproj_ln_kernel(mix_ref, xo_ref, x_ref, wm_ref, wx_ref, g_ref, b_ref, *rest):
    h = jnp.dot(mix_ref[...].astype(BF16), wm_ref[...], preferred_element_type=F32)
    h = h + jnp.dot(xo_ref[...].astype(BF16), wx_ref[...], preferred_element_type=F32)
    y = _layer_norm(ALPHA * x_ref[...] + h, g_ref[...], b_ref[...])
    if len(rest) == 1:
        (o_ref,) = rest
    else:
        rh_ref, rl_ref, o_ref, comb_ref, gate_ref, idx_ref = rest
        comb_ref[...], gate_ref[...], idx_ref[...] = _route_top2(y, rh_ref[...], rl_ref[...])
    o_ref[...] = y


def _outproj_ln(mix, xo, x, w_out, g, b, tm, router=None):
    m = x.shape[0]
    d_mix = mix.shape[1]
    n_blk = d_mix // D_X
    row_blk = lambda width: pl.BlockSpec((tm, width), lambda i: (i, 0))
    const = lambda shape, blk=0: pl.BlockSpec(shape, lambda i: (blk, 0))
    in_specs = [row_blk(d_mix), row_blk(D_X), row_blk(D_MODEL),
                const((d_mix, D_MODEL)), const((D_X, D_MODEL), n_blk),
                const((1, D_MODEL)), const((1, D_MODEL))]
    args = [mix, xo, x, w_out, w_out, g, b]
    out_specs = [row_blk(D_MODEL)]
    out_shape = [jax.ShapeDtypeStruct((m, D_MODEL), F32)]
    if router is not None:
        in_specs += [const((D_MODEL, LANES)), const((D_MODEL, LANES))]
        args += list(router)
        out_specs += [row_blk(LANES)] * 3
        out_shape += [jax.ShapeDtypeStruct((m, LANES), F32), jax.ShapeDtypeStruct((m, LANES), F32),
                      jax.ShapeDtypeStruct((m, LANES), jnp.int32)]
    out = pl.pallas_call(
        _outproj_ln_kernel,
        grid=(m // tm,),
        in_specs=in_specs,
        out_specs=out_specs,
        out_shape=out_shape,
        compiler_params=_params("parallel"),
        name="outproj_ln",
    )(*args)
    return out[0] if router is None else out


FFN_CHUNK = 256


def _swiglu_partial(xb, wg_ref, wu_ref, wd_ref, row_scale=None):
    y = None
    for c0 in range(0, wg_ref.shape[-1], FFN_CHUNK):
        hg = jnp.dot(xb, wg_ref[:, c0:c0 + FFN_CHUNK], preferred_element_type=F32)
        hu = jnp.dot(xb, wu_ref[:, c0:c0 + FFN_CHUNK], preferred_element_type=F32)
        h = _silu(hg) * hu
        if row_scale is not None:
            h = h * row_scale
        part = jnp.dot(h.astype(BF16), wd_ref[c0:c0 + FFN_CHUNK, :], preferred_element_type=F32)
        y = part if y is None else y + part
    return y


def _ffn_kernel(*refs, moe):
    if moe:
        x_ref, comb_ref, wg_ref, wu_ref, wd_ref, g_ref, b_ref, o_ref, xb_ref, acc_ref = refs
    else:
        x_ref, wg_ref, wu_ref, wd_ref, g_ref, b_ref, o_ref, xb_ref, acc_ref = refs
    e = pl.program_id(1)
    f = pl.program_id(2)
    first = jnp.logical_and(e == 0, f == 0)
    last = jnp.logical_and(e == pl.num_programs(1) - 1, f == pl.num_programs(2) - 1)

    @pl.when(first)
    def _():
        xb_ref[...] = x_ref[...].astype(BF16)
        acc_ref[...] = jnp.zeros_like(acc_ref)

    row_scale = None
    if moe:
        comb = comb_ref[...]
        lane = lax.broadcasted_iota(jnp.int32, comb.shape, 1)
        row_scale = _lane_pick(comb, lane, e)
    acc_ref[...] += _swiglu_partial(xb_ref[...], wg_ref, wu_ref, wd_ref, row_scale)

    @pl.when(last)
    def _():
        o_ref[...] = _layer_norm(ALPHA * x_ref[...] + acc_ref[...], g_ref[...], b_ref[...])


def _ffn_ln(x, comb, w_gu, w_down, layer, g, b, tm, tf):
    m = x.shape[0]
    moe = comb is not None
    n_e = w_gu.shape[1]
    n_f = D_FF // tf
    in_specs = [pl.BlockSpec((tm, D_MODEL), lambda i, e, f: (i, 0))]
    args = [x]
    if moe:
        in_specs.append(pl.BlockSpec((tm, LANES), lambda i, e, f: (i, 0)))
        args.append(comb)
    in_specs += [pl.BlockSpec((None, None, D_MODEL, tf), lambda i, e, f: (layer, e, 0, f)),
                 pl.BlockSpec((None, None, D_MODEL, tf), lambda i, e, f: (layer, e, 0, n_f + f)),
                 pl.BlockSpec((None, None, tf, D_MODEL), lambda i, e, f: (layer, e, f, 0)),
                 pl.BlockSpec((1, D_MODEL), lambda i, e, f: (0, 0)),
                 pl.BlockSpec((1, D_MODEL), lambda i, e, f: (0, 0))]
    args += [w_gu, w_gu, w_down, g, b]
    return pl.pallas_call(
        functools.partial(_ffn_kernel, moe=moe),
        grid=(m // tm, n_e, n_f),
        in_specs=in_specs,
        out_specs=pl.BlockSpec((tm, D_MODEL), lambda i, e, f: (i, 0)),
        out_shape=jax.ShapeDtypeStruct((m, D_MODEL), F32),
        scratch_shapes=[pltpu.VMEM((tm, D_MODEL), BF16), pltpu.VMEM((tm, D_MODEL), F32)],
        compiler_params=_params("parallel", "arbitrary", "arbitrary"),
        name="moe_ln" if moe else "ffn_ln",
    )(*args)


def _route_top2(x, wh, wl):
    xb = x.astype(BF16)
    lg = jnp.dot(xb, wh, preferred_element_type=F32) + jnp.dot(xb, wl, preferred_element_type=F32)
    lane = lax.broadcasted_iota(jnp.int32, lg.shape, 1).astype(F32)
    neg = -jnp.inf
    lg = jnp.where(lane < N_EXP, lg, neg)
    m1 = jnp.max(lg, axis=-1, keepdims=True)
    i1 = -jnp.max(jnp.where(lg == m1, -lane, -float(LANES)), axis=-1, keepdims=True)
    lg2 = jnp.where(lane == i1, neg, lg)
    m2 = jnp.max(lg2, axis=-1, keepdims=True)
    i2 = -jnp.max(jnp.where(lg2 == m2, -lane, -float(LANES)), axis=-1, keepdims=True)
    e2 = jnp.exp(m2 - m1)
    den = 1.0 + e2
    g1 = 1.0 / den
    g2 = e2 / den
    comb = jnp.where(lane == i1, g1, 0.0) + jnp.where(lane == i2, g2, 0.0)
    gates = jnp.where(lane == 0, g1, jnp.where(lane == 1, g2, 0.0))
    idx = jnp.where(lane == 0, i1, jnp.where(lane == 1, i2, 0.0)).astype(jnp.int32)
    return comb, gates, idx


MOE_TM = 512
FFN_TF = 1792
DISPATCH_TOKENS = 1024
COMBINE_TOKENS = 512


def _moe_plan(idx2, tm):
    t = idx2.shape[0]
    e_flat = idx2.reshape(-1)
    experts = jnp.arange(N_EXP, dtype=jnp.int32)
    onehot = (e_flat[:, None] == experts[None, :]).astype(jnp.int32)
    csum = jnp.cumsum(onehot, axis=0)
    rank = jnp.sum(csum * onehot, axis=1) - 1
    counts = csum[-1]
    ptiles = (counts + tm - 1) // tm
    tile_end = jnp.cumsum(ptiles)
    pstart = (tile_end - ptiles) * tm
    pos = jnp.sum(onehot * pstart[None, :], axis=1) + rank
    n_used = tile_end[-1]
    n_tiles = (2 * t) // tm + N_EXP
    jj = jnp.minimum(jnp.arange(n_tiles, dtype=jnp.int32), n_used - 1)
    tile_expert = jnp.sum((jj[:, None] >= tile_end[None, :]).astype(jnp.int32), axis=1)
    seg_base = jnp.concatenate([pstart + counts, (n_used * tm).reshape(1)])
    seg_len = jnp.concatenate([ptiles * tm - counts, ((n_tiles - n_used) * tm).reshape(1)])
    seg_end = jnp.cumsum(seg_len)
    j = jnp.arange(N_EXP * tm, dtype=jnp.int32)
    seg = jnp.sum((j[:, None] >= seg_end[None, :]).astype(jnp.int32), axis=1)
    seg_1h = (seg[:, None] == jnp.arange(N_EXP + 1, dtype=jnp.int32)[None, :]).astype(jnp.int32)
    pad_dst = j + jnp.sum(seg_1h * (seg_base - (seg_end - seg_len))[None, :], axis=1)
    return (pos.astype(jnp.int32), tile_expert.astype(jnp.int32), n_used.reshape(1).astype(jnp.int32),
            pad_dst.astype(jnp.int32), n_tiles)


ROW_TILE = (SUBLANES, LANES)
assert SUBLANES * LANES == D_MODEL


def _wait_rows(tiles_hbm_ref, n_rows, sem):
    rows = tiles_hbm_ref.at[pl.ds(0, n_rows)]
    pltpu.make_async_copy(rows, rows, sem).wait()


def _moe_dispatch_kernel(pos_ref, pad_dst_ref, x_ref, xs_ref, xr_ref, sem):
    tq = x_ref.shape[0]
    n_pad = pad_dst_ref.shape[1]
    xr_ref[...] = x_ref[...].reshape((tq,) + ROW_TILE)

    def send(grp, carry):
        for jr in range(SUBLANES):
            t = SUBLANES * grp + jr
            for k in range(2):
                pltpu.make_async_copy(xr_ref.at[t], xs_ref.at[pos_ref[0, 2 * t + k]], sem).start(priority=k)
        return carry

    lax.fori_loop(0, tq // SUBLANES, send, 0)

    def pad(grp, carry):
        for jr in range(SUBLANES):
            dst = pad_dst_ref[0, SUBLANES * grp + jr]
            pltpu.make_async_copy(xr_ref.at[jr], xs_ref.at[dst], sem).start(priority=jr % 2)
        return carry

    lax.fori_loop(0, n_pad // SUBLANES, pad, 0)
    _wait_rows(xs_ref, 2 * tq + n_pad, sem)


def _moe_dispatch(x, pos, pad_dst, n_rows_out):
    m = x.shape[0]
    tq = DISPATCH_TOKENS
    n_steps = m // tq
    n_pad = pad_dst.shape[0] // n_steps
    smem = functools.partial(pl.BlockSpec, memory_space=pltpu.SMEM)
    return pl.pallas_call(
        _moe_dispatch_kernel,
        grid=(n_steps,),
        in_specs=[smem((None, 1, 2 * tq), lambda i: (i, 0, 0)),
                  smem((None, 1, n_pad), lambda i: (i, 0, 0)),
                  pl.BlockSpec((tq, D_MODEL), lambda i: (i, 0))],
        out_specs=pl.BlockSpec(memory_space=pl.ANY),
        out_shape=jax.ShapeDtypeStruct((n_rows_out,) + ROW_TILE, F32),
        scratch_shapes=[pltpu.VMEM((tq,) + ROW_TILE, F32), pltpu.SemaphoreType.DMA(())],
        compiler_params=_params("arbitrary"),
        name="moe_dispatch",
    )(pos.reshape(n_steps, 1, 2 * tq), pad_dst.reshape(n_steps, 1, n_pad), x)


def _moe_ffn_kernel(te_ref, nu_ref, x_ref, wg_ref, wu_ref, wd_ref, o_ref, xb_ref, acc_ref):
    p = pl.program_id(0)
    f = pl.program_id(1)
    tm = xb_ref.shape[0]

    @pl.when(p < nu_ref[0])
    def _():
        @pl.when(f == 0)
        def _():
            xb_ref[...] = x_ref[...].reshape(tm, D_MODEL).astype(BF16)

        y = _swiglu_partial(xb_ref[...], wg_ref, wu_ref, wd_ref)
        last = pl.num_programs(1) - 1

        @pl.when(f == 0)
        def _():
            acc_ref[...] = y

        @pl.when(jnp.logical_and(f != 0, f != last))
        def _():
            acc_ref[...] += y

        @pl.when(f == last)
        def _():
            o_ref[...] = (acc_ref[...] + y).reshape((tm,) + ROW_TILE)

    @pl.when(jnp.logical_and(p >= nu_ref[0], f == 0))
    def _():
        o_ref[...] = jnp.zeros_like(o_ref)


def _moe_ffn(xs, tile_expert, n_used, w_gu, w_down, layer, tm, tf):
    n_tiles = xs.shape[0] // tm
    n_f = D_FF // tf
    assert n_f >= 2
    tile = lambda p, nu: jnp.minimum(p, nu[0] - 1)
    col = lambda p, f, nu: jnp.where(p < nu[0], f, n_f - 1)
    grid_spec = pltpu.PrefetchScalarGridSpec(
        num_scalar_prefetch=2,
        grid=(n_tiles, n_f),
        in_specs=[pl.BlockSpec((tm,) + ROW_TILE, lambda p, f, te, nu: (tile(p, nu), 0, 0)),
                  pl.BlockSpec((None, None, D_MODEL, tf),
                               lambda p, f, te, nu: (layer, te[p], 0, col(p, f, nu))),
                  pl.BlockSpec((None, None, D_MODEL, tf),
                               lambda p, f, te, nu: (layer, te[p], 0, n_f + col(p, f, nu))),
                  pl.BlockSpec((None, None, tf, D_MODEL),
                               lambda p, f, te, nu: (layer, te[p], col(p, f, nu), 0))],
        out_specs=pl.BlockSpec((tm,) + ROW_TILE, lambda p, f, te, nu: (p, 0, 0)),
        scratch_shapes=[pltpu.VMEM((tm, D_MODEL), BF16), pltpu.VMEM((tm, D_MODEL), F32)],
    )
    return pl.pallas_call(
        _moe_ffn_kernel,
        grid_spec=grid_spec,
        out_shape=jax.ShapeDtypeStruct(xs.shape, F32),
        compiler_params=_params("arbitrary", "arbitrary"),
        name="moe_ffn",
    )(tile_expert, n_used, xs, w_gu, w_gu, w_down)


def _moe_combine_ln_kernel(pos_ref, pos_next_ref, gate_ref, x_ref, ye_ref, g_ref, b_ref, o_ref,
                           buf_ref, sem):
    i = pl.program_id(0)
    tq = x_ref.shape[0]
    slot = lax.rem(i, 2)

    def start_fetch(rows_ref, to_slot):
        def fetch(grp, carry):
            for jr in range(SUBLANES):
                t = SUBLANES * grp + jr
                for k in range(2):
                    pltpu.make_async_copy(ye_ref.at[rows_ref[0, 2 * t + k]], buf_ref.at[to_slot, k * tq + t],
                                          sem.at[to_slot]).start(priority=k)
            return carry

        lax.fori_loop(0, tq // SUBLANES, fetch, 0)

    @pl.when(i == 0)
    def _():
        start_fetch(pos_ref, 0)

    @pl.when(i + 1 < pl.num_programs(0))
    def _():
        start_fetch(pos_next_ref, 1 - slot)

    _wait_rows(ye_ref, 2 * tq, sem.at[slot])
    gates = gate_ref[...]
    lane = lax.broadcasted_iota(jnp.int32, gates.shape, 1)
    y0 = buf_ref[slot, 0:tq].reshape(tq, D_MODEL)
    y1 = buf_ref[slot, tq:2 * tq].reshape(tq, D_MODEL)
    y = _lane_pick(gates, lane, 0) * y0 + _lane_pick(gates, lane, 1) * y1
    o_ref[...] = _layer_norm(ALPHA * x_ref[...] + y, g_ref[...], b_ref[...])


def _moe_combine_ln(x, gates, pos, ye, g, b):
    m = x.shape[0]
    tq = COMBINE_TOKENS
    n_steps = m // tq
    pos3 = pos.reshape(n_steps, 1, 2 * tq)
    smem = functools.partial(pl.BlockSpec, memory_space=pltpu.SMEM)
    return pl.pallas_call(
        _moe_combine_ln_kernel,
        grid=(n_steps,),
        in_specs=[smem((None, 1, 2 * tq), lambda i: (i, 0, 0)),
                  smem((None, 1, 2 * tq), lambda i: (jnp.minimum(i + 1, n_steps - 1), 0, 0)),
                  pl.BlockSpec((tq, LANES), lambda i: (i, 0)),
                  pl.BlockSpec((tq, D_MODEL), lambda i: (i, 0)),
                  pl.BlockSpec(memory_space=pl.ANY),
                  pl.BlockSpec((1, D_MODEL), lambda i: (0, 0)),
                  pl.BlockSpec((1, D_MODEL), lambda i: (0, 0))],
        out_specs=pl.BlockSpec((tq, D_MODEL), lambda i: (i, 0)),
        out_shape=jax.ShapeDtypeStruct((m, D_MODEL), F32),
        scratch_shapes=[pltpu.VMEM((2, 2 * tq) + ROW_TILE, F32), pltpu.SemaphoreType.DMA((2,))],
        compiler_params=_params("arbitrary"),
        name="moe_combine_ln",
    )(pos3, pos3, gates, x, ye, g, b)


def _moe_routed_ln(x, gates, idx, w_gu, w_down, layer, g, b):
    pos, tile_expert, n_used, pad_dst, n_tiles = _moe_plan(idx[:, :2], MOE_TM)
    xs = _moe_dispatch(x, pos, pad_dst, n_tiles * MOE_TM)
    ye = _moe_ffn(xs, tile_expert, n_used, w_gu, w_down, layer, MOE_TM, FFN_TF)
    return _moe_combine_ln(x, gates, pos, ye, g, b)


GDN_PASSES_QK = 1
GDN_PASSES_SOLVE = 1
GDN_PASSES_STATE = 1
INV_BLOCK_SHIFT = 4
GDN_SEQS = 4


def _gdn_chunk_kernel(qkv_ref, z_ref, ab_ref, cw_ref, alog_ref, dtb_ref, nw_ref, hist0_ref, s0_ref,
                      mix_ref, sout_ref, hout_ref, xh_ref, s_ref):
    n = pl.program_id(1)
    c = GDN_CHUNK
    hrows = SUBLANES
    n_seq = qkv_ref.shape[0]
    items = [(b, h) for b in range(n_seq) for h in range(H_A)]

    @pl.when(n == 0)
    def _():
        xh_ref[:, 0:hrows, :] = hist0_ref[...]
        s_ref[...] = s0_ref[...]

    xh_ref[:, hrows:hrows + c, :] = qkv_ref[...]

    row = lax.broadcasted_iota(jnp.int32, (c, c), 0)
    col = lax.broadcasted_iota(jnp.int32, (c, c), 1)
    causal = row >= col
    strict = row > col
    eye = jnp.where(row == col, 1.0, 0.0)
    blockdiag = (row >> INV_BLOCK_SHIFT) == (col >> INV_BLOCK_SHIFT)
    tril_ones = jnp.where(causal, 1.0, 0.0).astype(BF16)
    lane = lax.broadcasted_iota(jnp.int32, (c, LANES), 1)
    mm_s = functools.partial(_dot, passes=GDN_PASSES_SOLVE)

    beta_all, gcum_all, gcum_t = [], [], []
    for b in range(n_seq):
        ab = ab_ref[b]
        g_all = -jnp.exp(alog_ref[...]) * _softplus(ab + dtb_ref[...])
        beta_all.append(_sigmoid(ab))
        gc = _dot_exact_lhs(tril_ones, g_all)
        gcum_all.append(gc)
        gcum_t.append(gc.T)

    def conv_silu(b, c0):
        acc = xh_ref[b, hrows - 3:hrows - 3 + c, c0:c0 + LANES] * cw_ref[0:1, c0:c0 + LANES]
        for j in range(1, CONV_W):
            acc = acc + (xh_ref[b, hrows - 3 + j:hrows - 3 + j + c, c0:c0 + LANES]
                         * cw_ref[j:j + 1, c0:c0 + LANES])
        return _silu(acc)

    q, k, v, kb, egc, gc_col, gc_last, beta_col, decay = {}, {}, {}, {}, {}, {}, {}, {}, {}
    for it in items:
        b, h = it
        qi = conv_silu(b, h * DK_A)
        ki = conv_silu(b, QK_A + h * DK_A)
        v[it] = conv_silu(b, 2 * QK_A + h * DV_A)
        q[it] = qi * lax.rsqrt(jnp.sum(qi * qi, axis=-1, keepdims=True) + RMS_EPS) * (DK_A ** -0.5)
        k[it] = ki * lax.rsqrt(jnp.sum(ki * ki, axis=-1, keepdims=True) + RMS_EPS)
        gc_col[it] = _lane_pick(gcum_all[b], lane, h)
        beta_col[it] = _lane_pick(beta_all[b], lane, B_LANE + h)
        gc_last[it] = gc_col[it][c - 1:c, :]
        decay[it] = jnp.where(causal, jnp.exp(gc_col[it] - gcum_t[b][h:h + 1, :c]), 0.0)
        egc[it] = jnp.exp(gc_col[it])
        kb[it] = k[it] * beta_col[it]

    kk = {it: _dot(jnp.concatenate([kb[it], q[it]], axis=0), k[it], _NT, GDN_PASSES_QK) for it in items}
    a_mat = {it: jnp.where(strict, kk[it][:c] * decay[it], 0.0) for it in items}
    attn = {it: kk[it][c:] * decay[it] for it in items}

    a_d = {it: jnp.where(blockdiag, a_mat[it], 0.0) for it in items}
    a_n = {it: jnp.where(blockdiag, 0.0, a_mat[it]) for it in items}
    p2 = {it: mm_s(a_d[it], a_d[it]) for it in items}
    e_d = {it: p2[it] - a_d[it] - mm_s(a_d[it], p2[it]) for it in items}
    p4 = {it: mm_s(p2[it], p2[it]) for it in items}
    e_d = {it: e_d[it] + p4[it] + mm_s(e_d[it], p4[it]) for it in items}
    p8 = {it: mm_s(p4[it], p4[it]) for it in items}
    e_d = {it: e_d[it] + p8[it] + mm_s(e_d[it], p8[it]) for it in items}
    m_blk = {it: a_n[it] + mm_s(e_d[it], a_n[it]) for it in items}
    rhs = {it: jnp.concatenate([v[it] * beta_col[it], kb[it] * egc[it]], axis=1) for it in items}
    rhs = {it: rhs[it] + mm_s(e_d[it], rhs[it]) for it in items}
    m2 = {it: mm_s(m_blk[it], m_blk[it]) for it in items}
    f_y = {it: m2[it] - m_blk[it] - mm_s(m_blk[it], m2[it]) for it in items}
    sol = {it: rhs[it] + mm_s(f_y[it], rhs[it]) for it in items}

    s_old = {it: s_ref[it[0], it[1]] for it in items}
    wq = {it: _dot(jnp.concatenate([sol[it][:, DV_A:], q[it] * egc[it]], axis=0), s_old[it],
                   _NN, GDN_PASSES_STATE) for it in items}
    v_new = {it: sol[it][:, :DV_A] - wq[it][:c] for it in items}
    o = {it: wq[it][c:] + _dot(attn[it], v_new[it], _NN, GDN_PASSES_STATE) for it in items}
    for it in items:
        k_dec = k[it] * jnp.exp(gc_last[it] - gc_col[it])
        s_ref[it[0], it[1]] = (s_old[it] * jnp.exp(gc_last[it])
                               + _dot(k_dec, v_new[it], _TN, GDN_PASSES_STATE))
    for it in items:
        b, h = it
        oi = o[it]
        oi = oi * lax.rsqrt(jnp.mean(oi * oi, axis=-1, keepdims=True) + RMS_EPS) * nw_ref[...]
        mix_ref[b, :, h * DV_A:(h + 1) * DV_A] = oi * _silu(z_ref[b, :, h * DV_A:(h + 1) * DV_A])

    xh_ref[:, 0:hrows, :] = xh_ref[:, c:c + hrows, :]

    @pl.when(n == pl.num_programs(1) - 1)
    def _():
        sout_ref[...] = s_ref[...]
        hout_ref[...] = xh_ref[:, c:c + hrows, :]


def _gdn_chunked(proj, conv_w, alog_row, dtb_row, nw_row, hist0, s0):
    bsz, t, _ = proj.shape
    c = GDN_CHUNK
    nb = GDN_SEQS
    n_chunks = t // c
    row_spec = lambda width, blk: pl.BlockSpec((nb, c, width), lambda b, n: (b, n, blk))
    full2 = lambda shape: pl.BlockSpec(shape, lambda b, n: (0, 0))
    return pl.pallas_call(
        _gdn_chunk_kernel,
        grid=(bsz // nb, n_chunks),
        in_specs=[row_spec(CONV_DIM, 0),
                  row_spec(V_A, A_Z_OFF // V_A),
                  row_spec(LANES, A_AB_OFF // LANES),
                  full2((CONV_W, CONV_DIM)),
                  full2((1, LANES)), full2((1, LANES)), full2((1, DV_A)),
                  pl.BlockSpec((nb, SUBLANES, CONV_DIM), lambda b, n: (b, 0, 0)),
                  pl.BlockSpec((nb, H_A, DK_A, DV_A), lambda b, n: (b, 0, 0, 0))],
        out_specs=[pl.BlockSpec((nb, c, V_A), lambda b, n: (b, n, 0)),
                   pl.BlockSpec((nb, H_A, DK_A, DV_A), lambda b, n: (b, 0, 0, 0)),
                   pl.BlockSpec((nb, SUBLANES, CONV_DIM), lambda b, n: (b, 0, 0))],
        out_shape=[jax.ShapeDtypeStruct((bsz, t, V_A), F32),
                   jax.ShapeDtypeStruct((bsz, H_A, DK_A, DV_A), F32),
                   jax.ShapeDtypeStruct((bsz, SUBLANES, CONV_DIM), F32)],
        scratch_shapes=[pltpu.VMEM((nb, SUBLANES + c, CONV_DIM), F32),
                        pltpu.VMEM((nb, H_A, DK_A, DV_A), F32)],
        compiler_params=_params("parallel", "arbitrary"),
        name="gdn_chunked",
    )(proj, proj, proj, conv_w, alog_row, dtb_row, nw_row, hist0, s0)


def _gdn_step_kernel(qkv_ref, z_ref, ab_ref, cw_ref, alog_ref, dtb_ref, nw_ref, hist_ref, s_ref,
                     mix_ref, sout_ref, hout_ref, o_scr):
    rows = qkv_ref.shape[0]
    new = qkv_ref[...]
    y = hist_ref[0] * cw_ref[0:1, :]
    for j in range(1, CONV_W - 1):
        y = y + hist_ref[j] * cw_ref[j:j + 1, :]
    y = y + new * cw_ref[CONV_W - 1:CONV_W, :]
    qkv = _silu(y)
    for j in range(CONV_W - 2):
        hout_ref[j] = hist_ref[j + 1]
    hout_ref[CONV_W - 2] = new

    ab = ab_ref[...]
    lane = lax.broadcasted_iota(jnp.int32, ab.shape, 1)
    g_all = -jnp.exp(alog_ref[...]) * _softplus(ab + dtb_ref[...])
    beta_all = _sigmoid(ab)
    ri = lax.broadcasted_iota(jnp.int32, (DK_A, DK_A), 0)
    ci = lax.broadcasted_iota(jnp.int32, (DK_A, DK_A), 1)
    diag = ri == ci

    def as_column(r):
        return jnp.sum(jnp.where(diag, r, 0.0), axis=-1, keepdims=True)

    q, k, v, eg, beta = [], [], [], [], []
    for h in range(H_A):
        qh = qkv[:, h * DK_A:(h + 1) * DK_A]
        kh = qkv[:, QK_A + h * DK_A:QK_A + (h + 1) * DK_A]
        v.append(qkv[:, 2 * QK_A + h * DV_A:2 * QK_A + (h + 1) * DV_A])
        q.append(qh * lax.rsqrt(jnp.sum(qh * qh, axis=-1, keepdims=True) + RMS_EPS) * (DK_A ** -0.5))
        k.append(kh * lax.rsqrt(jnp.sum(kh * kh, axis=-1, keepdims=True) + RMS_EPS))
        eg.append(jnp.exp(_lane_pick(g_all, lane, h)))
        beta.append(_lane_pick(beta_all, lane, B_LANE + h))

    for h in range(H_A):
        seqs = range(rows)
        k_col = [as_column(k[h][s:s + 1, :]) for s in seqs]
        q_col = [as_column(q[h][s:s + 1, :]) for s in seqs]
        st = [s_ref[s, h] * eg[h][s:s + 1, :] for s in seqs]
        ks = [jnp.sum(k_col[s] * st[s], axis=0, keepdims=True) for s in seqs]
        u = [beta[h][s:s + 1, :] * (v[h][s:s + 1, :] - ks[s]) for s in seqs]
        st = [st[s] + k_col[s] * u[s] for s in seqs]
        for s in seqs:
            sout_ref[s, h] = st[s]
            o_scr[s:s + 1, h * DV_A:(h + 1) * DV_A] = jnp.sum(q_col[s] * st[s], axis=0, keepdims=True)

    for h in range(H_A):
        o = o_scr[:, h * DV_A:(h + 1) * DV_A]
        o = o * lax.rsqrt(jnp.mean(o * o, axis=-1, keepdims=True) + RMS_EPS) * nw_ref[...]
        mix_ref[:, h * DV_A:(h + 1) * DV_A] = o * _silu(z_ref[:, h * DV_A:(h + 1) * DV_A])


def _gdn_step(proj, conv_w, alog_row, dtb_row, nw_row, hist_t, s0, layer):
    bsz = proj.shape[0]
    rows = SUBLANES
    full1 = lambda shape: pl.BlockSpec(shape, lambda i: (0, 0))
    return pl.pallas_call(
        _gdn_step_kernel,
        grid=(bsz // rows,),
        in_specs=[pl.BlockSpec((rows, CONV_DIM), lambda i: (i, 0)),
                  pl.BlockSpec((rows, V_A), lambda i: (i, A_Z_OFF // V_A)),
                  pl.BlockSpec((rows, LANES), lambda i: (i, A_AB_OFF // LANES)),
                  full1((CONV_W, CONV_DIM)),
                  full1((1, LANES)), full1((1, LANES)), full1((1, DV_A)),
                  pl.BlockSpec((None, CONV_W - 1, rows, CONV_DIM), lambda i: (layer, 0, i, 0)),
                  pl.BlockSpec((None, rows, H_A, DK_A, DV_A), lambda i: (layer, i, 0, 0, 0))],
        out_specs=[pl.BlockSpec((rows, V_A), lambda i: (i, 0)),
                   pl.BlockSpec((rows, H_A, DK_A, DV_A), lambda i: (i, 0, 0, 0)),
                   pl.BlockSpec((CONV_W - 1, rows, CONV_DIM), lambda i: (0, i, 0))],
        out_shape=[jax.ShapeDtypeStruct((bsz, V_A), F32),
                   jax.ShapeDtypeStruct((bsz, H_A, DK_A, DV_A), F32),
                   jax.ShapeDtypeStruct((CONV_W - 1, bsz, CONV_DIM), F32)],
        scratch_shapes=[pltpu.VMEM((rows, V_A), F32)],
        compiler_params=_params("parallel"),
        name="gdn_step",
    )(proj, proj, proj, conv_w, alog_row, dtb_row, nw_row, hist_t, s0)


SGU_CHUNKS = 4


def _sgu_chunk_kernel(u_ref, v_ref, g_ref, b_ref, ws_ref, bs_ref, mix_ref):
    cb = CHUNK_B
    row = lax.broadcasted_iota(jnp.int32, (cb, cb), 0)
    col = lax.broadcasted_iota(jnp.int32, (cb, cb), 1)
    ws = [jnp.where(row >= col, ws_ref[gi], 0.0).astype(BF16) for gi in range(G_B)]
    for r0 in range(0, u_ref.shape[0], cb):
        u = _gelu_tanh(u_ref[r0:r0 + cb, :])
        v = _layer_norm(_gelu_tanh(v_ref[r0:r0 + cb, :]), g_ref[...], b_ref[...])
        for gi in range(G_B):
            mixed = _dot(ws[gi], v[:, gi * CH_B:(gi + 1) * CH_B]) + bs_ref[gi]
            mix_ref[r0:r0 + cb, gi * CH_B:(gi + 1) * CH_B] = u[:, gi * CH_B:(gi + 1) * CH_B] * mixed


def _sgu_chunked(proj, ln_g, ln_b, w_s, bs_b):
    bsz, t, _ = proj.shape
    c = CHUNK_B * SGU_CHUNKS
    return pl.pallas_call(
        _sgu_chunk_kernel,
        grid=(bsz, t // c),
        in_specs=[pl.BlockSpec((None, c, D_B), lambda b, n: (b, n, 0)),
                  pl.BlockSpec((None, c, D_B), lambda b, n: (b, n, 1)),
                  pl.BlockSpec((1, D_B), lambda b, n: (0, 0)),
                  pl.BlockSpec((1, D_B), lambda b, n: (0, 0)),
                  pl.BlockSpec((G_B, CHUNK_B, CHUNK_B), lambda b, n: (0, 0, 0)),
                  pl.BlockSpec((G_B, CHUNK_B, CH_B), lambda b, n: (0, 0, 0))],
        out_specs=pl.BlockSpec((None, c, D_B), lambda b, n: (b, n, 0)),
        out_shape=jax.ShapeDtypeStruct((bsz, t, D_B), F32),
        compiler_params=_params("parallel", "parallel"),
        name="sgu_chunked",
    )(proj, proj, ln_g, ln_b, w_s, bs_b)


def _sgu_first_kernel(u_ref, v_ref, g_ref, b_ref, w00_ref, b0_ref, mix_ref, vout_ref):
    u = _gelu_tanh(u_ref[...])
    v = _layer_norm(_gelu_tanh(v_ref[...]), g_ref[...], b_ref[...])
    vout_ref[...] = v
    mix_ref[...] = u * (w00_ref[...] * v + b0_ref[...])


def _sgu_first(proj, ln_g, ln_b, w00_row, b0_row):
    bsz = proj.shape[0]
    row = lambda: pl.BlockSpec((1, D_B), lambda i: (0, 0))
    return pl.pallas_call(
        _sgu_first_kernel,
        grid=(1,),
        in_specs=[pl.BlockSpec((bsz, D_B), lambda i: (0, 0)),
                  pl.BlockSpec((bsz, D_B), lambda i: (0, 1)),
                  row(), row(), row(), row()],
        out_specs=[pl.BlockSpec((bsz, D_B), lambda i: (0, 0)),
                   pl.BlockSpec((bsz, D_B), lambda i: (0, 0))],
        out_shape=[jax.ShapeDtypeStruct((bsz, D_B), F32),
                   jax.ShapeDtypeStruct((bsz, D_B), F32)],
        compiler_params=_params("arbitrary"),
        name="sgu_first",
    )(proj, proj, ln_g, ln_b, w00_row, b0_row)


def _head_masks(shape):
    lane = lax.broadcasted_iota(jnp.int32, shape, len(shape) - 1)
    return [(lane // DH_X) == h for h in range(H_X)]


def _xattn_kernel(q_ref, k_ref, v_ref, o_ref):
    q = q_ref[...]
    kb = k_ref[...].astype(BF16)
    vb = v_ref[...].astype(BF16)
    masks = _head_masks(q.shape)
    heads = range(H_X)
    s = [_dot(jnp.where(masks[h], q, 0.0), kb, _NT) * (DH_X ** -0.5) for h in heads]
    p = [jnp.exp(s[h] - jnp.max(s[h], axis=-1, keepdims=True)) for h in heads]
    p = [p[h] / jnp.sum(p[h], axis=-1, keepdims=True) for h in heads]
    pv = [_dot(p[h], vb) for h in heads]
    out = jnp.where(masks[0], pv[0], 0.0)
    for h in range(1, H_X):
        out = jnp.where(masks[h], pv[h], out)
    o_ref[...] = out


def _xattn(proj, xq_blk, mem_k, mem_v, layer, tq):
    bsz, t, _ = proj.shape
    return pl.pallas_call(
        _xattn_kernel,
        grid=(bsz, t // tq),
        in_specs=[pl.BlockSpec((None, tq, D_X), lambda b, i: (b, i, xq_blk)),
                  pl.BlockSpec((None, None, N_MEM, D_X), lambda b, i: (layer, b, 0, 0)),
                  pl.BlockSpec((None, None, N_MEM, D_X), lambda b, i: (layer, b, 0, 0))],
        out_specs=pl.BlockSpec((None, tq, D_X), lambda b, i: (b, i, 0)),
        out_shape=jax.ShapeDtypeStruct((bsz, t, D_X), F32),
        compiler_params=_params("parallel", "parallel"),
        name="xattn",
    )(proj, mem_k, mem_v)


def _xattn_step_kernel(q_ref, k_ref, v_ref, o_ref):
    rows = q_ref.shape[0]
    q = q_ref[...]
    hrow = lax.broadcasted_iota(jnp.int32, (SUBLANES, D_X), 0)
    hlane = lax.broadcasted_iota(jnp.int32, (SUBLANES, D_X), 1) // DH_X
    sel = hrow == hlane
    seqs = range(rows)
    sc = [_dot(jnp.where(sel, q[s:s + 1, :], 0.0), k_ref[s], _NT) * (DH_X ** -0.5) for s in seqs]
    p = [jnp.exp(sc[s] - jnp.max(sc[s], axis=-1, keepdims=True)) for s in seqs]
    p = [p[s] / jnp.sum(p[s], axis=-1, keepdims=True) for s in seqs]
    pv = [_dot(p[s], v_ref[s]) for s in seqs]
    for s in seqs:
        o_ref[s:s + 1, :] = jnp.sum(jnp.where(sel, pv[s], 0.0), axis=0, keepdims=True)


def _xattn_step(proj, xq_blk, mem_k, mem_v, layer):
    bsz = proj.shape[0]
    rows = SUBLANES
    return pl.pallas_call(
        _xattn_step_kernel,
        grid=(bsz // rows,),
        in_specs=[pl.BlockSpec((rows, D_X), lambda i: (i, xq_blk)),
                  pl.BlockSpec((None, rows, N_MEM, D_X), lambda i: (layer, i, 0, 0)),
                  pl.BlockSpec((None, rows, N_MEM, D_X), lambda i: (layer, i, 0, 0))],
        out_specs=pl.BlockSpec((rows, D_X), lambda i: (i, 0)),
        out_shape=jax.ShapeDtypeStruct((bsz, D_X), F32),
        compiler_params=_params("parallel"),
        name="xattn_step",
    )(proj, mem_k, mem_v)


def _pad_lanes(v):
    return jnp.zeros((1, LANES), F32).at[0, :v.shape[0]].set(v.astype(F32))


def _prep_weights(a_w_in, a_A_log, a_dt_bias, b_w_s, b_b_s, moe_router):
    qkvz = a_w_in[:, :, :A_XQ_OFF]
    a_cols = a_w_in[:, :, A_XQ_OFF:A_XQ_OFF + H_A]
    b_cols = a_w_in[:, :, A_XQ_OFF + H_A:A_XQ_OFF + 2 * H_A]
    xq = a_w_in[:, :, A_XQ_OFF + 2 * H_A:]
    ab = jnp.zeros(a_w_in.shape[:2] + (LANES,), F32)
    ab = ab.at[:, :, :H_A].set(a_cols).at[:, :, B_LANE:B_LANE + H_A].set(b_cols)
    a_w = jnp.concatenate([qkvz, xq, ab], axis=-1).astype(BF16)
    alog_rows = [_pad_lanes(a_A_log[j]) for j in range(a_A_log.shape[0])]
    dtb_rows = [_pad_lanes(a_dt_bias[j]) for j in range(a_dt_bias.shape[0])]
    bs_b = jnp.broadcast_to(b_b_s[..., None], b_b_s.shape + (CH_B,)).astype(F32)
    w00_rows = jnp.repeat(b_w_s[:, :, 0, 0], CH_B, axis=-1)[:, None, :]
    b0_rows = jnp.repeat(b_b_s[:, :, 0], CH_B, axis=-1)[:, None, :]
    r = jnp.zeros(moe_router.shape[:2] + (LANES,), F32).at[:, :, :N_EXP].set(moe_router)
    r_hi = r.astype(BF16)
    r_lo = (r - r_hi.astype(F32)).astype(BF16)
    return a_w, alog_rows, dtb_rows, bs_b, w00_rows, b0_rows, r_hi, r_lo


def kernel(x_prompt, x_sample, state_gdn, state_conv, cache_mem_k, cache_mem_v, mem_prompt,
           a_w_in, a_conv_w, a_A_log, a_dt_bias, a_norm_w,
           b_w_in, b_ln_g, b_ln_b, b_w_s, b_b_s,
           w_mem_kv, w_out, ln1_g, ln1_b, ln2_g, ln2_b,
           ffn_w_gu, ffn_w_down, moe_router, moe_w_gu, moe_w_down):
    n_p, t_p, _ = x_prompt.shape
    n_s = x_sample.shape[0]
    n_a = a_w_in.shape[0]

    a_w, alog_rows, dtb_rows, bs_b, w00_rows, b0_rows, r_hi, r_lo = _prep_weights(
        a_w_in, a_A_log, a_dt_bias, b_w_s, b_b_s, moe_router)
    ffn_gu = ffn_w_gu.astype(BF16)[:, None]
    ffn_dn = ffn_w_down.astype(BF16)[:, None]
    moe_gu = moe_w_gu.astype(BF16)
    moe_dn = moe_w_down.astype(BF16)
    row = lambda v: v.reshape(1, -1).astype(F32)

    w_kv = jnp.transpose(w_mem_kv, (1, 0, 2)).reshape(D_MODEL, DEPTH * 2 * D_X).astype(BF16)
    kv = _matmul(mem_prompt.reshape(n_p * N_MEM, D_MODEL), w_kv, 512, 1024, "mem_kv")
    kv = kv.reshape(n_p, N_MEM, DEPTH, 2, D_X)
    p_mem_k = jnp.transpose(kv[:, :, :, 0, :], (2, 0, 1, 3))
    p_mem_v = jnp.transpose(kv[:, :, :, 1, :], (2, 0, 1, 3))
    s_mem_k = cache_mem_k.reshape(DEPTH, n_s, N_MEM, D_X)
    s_mem_v = cache_mem_v.reshape(DEPTH, n_s, N_MEM, D_X)

    xp = x_prompt.reshape(n_p * t_p, D_MODEL)
    xs = x_sample.reshape(n_s, D_MODEL)
    p_hist0 = jnp.zeros((n_p, SUBLANES, CONV_DIM), F32)
    p_s0 = jnp.zeros((n_p, H_A, DK_A, DV_A), F32)
    s_hist_t = jnp.transpose(state_conv, (0, 2, 1, 3))

    p_gdn, p_conv, s_gdn, s_conv, s_sgu_v = [], [], [], [], []
    tm_p, tm_s = 1024, n_s
    for i in range(DEPTH):
        j = i // 2
        if i % 2 == 0:
            proj_p = _matmul(xp, a_w[j], 512, 1152, "in_proj_a")
            proj_s = _matmul(xs, a_w[j], tm_s, 1152, "in_proj_a_s")
            nw = row(a_norm_w[j])
            mix_p, sp, hp = _gdn_chunked(proj_p.reshape(n_p, t_p, A_COLS), a_conv_w[j],
                                         alog_rows[j], dtb_rows[j], nw, p_hist0, p_s0)
            mix_p = mix_p.reshape(n_p * t_p, V_A)
            p_gdn.append(sp)
            p_conv.append(hp[:, SUBLANES - (CONV_W - 1):, :])
            mix_s, ss, hs = _gdn_step(proj_s, a_conv_w[j], alog_rows[j], dtb_rows[j], nw,
                                      s_hist_t, state_gdn, j)
            s_gdn.append(ss)
            s_conv.append(jnp.transpose(hs, (1, 0, 2)))
            cols, xq_blk = A_COLS, A_XQ_OFF // D_X
        else:
            b_w = b_w_in[j].astype(BF16)
            proj_p = _matmul(xp, b_w, 512, 896, "in_proj_b")
            proj_s = _matmul(xs, b_w, tm_s, 896, "in_proj_b_s")
            mix_p = _sgu_chunked(proj_p.reshape(n_p, t_p, B_COLS), row(b_ln_g[j]), row(b_ln_b[j]),
                                 b_w_s[j], bs_b[j]).reshape(n_p * t_p, D_B)
            mix_s, v_s = _sgu_first(proj_s, row(b_ln_g[j]), row(b_ln_b[j]), w00_rows[j], b0_rows[j])
            s_sgu_v.append(v_s.reshape(n_s, 1, D_B))
            cols, xq_blk = B_COLS, B_XQ_OFF // D_X
        xo_p = _xattn(proj_p.reshape(n_p, t_p, cols), xq_blk, p_mem_k, p_mem_v, i, 512)
        xo_s = _xattn_step(proj_s, xq_blk, s_mem_k, s_mem_v, i)
        g1, b1, g2, b2 = row(ln1_g[i]), row(ln1_b[i]), row(ln2_g[i]), row(ln2_b[i])
        w_o = w_out[i].astype(BF16)
        xo_p = xo_p.reshape(n_p * t_p, D_X)
        if i % 2 == 0:
            xp = _outproj_ln(mix_p, xo_p, xp, w_o, g1, b1, 512)
            xs = _outproj_ln(mix_s, xo_s, xs, w_o, g1, b1, tm_s)
            xp = _ffn_ln(xp, None, ffn_gu, ffn_dn, j, g2, b2, tm_p, FFN_TF)
            xs = _ffn_ln(xs, None, ffn_gu, ffn_dn, j, g2, b2, tm_s, FFN_TF)
        else:
            router = (r_hi[j], r_lo[j])
            xp, _, gates_p, idx_p = _outproj_ln(mix_p, xo_p, xp, w_o, g1, b1, 512, router)
            xs, comb_s, _, _ = _outproj_ln(mix_s, xo_s, xs, w_o, g1, b1, tm_s, router)
            xp = _moe_routed_ln(xp, gates_p, idx_p, moe_gu, moe_dn, j, g2, b2)
            xs = _ffn_ln(xs, comb_s, moe_gu, moe_dn, j, g2, b2, tm_s, FFN_TF)

    mem_shape = (DEPTH, n_p, N_MEM, H_X, DH_X)
    return (xp.reshape(n_p, t_p, D_MODEL),
            xs.reshape(n_s, 1, D_MODEL),
            jnp.stack(p_gdn),
            jnp.stack(p_conv),
            p_mem_k.reshape(mem_shape),
            p_mem_v.reshape(mem_shape),
            jnp.stack(s_gdn),
            jnp.stack(s_conv),
            jnp.stack(s_sgu_v))
```

```python
import functools

import jax
import jax.numpy as jnp
from jax import lax
from jax.experimental import pallas as pl
from jax.experimental.pallas import tpu as pltpu

F32 = jnp.float32
BF16 = jnp.bfloat16

D_MODEL = 1024
DEPTH = 4
H_A = 6
DK_A = 128
DV_A = 128
QK_A = H_A * DK_A
V_A = H_A * DV_A
CONV_W = 4
CONV_DIM = 2 * QK_A + V_A
GDN_CHUNK = 64
G_B = 6
CH_B = 128
D_B = G_B * CH_B
CHUNK_B = 128
N_MEM = 256
H_X = 4
DH_X = 64
D_X = H_X * DH_X
D_FF = 3584
N_EXP = 8
ALPHA = (2 * DEPTH) ** 0.25
LN_EPS = 1e-5
RMS_EPS = 1e-6

LANES = 128
SUBLANES = 8
VMEM_LIMIT_BYTES = 56 * 1024 * 1024

A_Z_OFF = CONV_DIM
A_XQ_OFF = CONV_DIM + V_A
A_AB_OFF = A_XQ_OFF + D_X
A_COLS = A_AB_OFF + LANES
B_LANE = 8
B_XQ_OFF = 2 * D_B
B_COLS = 2 * D_B + D_X


def _params(*sem):
    return pltpu.CompilerParams(dimension_semantics=sem, vmem_limit_bytes=VMEM_LIMIT_BYTES)


def _split2(x):
    hi = x.astype(BF16)
    lo = (x - hi.astype(F32)).astype(BF16)
    return hi, lo


_NN = (((1,), (0,)), ((), ()))
_NT = (((1,), (1,)), ((), ()))
_TN = (((0,), (0,)), ((), ()))


def _dot(a, b, dims=_NN, passes=1):
    if passes == 1:
        return lax.dot_general(a.astype(BF16), b.astype(BF16), dims, preferred_element_type=F32)
    ah, al = _split2(a)
    bh, bl = _split2(b)
    dg = functools.partial(lax.dot_general, dimension_numbers=dims, preferred_element_type=F32)
    return dg(ah, bh) + (dg(ah, bl) + dg(al, bh))


def _dot_exact_lhs(lhs_bf16, x):
    x1 = x.astype(BF16)
    r1 = x - x1.astype(F32)
    x2 = r1.astype(BF16)
    x3 = (r1 - x2.astype(F32)).astype(BF16)
    dg = functools.partial(lax.dot_general, dimension_numbers=_NN, preferred_element_type=F32)
    return dg(lhs_bf16, x1) + (dg(lhs_bf16, x2) + dg(lhs_bf16, x3))


def _sigmoid(x):
    return 1.0 / (1.0 + jnp.exp(-x))


def _silu(x):
    return x * _sigmoid(x)


def _softplus(x):
    return jnp.maximum(x, 0.0) + jnp.log1p(jnp.exp(-jnp.abs(x)))


def _gelu_tanh(x):
    c = 0.7978845608028654
    return 0.5 * x * (1.0 + jnp.tanh(c * (x + 0.044715 * (x * x * x))))


def _layer_norm(y, g, b):
    mu = jnp.mean(y, axis=-1, keepdims=True)
    d = y - mu
    var = jnp.mean(d * d, axis=-1, keepdims=True)
    return d * lax.rsqrt(var + LN_EPS) * g + b


def _lane_pick(x, lane_iota, idx):
    return jnp.sum(jnp.where(lane_iota == idx, x, 0.0), axis=-1, keepdims=True)


def _mm_kernel(x_ref, w_ref, o_ref, *, chunk):
    xb = x_ref[...].astype(BF16)
    for c0 in range(0, o_ref.shape[1], chunk):
        o_ref[:, c0:c0 + chunk] = jnp.dot(xb, w_ref[:, c0:c0 + chunk], preferred_element_type=F32)


def _matmul(x, w, tm, chunk, name):
    m, k = x.shape
    n = w.shape[1]
    return pl.pallas_call(
        functools.partial(_mm_kernel, chunk=chunk),
        grid=(m // tm,),
        in_specs=[pl.BlockSpec((tm, k), lambda i: (i, 0)),
                  pl.BlockSpec((k, n), lambda i: (0, 0))],
        out_specs=pl.BlockSpec((tm, n), lambda i: (i, 0)),
        out_shape=jax.ShapeDtypeStruct((m, n), F32),
        compiler_params=_params("parallel"),
        name=name,
    )(x, w)


def _outproj_ln_kernel(mix_ref, xo_ref, x_ref, wm_ref, wx_ref, g_ref, b_ref, *rest):
    h = jnp.dot(mix_ref[...].astype(BF16), wm_ref[...], preferred_element_type=F32)
    h = h + jnp.dot(xo_ref[...].astype(BF16), wx_ref[...], preferred_element_type=F32)
    y = _layer_norm(ALPHA * x_ref[...] + h, g_ref[...], b_ref[...])
    if len(rest) == 1:
        (o_ref,) = rest
    else:
        rh_ref, rl_ref, o_ref, comb_ref, gate_ref, idx_ref = rest
        comb_ref[...], gate_ref[...], idx_ref[...] = _route_top2(y, rh_ref[...], rl_ref[...])
    o_ref[...] = y


def _outproj_ln(mix, xo, x, w_out, g, b, tm, router=None):
    m = x.shape[0]
    d_mix = mix.shape[1]
    n_blk = d_mix // D_X
    row_blk = lambda width: pl.BlockSpec((tm, width), lambda i: (i, 0))
    const = lambda shape, blk=0: pl.BlockSpec(shape, lambda i: (blk, 0))
    in_specs = [row_blk(d_mix), row_blk(D_X), row_blk(D_MODEL),
                const((d_mix, D_MODEL)), const((D_X, D_MODEL), n_blk),
                const((1, D_MODEL)), const((1, D_MODEL))]
    args = [mix, xo, x, w_out, w_out, g, b]
    out_specs = [row_blk(D_MODEL)]
    out_shape = [jax.ShapeDtypeStruct((m, D_MODEL), F32)]
    if router is not None:
        in_specs += [const((D_MODEL, LANES)), const((D_MODEL, LANES))]
        args += list(router)
        out_specs += [row_blk(LANES)] * 3
        out_shape += [jax.ShapeDtypeStruct((m, LANES), F32), jax.ShapeDtypeStruct((m, LANES), F32),
                      jax.ShapeDtypeStruct((m, LANES), jnp.int32)]
    out = pl.pallas_call(
        _outproj_ln_kernel,
        grid=(m // tm,),
        in_specs=in_specs,
        out_specs=out_specs,
        out_shape=out_shape,
        compiler_params=_params("parallel"),
        name="outproj_ln",
    )(*args)
    return out[0] if router is None else out


FFN_CHUNK = 256


def _swiglu_partial(xb, wg_ref, wu_ref, wd_ref, row_scale=None):
    y = None
    for c0 in range(0, wg_ref.shape[-1], FFN_CHUNK):
        hg = jnp.dot(xb, wg_ref[:, c0:c0 + FFN_CHUNK], preferred_element_type=F32)
        hu = jnp.dot(xb, wu_ref[:, c0:c0 + FFN_CHUNK], preferred_element_type=F32)
        h = _silu(hg) * hu
        if row_scale is not None:
            h = h * row_scale
        part = jnp.dot(h.astype(BF16), wd_ref[c0:c0 + FFN_CHUNK, :], preferred_element_type=F32)
        y = part if y is None else y + part
    return y


def _ffn_kernel(*refs, moe):
    if moe:
        x_ref, comb_ref, wg_ref, wu_ref, wd_ref, g_ref, b_ref, o_ref, xb_ref, acc_ref = refs
    else:
        x_ref, wg_ref, wu_ref, wd_ref, g_ref, b_ref, o_ref, xb_ref, acc_ref = refs
    e = pl.program_id(1)
    f = pl.program_id(2)
    first = jnp.logical_and(e == 0, f == 0)
    last = jnp.logical_and(e == pl.num_programs(1) - 1, f == pl.num_programs(2) - 1)

    @pl.when(first)
    def _():
        xb_ref[...] = x_ref[...].astype(BF16)
        acc_ref[...] = jnp.zeros_like(acc_ref)

    row_scale = None
    if moe:
        comb = comb_ref[...]
        lane = lax.broadcasted_iota(jnp.int32, comb.shape, 1)
        row_scale = _lane_pick(comb, lane, e)
    acc_ref[...] += _swiglu_partial(xb_ref[...], wg_ref, wu_ref, wd_ref, row_scale)

    @pl.when(last)
    def _():
        o_ref[...] = _layer_norm(ALPHA * x_ref[...] + acc_ref[...], g_ref[...], b_ref[...])


def _ffn_ln(x, comb, w_gu, w_down, layer, g, b, tm, tf):
    m = x.shape[0]
    moe = comb is not None
    n_e = w_gu.shape[1]
    n_f = D_FF // tf
    in_specs = [pl.BlockSpec((tm, D_MODEL), lambda i, e, f: (i, 0))]
    args = [x]
    if moe:
        in_specs.append(pl.BlockSpec((tm, LANES), lambda i, e, f: (i, 0)))
        args.append(comb)
    in_specs += [pl.BlockSpec((None, None, D_MODEL, tf), lambda i, e, f: (layer, e, 0, f)),
                 pl.BlockSpec((None, None, D_MODEL, tf), lambda i, e, f: (layer, e, 0, n_f + f)),
                 pl.BlockSpec((None, None, tf, D_MODEL), lambda i, e, f: (layer, e, f, 0)),
                 pl.BlockSpec((1, D_MODEL), lambda i, e, f: (0, 0)),
                 pl.BlockSpec((1, D_MODEL), lambda i, e, f: (0, 0))]
    args += [w_gu, w_gu, w_down, g, b]
    return pl.pallas_call(
        functools.partial(_ffn_kernel, moe=moe),
        grid=(m // tm, n_e, n_f),
        in_specs=in_specs,
        out_specs=pl.BlockSpec((tm, D_MODEL), lambda i, e, f: (i, 0)),
        out_shape=jax.ShapeDtypeStruct((m, D_MODEL), F32),
        scratch_shapes=[pltpu.VMEM((tm, D_MODEL), BF16), pltpu.VMEM((tm, D_MODEL), F32)],
        compiler_params=_params("parallel", "arbitrary", "arbitrary"),
        name="moe_ln" if moe else "ffn_ln",
    )(*args)


def _route_top2(x, wh, wl):
    xb = x.astype(BF16)
    lg = jnp.dot(xb, wh, preferred_element_type=F32) + jnp.dot(xb, wl, preferred_element_type=F32)
    lane = lax.broadcasted_iota(jnp.int32, lg.shape, 1).astype(F32)
    neg = -jnp.inf
    lg = jnp.where(lane < N_EXP, lg, neg)
    m1 = jnp.max(lg, axis=-1, keepdims=True)
    i1 = -jnp.max(jnp.where(lg == m1, -lane, -float(LANES)), axis=-1, keepdims=True)
    lg2 = jnp.where(lane == i1, neg, lg)
    m2 = jnp.max(lg2, axis=-1, keepdims=True)
    i2 = -jnp.max(jnp.where(lg2 == m2, -lane, -float(LANES)), axis=-1, keepdims=True)
    e2 = jnp.exp(m2 - m1)
    den = 1.0 + e2
    g1 = 1.0 / den
    g2 = e2 / den
    comb = jnp.where(lane == i1, g1, 0.0) + jnp.where(lane == i2, g2, 0.0)
    gates = jnp.where(lane == 0, g1, jnp.where(lane == 1, g2, 0.0))
    idx = jnp.where(lane == 0, i1, jnp.where(lane == 1, i2, 0.0)).astype(jnp.int32)
    return comb, gates, idx


MOE_TM = 512
FFN_TF = 1792
DISPATCH_TOKENS = 1024
COMBINE_TOKENS = 256


def _moe_plan(idx2, tm):
    t = idx2.shape[0]
    e_flat = idx2.reshape(-1)
    experts = jnp.arange(N_EXP, dtype=jnp.int32)
    onehot = (e_flat[:, None] == experts[None, :]).astype(jnp.int32)
    csum = jnp.cumsum(onehot, axis=0)
    rank = jnp.sum(csum * onehot, axis=1) - 1
    counts = csum[-1]
    ptiles = (counts + tm - 1) // tm
    tile_end = jnp.cumsum(ptiles)
    pstart = (tile_end - ptiles) * tm
    pos = jnp.sum(onehot * pstart[None, :], axis=1) + rank
    n_used = tile_end[-1]
    n_tiles = (2 * t) // tm + N_EXP
    jj = jnp.minimum(jnp.arange(n_tiles, dtype=jnp.int32), n_used - 1)
    tile_expert = jnp.sum((jj[:, None] >= tile_end[None, :]).astype(jnp.int32), axis=1)
    seg_base = jnp.concatenate([pstart + counts, (n_used * tm).reshape(1)])
    seg_len = jnp.concatenate([ptiles * tm - counts, ((n_tiles - n_used) * tm).reshape(1)])
    seg_end = jnp.cumsum(seg_len)
    j = jnp.arange(N_EXP * tm, dtype=jnp.int32)
    seg = jnp.sum((j[:, None] >= seg_end[None, :]).astype(jnp.int32), axis=1)
    seg_1h = (seg[:, None] == jnp.arange(N_EXP + 1, dtype=jnp.int32)[None, :]).astype(jnp.int32)
    pad_dst = j + jnp.sum(seg_1h * (seg_base - (seg_end - seg_len))[None, :], axis=1)
    return (pos.astype(jnp.int32), tile_expert.astype(jnp.int32), n_used.reshape(1).astype(jnp.int32),
            pad_dst.astype(jnp.int32), n_tiles)


ROW_TILE = (SUBLANES, LANES)
assert SUBLANES * LANES == D_MODEL


def _wait_rows(tiles_hbm_ref, n_rows, sem):
    rows = tiles_hbm_ref.at[pl.ds(0, n_rows)]
    pltpu.make_async_copy(rows, rows, sem).wait()


def _moe_dispatch_kernel(pos_ref, pad_dst_ref, x_ref, xs_ref, xr_ref, sem):
    tq = x_ref.shape[0]
    n_pad = pad_dst_ref.shape[1]
    xr_ref[...] = x_ref[...].reshape((tq,) + ROW_TILE)

    def send(grp, carry):
        for jr in range(SUBLANES):
            t = SUBLANES * grp + jr
            for k in range(2):
                pltpu.make_async_copy(xr_ref.at[t], xs_ref.at[pos_ref[0, 2 * t + k]], sem).start(priority=k)
        return carry

    lax.fori_loop(0, tq // SUBLANES, send, 0)

    def pad(grp, carry):
        for jr in range(SUBLANES):
            dst = pad_dst_ref[0, SUBLANES * grp + jr]
            pltpu.make_async_copy(xr_ref.at[jr], xs_ref.at[dst], sem).start(priority=jr % 2)
        return carry

    lax.fori_loop(0, n_pad // SUBLANES, pad, 0)
    _wait_rows(xs_ref, 2 * tq + n_pad, sem)


def _moe_dispatch(x, pos, pad_dst, n_rows_out):
    m = x.shape[0]
    tq = DISPATCH_TOKENS
    n_steps = m // tq
    n_pad = pad_dst.shape[0] // n_steps
    smem = functools.partial(pl.BlockSpec, memory_space=pltpu.SMEM)
    return pl.pallas_call(
        _moe_dispatch_kernel,
        grid=(n_steps,),
        in_specs=[smem((None, 1, 2 * tq), lambda i: (i, 0, 0)),
                  smem((None, 1, n_pad), lambda i: (i, 0, 0)),
                  pl.BlockSpec((tq, D_MODEL), lambda i: (i, 0))],
        out_specs=pl.BlockSpec(memory_space=pl.ANY),
        out_shape=jax.ShapeDtypeStruct((n_rows_out,) + ROW_TILE, F32),
        scratch_shapes=[pltpu.VMEM((tq,) + ROW_TILE, F32), pltpu.SemaphoreType.DMA(())],
        compiler_params=_params("arbitrary"),
        name="moe_dispatch",
    )(pos.reshape(n_steps, 1, 2 * tq), pad_dst.reshape(n_steps, 1, n_pad), x)


def _moe_ffn_kernel(te_ref, nu_ref, x_ref, wg_ref, wu_ref, wd_ref, o_ref, xb_ref, acc_ref):
    p = pl.program_id(0)
    f = pl.program_id(1)
    tm = xb_ref.shape[0]

    @pl.when(p < nu_ref[0])
    def _():
        @pl.when(f == 0)
        def _():
            xb_ref[...] = x_ref[...].reshape(tm, D_MODEL).astype(BF16)

        y = _swiglu_partial(xb_ref[...], wg_ref, wu_ref, wd_ref)
        last = pl.num_programs(1) - 1

        @pl.when(f == 0)
        def _():
            acc_ref[...] = y

        @pl.when(jnp.logical_and(f != 0, f != last))
        def _():
            acc_ref[...] += y

        @pl.when(f == last)
        def _():
            o_ref[...] = (acc_ref[...] + y).reshape((tm,) + ROW_TILE)

    @pl.when(jnp.logical_and(p >= nu_ref[0], f == 0))
    def _():
        o_ref[...] = jnp.zeros_like(o_ref)


def _moe_ffn(xs, tile_expert, n_used, w_gu, w_down, layer, tm, tf):
    n_tiles = xs.shape[0] // tm
    n_f = D_FF // tf
    assert n_f >= 2
    tile = lambda p, nu: jnp.minimum(p, nu[0] - 1)
    col = lambda p, f, nu: jnp.where(p < nu[0], f, n_f - 1)
    grid_spec = pltpu.PrefetchScalarGridSpec(
        num_scalar_prefetch=2,
        grid=(n_tiles, n_f),
        in_specs=[pl.BlockSpec((tm,) + ROW_TILE, lambda p, f, te, nu: (tile(p, nu), 0, 0)),
                  pl.BlockSpec((None, None, D_MODEL, tf),
                               lambda p, f, te, nu: (layer, te[p], 0, col(p, f, nu))),
                  pl.BlockSpec((None, None, D_MODEL, tf),
                               lambda p, f, te, nu: (layer, te[p], 0, n_f + col(p, f, nu))),
                  pl.BlockSpec((None, None, tf, D_MODEL),
                               lambda p, f, te, nu: (layer, te[p], col(p, f, nu), 0))],
        out_specs=pl.BlockSpec((tm,) + ROW_TILE, lambda p, f, te, nu: (p, 0, 0)),
        scratch_shapes=[pltpu.VMEM((tm, D_MODEL), BF16), pltpu.VMEM((tm, D_MODEL), F32)],
    )
    return pl.pallas_call(
        _moe_ffn_kernel,
        grid_spec=grid_spec,
        out_shape=jax.ShapeDtypeStruct(xs.shape, F32),
        compiler_params=_params("arbitrary", "arbitrary"),
        name="moe_ffn",
    )(tile_expert, n_used, xs, w_gu, w_gu, w_down)


def _moe_combine_ln_kernel(pos_ref, pos_next_ref, gate_ref, x_ref, ye_ref, g_ref, b_ref, o_ref,
                           buf_ref, sem):
    i = pl.program_id(0)
    tq = x_ref.shape[0]
    slot = lax.rem(i, 2)

    def start_fetch(rows_ref, to_slot):
        def fetch(grp, carry):
            for jr in range(SUBLANES):
                t = SUBLANES * grp + jr
                for k in range(2):
                    pltpu.make_async_copy(ye_ref.at[rows_ref[0, 2 * t + k]], buf_ref.at[to_slot, k * tq + t],
                                          sem.at[to_slot]).start(priority=k)
            return carry

        lax.fori_loop(0, tq // SUBLANES, fetch, 0)

    @pl.when(i == 0)
    def _():
        start_fetch(pos_ref, 0)

    @pl.when(i + 1 < pl.num_programs(0))
    def _():
        start_fetch(pos_next_ref, 1 - slot)

    _wait_rows(ye_ref, 2 * tq, sem.at[slot])
    gates = gate_ref[...]
    lane = lax.broadcasted_iota(jnp.int32, gates.shape, 1)
    y0 = buf_ref[slot, 0:tq].reshape(tq, D_MODEL)
    y1 = buf_ref[slot, tq:2 * tq].reshape(tq, D_MODEL)
    y = _lane_pick(gates, lane, 0) * y0 + _lane_pick(gates, lane, 1) * y1
    o_ref[...] = _layer_norm(ALPHA * x_ref[...] + y, g_ref[...], b_ref[...])


def _moe_combine_ln(x, gates, pos, ye, g, b):
    m = x.shape[0]
    tq = COMBINE_TOKENS
    n_steps = m // tq
    pos3 = pos.reshape(n_steps, 1, 2 * tq)
    smem = functools.partial(pl.BlockSpec, memory_space=pltpu.SMEM)
    return pl.pallas_call(
        _moe_combine_ln_kernel,
        grid=(n_steps,),
        in_specs=[smem((None, 1, 2 * tq), lambda i: (i, 0, 0)),
                  smem((None, 1, 2 * tq), lambda i: (jnp.minimum(i + 1, n_steps - 1), 0, 0)),
                  pl.BlockSpec((tq, LANES), lambda i: (i, 0)),
                  pl.BlockSpec((tq, D_MODEL), lambda i: (i, 0)),
                  pl.BlockSpec(memory_space=pl.ANY),
                  pl.BlockSpec((1, D_MODEL), lambda i: (0, 0)),
                  pl.BlockSpec((1, D_MODEL), lambda i: (0, 0))],
        out_specs=pl.BlockSpec((tq, D_MODEL), lambda i: (i, 0)),
        out_shape=jax.ShapeDtypeStruct((m, D_MODEL), F32),
        scratch_shapes=[pltpu.VMEM((2, 2 * tq) + ROW_TILE, F32), pltpu.SemaphoreType.DMA((2,))],
        compiler_params=_params("arbitrary"),
        name="moe_combine_ln",
    )(pos3, pos3, gates, x, ye, g, b)


def _moe_routed_ln(x, gates, idx, w_gu, w_down, layer, g, b):
    pos, tile_expert, n_used, pad_dst, n_tiles = _moe_plan(idx[:, :2], MOE_TM)
    xs = _moe_dispatch(x, pos, pad_dst, n_tiles * MOE_TM)
    ye = _moe_ffn(xs, tile_expert, n_used, w_gu, w_down, layer, MOE_TM, FFN_TF)
    return _moe_combine_ln(x, gates, pos, ye, g, b)


GDN_PASSES_QK = 1
GDN_PASSES_SOLVE = 1
GDN_PASSES_STATE = 1
INV_BLOCK_SHIFT = 4
GDN_SEQS = 4


def _gdn_chunk_kernel(qkv_ref, z_ref, ab_ref, cw_ref, alog_ref, dtb_ref, nw_ref, hist0_ref, s0_ref,
                      mix_ref, sout_ref, hout_ref, xh_ref, s_ref):
    n = pl.program_id(1)
    c = GDN_CHUNK
    hrows = SUBLANES
    n_seq = qkv_ref.shape[0]
    items = [(b, h) for b in range(n_seq) for h in range(H_A)]

    @pl.when(n == 0)
    def _():
        xh_ref[:, 0:hrows, :] = hist0_ref[...]
        s_ref[...] = s0_ref[...]

    xh_ref[:, hrows:hrows + c, :] = qkv_ref[...]

    row = lax.broadcasted_iota(jnp.int32, (c, c), 0)
    col = lax.broadcasted_iota(jnp.int32, (c, c), 1)
    causal = row >= col
    strict = row > col
    eye = jnp.where(row == col, 1.0, 0.0)
    blockdiag = (row >> INV_BLOCK_SHIFT) == (col >> INV_BLOCK_SHIFT)
    tril_ones = jnp.where(causal, 1.0, 0.0).astype(BF16)
    lane = lax.broadcasted_iota(jnp.int32, (c, LANES), 1)
    mm_s = functools.partial(_dot, passes=GDN_PASSES_SOLVE)

    beta_all, gcum_all, gcum_t = [], [], []
    for b in range(n_seq):
        ab = ab_ref[b]
        g_all = -jnp.exp(alog_ref[...]) * _softplus(ab + dtb_ref[...])
        beta_all.append(_sigmoid(ab))
        gc = _dot_exact_lhs(tril_ones, g_all)
        gcum_all.append(gc)
        gcum_t.append(gc.T)

    def conv_silu(b, c0):
        acc = xh_ref[b, hrows - 3:hrows - 3 + c, c0:c0 + LANES] * cw_ref[0:1, c0:c0 + LANES]
        for j in range(1, CONV_W):
            acc = acc + (xh_ref[b, hrows - 3 + j:hrows - 3 + j + c, c0:c0 + LANES]
                         * cw_ref[j:j + 1, c0:c0 + LANES])
        return _silu(acc)

    q, k, v, kb, egc, gc_col, gc_last, beta_col, decay = {}, {}, {}, {}, {}, {}, {}, {}, {}
    for it in items:
        b, h = it
        qi = conv_silu(b, h * DK_A)
        ki = conv_silu(b, QK_A + h * DK_A)
        v[it] = conv_silu(b, 2 * QK_A + h * DV_A)
        q[it] = qi * lax.rsqrt(jnp.sum(qi * qi, axis=-1, keepdims=True) + RMS_EPS) * (DK_A ** -0.5)
        k[it] = ki * lax.rsqrt(jnp.sum(ki * ki, axis=-1, keepdims=True) + RMS_EPS)
        gc_col[it] = _lane_pick(gcum_all[b], lane, h)
        beta_col[it] = _lane_pick(beta_all[b], lane, B_LANE + h)
        gc_last[it] = gc_col[it][c - 1:c, :]
        decay[it] = jnp.where(causal, jnp.exp(gc_col[it] - gcum_t[b][h:h + 1, :c]), 0.0)
        egc[it] = jnp.exp(gc_col[it])
        kb[it] = k[it] * beta_col[it]

    kk = {it: _dot(jnp.concatenate([kb[it], q[it]], axis=0), k[it], _NT, GDN_PASSES_QK) for it in items}
    a_mat = {it: jnp.where(strict, kk[it][:c] * decay[it], 0.0) for it in items}
    attn = {it: kk[it][c:] * decay[it] for it in items}

    a_d = {it: jnp.where(blockdiag, a_mat[it], 0.0) for it in items}
    a_n = {it: jnp.where(blockdiag, 0.0, a_mat[it]) for it in items}
    p2 = {it: mm_s(a_d[it], a_d[it]) for it in items}
    e_d = {it: p2[it] - a_d[it] - mm_s(a_d[it], p2[it]) for it in items}
    p4 = {it: mm_s(p2[it], p2[it]) for it in items}
    e_d = {it: e_d[it] + p4[it] + mm_s(e_d[it], p4[it]) for it in items}
    p8 = {it: mm_s(p4[it], p4[it]) for it in items}
    e_d = {it: e_d[it] + p8[it] + mm_s(e_d[it], p8[it]) for it in items}
    m_blk = {it: a_n[it] + mm_s(e_d[it], a_n[it]) for it in items}
    rhs = {it: jnp.concatenate([v[it] * beta_col[it], kb[it] * egc[it]], axis=1) for it in items}
    rhs = {it: rhs[it] + mm_s(e_d[it], rhs[it]) for it in items}
    m2 = {it: mm_s(m_blk[it], m_blk[it]) for it in items}
    f_y = {it: m2[it] - m_blk[it] - mm_s(m_blk[it], m2[it]) for it in items}
    sol = {it: rhs[it] + mm_s(f_y[it], rhs[it]) for it in items}

    s_old = {it: s_ref[it[0], it[1]] for it in items}
    wq = {it: _dot(jnp.concatenate([sol[it][:, DV_A:], q[it] * egc[it]], axis=0), s_old[it],
                   _NN, GDN_PASSES_STATE) for it in items}
    v_new = {it: sol[it][:, :DV_A] - wq[it][:c] for it in items}
    o = {it: wq[it][c:] + _dot(attn[it], v_new[it], _NN, GDN_PASSES_STATE) for it in items}
    for it in items:
        k_dec = k[it] * jnp.exp(gc_last[it] - gc_col[it])
        s_ref[it[0], it[1]] = (s_old[it] * jnp.exp(gc_last[it])
                               + _dot(k_dec, v_new[it], _TN, GDN_PASSES_STATE))
    for it in items:
        b, h = it
        oi = o[it]
        oi = oi * lax.rsqrt(jnp.mean(oi * oi, axis=-1, keepdims=True) + RMS_EPS) * nw_ref[...]
        mix_ref[b, :, h * DV_A:(h + 1) * DV_A] = oi * _silu(z_ref[b, :, h * DV_A:(h + 1) * DV_A])

    xh_ref[:, 0:hrows, :] = xh_ref[:, c:c + hrows, :]

    @pl.when(n == pl.num_programs(1) - 1)
    def _():
        sout_ref[...] = s_ref[...]
        hout_ref[...] = xh_ref[:, c:c + hrows, :]


def _gdn_chunked(proj, conv_w, alog_row, dtb_row, nw_row, hist0, s0):
    bsz, t, _ = proj.shape
    c = GDN_CHUNK
    nb = GDN_SEQS
    n_chunks = t // c
    row_spec = lambda width, blk: pl.BlockSpec((nb, c, width), lambda b, n: (b, n, blk))
    full2 = lambda shape: pl.BlockSpec(shape, lambda b, n: (0, 0))
    return pl.pallas_call(
        _gdn_chunk_kernel,
        grid=(bsz // nb, n_chunks),
        in_specs=[row_spec(CONV_DIM, 0),
                  row_spec(V_A, A_Z_OFF // V_A),
                  row_spec(LANES, A_AB_OFF // LANES),
                  full2((CONV_W, CONV_DIM)),
                  full2((1, LANES)), full2((1, LANES)), full2((1, DV_A)),
                  pl.BlockSpec((nb, SUBLANES, CONV_DIM), lambda b, n: (b, 0, 0)),
                  pl.BlockSpec((nb, H_A, DK_A, DV_A), lambda b, n: (b, 0, 0, 0))],
        out_specs=[pl.BlockSpec((nb, c, V_A), lambda b, n: (b, n, 0)),
                   pl.BlockSpec((nb, H_A, DK_A, DV_A), lambda b, n: (b, 0, 0, 0)),
                   pl.BlockSpec((nb, SUBLANES, CONV_DIM), lambda b, n: (b, 0, 0))],
        out_shape=[jax.ShapeDtypeStruct((bsz, t, V_A), F32),
                   jax.ShapeDtypeStruct((bsz, H_A, DK_A, DV_A), F32),
                   jax.ShapeDtypeStruct((bsz, SUBLANES, CONV_DIM), F32)],
        scratch_shapes=[pltpu.VMEM((nb, SUBLANES + c, CONV_DIM), F32),
                        pltpu.VMEM((nb, H_A, DK_A, DV_A), F32)],
        compiler_params=_params("parallel", "arbitrary"),
        name="gdn_chunked",
    )(proj, proj, proj, conv_w, alog_row, dtb_row, nw_row, hist0, s0)


def _gdn_step_kernel(qkv_ref, z_ref, ab_ref, cw_ref, alog_ref, dtb_ref, nw_ref, hist_ref, s_ref,
                     mix_ref, sout_ref, hout_ref, o_scr):
    rows = qkv_ref.shape[0]
    new = qkv_ref[...]
    y = hist_ref[0] * cw_ref[0:1, :]
    for j in range(1, CONV_W - 1):
        y = y + hist_ref[j] * cw_ref[j:j + 1, :]
    y = y + new * cw_ref[CONV_W - 1:CONV_W, :]
    qkv = _silu(y)
    for j in range(CONV_W - 2):
        hout_ref[j] = hist_ref[j + 1]
    hout_ref[CONV_W - 2] = new

    ab = ab_ref[...]
    lane = lax.broadcasted_iota(jnp.int32, ab.shape, 1)
    g_all = -jnp.exp(alog_ref[...]) * _softplus(ab + dtb_ref[...])
    beta_all = _sigmoid(ab)
    ri = lax.broadcasted_iota(jnp.int32, (DK_A, DK_A), 0)
    ci = lax.broadcasted_iota(jnp.int32, (DK_A, DK_A), 1)
    diag = ri == ci

    def as_column(r):
        return jnp.sum(jnp.where(diag, r, 0.0), axis=-1, keepdims=True)

    q, k, v, eg, beta = [], [], [], [], []
    for h in range(H_A):
        qh = qkv[:, h * DK_A:(h + 1) * DK_A]
        kh = qkv[:, QK_A + h * DK_A:QK_A + (h + 1) * DK_A]
        v.append(qkv[:, 2 * QK_A + h * DV_A:2 * QK_A + (h + 1) * DV_A])
        q.append(qh * lax.rsqrt(jnp.sum(qh * qh, axis=-1, keepdims=True) + RMS_EPS) * (DK_A ** -0.5))
        k.append(kh * lax.rsqrt(jnp.sum(kh * kh, axis=-1, keepdims=True) + RMS_EPS))
        eg.append(jnp.exp(_lane_pick(g_all, lane, h)))
        beta.append(_lane_pick(beta_all, lane, B_LANE + h))

    for h in range(H_A):
        seqs = range(rows)
        k_col = [as_column(k[h][s:s + 1, :]) for s in seqs]
        q_col = [as_column(q[h][s:s + 1, :]) for s in seqs]
        st = [s_ref[s, h] * eg[h][s:s + 1, :] for s in seqs]
        ks = [jnp.sum(k_col[s] * st[s], axis=0, keepdims=True) for s in seqs]
        u = [beta[h][s:s + 1, :] * (v[h][s:s + 1, :] - ks[s]) for s in seqs]
        st = [st[s] + k_col[s] * u[s] for s in seqs]
        for s in seqs:
            sout_ref[s, h] = st[s]
            o_scr[s:s + 1, h * DV_A:(h + 1) * DV_A] = jnp.sum(q_col[s] * st[s], axis=0, keepdims=True)

    for h in range(H_A):
        o = o_scr[:, h * DV_A:(h + 1) * DV_A]
        o = o * lax.rsqrt(jnp.mean(o * o, axis=-1, keepdims=True) + RMS_EPS) * nw_ref[...]
        mix_ref[:, h * DV_A:(h + 1) * DV_A] = o * _silu(z_ref[:, h * DV_A:(h + 1) * DV_A])


def _gdn_step(proj, conv_w, alog_row, dtb_row, nw_row, hist_t, s0, layer):
    bsz = proj.shape[0]
    rows = SUBLANES
    full1 = lambda shape: pl.BlockSpec(shape, lambda i: (0, 0))
    return pl.pallas_call(
        _gdn_step_kernel,
        grid=(bsz // rows,),
        in_specs=[pl.BlockSpec((rows, CONV_DIM), lambda i: (i, 0)),
                  pl.BlockSpec((rows, V_A), lambda i: (i, A_Z_OFF // V_A)),
                  pl.BlockSpec((rows, LANES), lambda i: (i, A_AB_OFF // LANES)),
                  full1((CONV_W, CONV_DIM)),
                  full1((1, LANES)), full1((1, LANES)), full1((1, DV_A)),
                  pl.BlockSpec((None, CONV_W - 1, rows, CONV_DIM), lambda i: (layer, 0, i, 0)),
                  pl.BlockSpec((None, rows, H_A, DK_A, DV_A), lambda i: (layer, i, 0, 0, 0))],
        out_specs=[pl.BlockSpec((rows, V_A), lambda i: (i, 0)),
                   pl.BlockSpec((rows, H_A, DK_A, DV_A), lambda i: (i, 0, 0, 0)),
                   pl.BlockSpec((CONV_W - 1, rows, CONV_DIM), lambda i: (0, i, 0))],
        out_shape=[jax.ShapeDtypeStruct((bsz, V_A), F32),
                   jax.ShapeDtypeStruct((bsz, H_A, DK_A, DV_A), F32),
                   jax.ShapeDtypeStruct((CONV_W - 1, bsz, CONV_DIM), F32)],
        scratch_shapes=[pltpu.VMEM((rows, V_A), F32)],
        compiler_params=_params("parallel"),
        name="gdn_step",
    )(proj, proj, proj, conv_w, alog_row, dtb_row, nw_row, hist_t, s0)


SGU_CHUNKS = 4


def _sgu_chunk_kernel(u_ref, v_ref, g_ref, b_ref, ws_ref, bs_ref, mix_ref):
    cb = CHUNK_B
    row = lax.broadcasted_iota(jnp.int32, (cb, cb), 0)
    col = lax.broadcasted_iota(jnp.int32, (cb, cb), 1)
    ws = [jnp.where(row >= col, ws_ref[gi], 0.0).astype(BF16) for gi in range(G_B)]
    for r0 in range(0, u_ref.shape[0], cb):
        u = _gelu_tanh(u_ref[r0:r0 + cb, :])
        v = _layer_norm(_gelu_tanh(v_ref[r0:r0 + cb, :]), g_ref[...], b_ref[...])
        for gi in range(G_B):
            mixed = _dot(ws[gi], v[:, gi * CH_B:(gi + 1) * CH_B]) + bs_ref[gi]
            mix_ref[r0:r0 + cb, gi * CH_B:(gi + 1) * CH_B] = u[:, gi * CH_B:(gi + 1) * CH_B] * mixed


def _sgu_chunked(proj, ln_g, ln_b, w_s, bs_b):
    bsz, t, _ = proj.shape
    c = CHUNK_B * SGU_CHUNKS
    return pl.pallas_call(
        _sgu_chunk_kernel,
        grid=(bsz, t // c),
        in_specs=[pl.BlockSpec((None, c, D_B), lambda b, n: (b, n, 0)),
                  pl.BlockSpec((None, c, D_B), lambda b, n: (b, n, 1)),
                  pl.BlockSpec((1, D_B), lambda b, n: (0, 0)),
                  pl.BlockSpec((1, D_B), lambda b, n: (0, 0)),
                  pl.BlockSpec((G_B, CHUNK_B, CHUNK_B), lambda b, n: (0, 0, 0)),
                  pl.BlockSpec((G_B, CHUNK_B, CH_B), lambda b, n: (0, 0, 0))],
        out_specs=pl.BlockSpec((None, c, D_B), lambda b, n: (b, n, 0)),
        out_shape=jax.ShapeDtypeStruct((bsz, t, D_B), F32),
        compiler_params=_params("parallel", "parallel"),
        name="sgu_chunked",
    )(proj, proj, ln_g, ln_b, w_s, bs_b)


def _sgu_first_kernel(u_ref, v_ref, g_ref, b_ref, w00_ref, b0_ref, mix_ref, vout_ref):
    u = _gelu_tanh(u_ref[...])
    v = _layer_norm(_gelu_tanh(v_ref[...]), g_ref[...], b_ref[...])
    vout_ref[...] = v
    mix_ref[...] = u * (w00_ref[...] * v + b0_ref[...])


def _sgu_first(proj, ln_g, ln_b, w00_row, b0_row):
    bsz = proj.shape[0]
    row = lambda: pl.BlockSpec((1, D_B), lambda i: (0, 0))
    return pl.pallas_call(
        _sgu_first_kernel,
        grid=(1,),
        in_specs=[pl.BlockSpec((bsz, D_B), lambda i: (0, 0)),
                  pl.BlockSpec((bsz, D_B), lambda i: (0, 1)),
                  row(), row(), row(), row()],
        out_specs=[pl.BlockSpec((bsz, D_B), lambda i: (0, 0)),
                   pl.BlockSpec((bsz, D_B), lambda i: (0, 0))],
        out_shape=[jax.ShapeDtypeStruct((bsz, D_B), F32),
                   jax.ShapeDtypeStruct((bsz, D_B), F32)],
        compiler_params=_params("arbitrary"),
        name="sgu_first",
    )(proj, proj, ln_g, ln_b, w00_row, b0_row)


def _head_masks(shape):
    lane = lax.broadcasted_iota(jnp.int32, shape, len(shape) - 1)
    return [(lane // DH_X) == h for h in range(H_X)]


def _xattn_kernel(q_ref, k_ref, v_ref, o_ref):
    q = q_ref[...]
    kb = k_ref[...].astype(BF16)
    vb = v_ref[...].astype(BF16)
    masks = _head_masks(q.shape)
    heads = range(H_X)
    s = [_dot(jnp.where(masks[h], q, 0.0), kb, _NT) * (DH_X ** -0.5) for h in heads]
    p = [jnp.exp(s[h] - jnp.max(s[h], axis=-1, keepdims=True)) for h in heads]
    p = [p[h] / jnp.sum(p[h], axis=-1, keepdims=True) for h in heads]
    pv = [_dot(p[h], vb) for h in heads]
    out = jnp.where(masks[0], pv[0], 0.0)
    for h in range(1, H_X):
        out = jnp.where(masks[h], pv[h], out)
    o_ref[...] = out


def _xattn(proj, xq_blk, mem_k, mem_v, layer, tq):
    bsz, t, _ = proj.shape
    return pl.pallas_call(
        _xattn_kernel,
        grid=(bsz, t // tq),
        in_specs=[pl.BlockSpec((None, tq, D_X), lambda b, i: (b, i, xq_blk)),
                  pl.BlockSpec((None, None, N_MEM, D_X), lambda b, i: (layer, b, 0, 0)),
                  pl.BlockSpec((None, None, N_MEM, D_X), lambda b, i: (layer, b, 0, 0))],
        out_specs=pl.BlockSpec((None, tq, D_X), lambda b, i: (b, i, 0)),
        out_shape=jax.ShapeDtypeStruct((bsz, t, D_X), F32),
        compiler_params=_params("parallel", "parallel"),
        name="xattn",
    )(proj, mem_k, mem_v)


def _xattn_step_kernel(q_ref, k_ref, v_ref, o_ref):
    rows = q_ref.shape[0]
    q = q_ref[...]
    hrow = lax.broadcasted_iota(jnp.int32, (SUBLANES, D_X), 0)
    hlane = lax.broadcasted_iota(jnp.int32, (SUBLANES, D_X), 1) // DH_X
    sel = hrow == hlane
    seqs = range(rows)
    sc = [_dot(jnp.where(sel, q[s:s + 1, :], 0.0), k_ref[s], _NT) * (DH_X ** -0.5) for s in seqs]
    p = [jnp.exp(sc[s] - jnp.max(sc[s], axis=-1, keepdims=True)) for s in seqs]
    p = [p[s] / jnp.sum(p[s], axis=-1, keepdims=True) for s in seqs]
    pv = [_dot(p[s], v_ref[s]) for s in seqs]
    for s in seqs:
        o_ref[s:s + 1, :] = jnp.sum(jnp.where(sel, pv[s], 0.0), axis=0, keepdims=True)


def _xattn_step(proj, xq_blk, mem_k, mem_v, layer):
    bsz = proj.shape[0]
    rows = SUBLANES
    return pl.pallas_call(
        _xattn_step_kernel,
        grid=(bsz // rows,),
        in_specs=[pl.BlockSpec((rows, D_X), lambda i: (i, xq_blk)),
                  pl.BlockSpec((None, rows, N_MEM, D_X), lambda i: (layer, i, 0, 0)),
                  pl.BlockSpec((None, rows, N_MEM, D_X), lambda i: (layer, i, 0, 0))],
        out_specs=pl.BlockSpec((rows, D_X), lambda i: (i, 0)),
        out_shape=jax.ShapeDtypeStruct((bsz, D_X), F32),
        compiler_params=_params("parallel"),
        name="xattn_step",
    )(proj, mem_k, mem_v)


def _pad_lanes(v):
    return jnp.zeros((1, LANES), F32).at[0, :v.shape[0]].set(v.astype(F32))


def _prep_weights(a_w_in, a_A_log, a_dt_bias, b_w_s, b_b_s, moe_router):
    qkvz = a_w_in[:, :, :A_XQ_OFF]
    a_cols = a_w_in[:, :, A_XQ_OFF:A_XQ_OFF + H_A]
    b_cols = a_w_in[:, :, A_XQ_OFF + H_A:A_XQ_OFF + 2 * H_A]
    xq = a_w_in[:, :, A_XQ_OFF + 2 * H_A:]
    ab = jnp.zeros(a_w_in.shape[:2] + (LANES,), F32)
    ab = ab.at[:, :, :H_A].set(a_cols).at[:, :, B_LANE:B_LANE + H_A].set(b_cols)
    a_w = jnp.concatenate([qkvz, xq, ab], axis=-1).astype(BF16)
    alog_rows = [_pad_lanes(a_A_log[j]) for j in range(a_A_log.shape[0])]
    dtb_rows = [_pad_lanes(a_dt_bias[j]) for j in range(a_dt_bias.shape[0])]
    bs_b = jnp.broadcast_to(b_b_s[..., None], b_b_s.shape + (CH_B,)).astype(F32)
    w00_rows = jnp.repeat(b_w_s[:, :, 0, 0], CH_B, axis=-1)[:, None, :]
    b0_rows = jnp.repeat(b_b_s[:, :, 0], CH_B, axis=-1)[:, None, :]
    r = jnp.zeros(moe_router.shape[:2] + (LANES,), F32).at[:, :, :N_EXP].set(moe_router)
    r_hi = r.astype(BF16)
    r_lo = (r - r_hi.astype(F32)).astype(BF16)
    return a_w, alog_rows, dtb_rows, bs_b, w00_rows, b0_rows, r_hi, r_lo


def kernel(x_prompt, x_sample, state_gdn, state_conv, cache_mem_k, cache_mem_v, mem_prompt,
           a_w_in, a_conv_w, a_A_log, a_dt_bias, a_norm_w,
           b_w_in, b_ln_g, b_ln_b, b_w_s, b_b_s,
           w_mem_kv, w_out, ln1_g, ln1_b, ln2_g, ln2_b,
           ffn_w_gu, ffn_w_down, moe_router, moe_w_gu, moe_w_down):
    n_p, t_p, _ = x_prompt.shape
    n_s = x_sample.shape[0]
    n_a = a_w_in.shape[0]

    a_w, alog_rows, dtb_rows, bs_b, w00_rows, b0_rows, r_hi, r_lo = _prep_weights(
        a_w_in, a_A_log, a_dt_bias, b_w_s, b_b_s, moe_router)
    ffn_gu = ffn_w_gu.astype(BF16)[:, None]
    ffn_dn = ffn_w_down.astype(BF16)[:, None]
    moe_gu = moe_w_gu.astype(BF16)
    moe_dn = moe_w_down.astype(BF16)
    row = lambda v: v.reshape(1, -1).astype(F32)

    w_kv = jnp.transpose(w_mem_kv, (1, 0, 2)).reshape(D_MODEL, DEPTH * 2 * D_X).astype(BF16)
    kv = _matmul(mem_prompt.reshape(n_p * N_MEM, D_MODEL), w_kv, 512, 1024, "mem_kv")
    kv = kv.reshape(n_p, N_MEM, DEPTH, 2, D_X)
    p_mem_k = jnp.transpose(kv[:, :, :, 0, :], (2, 0, 1, 3))
    p_mem_v = jnp.transpose(kv[:, :, :, 1, :], (2, 0, 1, 3))
    s_mem_k = cache_mem_k.reshape(DEPTH, n_s, N_MEM, D_X)
    s_mem_v = cache_mem_v.reshape(DEPTH, n_s, N_MEM, D_X)

    xp = x_prompt.reshape(n_p * t_p, D_MODEL)
    xs = x_sample.reshape(n_s, D_MODEL)
    p_hist0 = jnp.zeros((n_p, SUBLANES, CONV_DIM), F32)
    p_s0 = jnp.zeros((n_p, H_A, DK_A, DV_A), F32)
    s_hist_t = jnp.transpose(state_conv, (0, 2, 1, 3))

    p_gdn, p_conv, s_gdn, s_conv, s_sgu_v = [], [], [], [], []
    tm_p, tm_s = 1024, n_s
    for i in range(DEPTH):
        j = i // 2
        if i % 2 == 0:
            proj_p = _matmul(xp, a_w[j], 512, 1152, "in_proj_a")
            proj_s = _matmul(xs, a_w[j], tm_s, 1152, "in_proj_a_s")
            nw = row(a_norm_w[j])
            mix_p, sp, hp = _gdn_chunked(proj_p.reshape(n_p, t_p, A_COLS), a_conv_w[j],
                                         alog_rows[j], dtb_rows[j], nw, p_hist0, p_s0)
            mix_p = mix_p.reshape(n_p * t_p, V_A)
            p_gdn.append(sp)
            p_conv.append(hp[:, SUBLANES - (CONV_W - 1):, :])
            mix_s, ss, hs = _gdn_step(proj_s, a_conv_w[j], alog_rows[j], dtb_rows[j], nw,
                                      s_hist_t, state_gdn, j)
            s_gdn.append(ss)
            s_conv.append(jnp.transpose(hs, (1, 0, 2)))
            cols, xq_blk = A_COLS, A_XQ_OFF // D_X
        else:
            b_w = b_w_in[j].astype(BF16)
            proj_p = _matmul(xp, b_w, 512, 896, "in_proj_b")
            proj_s = _matmul(xs, b_w, tm_s, 896, "in_proj_b_s")
            mix_p = _sgu_chunked(proj_p.reshape(n_p, t_p, B_COLS), row(b_ln_g[j]), row(b_ln_b[j]),
                                 b_w_s[j], bs_b[j]).reshape(n_p * t_p, D_B)
            mix_s, v_s = _sgu_first(proj_s, row(b_ln_g[j]), row(b_ln_b[j]), w00_rows[j], b0_rows[j])
            s_sgu_v.append(v_s.reshape(n_s, 1, D_B))
            cols, xq_blk = B_COLS, B_XQ_OFF // D_X
        xo_p = _xattn(proj_p.reshape(n_p, t_p, cols), xq_blk, p_mem_k, p_mem_v, i, 512)
        xo_s = _xattn_step(proj_s, xq_blk, s_mem_k, s_mem_v, i)
        g1, b1, g2, b2 = row(ln1_g[i]), row(ln1_b[i]), row(ln2_g[i]), row(ln2_b[i])
        w_o = w_out[i].astype(BF16)
        xo_p = xo_p.reshape(n_p * t_p, D_X)
        if i % 2 == 0:
            xp = _outproj_ln(mix_p, xo_p, xp, w_o, g1, b1, 512)
            xs = _outproj_ln(mix_s, xo_s, xs, w_o, g1, b1, tm_s)
            xp = _ffn_ln(xp, None, ffn_gu, ffn_dn, j, g2, b2, tm_p, FFN_TF)
            xs = _ffn_ln(xs, None, ffn_gu, ffn_dn, j, g2, b2, tm_s, FFN_TF)
        else:
            router = (r_hi[j], r_lo[j])
            xp, _, gates_p, idx_p = _outproj_ln(mix_p, xo_p, xp, w_o, g1, b1, 512, router)
            xs, comb_s, _, _ = _outproj_ln(mix_s, xo_s, xs, w_o, g1, b1, tm_s, router)
            xp = _moe_routed_ln(xp, gates_p, idx_p, moe_gu, moe_dn, j, g2, b2)
            xs = _ffn_ln(xs, comb_s, moe_gu, moe_dn, j, g2, b2, tm_s, FFN_TF)

    mem_shape = (DEPTH, n_p, N_MEM, H_X, DH_X)
    return (xp.reshape(n_p, t_p, D_MODEL),
            xs.reshape(n_s, 1, D_MODEL),
            jnp.stack(p_gdn),
            jnp.stack(p_conv),
            p_mem_k.reshape(mem_shape),
            p_mem_v.reshape(mem_shape),
            jnp.stack(s_gdn),
            jnp.stack(s_conv),
            jnp.stack(s_sgu_v))
```

```python
import functools

import jax
import jax.numpy as jnp
from jax import lax
from jax.experimental import pallas as pl
from jax.experimental.pallas import tpu as pltpu

F32 = jnp.float32
BF16 = jnp.bfloat16

D_MODEL = 1024
DEPTH = 4
H_A = 6
DK_A = 128
DV_A = 128
QK_A = H_A * DK_A
V_A = H_A * DV_A
CONV_W = 4
CONV_DIM = 2 * QK_A + V_A
GDN_CHUNK = 64
G_B = 6
CH_B = 128
D_B = G_B * CH_B
CHUNK_B = 128
N_MEM = 256
H_X = 4
DH_X = 64
D_X = H_X * DH_X
D_FF = 3584
N_EXP = 8
ALPHA = (2 * DEPTH) ** 0.25
LN_EPS = 1e-5
RMS_EPS = 1e-6

LANES = 128
SUBLANES = 8
VMEM_LIMIT_BYTES = 56 * 1024 * 1024

A_Z_OFF = CONV_DIM
A_XQ_OFF = CONV_DIM + V_A
A_AB_OFF = A_XQ_OFF + D_X
A_COLS = A_AB_OFF + LANES
B_LANE = 8
B_XQ_OFF = 2 * D_B
B_COLS = 2 * D_B + D_X


def _params(*sem):
    return pltpu.CompilerParams(dimension_semantics=sem, vmem_limit_bytes=VMEM_LIMIT_BYTES)


def _split2(x):
    hi = x.astype(BF16)
    lo = (x - hi.astype(F32)).astype(BF16)
    return hi, lo


_NN = (((1,), (0,)), ((), ()))
_NT = (((1,), (1,)), ((), ()))
_TN = (((0,), (0,)), ((), ()))


def _dot(a, b, dims=_NN, passes=1):
    if passes == 1:
        return lax.dot_general(a.astype(BF16), b.astype(BF16), dims, preferred_element_type=F32)
    ah, al = _split2(a)
    bh, bl = _split2(b)
    dg = functools.partial(lax.dot_general, dimension_numbers=dims, preferred_element_type=F32)
    return dg(ah, bh) + (dg(ah, bl) + dg(al, bh))


def _dot_exact_lhs(lhs_bf16, x):
    x1 = x.astype(BF16)
    r1 = x - x1.astype(F32)
    x2 = r1.astype(BF16)
    x3 = (r1 - x2.astype(F32)).astype(BF16)
    dg = functools.partial(lax.dot_general, dimension_numbers=_NN, preferred_element_type=F32)
    return dg(lhs_bf16, x1) + (dg(lhs_bf16, x2) + dg(lhs_bf16, x3))


def _sigmoid(x):
    return 1.0 / (1.0 + jnp.exp(-x))


def _silu(x):
    return x * _sigmoid(x)


def _softplus(x):
    return jnp.maximum(x, 0.0) + jnp.log1p(jnp.exp(-jnp.abs(x)))


def _gelu_tanh(x):
    c = 0.7978845608028654
    return 0.5 * x * (1.0 + jnp.tanh(c * (x + 0.044715 * (x * x * x))))


def _layer_norm(y, g, b):
    mu = jnp.mean(y, axis=-1, keepdims=True)
    d = y - mu
    var = jnp.mean(d * d, axis=-1, keepdims=True)
    return d * lax.rsqrt(var + LN_EPS) * g + b


def _lane_pick(x, lane_iota, idx):
    return jnp.sum(jnp.where(lane_iota == idx, x, 0.0), axis=-1, keepdims=True)


def _mm_kernel(x_ref, w_ref, o_ref, *, chunk):
    xb = x_ref[...].astype(BF16)
    for c0 in range(0, o_ref.shape[1], chunk):
        o_ref[:, c0:c0 + chunk] = jnp.dot(xb, w_ref[:, c0:c0 + chunk], preferred_element_type=F32)


def _matmul(x, w, tm, chunk, name):
    m, k = x.shape
    n = w.shape[1]
    return pl.pallas_call(
        functools.partial(_mm_kernel, chunk=chunk),
        grid=(m // tm,),
        in_specs=[pl.BlockSpec((tm, k), lambda i: (i, 0)),
                  pl.BlockSpec((k, n), lambda i: (0, 0))],
        out_specs=pl.BlockSpec((tm, n), lambda i: (i, 0)),
        out_shape=jax.ShapeDtypeStruct((m, n), F32),
        compiler_params=_params("parallel"),
        name=name,
    )(x, w)


def _outproj_ln_kernel(mix_ref, xo_ref, x_ref, wm_ref, wx_ref, g_ref, b_ref, *rest):
    h = jnp.dot(mix_ref[...].astype(BF16), wm_ref[...], preferred_element_type=F32)
    h = h + jnp.dot(xo_ref[...].astype(BF16), wx_ref[...], preferred_element_type=F32)
    y = _layer_norm(ALPHA * x_ref[...] + h, g_ref[...], b_ref[...])
    if len(rest) == 1:
        (o_ref,) = rest
    else:
        rh_ref, rl_ref, o_ref, comb_ref, gate_ref, idx_ref = rest
        comb_ref[...], gate_ref[...], idx_ref[...] = _route_top2(y, rh_ref[...], rl_ref[...])
    o_ref[...] = y


def _outproj_ln(mix, xo, x, w_out, g, b, tm, router=None):
    m = x.shape[0]
    d_mix = mix.shape[1]
    n_blk = d_mix // D_X
    row_blk = lambda width: pl.BlockSpec((tm, width), lambda i: (i, 0))
    const = lambda shape, blk=0: pl.BlockSpec(shape, lambda i: (blk, 0))
    in_specs = [row_blk(d_mix), row_blk(D_X), row_blk(D_MODEL),
                const((d_mix, D_MODEL)), const((D_X, D_MODEL), n_blk),
                const((1, D_MODEL)), const((1, D_MODEL))]
    args = [mix, xo, x, w_out, w_out, g, b]
    out_specs = [row_blk(D_MODEL)]
    out_shape = [jax.ShapeDtypeStruct((m, D_MODEL), F32)]
    if router is not None:
        in_specs += [const((D_MODEL, LANES)), const((D_MODEL, LANES))]
        args += list(router)
        out_specs += [row_blk(LANES)] * 3
        out_shape += [jax.ShapeDtypeStruct((m, LANES), F32), jax.ShapeDtypeStruct((m, LANES), F32),
                      jax.ShapeDtypeStruct((m, LANES), jnp.int32)]
    out = pl.pallas_call(
        _outproj_ln_kernel,
        grid=(m // tm,),
        in_specs=in_specs,
        out_specs=out_specs,
        out_shape=out_shape,
        compiler_params=_params("parallel"),
        name="outproj_ln",
    )(*args)
    return out[0] if router is None else out


FFN_CHUNK = 256


def _swiglu_partial(xb, wg_ref, wu_ref, wd_ref, row_scale=None):
    y = None
    for c0 in range(0, wg_ref.shape[-1], FFN_CHUNK):
        hg = jnp.dot(xb, wg_ref[:, c0:c0 + FFN_CHUNK], preferred_element_type=F32)
        hu = jnp.dot(xb, wu_ref[:, c0:c0 + FFN_CHUNK], preferred_element_type=F32)
        h = _silu(hg) * hu
        if row_scale is not None:
            h = h * row_scale
        part = jnp.dot(h.astype(BF16), wd_ref[c0:c0 + FFN_CHUNK, :], preferred_element_type=F32)
        y = part if y is None else y + part
    return y


def _ffn_kernel(*refs, moe):
    if moe:
        x_ref, comb_ref, wg_ref, wu_ref, wd_ref, g_ref, b_ref, o_ref, xb_ref, acc_ref = refs
    else:
        x_ref, wg_ref, wu_ref, wd_ref, g_ref, b_ref, o_ref, xb_ref, acc_ref = refs
    e = pl.program_id(1)
    f = pl.program_id(2)
    first = jnp.logical_and(e == 0, f == 0)
    last = jnp.logical_and(e == pl.num_programs(1) - 1, f == pl.num_programs(2) - 1)

    @pl.when(first)
    def _():
        xb_ref[...] = x_ref[...].astype(BF16)
        acc_ref[...] = jnp.zeros_like(acc_ref)

    row_scale = None
    if moe:
        comb = comb_ref[...]
        lane = lax.broadcasted_iota(jnp.int32, comb.shape, 1)
        row_scale = _lane_pick(comb, lane, e)
    acc_ref[...] += _swiglu_partial(xb_ref[...], wg_ref, wu_ref, wd_ref, row_scale)

    @pl.when(last)
    def _():
        o_ref[...] = _layer_norm(ALPHA * x_ref[...] + acc_ref[...], g_ref[...], b_ref[...])


def _ffn_ln(x, comb, w_gu, w_down, layer, g, b, tm, tf):
    m = x.shape[0]
    moe = comb is not None
    n_e = w_gu.shape[1]
    n_f = D_FF // tf
    in_specs = [pl.BlockSpec((tm, D_MODEL), lambda i, e, f: (i, 0))]
    args = [x]
    if moe:
        in_specs.append(pl.BlockSpec((tm, LANES), lambda i, e, f: (i, 0)))
        args.append(comb)
    in_specs += [pl.BlockSpec((None, None, D_MODEL, tf), lambda i, e, f: (layer, e, 0, f)),
                 pl.BlockSpec((None, None, D_MODEL, tf), lambda i, e, f: (layer, e, 0, n_f + f)),
                 pl.BlockSpec((None, None, tf, D_MODEL), lambda i, e, f: (layer, e, f, 0)),
                 pl.BlockSpec((1, D_MODEL), lambda i, e, f: (0, 0)),
                 pl.BlockSpec((1, D_MODEL), lambda i, e, f: (0, 0))]
    args += [w_gu, w_gu, w_down, g, b]
    return pl.pallas_call(
        functools.partial(_ffn_kernel, moe=moe),
        grid=(m // tm, n_e, n_f),
        in_specs=in_specs,
        out_specs=pl.BlockSpec((tm, D_MODEL), lambda i, e, f: (i, 0)),
        out_shape=jax.ShapeDtypeStruct((m, D_MODEL), F32),
        scratch_shapes=[pltpu.VMEM((tm, D_MODEL), BF16), pltpu.VMEM((tm, D_MODEL), F32)],
        compiler_params=_params("parallel", "arbitrary", "arbitrary"),
        name="moe_ln" if moe else "ffn_ln",
    )(*args)


def _route_top2(x, wh, wl):
    xb = x.astype(BF16)
    lg = jnp.dot(xb, wh, preferred_element_type=F32) + jnp.dot(xb, wl, preferred_element_type=F32)
    lane = lax.broadcasted_iota(jnp.int32, lg.shape, 1).astype(F32)
    neg = -jnp.inf
    lg = jnp.where(lane < N_EXP, lg, neg)
    m1 = jnp.max(lg, axis=-1, keepdims=True)
    i1 = -jnp.max(jnp.where(lg == m1, -lane, -float(LANES)), axis=-1, keepdims=True)
    lg2 = jnp.where(lane == i1, neg, lg)
    m2 = jnp.max(lg2, axis=-1, keepdims=True)
    i2 = -jnp.max(jnp.where(lg2 == m2, -lane, -float(LANES)), axis=-1, keepdims=True)
    e2 = jnp.exp(m2 - m1)
    den = 1.0 + e2
    g1 = 1.0 / den
    g2 = e2 / den
    comb = jnp.where(lane == i1, g1, 0.0) + jnp.where(lane == i2, g2, 0.0)
    gates = jnp.where(lane == 0, g1, jnp.where(lane == 1, g2, 0.0))
    idx = jnp.where(lane == 0, i1, jnp.where(lane == 1, i2, 0.0)).astype(jnp.int32)
    return comb, gates, idx


MOE_TM = 512
FFN_TF = 1792
DISPATCH_TOKENS = 1024
COMBINE_TOKENS = 256


def _moe_plan(idx2, tm):
    t = idx2.shape[0]
    e_flat = idx2.reshape(-1)
    experts = jnp.arange(N_EXP, dtype=jnp.int32)
    onehot = (e_flat[:, None] == experts[None, :]).astype(jnp.int32)
    csum = jnp.cumsum(onehot, axis=0)
    rank = jnp.sum(csum * onehot, axis=1) - 1
    counts = csum[-1]
    ptiles = (counts + tm - 1) // tm
    tile_end = jnp.cumsum(ptiles)
    pstart = (tile_end - ptiles) * tm
    pos = jnp.sum(onehot * pstart[None, :], axis=1) + rank
    n_used = tile_end[-1]
    n_tiles = (2 * t) // tm + N_EXP
    jj = jnp.minimum(jnp.arange(n_tiles, dtype=jnp.int32), n_used - 1)
    tile_expert = jnp.sum((jj[:, None] >= tile_end[None, :]).astype(jnp.int32), axis=1)
    seg_base = jnp.concatenate([pstart + counts, (n_used * tm).reshape(1)])
    seg_len = jnp.concatenate([ptiles * tm - counts, ((n_tiles - n_used) * tm).reshape(1)])
    seg_end = jnp.cumsum(seg_len)
    j = jnp.arange(N_EXP * tm, dtype=jnp.int32)
    seg = jnp.sum((j[:, None] >= seg_end[None, :]).astype(jnp.int32), axis=1)
    seg_1h = (seg[:, None] == jnp.arange(N_EXP + 1, dtype=jnp.int32)[None, :]).astype(jnp.int32)
    pad_dst = j + jnp.sum(seg_1h * (seg_base - (seg_end - seg_len))[None, :], axis=1)
    return (pos.astype(jnp.int32), tile_expert.astype(jnp.int32), n_used.reshape(1).astype(jnp.int32),
            pad_dst.astype(jnp.int32), n_tiles)


ROW_TILE = (SUBLANES, LANES)
assert SUBLANES * LANES == D_MODEL


def _wait_rows(tiles_hbm_ref, n_rows, sem):
    rows = tiles_hbm_ref.at[pl.ds(0, n_rows)]
    pltpu.make_async_copy(rows, rows, sem).wait()


def _moe_dispatch_kernel(pos_ref, pad_dst_ref, x_ref, xs_ref, xr_ref, sem):
    tq = x_ref.shape[0]
    n_pad = pad_dst_ref.shape[1]
    xr_ref[...] = x_ref[...].reshape((tq,) + ROW_TILE)

    def send(grp, carry):
        for jr in range(SUBLANES):
            t = SUBLANES * grp + jr
            for k in range(2):
                pltpu.make_async_copy(xr_ref.at[t], xs_ref.at[pos_ref[0, 2 * t + k]], sem).start(priority=k)
        return carry

    lax.fori_loop(0, tq // SUBLANES, send, 0)

    def pad(grp, carry):
        for jr in range(SUBLANES):
            dst = pad_dst_ref[0, SUBLANES * grp + jr]
            pltpu.make_async_copy(xr_ref.at[jr], xs_ref.at[dst], sem).start(priority=jr % 2)
        return carry

    lax.fori_loop(0, n_pad // SUBLANES, pad, 0)
    _wait_rows(xs_ref, 2 * tq + n_pad, sem)


def _moe_dispatch(x, pos, pad_dst, n_rows_out):
    m = x.shape[0]
    tq = DISPATCH_TOKENS
    n_steps = m // tq
    n_pad = pad_dst.shape[0] // n_steps
    smem = functools.partial(pl.BlockSpec, memory_space=pltpu.SMEM)
    return pl.pallas_call(
        _moe_dispatch_kernel,
        grid=(n_steps,),
        in_specs=[smem((None, 1, 2 * tq), lambda i: (i, 0, 0)),
                  smem((None, 1, n_pad), lambda i: (i, 0, 0)),
                  pl.BlockSpec((tq, D_MODEL), lambda i: (i, 0))],
        out_specs=pl.BlockSpec(memory_space=pl.ANY),
        out_shape=jax.ShapeDtypeStruct((n_rows_out,) + ROW_TILE, F32),
        scratch_shapes=[pltpu.VMEM((tq,) + ROW_TILE, F32), pltpu.SemaphoreType.DMA(())],
        compiler_params=_params("arbitrary"),
        name="moe_dispatch",
    )(pos.reshape(n_steps, 1, 2 * tq), pad_dst.reshape(n_steps, 1, n_pad), x)


def _moe_ffn_kernel(te_ref, nu_ref, x_ref, wg_ref, wu_ref, wd_ref, o_ref, xb_ref, acc_ref):
    p = pl.program_id(0)
    f = pl.program_id(1)
    tm = xb_ref.shape[0]

    @pl.when(p < nu_ref[0])
    def _():
        @pl.when(f == 0)
        def _():
            xb_ref[...] = x_ref[...].reshape(tm, D_MODEL).astype(BF16)

        y = _swiglu_partial(xb_ref[...], wg_ref, wu_ref, wd_ref)
        last = pl.num_programs(1) - 1

        @pl.when(f == 0)
        def _():
            acc_ref[...] = y

        @pl.when(jnp.logical_and(f != 0, f != last))
        def _():
            acc_ref[...] += y

        @pl.when(f == last)
        def _():
            o_ref[...] = (acc_ref[...] + y).reshape((tm,) + ROW_TILE)

    @pl.when(jnp.logical_and(p >= nu_ref[0], f == 0))
    def _():
        o_ref[...] = jnp.zeros_like(o_ref)


def _moe_ffn(xs, tile_expert, n_used, w_gu, w_down, layer, tm, tf):
    n_tiles = xs.shape[0] // tm
    n_f = D_FF // tf
    assert n_f >= 2
    tile = lambda p, nu: jnp.minimum(p, nu[0] - 1)
    col = lambda p, f, nu: jnp.where(p < nu[0], f, n_f - 1)
    grid_spec = pltpu.PrefetchScalarGridSpec(
        num_scalar_prefetch=2,
        grid=(n_tiles, n_f),
        in_specs=[pl.BlockSpec((tm,) + ROW_TILE, lambda p, f, te, nu: (tile(p, nu), 0, 0)),
                  pl.BlockSpec((None, None, D_MODEL, tf),
                               lambda p, f, te, nu: (layer, te[p], 0, col(p, f, nu))),
                  pl.BlockSpec((None, None, D_MODEL, tf),
                               lambda p, f, te, nu: (layer, te[p], 0, n_f + col(p, f, nu))),
                  pl.BlockSpec((None, None, tf, D_MODEL),
                               lambda p, f, te, nu: (layer, te[p], col(p, f, nu), 0))],
        out_specs=pl.BlockSpec((tm,) + ROW_TILE, lambda p, f, te, nu: (p, 0, 0)),
        scratch_shapes=[pltpu.VMEM((tm, D_MODEL), BF16), pltpu.VMEM((tm, D_MODEL), F32)],
    )
    return pl.pallas_call(
        _moe_ffn_kernel,
        grid_spec=grid_spec,
        out_shape=jax.ShapeDtypeStruct(xs.shape, F32),
        compiler_params=_params("arbitrary", "arbitrary"),
        name="moe_ffn",
    )(tile_expert, n_used, xs, w_gu, w_gu, w_down)


def _moe_combine_ln_kernel(pos_ref, pos_next_ref, gate_ref, x_ref, ye_ref, g_ref, b_ref, o_ref,
                           buf_ref, sem):
    i = pl.program_id(0)
    tq = x_ref.shape[0]
    slot = lax.rem(i, 2)

    def start_fetch(rows_ref, to_slot):
        def fetch(grp, carry):
            for jr in range(SUBLANES):
                t = SUBLANES * grp + jr
                for k in range(2):
                    pltpu.make_async_copy(ye_ref.at[rows_ref[0, 2 * t + k]], buf_ref.at[to_slot, k * tq + t],
                                          sem.at[to_slot]).start(priority=k)
            return carry

        lax.fori_loop(0, tq // SUBLANES, fetch, 0)

    @pl.when(i == 0)
    def _():
        start_fetch(pos_ref, 0)

    @pl.when(i + 1 < pl.num_programs(0))
    def _():
        start_fetch(pos_next_ref, 1 - slot)

    _wait_rows(ye_ref, 2 * tq, sem.at[slot])
    gates = gate_ref[...]
    lane = lax.broadcasted_iota(jnp.int32, gates.shape, 1)
    y0 = buf_ref[slot, 0:tq].reshape(tq, D_MODEL)
    y1 = buf_ref[slot, tq:2 * tq].reshape(tq, D_MODEL)
    y = _lane_pick(gates, lane, 0) * y0 + _lane_pick(gates, lane, 1) * y1
    o_ref[...] = _layer_norm(ALPHA * x_ref[...] + y, g_ref[...], b_ref[...])


def _moe_combine_ln(x, gates, pos, ye, g, b):
    m = x.shape[0]
    tq = COMBINE_TOKENS
    n_steps = m // tq
    pos3 = pos.reshape(n_steps, 1, 2 * tq)
    smem = functools.partial(pl.BlockSpec, memory_space=pltpu.SMEM)
    return pl.pallas_call(
        _moe_combine_ln_kernel,
        grid=(n_steps,),
        in_specs=[smem((None, 1, 2 * tq), lambda i: (i, 0, 0)),
                  smem((None, 1, 2 * tq), lambda i: (jnp.minimum(i + 1, n_steps - 1), 0, 0)),
                  pl.BlockSpec((tq, LANES), lambda i: (i, 0)),
                  pl.BlockSpec((tq, D_MODEL), lambda i: (i, 0)),
                  pl.BlockSpec(memory_space=pl.ANY),
                  pl.BlockSpec((1, D_MODEL), lambda i: (0, 0)),
                  pl.BlockSpec((1, D_MODEL), lambda i: (0, 0))],
        out_specs=pl.BlockSpec((tq, D_MODEL), lambda i: (i, 0)),
        out_shape=jax.ShapeDtypeStruct((m, D_MODEL), F32),
        scratch_shapes=[pltpu.VMEM((2, 2 * tq) + ROW_TILE, F32), pltpu.SemaphoreType.DMA((2,))],
        compiler_params=_params("arbitrary"),
        name="moe_combine_ln",
    )(pos3, pos3, gates, x, ye, g, b)


def _moe_routed_ln(x, gates, idx, w_gu, w_down, layer, g, b):
    pos, tile_expert, n_used, pad_dst, n_tiles = _moe_plan(idx[:, :2], MOE_TM)
    xs = _moe_dispatch(x, pos, pad_dst, n_tiles * MOE_TM)
    ye = _moe_ffn(xs, tile_expert, n_used, w_gu, w_down, layer, MOE_TM, FFN_TF)
    return _moe_combine_ln(x, gates, pos, ye, g, b)


GDN_PASSES_QK = 1
GDN_PASSES_SOLVE = 1
GDN_PASSES_STATE = 1
INV_BLOCK_SHIFT = 4
GDN_SEQS = 4


def _gdn_chunk_kernel(qkv_ref, z_ref, ab_ref, cw_ref, alog_ref, dtb_ref, nw_ref, hist0_ref, s0_ref,
                      mix_ref, sout_ref, hout_ref, xh_ref, s_ref):
    n = pl.program_id(1)
    c = GDN_CHUNK
    hrows = SUBLANES
    n_seq = qkv_ref.shape[0]
    items = [(b, h) for b in range(n_seq) for h in range(H_A)]

    @pl.when(n == 0)
    def _():
        xh_ref[:, 0:hrows, :] = hist0_ref[...]
        s_ref[...] = s0_ref[...]

    xh_ref[:, hrows:hrows + c, :] = qkv_ref[...]

    row = lax.broadcasted_iota(jnp.int32, (c, c), 0)
    col = lax.broadcasted_iota(jnp.int32, (c, c), 1)
    causal = row >= col
    strict = row > col
    eye = jnp.where(row == col, 1.0, 0.0)
    blockdiag = (row >> INV_BLOCK_SHIFT) == (col >> INV_BLOCK_SHIFT)
    tril_ones = jnp.where(causal, 1.0, 0.0).astype(BF16)
    lane = lax.broadcasted_iota(jnp.int32, (c, LANES), 1)
    mm_s = functools.partial(_dot, passes=GDN_PASSES_SOLVE)

    beta_all, gcum_all, gcum_t = [], [], []
    for b in range(n_seq):
        ab = ab_ref[b]
        g_all = -jnp.exp(alog_ref[...]) * _softplus(ab + dtb_ref[...])
        beta_all.append(_sigmoid(ab))
        gc = _dot_exact_lhs(tril_ones, g_all)
        gcum_all.append(gc)
        gcum_t.append(gc.T)

    def conv_silu(b, c0):
        acc = xh_ref[b, hrows - 3:hrows - 3 + c, c0:c0 + LANES] * cw_ref[0:1, c0:c0 + LANES]
        for j in range(1, CONV_W):
            acc = acc + (xh_ref[b, hrows - 3 + j:hrows - 3 + j + c, c0:c0 + LANES]
                         * cw_ref[j:j + 1, c0:c0 + LANES])
        return _silu(acc)

    q, k, v, kb, egc, gc_col, gc_last, beta_col, decay = {}, {}, {}, {}, {}, {}, {}, {}, {}
    for it in items:
        b, h = it
        qi = conv_silu(b, h * DK_A)
        ki = conv_silu(b, QK_A + h * DK_A)
        v[it] = conv_silu(b, 2 * QK_A + h * DV_A)
        q[it] = qi * lax.rsqrt(jnp.sum(qi * qi, axis=-1, keepdims=True) + RMS_EPS) * (DK_A ** -0.5)
        k[it] = ki * lax.rsqrt(jnp.sum(ki * ki, axis=-1, keepdims=True) + RMS_EPS)
        gc_col[it] = _lane_pick(gcum_all[b], lane, h)
        beta_col[it] = _lane_pick(beta_all[b], lane, B_LANE + h)
        gc_last[it] = gc_col[it][c - 1:c, :]
        decay[it] = jnp.where(causal, jnp.exp(gc_col[it] - gcum_t[b][h:h + 1, :c]), 0.0)
        egc[it] = jnp.exp(gc_col[it])
        kb[it] = k[it] * beta_col[it]

    kk = {it: _dot(jnp.concatenate([kb[it], q[it]], axis=0), k[it], _NT, GDN_PASSES_QK) for it in items}
    a_mat = {it: jnp.where(strict, kk[it][:c] * decay[it], 0.0) for it in items}
    attn = {it: kk[it][c:] * decay[it] for it in items}

    a_d = {it: jnp.where(blockdiag, a_mat[it], 0.0) for it in items}
    a_n = {it: jnp.where(blockdiag, 0.0, a_mat[it]) for it in items}
    p2 = {it: mm_s(a_d[it], a_d[it]) for it in items}
    e_d = {it: p2[it] - a_d[it] - mm_s(a_d[it], p2[it]) for it in items}
    p4 = {it: mm_s(p2[it], p2[it]) for it in items}
    e_d = {it: e_d[it] + p4[it] + mm_s(e_d[it], p4[it]) for it in items}
    p8 = {it: mm_s(p4[it], p4[it]) for it in items}
    e_d = {it: e_d[it] + p8[it] + mm_s(e_d[it], p8[it]) for it in items}
    m_blk = {it: a_n[it] + mm_s(e_d[it], a_n[it]) for it in items}
    rhs = {it: jnp.concatenate([v[it] * beta_col[it], kb[it] * egc[it]], axis=1) for it in items}
    rhs = {it: rhs[it] + mm_s(e_d[it], rhs[it]) for it in items}
    m2 = {it: mm_s(m_blk[it], m_blk[it]) for it in items}
    f_y = {it: m2[it] - m_blk[it] - mm_s(m_blk[it], m2[it]) for it in items}
    sol = {it: rhs[it] + mm_s(f_y[it], rhs[it]) for it in items}

    s_old = {it: s_ref[it[0], it[1]] for it in items}
    wq = {it: _dot(jnp.concatenate([sol[it][:, DV_A:], q[it] * egc[it]], axis=0), s_old[it],
                   _NN, GDN_PASSES_STATE) for it in items}
    v_new = {it: sol[it][:, :DV_A] - wq[it][:c] for it in items}
    o = {it: wq[it][c:] + _dot(attn[it], v_new[it], _NN, GDN_PASSES_STATE) for it in items}
    for it in items:
        k_dec = k[it] * jnp.exp(gc_last[it] - gc_col[it])
        s_ref[it[0], it[1]] = (s_old[it] * jnp.exp(gc_last[it])
                               + _dot(k_dec, v_new[it], _TN, GDN_PASSES_STATE))
    for it in items:
        b, h = it
        oi = o[it]
        oi = oi * lax.rsqrt(jnp.mean(oi * oi, axis=-1, keepdims=True) + RMS_EPS) * nw_ref[...]
        mix_ref[b, :, h * DV_A:(h + 1) * DV_A] = oi * _silu(z_ref[b, :, h * DV_A:(h + 1) * DV_A])

    xh_ref[:, 0:hrows, :] = xh_ref[:, c:c + hrows, :]

    @pl.when(n == pl.num_programs(1) - 1)
    def _():
        sout_ref[...] = s_ref[...]
        hout_ref[...] = xh_ref[:, c:c + hrows, :]


def _gdn_chunked(proj, conv_w, alog_row, dtb_row, nw_row, hist0, s0):
    bsz, t, _ = proj.shape
    c = GDN_CHUNK
    nb = GDN_SEQS
    n_chunks = t // c
    row_spec = lambda width, blk: pl.BlockSpec((nb, c, width), lambda b, n: (b, n, blk))
    full2 = lambda shape: pl.BlockSpec(shape, lambda b, n: (0, 0))
    return pl.pallas_call(
        _gdn_chunk_kernel,
        grid=(bsz // nb, n_chunks),
        in_specs=[row_spec(CONV_DIM, 0),
                  row_spec(V_A, A_Z_OFF // V_A),
                  row_spec(LANES, A_AB_OFF // LANES),
                  full2((CONV_W, CONV_DIM)),
                  full2((1, LANES)), full2((1, LANES)), full2((1, DV_A)),
                  pl.BlockSpec((nb, SUBLANES, CONV_DIM), lambda b, n: (b, 0, 0)),
                  pl.BlockSpec((nb, H_A, DK_A, DV_A), lambda b, n: (b, 0, 0, 0))],
        out_specs=[pl.BlockSpec((nb, c, V_A), lambda b, n: (b, n, 0)),
                   pl.BlockSpec((nb, H_A, DK_A, DV_A), lambda b, n: (b, 0, 0, 0)),
                   pl.BlockSpec((nb, SUBLANES, CONV_DIM), lambda b, n: (b, 0, 0))],
        out_shape=[jax.ShapeDtypeStruct((bsz, t, V_A), F32),
                   jax.ShapeDtypeStruct((bsz, H_A, DK_A, DV_A), F32),
                   jax.ShapeDtypeStruct((bsz, SUBLANES, CONV_DIM), F32)],
        scratch_shapes=[pltpu.VMEM((nb, SUBLANES + c, CONV_DIM), F32),
                        pltpu.VMEM((nb, H_A, DK_A, DV_A), F32)],
        compiler_params=_params("parallel", "arbitrary"),
        name="gdn_chunked",
    )(proj, proj, proj, conv_w, alog_row, dtb_row, nw_row, hist0, s0)


def _gdn_step_kernel(qkv_ref, z_ref, ab_ref, cw_ref, alog_ref, dtb_ref, nw_ref, hist_ref, s_ref,
                     mix_ref, sout_ref, hout_ref, o_scr):
    rows = qkv_ref.shape[0]
    new = qkv_ref[...]
    y = hist_ref[0] * cw_ref[0:1, :]
    for j in range(1, CONV_W - 1):
        y = y + hist_ref[j] * cw_ref[j:j + 1, :]
    y = y + new * cw_ref[CONV_W - 1:CONV_W, :]
    qkv = _silu(y)
    for j in range(CONV_W - 2):
        hout_ref[j] = hist_ref[j + 1]
    hout_ref[CONV_W - 2] = new

    ab = ab_ref[...]
    lane = lax.broadcasted_iota(jnp.int32, ab.shape, 1)
    g_all = -jnp.exp(alog_ref[...]) * _softplus(ab + dtb_ref[...])
    beta_all = _sigmoid(ab)
    ri = lax.broadcasted_iota(jnp.int32, (DK_A, DK_A), 0)
    ci = lax.broadcasted_iota(jnp.int32, (DK_A, DK_A), 1)
    diag = ri == ci

    def as_column(r):
        return jnp.sum(jnp.where(diag, r, 0.0), axis=-1, keepdims=True)

    q, k, v, eg, beta = [], [], [], [], []
    for h in range(H_A):
        qh = qkv[:, h * DK_A:(h + 1) * DK_A]
        kh = qkv[:, QK_A + h * DK_A:QK_A + (h + 1) * DK_A]
        v.append(qkv[:, 2 * QK_A + h * DV_A:2 * QK_A + (h + 1) * DV_A])
        q.append(qh * lax.rsqrt(jnp.sum(qh * qh, axis=-1, keepdims=True) + RMS_EPS) * (DK_A ** -0.5))
        k.append(kh * lax.rsqrt(jnp.sum(kh * kh, axis=-1, keepdims=True) + RMS_EPS))
        eg.append(jnp.exp(_lane_pick(g_all, lane, h)))
        beta.append(_lane_pick(beta_all, lane, B_LANE + h))

    for h in range(H_A):
        seqs = range(rows)
        k_col = [as_column(k[h][s:s + 1, :]) for s in seqs]
        q_col = [as_column(q[h][s:s + 1, :]) for s in seqs]
        st = [s_ref[s, h] * eg[h][s:s + 1, :] for s in seqs]
        ks = [jnp.sum(k_col[s] * st[s], axis=0, keepdims=True) for s in seqs]
        u = [beta[h][s:s + 1, :] * (v[h][s:s + 1, :] - ks[s]) for s in seqs]
        st = [st[s] + k_col[s] * u[s] for s in seqs]
        for s in seqs:
            sout_ref[s, h] = st[s]
            o_scr[s:s + 1, h * DV_A:(h + 1) * DV_A] = jnp.sum(q_col[s] * st[s], axis=0, keepdims=True)

    for h in range(H_A):
        o = o_scr[:, h * DV_A:(h + 1) * DV_A]
        o = o * lax.rsqrt(jnp.mean(o * o, axis=-1, keepdims=True) + RMS_EPS) * nw_ref[...]
        mix_ref[:, h * DV_A:(h + 1) * DV_A] = o * _silu(z_ref[:, h * DV_A:(h + 1) * DV_A])


def _gdn_step(proj, conv_w, alog_row, dtb_row, nw_row, hist_t, s0, layer):
    bsz = proj.shape[0]
    rows = SUBLANES
    full1 = lambda shape: pl.BlockSpec(shape, lambda i: (0, 0))
    return pl.pallas_call(
        _gdn_step_kernel,
        grid=(bsz // rows,),
        in_specs=[pl.BlockSpec((rows, CONV_DIM), lambda i: (i, 0)),
                  pl.BlockSpec((rows, V_A), lambda i: (i, A_Z_OFF // V_A)),
                  pl.BlockSpec((rows, LANES), lambda i: (i, A_AB_OFF // LANES)),
                  full1((CONV_W, CONV_DIM)),
                  full1((1, LANES)), full1((1, LANES)), full1((1, DV_A)),
                  pl.BlockSpec((None, CONV_W - 1, rows, CONV_DIM), lambda i: (layer, 0, i, 0)),
                  pl.BlockSpec((None, rows, H_A, DK_A, DV_A), lambda i: (layer, i, 0, 0, 0))],
        out_specs=[pl.BlockSpec((rows, V_A), lambda i: (i, 0)),
                   pl.BlockSpec((rows, H_A, DK_A, DV_A), lambda i: (i, 0, 0, 0)),
                   pl.BlockSpec((CONV_W - 1, rows, CONV_DIM), lambda i: (0, i, 0))],
        out_shape=[jax.ShapeDtypeStruct((bsz, V_A), F32),
                   jax.ShapeDtypeStruct((bsz, H_A, DK_A, DV_A), F32),
                   jax.ShapeDtypeStruct((CONV_W - 1, bsz, CONV_DIM), F32)],
        scratch_shapes=[pltpu.VMEM((rows, V_A), F32)],
        compiler_params=_params("parallel"),
        name="gdn_step",
    )(proj, proj, proj, conv_w, alog_row, dtb_row, nw_row, hist_t, s0)


SGU_CHUNKS = 4


def _sgu_chunk_kernel(u_ref, v_ref, g_ref, b_ref, ws_ref, bs_ref, mix_ref):
    cb = CHUNK_B
    row = lax.broadcasted_iota(jnp.int32, (cb, cb), 0)
    col = lax.broadcasted_iota(jnp.int32, (cb, cb), 1)
    ws = [jnp.where(row >= col, ws_ref[gi], 0.0).astype(BF16) for gi in range(G_B)]
    for r0 in range(0, u_ref.shape[0], cb):
        u = _gelu_tanh(u_ref[r0:r0 + cb, :])
        v = _layer_norm(_gelu_tanh(v_ref[r0:r0 + cb, :]), g_ref[...], b_ref[...])
        for gi in range(G_B):
            mixed = _dot(ws[gi], v[:, gi * CH_B:(gi + 1) * CH_B]) + bs_ref[gi]
            mix_ref[r0:r0 + cb, gi * CH_B:(gi + 1) * CH_B] = u[:, gi * CH_B:(gi + 1) * CH_B] * mixed


def _sgu_chunked(proj, ln_g, ln_b, w_s, bs_b):
    bsz, t, _ = proj.shape
    c = CHUNK_B * SGU_CHUNKS
    return pl.pallas_call(
        _sgu_chunk_kernel,
        grid=(bsz, t // c),
        in_specs=[pl.BlockSpec((None, c, D_B), lambda b, n: (b, n, 0)),
                  pl.BlockSpec((None, c, D_B), lambda b, n: (b, n, 1)),
                  pl.BlockSpec((1, D_B), lambda b, n: (0, 0)),
                  pl.BlockSpec((1, D_B), lambda b, n: (0, 0)),
                  pl.BlockSpec((G_B, CHUNK_B, CHUNK_B), lambda b, n: (0, 0, 0)),
                  pl.BlockSpec((G_B, CHUNK_B, CH_B), lambda b, n: (0, 0, 0))],
        out_specs=pl.BlockSpec((None, c, D_B), lambda b, n: (b, n, 0)),
        out_shape=jax.ShapeDtypeStruct((bsz, t, D_B), F32),
        compiler_params=_params("parallel", "parallel"),
        name="sgu_chunked",
    )(proj, proj, ln_g, ln_b, w_s, bs_b)


def _sgu_first_kernel(u_ref, v_ref, g_ref, b_ref, w00_ref, b0_ref, mix_ref, vout_ref):
    u = _gelu_tanh(u_ref[...])
    v = _layer_norm(_gelu_tanh(v_ref[...]), g_ref[...], b_ref[...])
    vout_ref[...] = v
    mix_ref[...] = u * (w00_ref[...] * v + b0_ref[...])


def _sgu_first(proj, ln_g, ln_b, w00_row, b0_row):
    bsz = proj.shape[0]
    row = lambda: pl.BlockSpec((1, D_B), lambda i: (0, 0))
    return pl.pallas_call(
        _sgu_first_kernel,
        grid=(1,),
        in_specs=[pl.BlockSpec((bsz, D_B), lambda i: (0, 0)),
                  pl.BlockSpec((bsz, D_B), lambda i: (0, 1)),
                  row(), row(), row(), row()],
        out_specs=[pl.BlockSpec((bsz, D_B), lambda i: (0, 0)),
                   pl.BlockSpec((bsz, D_B), lambda i: (0, 0))],
        out_shape=[jax.ShapeDtypeStruct((bsz, D_B), F32),
                   jax.ShapeDtypeStruct((bsz, D_B), F32)],
        compiler_params=_params("arbitrary"),
        name="sgu_first",
    )(proj, proj, ln_g, ln_b, w00_row, b0_row)


def _head_masks(shape):
    lane = lax.broadcasted_iota(jnp.int32, shape, len(shape) - 1)
    return [(lane // DH_X) == h for h in range(H_X)]


def _xattn_kernel(q_ref, k_ref, v_ref, o_ref):
    q = q_ref[...]
    kb = k_ref[...].astype(BF16)
    vb = v_ref[...].astype(BF16)
    masks = _head_masks(q.shape)
    heads = range(H_X)
    s = [_dot(jnp.where(masks[h], q, 0.0), kb, _NT) * (DH_X ** -0.5) for h in heads]
    p = [jnp.exp(s[h] - jnp.max(s[h], axis=-1, keepdims=True)) for h in heads]
    p = [p[h] / jnp.sum(p[h], axis=-1, keepdims=True) for h in heads]
    pv = [_dot(p[h], vb) for h in heads]
    out = jnp.where(masks[0], pv[0], 0.0)
    for h in range(1, H_X):
        out = jnp.where(masks[h], pv[h], out)
    o_ref[...] = out


def _xattn(proj, xq_blk, mem_k, mem_v, layer, tq):
    bsz, t, _ = proj.shape
    return pl.pallas_call(
        _xattn_kernel,
        grid=(bsz, t // tq),
        in_specs=[pl.BlockSpec((None, tq, D_X), lambda b, i: (b, i, xq_blk)),
                  pl.BlockSpec((None, None, N_MEM, D_X), lambda b, i: (layer, b, 0, 0)),
                  pl.BlockSpec((None, None, N_MEM, D_X), lambda b, i: (layer, b, 0, 0))],
        out_specs=pl.BlockSpec((None, tq, D_X), lambda b, i: (b, i, 0)),
        out_shape=jax.ShapeDtypeStruct((bsz, t, D_X), F32),
        compiler_params=_params("parallel", "parallel"),
        name="xattn",
    )(proj, mem_k, mem_v)


def _xattn_step_kernel(q_ref, k_ref, v_ref, o_ref):
    rows = q_ref.shape[0]
    q = q_ref[...]
    hrow = lax.broadcasted_iota(jnp.int32, (SUBLANES, D_X), 0)
    hlane = lax.broadcasted_iota(jnp.int32, (SUBLANES, D_X), 1) // DH_X
    sel = hrow == hlane
    seqs = range(rows)
    sc = [_dot(jnp.where(sel, q[s:s + 1, :], 0.0), k_ref[s], _NT) * (DH_X ** -0.5) for s in seqs]
    p = [jnp.exp(sc[s] - jnp.max(sc[s], axis=-1, keepdims=True)) for s in seqs]
    p = [p[s] / jnp.sum(p[s], axis=-1, keepdims=True) for s in seqs]
    pv = [_dot(p[s], v_ref[s]) for s in seqs]
    for s in seqs:
        o_ref[s:s + 1, :] = jnp.sum(jnp.where(sel, pv[s], 0.0), axis=0, keepdims=True)


def _xattn_step(proj, xq_blk, mem_k, mem_v, layer):
    bsz = proj.shape[0]
    rows = SUBLANES
    return pl.pallas_call(
        _xattn_step_kernel,
        grid=(bsz // rows,),
        in_specs=[pl.BlockSpec((rows, D_X), lambda i: (i, xq_blk)),
                  pl.BlockSpec((None, rows, N_MEM, D_X), lambda i: (layer, i, 0, 0)),
                  pl.BlockSpec((None, rows, N_MEM, D_X), lambda i: (layer, i, 0, 0))],
        out_specs=pl.BlockSpec((rows, D_X), lambda i: (i, 0)),
        out_shape=jax.ShapeDtypeStruct((bsz, D_X), F32),
        compiler_params=_params("parallel"),
        name="xattn_step",
    )(proj, mem_k, mem_v)


PROJ_ROWS = 512
FFN_ROWS = 1024


def _pad_lanes(v):
    return jnp.zeros((1, LANES), F32).at[0, :v.shape[0]].set(v.astype(F32))


def _prep_weights(a_w_in, a_A_log, a_dt_bias, b_w_s, b_b_s, moe_router):
    qkvz = a_w_in[:, :, :A_XQ_OFF]
    a_cols = a_w_in[:, :, A_XQ_OFF:A_XQ_OFF + H_A]
    b_cols = a_w_in[:, :, A_XQ_OFF + H_A:A_XQ_OFF + 2 * H_A]
    xq = a_w_in[:, :, A_XQ_OFF + 2 * H_A:]
    ab = jnp.zeros(a_w_in.shape[:2] + (LANES,), F32)
    ab = ab.at[:, :, :H_A].set(a_cols).at[:, :, B_LANE:B_LANE + H_A].set(b_cols)
    a_w = jnp.concatenate([qkvz, xq, ab], axis=-1).astype(BF16)
    alog_rows = [_pad_lanes(a_A_log[j]) for j in range(a_A_log.shape[0])]
    dtb_rows = [_pad_lanes(a_dt_bias[j]) for j in range(a_dt_bias.shape[0])]
    bs_b = jnp.broadcast_to(b_b_s[..., None], b_b_s.shape + (CH_B,)).astype(F32)
    w00_rows = jnp.repeat(b_w_s[:, :, 0, 0], CH_B, axis=-1)[:, None, :]
    b0_rows = jnp.repeat(b_b_s[:, :, 0], CH_B, axis=-1)[:, None, :]
    r = jnp.zeros(moe_router.shape[:2] + (LANES,), F32).at[:, :, :N_EXP].set(moe_router)
    r_hi = r.astype(BF16)
    r_lo = (r - r_hi.astype(F32)).astype(BF16)
    return a_w, alog_rows, dtb_rows, bs_b, w00_rows, b0_rows, r_hi, r_lo


def kernel(x_prompt, x_sample, state_gdn, state_conv, cache_mem_k, cache_mem_v, mem_prompt,
           a_w_in, a_conv_w, a_A_log, a_dt_bias, a_norm_w,
           b_w_in, b_ln_g, b_ln_b, b_w_s, b_b_s,
           w_mem_kv, w_out, ln1_g, ln1_b, ln2_g, ln2_b,
           ffn_w_gu, ffn_w_down, moe_router, moe_w_gu, moe_w_down):
    n_p, t_p, _ = x_prompt.shape
    n_s = x_sample.shape[0]
    n_a = a_w_in.shape[0]

    a_w, alog_rows, dtb_rows, bs_b, w00_rows, b0_rows, r_hi, r_lo = _prep_weights(
        a_w_in, a_A_log, a_dt_bias, b_w_s, b_b_s, moe_router)
    ffn_gu = ffn_w_gu.astype(BF16)[:, None]
    ffn_dn = ffn_w_down.astype(BF16)[:, None]
    moe_gu = moe_w_gu.astype(BF16)
    moe_dn = moe_w_down.astype(BF16)
    row = lambda v: v.reshape(1, -1).astype(F32)

    w_kv = jnp.transpose(w_mem_kv, (1, 0, 2)).reshape(D_MODEL, DEPTH * 2 * D_X).astype(BF16)
    kv = _matmul(mem_prompt.reshape(n_p * N_MEM, D_MODEL), w_kv, PROJ_ROWS, 2 * DEPTH * D_X // 2, "mem_kv")
    kv = kv.reshape(n_p, N_MEM, DEPTH, 2, D_X)
    p_mem_k = jnp.transpose(kv[:, :, :, 0, :], (2, 0, 1, 3))
    p_mem_v = jnp.transpose(kv[:, :, :, 1, :], (2, 0, 1, 3))
    s_mem_k = cache_mem_k.reshape(DEPTH, n_s, N_MEM, D_X)
    s_mem_v = cache_mem_v.reshape(DEPTH, n_s, N_MEM, D_X)

    xp = x_prompt.reshape(n_p * t_p, D_MODEL)
    xs = x_sample.reshape(n_s, D_MODEL)
    p_hist0 = jnp.zeros((n_p, SUBLANES, CONV_DIM), F32)
    p_s0 = jnp.zeros((n_p, H_A, DK_A, DV_A), F32)
    s_hist_t = jnp.transpose(state_conv, (0, 2, 1, 3))

    p_gdn, p_conv, s_gdn, s_conv, s_sgu_v = [], [], [], [], []
    tm_p, tm_s = FFN_ROWS, n_s
    for i in range(DEPTH):
        j = i // 2
        if i % 2 == 0:
            proj_p = _matmul(xp, a_w[j], PROJ_ROWS, A_COLS // 3, "in_proj_a")
            proj_s = _matmul(xs, a_w[j], tm_s, A_COLS // 3, "in_proj_a_s")
            nw = row(a_norm_w[j])
            mix_p, sp, hp = _gdn_chunked(proj_p.reshape(n_p, t_p, A_COLS), a_conv_w[j],
                                         alog_rows[j], dtb_rows[j], nw, p_hist0, p_s0)
            mix_p = mix_p.reshape(n_p * t_p, V_A)
            p_gdn.append(sp)
            p_conv.append(hp[:, SUBLANES - (CONV_W - 1):, :])
            mix_s, ss, hs = _gdn_step(proj_s, a_conv_w[j], alog_rows[j], dtb_rows[j], nw,
                                      s_hist_t, state_gdn, j)
            s_gdn.append(ss)
            s_conv.append(jnp.transpose(hs, (1, 0, 2)))
            cols, xq_blk = A_COLS, A_XQ_OFF // D_X
        else:
            b_w = b_w_in[j].astype(BF16)
            proj_p = _matmul(xp, b_w, PROJ_ROWS, B_COLS // 2, "in_proj_b")
            proj_s = _matmul(xs, b_w, tm_s, B_COLS // 2, "in_proj_b_s")
            mix_p = _sgu_chunked(proj_p.reshape(n_p, t_p, B_COLS), row(b_ln_g[j]), row(b_ln_b[j]),
                                 b_w_s[j], bs_b[j]).reshape(n_p * t_p, D_B)
            mix_s, v_s = _sgu_first(proj_s, row(b_ln_g[j]), row(b_ln_b[j]), w00_rows[j], b0_rows[j])
            s_sgu_v.append(v_s.reshape(n_s, 1, D_B))
            cols, xq_blk = B_COLS, B_XQ_OFF // D_X
        xo_p = _xattn(proj_p.reshape(n_p, t_p, cols), xq_blk, p_mem_k, p_mem_v, i, PROJ_ROWS)
        xo_s = _xattn_step(proj_s, xq_blk, s_mem_k, s_mem_v, i)
        g1, b1, g2, b2 = row(ln1_g[i]), row(ln1_b[i]), row(ln2_g[i]), row(ln2_b[i])
        w_o = w_out[i].astype(BF16)
        xo_p = xo_p.reshape(n_p * t_p, D_X)
        if i % 2 == 0:
            xp = _outproj_ln(mix_p, xo_p, xp, w_o, g1, b1, PROJ_ROWS)
            xs = _outproj_ln(mix_s, xo_s, xs, w_o, g1, b1, tm_s)
            xp = _ffn_ln(xp, None, ffn_gu, ffn_dn, j, g2, b2, tm_p, FFN_TF)
            xs = _ffn_ln(xs, None, ffn_gu, ffn_dn, j, g2, b2, tm_s, FFN_TF)
        else:
            router = (r_hi[j], r_lo[j])
            xp, _, gates_p, idx_p = _outproj_ln(mix_p, xo_p, xp, w_o, g1, b1, PROJ_ROWS, router)
            xs, comb_s, _, _ = _outproj_ln(mix_s, xo_s, xs, w_o, g1, b1, tm_s, router)
            xp = _moe_routed_ln(xp, gates_p, idx_p, moe_gu, moe_dn, j, g2, b2)
            xs = _ffn_ln(xs, comb_s, moe_gu, moe_dn, j, g2, b2, tm_s, FFN_TF)

    mem_shape = (DEPTH, n_p, N_MEM, H_X, DH_X)
    return (xp.reshape(n_p, t_p, D_MODEL),
            xs.reshape(n_s, 1, D_MODEL),
            jnp.stack(p_gdn),
            jnp.stack(p_conv),
            p_mem_k.reshape(mem_shape),
            p_mem_v.reshape(mem_shape),
            jnp.stack(s_gdn),
            jnp.stack(s_conv),
            jnp.stack(s_sgu_v))
```
